```python
import math
import jax, jax.numpy as jnp
from jax import lax
import numpy as np

D_MODEL = 2048
BATCH = 4
SEQ = 2048
DEPTH = 2
DEC_BATCH = 128
DEC_SEQ = 8
PAST_LEN = 16384
PAGE_SIZE = 128

N_MEM = 256
MIX_W = D_MODEL // 2
N_BRANCH = 3
RW_HD = 64
RW_HEADS = MIX_W // RW_HD
RW_DECAY_LORA = 64
RW_AAA_LORA = 64
RW_GATE_LORA = 128
RW_COLS = 3 * MIX_W + RW_DECAY_LORA + RW_AAA_LORA + RW_GATE_LORA
RW_GN_EPS = 64e-5
ML_HEADS = 8
ML_HD = MIX_W // ML_HEADS
ML_COLS = 4 * MIX_W + 2 * ML_HEADS
ML_GATE_CAP = 15.0
RT_HEADS = 4
RT_HD = MIX_W // RT_HEADS
RT_COLS = 4 * MIX_W
ROPE_BASE = 10000.0
IN_COLS = RW_COLS + ML_COLS + RT_COLS + N_BRANCH * D_MODEL
X_HEADS = 4
X_HD = 128
X_W = X_HEADS * X_HD
D_FF = 4 * D_MODEL
CHUNK = 64
EPS = 1e-6

kernel_name = 'hybrid_rwkv7_mlstm_retnet_memxattn_step'


def _split(x, sizes):
    idx = [int(i) for i in np.cumsum(sizes)[:-1]]
    return jnp.split(x, idx, axis=-1)


def _rmsnorm(x, g):
    xf = x.astype(jnp.float32)
    y = xf * lax.rsqrt(jnp.mean(xf * xf, axis=-1, keepdims=True) + EPS)
    return (y * g.astype(jnp.float32)).astype(x.dtype)


def _head_rms(x):
    return x * lax.rsqrt(jnp.mean(x * x, axis=-1, keepdims=True) + EPS)


def _rope(x, pos):
    half = x.shape[-1] // 2
    inv = ROPE_BASE ** (-jnp.arange(half, dtype=jnp.float32) / half)
    ang = pos[:, None] * inv[None, :]
    cos = jnp.cos(ang)[None, :, None, :]
    sin = jnp.sin(ang)[None, :, None, :]
    x1, x2 = x[..., :half], x[..., half:]
    return jnp.concatenate([x1 * cos - x2 * sin, x1 * sin + x2 * cos], axis=-1)


def _ret_log_decay():
    return jnp.log(1.0 - jnp.exp(jnp.linspace(math.log(1.0 / 32), math.log(1.0 / 512), RT_HEADS)))


def _chunked(step, carry, xs, L):
    c = CHUNK if L % CHUNK == 0 else L
    n = L // c
    def to_blocks(t):
        return jnp.moveaxis(t.reshape((t.shape[0], n, c) + t.shape[2:]), 1, 0)
    carry, ys = lax.scan(step, carry, tuple(to_blocks(t) for t in xs))
    ys = jnp.moveaxis(ys, 0, 1)
    return carry, ys.reshape((ys.shape[0], L) + ys.shape[3:])


def _rwkv_branch(u, shift0, s0, mu, w0, w_up, a0, a_up, g_up, k_k, k_a, r_k, gn_g, gn_b):
    B, L, _ = u.shape
    f32 = jnp.float32
    uf = u.astype(f32)
    prev = jnp.concatenate([shift0.astype(f32)[:, None, :], uf[:, :-1]], axis=1)
    z = uf + (prev - uf) * mu
    r, k, v, wd, ad, gd = _split(z, [MIX_W, MIX_W, MIX_W, RW_DECAY_LORA, RW_AAA_LORA, RW_GATE_LORA])
    w_log = -jax.nn.softplus(-(w0 + jnp.tanh(wd) @ w_up)) - 0.5
    decay = jnp.exp(-jnp.exp(w_log))
    a = jax.nn.sigmoid(a0 + ad @ a_up)
    g = jax.nn.sigmoid(gd) @ g_up
    heads = lambda t: t.reshape(B, L, RW_HEADS, RW_HD)
    kk = heads(k * k_k)
    kk = kk / jnp.maximum(jnp.sqrt(jnp.sum(kk * kk, axis=-1, keepdims=True)), 1e-12)
    k = heads(k * (1.0 + (a - 1.0) * k_a))
    r, v, decay, a = heads(r), heads(v), heads(decay), heads(a)

    def step(S, inp):
        r_t, k_t, v_t, w_t, kk_t, a_t = inp
        sa = jnp.einsum('bhvk,bhk->bhv', S, kk_t)
        S = (S * w_t[:, :, None, :] - sa[..., None] * (kk_t * a_t)[:, :, None, :]
             + v_t[..., None] * k_t[:, :, None, :])
        return S, jnp.einsum('bhvk,bhk->bhv', S, r_t)

    tm = lambda t: jnp.moveaxis(t, 1, 0)
    S, y = lax.scan(step, s0.astype(f32), (tm(r), tm(k), tm(v), tm(decay), tm(kk), tm(a)))
    y = jnp.moveaxis(y, 0, 1)
    mean = jnp.mean(y, axis=-1, keepdims=True)
    var = jnp.mean(jnp.square(y - mean), axis=-1, keepdims=True)
    y = ((y - mean) * lax.rsqrt(var + RW_GN_EPS)).reshape(B, L, MIX_W) * gn_g + gn_b
    bonus = (jnp.sum(r * k * r_k, axis=-1, keepdims=True) * v).reshape(B, L, MIX_W)
    return (y + bonus) * g, S, u[:, -1]


def _mlstm_block(carry, inp):
    C0, n0, m0 = carry
    q, k, v, ig, lf = inp
    c = q.shape[1]
    b = jnp.moveaxis(jnp.cumsum(lf, axis=1), 1, 2)
    igh = jnp.moveaxis(ig, 1, 2)
    causal = jnp.tril(jnp.ones((c, c), bool))
    dlog = jnp.where(causal, b[..., :, None] - b[..., None, :] + igh[..., None, :], -jnp.inf)
    m_inter = b + m0[..., None]
    m_t = jnp.maximum(m_inter, jnp.max(dlog, axis=-1))
    wts = jnp.exp(dlog - m_t[..., None]) * jnp.einsum('bthd,bshd->bhts', q, k)
    s_inter = jnp.exp(m_inter - m_t)
    num = jnp.einsum('bhts,bshe->bthe', wts, v) + jnp.einsum('bht,bthd,bhde->bthe', s_inter, q, C0)
    den = jnp.sum(wts, axis=-1) + s_inter * jnp.einsum('bthd,bhd->bht', q, n0)
    h = num / jnp.moveaxis(jnp.maximum(jnp.abs(den), jnp.exp(-m_t)), 1, 2)[..., None]
    m_new = m_t[..., -1]
    w_end = jnp.exp(b[..., -1:] - b + igh - m_new[..., None])
    f_end = jnp.exp(b[..., -1] + m0 - m_new)
    C = f_end[..., None, None] * C0 + jnp.einsum('bhs,bshd,bshe->bhde', w_end, k, v)
    n = f_end[..., None] * n0 + jnp.einsum('bhs,bshd->bhd', w_end, k)
    return (C, n, m_new), h


def _mlstm_branch(cols, c0, n0, m0, i_b, f_b, norm_g):
    B, L, _ = cols.shape
    f32 = jnp.float32
    q, k, v, o, ig, fg = _split(cols.astype(f32), [MIX_W] * 4 + [ML_HEADS] * 2)
    heads = lambda t: t.reshape(B, L, ML_HEADS, ML_HD)
    q, k, v = heads(q), heads(k) * ML_HD ** -0.5, heads(v)
    ig = ML_GATE_CAP * jnp.tanh((ig + i_b) / ML_GATE_CAP)
    lf = jax.nn.log_sigmoid(ML_GATE_CAP * jnp.tanh((fg + f_b) / ML_GATE_CAP))
    (C, n, m), h = _chunked(_mlstm_block, (c0.astype(f32), n0.astype(f32), m0.astype(f32)), (q, k, v, ig, lf), L)
    h = _head_rms(h).reshape(B, L, MIX_W) * norm_g
    return jax.nn.sigmoid(o) * h, C, n, m


def _ret_branch(cols, s0, pos0):
    B, L, _ = cols.shape
    f32 = jnp.float32
    q, k, v, g = _split(cols.astype(f32), [MIX_W] * 4)
    heads = lambda t: t.reshape(B, L, RT_HEADS, RT_HD)
    pos = jnp.arange(L, dtype=f32) + float(pos0)
    q = _rope(heads(q), pos)
    k = _rope(heads(k), pos) * RT_HD ** -0.5
    v = heads(v)
    log_g = _ret_log_decay()

    def step(S0, inp):
        qc, kc, vc = inp
        c = qc.shape[1]
        t = jnp.arange(c, dtype=f32)
        diff = t[:, None] - t[None, :]
        dec = jnp.where(diff >= 0, jnp.exp(log_g[:, None, None] * jnp.maximum(diff, 0.0)), 0.0)
        inner = jnp.einsum('bhts,bshe->bthe', jnp.einsum('bthd,bshd->bhts', qc, kc) * dec, vc)
        cross = jnp.einsum('bthd,bhde->bthe', qc, S0) * jnp.exp(log_g[None, :] * (t[:, None] + 1.0))[None, :, :, None]
        S = (jnp.exp(log_g * c)[None, :, None, None] * S0
             + jnp.einsum('bshd,bshe,sh->bhde', kc, vc, jnp.exp(log_g[None, :] * (c - 1.0 - t)[:, None])))
        return S, inner + cross

    S, y = _chunked(step, s0.astype(f32), (q, k, v), L)
    y = _head_rms(y).reshape(B, L, MIX_W)
    return jax.nn.silu(g) * y, S


def _mem_kv(mem, g, wkv):
    B, M, _ = mem.shape
    kv = _rmsnorm(mem, g) @ wkv
    k, v = jnp.split(kv, 2, axis=-1)
    return k.reshape(B, M, X_HEADS, X_HD), v.reshape(B, M, X_HEADS, X_HD)


def _cross_attn(h, mk, mv, wq, wo):
    B, L, _ = h.shape
    q = (h @ wq).reshape(B, L, X_HEADS, X_HD)
    s = jnp.einsum('blhd,bmhd->bhlm', q, mk).astype(jnp.float32) * X_HD ** -0.5
    p = jax.nn.softmax(s, axis=-1).astype(h.dtype)
    o = jnp.einsum('bhlm,bmhd->blhd', p, mv).reshape(B, L, X_W)
    return o @ wo


def _layer(x, mem_k, mem_v, init, pos0, l, W):
    shift0, s_rw0, c0, n0, m0, s_rt0 = init
    B, L, _ = x.shape
    dt = x.dtype
    h = _rmsnorm(x, W['g_pre_mix'][l])
    z = h @ W['w_in'][l]
    u_rw, c_ml, c_rt, c_gate = _split(z, [RW_COLS, ML_COLS, RT_COLS, N_BRANCH * D_MODEL])
    y_rw, s_rw, shift = _rwkv_branch(u_rw, shift0, s_rw0, W['rw_mu'][l], W['rw_w0'][l], W['rw_w_up'][l],
                                     W['rw_a0'][l], W['rw_a_up'][l], W['rw_g_up'][l], W['rw_k_k'][l],
                                     W['rw_k_a'][l], W['rw_r_k'][l], W['rw_gn_g'][l], W['rw_gn_b'][l])
    y_ml, c_new, n_new, m_new = _mlstm_branch(c_ml, c0, n0, m0, W['ml_i_b'][l], W['ml_f_b'][l], W['ml_norm_g'][l])
    y_rt, s_rt = _ret_branch(c_rt, s_rt0, pos0)
    ys = jnp.stack([y_rw, y_ml, y_rt], axis=2).astype(dt)
    proj = jnp.einsum('blcw,cwd->blcd', ys, W['w_br'][l])
    gates = jax.nn.sigmoid(c_gate.astype(jnp.float32)).reshape(B, L, N_BRANCH, D_MODEL)
    merged = jnp.sum(gates * proj.astype(jnp.float32), axis=2).astype(dt)
    x = x + _rmsnorm(merged @ W['w_out'][l], W['g_post_mix'][l])
    h = _rmsnorm(x, W['g_pre_x'][l])
    x = x + _rmsnorm(_cross_attn(h, mem_k, mem_v, W['x_wq'][l], W['x_wo'][l]), W['g_post_x'][l])
    h = _rmsnorm(x, W['g_pre_ff'][l])
    ff = jnp.square(jax.nn.relu(h @ W['ff_w1'][l])) @ W['ff_w2'][l]
    x = x + _rmsnorm(ff, W['g_post_ff'][l])
    new = (shift.astype(dt), s_rw.astype(dt), c_new.astype(dt), n_new.astype(dt), m_new.astype(dt), s_rt.astype(dt))
    return x, new


def setup_inputs(seed: int = 0) -> dict:
    key = jax.random.key(seed)
    ks = list(jax.random.split(key, 64))
    cnt = [0]
    f32 = jnp.float32
    def nk():
        cnt[0] += 1
        return ks[cnt[0] - 1]
    def nrm(shape, scale=1.0):
        return jax.random.normal(nk(), shape, f32) * scale
    def uni(shape, lo, hi):
        return jax.random.uniform(nk(), shape, f32, lo, hi)
    def gain(shape):
        return 1.0 + nrm(shape, 0.05)
    D = D_MODEL
    inp = {}
    inp['x_prompt'] = nrm((BATCH, SEQ, D))
    inp['x_sample'] = nrm((DEC_BATCH, DEC_SEQ, D))
    inp['mem_prompt'] = nrm((BATCH, N_MEM, D))
    inp['state_rwkv_shift'] = nrm((DEPTH, DEC_BATCH, RW_COLS))
    inp['state_rwkv'] = nrm((DEPTH, DEC_BATCH, RW_HEADS, RW_HD, RW_HD), 0.3)
    inp['state_mlstm_c'] = nrm((DEPTH, DEC_BATCH, ML_HEADS, ML_HD, ML_HD), 0.3)
    inp['state_mlstm_n'] = nrm((DEPTH, DEC_BATCH, ML_HEADS, ML_HD), 0.3)
    inp['state_mlstm_m'] = uni((DEPTH, DEC_BATCH, ML_HEADS), -2.0, 2.0)
    inp['state_ret'] = nrm((DEPTH, DEC_BATCH, RT_HEADS, RT_HD, RT_HD))
    inp['cache_mem_k'] = nrm((DEPTH, DEC_BATCH, N_MEM, X_HEADS, X_HD))
    inp['cache_mem_v'] = nrm((DEPTH, DEC_BATCH, N_MEM, X_HEADS, X_HD))
    for name in ['g_pre_mix', 'g_post_mix', 'g_pre_x', 'g_post_x', 'g_pre_ff', 'g_post_ff', 'g_mem']:
        inp[name] = gain((DEPTH, D))
    inp['w_in'] = nrm((DEPTH, D, IN_COLS), D ** -0.5)
    inp['rw_mu'] = uni((DEPTH, RW_COLS), 0.0, 1.0)
    inp['rw_w0'] = uni((DEPTH, MIX_W), -6.0, -1.0)
    inp['rw_w_up'] = nrm((DEPTH, RW_DECAY_LORA, MIX_W), 0.5 * RW_DECAY_LORA ** -0.5)
    inp['rw_a0'] = nrm((DEPTH, MIX_W), 0.1)
    inp['rw_a_up'] = nrm((DEPTH, RW_AAA_LORA, MIX_W), RW_AAA_LORA ** -0.5)
    inp['rw_g_up'] = nrm((DEPTH, RW_GATE_LORA, MIX_W), RW_GATE_LORA ** -0.5)
    inp['rw_k_k'] = 0.85 + nrm((DEPTH, MIX_W), 0.05)
    inp['rw_k_a'] = 1.0 + nrm((DEPTH, MIX_W), 0.05)
    inp['rw_r_k'] = nrm((DEPTH, RW_HEADS, RW_HD), 0.1)
    inp['rw_gn_g'] = gain((DEPTH, MIX_W))
    inp['rw_gn_b'] = nrm((DEPTH, MIX_W), 0.01)
    inp['ml_i_b'] = nrm((DEPTH, ML_HEADS), 0.1)
    inp['ml_f_b'] = jnp.linspace(3.0, 6.0, ML_HEADS)[None, :] + nrm((DEPTH, ML_HEADS), 0.1)
    inp['ml_norm_g'] = gain((DEPTH, MIX_W))
    inp['w_br'] = nrm((DEPTH, N_BRANCH, MIX_W, D), MIX_W ** -0.5)
    inp['w_out'] = nrm((DEPTH, D, D), D ** -0.5)
    inp['x_wq'] = nrm((DEPTH, D, X_W), D ** -0.5)
    inp['x_wkv'] = nrm((DEPTH, D, 2 * X_W), D ** -0.5)
    inp['x_wo'] = nrm((DEPTH, X_W, D), X_W ** -0.5)
    inp['ff_w1'] = nrm((DEPTH, D, D_FF), D ** -0.5)
    inp['ff_w2'] = nrm((DEPTH, D_FF, D), D_FF ** -0.5)
    return inp


def reference(x_prompt, x_sample, mem_prompt, state_rwkv_shift, state_rwkv, state_mlstm_c, state_mlstm_n,
              state_mlstm_m, state_ret, cache_mem_k, cache_mem_v, g_pre_mix, g_post_mix, g_pre_x, g_post_x,
              g_pre_ff, g_post_ff, g_mem, w_in, rw_mu, rw_w0, rw_w_up, rw_a0, rw_a_up, rw_g_up, rw_k_k, rw_k_a,
              rw_r_k, rw_gn_g, rw_gn_b, ml_i_b, ml_f_b, ml_norm_g, w_br, w_out, x_wq, x_wkv, x_wo, ff_w1, ff_w2):
    W = dict(g_pre_mix=g_pre_mix, g_post_mix=g_post_mix, g_pre_x=g_pre_x, g_post_x=g_post_x,
             g_pre_ff=g_pre_ff, g_post_ff=g_post_ff, w_in=w_in, rw_mu=rw_mu, rw_w0=rw_w0, rw_w_up=rw_w_up,
             rw_a0=rw_a0, rw_a_up=rw_a_up, rw_g_up=rw_g_up, rw_k_k=rw_k_k, rw_k_a=rw_k_a, rw_r_k=rw_r_k,
             rw_gn_g=rw_gn_g, rw_gn_b=rw_gn_b, ml_i_b=ml_i_b, ml_f_b=ml_f_b, ml_norm_g=ml_norm_g,
             w_br=w_br, w_out=w_out, x_wq=x_wq, x_wo=x_wo, ff_w1=ff_w1, ff_w2=ff_w2)
    f32 = jnp.float32
    bp = x_prompt.shape[0]
    xp = x_prompt
    st_p, mk_p, mv_p = [], [], []
    for l in range(DEPTH):
        mk, mv = _mem_kv(mem_prompt, g_mem[l], x_wkv[l])
        init = (jnp.zeros((bp, RW_COLS), f32), jnp.zeros((bp, RW_HEADS, RW_HD, RW_HD), f32),
                jnp.zeros((bp, ML_HEADS, ML_HD, ML_HD), f32), jnp.zeros((bp, ML_HEADS, ML_HD), f32),
                jnp.zeros((bp, ML_HEADS), f32), jnp.zeros((bp, RT_HEADS, RT_HD, RT_HD), f32))
        xp, st = _layer(xp, mk, mv, init, 0, l, W)
        st_p.append(st)
        mk_p.append(mk)
        mv_p.append(mv)
    xs = x_sample
    st_s = []
    for l in range(DEPTH):
        init = (state_rwkv_shift[l], state_rwkv[l], state_mlstm_c[l], state_mlstm_n[l], state_mlstm_m[l], state_ret[l])
        xs, st = _layer(xs, cache_mem_k[l], cache_mem_v[l], init, PAST_LEN, l, W)
        st_s.append(st)
    p_rwkv_shift = jnp.stack([s[0] for s in st_p])
    p_rwkv = jnp.stack([s[1] for s in st_p])
    p_mlstm_c = jnp.stack([s[2] for s in st_p])
    p_mlstm_n = jnp.stack([s[3] for s in st_p])
    p_mlstm_m = jnp.stack([s[4] for s in st_p])
    p_ret = jnp.stack([s[5] for s in st_p])
    p_mem_k = jnp.stack(mk_p)
    p_mem_v = jnp.stack(mv_p)
    s_rwkv_shift = jnp.stack([s[0] for s in st_s])
    s_rwkv = jnp.stack([s[1] for s in st_s])
    s_mlstm_c = jnp.stack([s[2] for s in st_s])
    s_mlstm_n = jnp.stack([s[3] for s in st_s])
    s_mlstm_m = jnp.stack([s[4] for s in st_s])
    s_ret = jnp.stack([s[5] for s in st_s])
    return (xp, xs, p_rwkv_shift, p_rwkv, p_mlstm_c, p_mlstm_n, p_mlstm_m, p_ret, p_mem_k, p_mem_v,
            s_rwkv_shift, s_rwkv, s_mlstm_c, s_mlstm_n, s_mlstm_m, s_ret)
```

```python
import functools
import math

import jax
import jax.numpy as jnp
from jax import lax
from jax.experimental import pallas as pl
from jax.experimental.pallas import tpu as pltpu

F32 = jnp.float32
BF16 = jnp.bfloat16

D_MODEL = 2048
MIX_W = D_MODEL // 2
RW_HD = 64
RW_HEADS = MIX_W // RW_HD
RW_LORA = (64, 64, 128)
RW_COLS = 3 * MIX_W + sum(RW_LORA)
RW_GN_EPS = 64e-5
ML_HEADS = 8
ML_HD = MIX_W // ML_HEADS
ML_GATE_CAP = 15.0
RT_HEADS = 4
RT_HD = MIX_W // RT_HEADS
ROPE_BASE = 10000.0
X_HEADS = 4
X_HD = 128
X_W = X_HEADS * X_HD
D_FF = 4 * D_MODEL
CHUNK = 64
EPS = 1e-6
PAST_LEN = 16384

Z_RW = 0
Z_MLG = RW_COLS
Z_ML = RW_COLS + 256
Z_RT = Z_ML + 4 * MIX_W
Z_GATE = Z_RT + 4 * MIX_W
Z_COLS = Z_GATE + 3 * D_MODEL

LANES = 128
RW_VP = RW_HD // 2
RW_CHAINS = LANES // 2


def _cparams(*sem):
    return pltpu.CompilerParams(dimension_semantics=sem, vmem_limit_bytes=56 * 1024 * 1024)


def _sigmoid(x):
    return 1.0 / (1.0 + jnp.exp(-x))


def _softplus(x):
    return jnp.maximum(x, 0.0) + jnp.log(1.0 + jnp.exp(-jnp.abs(x)))


def _bdot(a, b):
    return jnp.dot(a.astype(BF16), b.astype(BF16), preferred_element_type=F32)


def _split3(x):
    hi = x.astype(BF16)
    r1 = x - hi.astype(F32)
    mid = r1.astype(BF16)
    lo = (r1 - mid.astype(F32)).astype(BF16)
    return hi, mid, lo


def _dot_exact_lhs(m, x):
    mb = m.astype(BF16)
    hi, mid, lo = _split3(x)
    return (jnp.dot(mb, hi, preferred_element_type=F32) + jnp.dot(mb, mid, preferred_element_type=F32)
            + jnp.dot(mb, lo, preferred_element_type=F32))


def _dot_exact_rhs(x, m):
    mb = m.astype(BF16)
    hi, mid, lo = _split3(x)
    return (jnp.dot(hi, mb, preferred_element_type=F32) + jnp.dot(mid, mb, preferred_element_type=F32)
            + jnp.dot(lo, mb, preferred_element_type=F32))


def _segsum(x, seg):
    blk = 256
    ri = lax.broadcasted_iota(jnp.int32, (blk, blk), 0) // seg
    ci = lax.broadcasted_iota(jnp.int32, (blk, blk), 1) // seg
    ones = (ri == ci).astype(F32)
    parts = [_dot_exact_rhs(x[:, c:c + blk], ones) for c in range(0, x.shape[1], blk)]
    return parts[0] if len(parts) == 1 else jnp.concatenate(parts, axis=1)


def _rms_rows(x, g):
    return x * lax.rsqrt(jnp.mean(x * x, axis=-1, keepdims=True) + EPS) * g


def _rmsnorm_kernel(x_ref, g_ref, o_ref):
    o_ref[...] = _rms_rows(x_ref[...], g_ref[...]).astype(o_ref.dtype)


def _rmsnorm(x, g, tm=512):
    m, d = x.shape
    return pl.pallas_call(
        _rmsnorm_kernel,
        grid=(m // tm,),
        in_specs=[pl.BlockSpec((tm, d), lambda i: (i, 0)), pl.BlockSpec((1, d), lambda i: (0, 0))],
        out_specs=pl.BlockSpec((tm, d), lambda i: (i, 0)),
        out_shape=jax.ShapeDtypeStruct((m, d), BF16),
        compiler_params=_cparams("parallel"),
        name="rmsnorm",
    )(x, g.reshape(1, d))


def _mm_kernel(x_ref, w_ref, o_ref):
    o_ref[...] = jnp.dot(x_ref[...], w_ref[...], preferred_element_type=F32)


def _mm(x, w, tm, tn):
    m, k = x.shape
    n = w.shape[1]
    return pl.pallas_call(
        _mm_kernel,
        grid=(n // tn, m // tm),
        in_specs=[pl.BlockSpec((tm, k), lambda j, i: (i, 0)), pl.BlockSpec((k, tn), lambda j, i: (0, j))],
        out_specs=pl.BlockSpec((tm, tn), lambda j, i: (i, j)),
        out_shape=jax.ShapeDtypeStruct((m, n), F32),
        compiler_params=_cparams("parallel", "parallel"),
        name="matmul",
    )(x, w)


def _merge_kernel(y0_ref, y1_ref, y2_ref, w_ref, g0_ref, g1_ref, g2_ref, o_ref):
    acc = _sigmoid(g0_ref[...]) * jnp.dot(y0_ref[...], w_ref[0], preferred_element_type=F32)
    acc += _sigmoid(g1_ref[...]) * jnp.dot(y1_ref[...], w_ref[1], preferred_element_type=F32)
    acc += _sigmoid(g2_ref[...]) * jnp.dot(y2_ref[...], w_ref[2], preferred_element_type=F32)
    o_ref[...] = acc.astype(o_ref.dtype)


def _merge(y_rw, y_ml, y_rt, w_br, z, tm=512, tn=512):
    m, kw = y_rw.shape
    d = w_br.shape[2]
    gb = Z_GATE // tn
    gspec = lambda c: pl.BlockSpec((tm, tn), lambda j, i: (i, gb + c * (d // tn) + j))
    yspec = pl.BlockSpec((tm, kw), lambda j, i: (i, 0))
    return pl.pallas_call(
        _merge_kernel,
        grid=(d // tn, m // tm),
        in_specs=[yspec, yspec, yspec, pl.BlockSpec((3, kw, tn), lambda j, i: (0, 0, j)),
                  gspec(0), gspec(1), gspec(2)],
        out_specs=pl.BlockSpec((tm, tn), lambda j, i: (i, j)),
        out_shape=jax.ShapeDtypeStruct((m, d), BF16),
        compiler_params=_cparams("parallel", "parallel"),
        name="branch_merge",
    )(y_rw, y_ml, y_rt, w_br, z, z, z)


def _residual_epilogue(acc, x_ref, gpost_ref, gnext_ref, xo_ref, ho_ref):
    xn = x_ref[...] + _rms_rows(acc, gpost_ref[...])
    xo_ref[...] = xn
    ho_ref[...] = _rms_rows(xn, gnext_ref[...]).astype(ho_ref.dtype)


def _proj_res_kernel(a_ref, w_ref, x_ref, gpost_ref, gnext_ref, xo_ref, ho_ref):
    acc = jnp.dot(a_ref[...], w_ref[...], preferred_element_type=F32)
    _residual_epilogue(acc, x_ref, gpost_ref, gnext_ref, xo_ref, ho_ref)


def _proj_res(a, w, x, g_post, g_next, tm=512):
    m, k = a.shape
    d = w.shape[1]
    row = lambda i: (i, 0)
    fixed = lambda i: (0, 0)
    return pl.pallas_call(
        _proj_res_kernel,
        grid=(m // tm,),
        in_specs=[pl.BlockSpec((tm, k), row), pl.BlockSpec((k, d), fixed), pl.BlockSpec((tm, d), row),
                  pl.BlockSpec((1, d), fixed), pl.BlockSpec((1, d), fixed)],
        out_specs=[pl.BlockSpec((tm, d), row), pl.BlockSpec((tm, d), row)],
        out_shape=[jax.ShapeDtypeStruct((m, d), F32), jax.ShapeDtypeStruct((m, d), BF16)],
        compiler_params=_cparams("parallel"),
        name="out_proj_residual",
    )(a, w, x, g_post.reshape(1, d), g_next.reshape(1, d))


def _mlp_kernel(h_ref, w1_ref, w2_ref, x_ref, gpost_ref, gnext_ref, xo_ref, ho_ref, acc_ref):
    k = pl.program_id(1)
    u = jnp.dot(h_ref[...], w1_ref[...], preferred_element_type=F32)
    u = jnp.square(jnp.maximum(u, 0.0)).astype(BF16)
    part = jnp.dot(u, w2_ref[...], preferred_element_type=F32)

    @pl.when(k == 0)
    def _():
        acc_ref[...] = part

    @pl.when(k > 0)
    def _():
        acc_ref[...] += part

    @pl.when(k == pl.num_programs(1) - 1)
    def _():
        _residual_epilogue(acc_ref[...], x_ref, gpost_ref, gnext_ref, xo_ref, ho_ref)


def _mlp(h, w1, w2, x, g_post, g_next, tm=512, tf=512):
    m, d = h.shape
    f = w1.shape[1]
    row = lambda i, k: (i, 0)
    fixed = lambda i, k: (0, 0)
    return pl.pallas_call(
        _mlp_kernel,
        grid=(m // tm, f // tf),
        in_specs=[pl.BlockSpec((tm, d), row), pl.BlockSpec((d, tf), lambda i, k: (0, k)),
                  pl.BlockSpec((tf, d), lambda i, k: (k, 0)), pl.BlockSpec((tm, d), row),
                  pl.BlockSpec((1, d), fixed), pl.BlockSpec((1, d), fixed)],
        out_specs=[pl.BlockSpec((tm, d), row), pl.BlockSpec((tm, d), row)],
        out_shape=[jax.ShapeDtypeStruct((m, d), F32), jax.ShapeDtypeStruct((m, d), BF16)],
        scratch_shapes=[pltpu.VMEM((tm, d), F32)],
        compiler_params=_cparams("parallel", "arbitrary"),
        name="relu2_mlp",
    )(h, w1, w2, x, g_post.reshape(1, d), g_next.reshape(1, d))


def _attn_kernel(bb, lq, h_ref, wq_ref, mk_ref, mv_ref, wo_ref, x_ref, gpost_ref, gnext_ref, xo_ref, ho_ref):
    q = jnp.dot(h_ref[...], wq_ref[...], preferred_element_type=F32)
    outs = []
    for hd in range(X_HEADS):
        sl = slice(hd * X_HD, (hd + 1) * X_HD)
        q3 = q[:, sl].reshape(bb, lq, X_HD).astype(BF16)
        k3 = mk_ref[:, :, sl].astype(BF16)
        v3 = mv_ref[:, :, sl].astype(BF16)
        s = jnp.einsum('bqd,bmd->bqm', q3, k3, preferred_element_type=F32) * (X_HD ** -0.5)
        s = s - jnp.max(s, axis=-1, keepdims=True)
        e = jnp.exp(s)
        p = (e / jnp.sum(e, axis=-1, keepdims=True)).astype(BF16)
        o3 = jnp.einsum('bqm,bmd->bqd', p, v3, preferred_element_type=F32)
        outs.append(o3.reshape(bb * lq, X_HD))
    o = jnp.concatenate(outs, axis=1).astype(BF16)
    acc = jnp.dot(o, wo_ref[...], preferred_element_type=F32)
    _residual_epilogue(acc, x_ref, gpost_ref, gnext_ref, xo_ref, ho_ref)


def _attn(h, x, mk, mv, wq, wo, g_post, g_next, row0, n_seq, seq_len, bb, lq):
    d = h.shape[1]
    rows = bb * lq
    nl = seq_len // lq
    n_mem = mk.shape[1]
    rb0 = row0 // rows
    rowmap = lambda b, l: (rb0 + b * nl + l, 0)
    outmap = lambda b, l: (b * nl + l, 0)
    fixed = lambda b, l: (0, 0)
    kvspec = pl.BlockSpec((bb, n_mem, X_W), lambda b, l: (b, 0, 0))
    m = n_seq * seq_len
    return pl.pallas_call(
        functools.partial(_attn_kernel, bb, lq),
        grid=(n_seq // bb, nl),
        in_specs=[pl.BlockSpec((rows, d), rowmap), pl.BlockSpec((d, X_W), fixed), kvspec, kvspec,
                  pl.BlockSpec((X_W, d), fixed), pl.BlockSpec((rows, d), rowmap),
                  pl.BlockSpec((1, d), fixed), pl.BlockSpec((1, d), fixed)],
        out_specs=[pl.BlockSpec((rows, d), outmap), pl.BlockSpec((rows, d), outmap)],
        out_shape=[jax.ShapeDtypeStruct((m, d), F32), jax.ShapeDtypeStruct((m, d), BF16)],
        compiler_params=_cparams("parallel", "parallel"),
        name="mem_cross_attention",
    )(h, wq, mk, mv, wo, x, g_post.reshape(1, d), g_next.reshape(1, d))


def _rwkv_prep_kernel(bb, lt, u_ref, s0_ref, mu_ref, w0_ref, wup_ref, a0_ref, aup_ref, gup_ref, kk_ref, ka_ref,
                      rk_ref, r_o, w_o, k_o, kk_o, b_o, v_o, g_o, bonus_o, carry_ref):
    l = pl.program_id(1)
    rows = bb * lt
    u = u_ref[...]
    if bb == 1:
        first = jnp.where(l == 0, s0_ref[0], carry_ref[...])
        first_rows = jnp.broadcast_to(first, (rows, RW_COLS))
    else:
        first_rows = jnp.broadcast_to(s0_ref[...], (bb, lt, RW_COLS)).reshape(rows, RW_COLS)
    rowid = lax.broadcasted_iota(jnp.int32, (rows, 1), 0) % lt
    prev = jnp.where(rowid == 0, first_rows, pltpu.roll(u, 1, 0))
    if bb == 1:
        carry_ref[...] = u[rows - 1:rows, :]
    z = u + (prev - u) * mu_ref[...]
    r = z[:, 0:MIX_W]
    k = z[:, MIX_W:2 * MIX_W]
    v = z[:, 2 * MIX_W:3 * MIX_W]
    o = 3 * MIX_W
    wd = z[:, o:o + RW_LORA[0]]
    ad = z[:, o + RW_LORA[0]:o + RW_LORA[0] + RW_LORA[1]]
    gd = z[:, o + RW_LORA[0] + RW_LORA[1]:RW_COLS]
    w_log = -_softplus(-(w0_ref[...] + _bdot(jnp.tanh(wd), wup_ref[...]))) - 0.5
    decay = jnp.exp(-jnp.exp(w_log))
    a = _sigmoid(a0_ref[...] + _bdot(ad, aup_ref[...]))
    g = _bdot(_sigmoid(gd), gup_ref[...])
    kk = k * kk_ref[...]
    kk = kk / jnp.maximum(jnp.sqrt(_segsum(kk * kk, RW_HD)), 1e-12)
    k = k * (1.0 + (a - 1.0) * ka_ref[...])
    r_o[...] = r
    w_o[...] = decay
    k_o[...] = k
    kk_o[...] = kk
    b_o[...] = kk * a
    v_o[...] = v
    g_o[...] = g
    bonus_o[...] = _segsum(r * k * rk_ref[...], RW_HD) * v


def _rwkv_prep(z, shift0, p, row0, n_seq, seq_len, bb, lt):
    rows = bb * lt
    nl = seq_len // lt
    assert bb == 1 or nl == 1
    rb0 = row0 // rows
    m = n_seq * seq_len
    fixed = lambda b, l: (0, 0)
    outmap = lambda b, l: (b * nl + l, 0)
    vec = lambda width: pl.BlockSpec((1, width), fixed)
    ospec = pl.BlockSpec((rows, MIX_W), outmap)
    oshape = jax.ShapeDtypeStruct((m, MIX_W), F32)
    return pl.pallas_call(
        functools.partial(_rwkv_prep_kernel, bb, lt),
        grid=(n_seq // bb, nl),
        in_specs=[pl.BlockSpec((rows, RW_COLS), lambda b, l: (rb0 + b * nl + l, 0)),
                  pl.BlockSpec((bb, 1, RW_COLS), lambda b, l: (b, 0, 0)),
                  vec(RW_COLS), vec(MIX_W), pl.BlockSpec((RW_LORA[0], MIX_W), fixed), vec(MIX_W),
                  pl.BlockSpec((RW_LORA[1], MIX_W), fixed), pl.BlockSpec((RW_LORA[2], MIX_W), fixed),
                  vec(MIX_W), vec(MIX_W), vec(MIX_W)],
        out_specs=[ospec] * 8,
        out_shape=[oshape] * 8,
        scratch_shapes=[pltpu.VMEM((1, RW_COLS), F32)],
        compiler_params=_cparams("parallel", "arbitrary"),
        name="rwkv_prep",
    )(z, shift0, p['mu'], p['w0'], p['w_up'], p['a0'], p['a_up'], p['g_up'], p['k_k'], p['k_a'], p['r_k'])


def _rwkv_scan_kernel(tc, r_ref, w_ref, k_ref, kk_ref, b_ref, v_ref, s0_ref, y_ref, s_ref):
    @pl.when(pl.program_id(1) == 0)
    def _():
        s_ref[...] = s0_ref[...]

    def step(t, carry):
        for vp in range(RW_VP):
            s = s_ref[0, vp]
            sa = jnp.sum(s * kk_ref[0, t], axis=0, keepdims=True)
            sn = s * w_ref[0, t] - sa * b_ref[0, t] + v_ref[0, t, vp:vp + 1, :] * k_ref[0, t]
            s_ref[0, vp] = sn
            y_ref[0, t, vp:vp + 1, :] = jnp.sum(sn * r_ref[0, t], axis=0, keepdims=True)
        return carry

    lax.fori_loop(0, tc, step, 0)


def _rwkv_scan(r, w, k, kk, b, v, s0, tc):
    g, l = r.shape[0], r.shape[1]
    kspec = pl.BlockSpec((1, tc, RW_HD, LANES), lambda i, c: (i, c, 0, 0))
    vspec = pl.BlockSpec((1, tc, RW_VP, LANES), lambda i, c: (i, c, 0, 0))
    sspec = pl.BlockSpec((1, RW_VP, RW_HD, LANES), lambda i, c: (i, 0, 0, 0))
    return pl.pallas_call(
        functools.partial(_rwkv_scan_kernel, tc),
        grid=(g, l // tc),
        in_specs=[kspec] * 5 + [vspec, sspec],
        out_specs=[vspec, sspec],
        out_shape=[jax.ShapeDtypeStruct((g, l, RW_VP, LANES), F32),
                   jax.ShapeDtypeStruct((g, RW_VP, RW_HD, LANES), F32)],
        compiler_params=_cparams("parallel", "arbitrary"),
        name="rwkv_scan",
    )(r, w, k, kk, b, v, s0)


def _rwkv_post_kernel(y_ref, g_ref, bonus_ref, gng_ref, gnb_ref, o_ref):
    y = y_ref[...]
    mean = _segsum(y, RW_HD) * (1.0 / RW_HD)
    yc = y - mean
    var = _segsum(yc * yc, RW_HD) * (1.0 / RW_HD)
    yn = yc * lax.rsqrt(var + RW_GN_EPS) * gng_ref[...] + gnb_ref[...]
    o_ref[...] = ((yn + bonus_ref[...]) * g_ref[...]).astype(o_ref.dtype)


def _rwkv_post(y, g, bonus, gn_g, gn_b, tm=256):
    m = y.shape[0]
    row = lambda i: (i, 0)
    fixed = lambda i: (0, 0)
    spec = pl.BlockSpec((tm, MIX_W), row)
    return pl.pallas_call(
        _rwkv_post_kernel,
        grid=(m // tm,),
        in_specs=[spec, spec, spec, pl.BlockSpec((1, MIX_W), fixed), pl.BlockSpec((1, MIX_W), fixed)],
        out_specs=spec,
        out_shape=jax.ShapeDtypeStruct((m, MIX_W), BF16),
        compiler_params=_cparams("parallel"),
        name="rwkv_post",
    )(y, g, bonus, gn_g, gn_b)


def _to_scan_keys(a, n_seq, seq_len):
    nb = RW_CHAINS // RW_HEADS
    g = n_seq // nb
    a = a.reshape(g, nb, seq_len, RW_HEADS, RW_HD).transpose(0, 2, 4, 1, 3).reshape(g, seq_len, RW_HD, RW_CHAINS)
    return jnp.concatenate([a, a], axis=-1)


def _to_scan_vals(a, n_seq, seq_len):
    nb = RW_CHAINS // RW_HEADS
    g = n_seq // nb
    a = a.reshape(g, nb, seq_len, RW_HEADS, 2, RW_VP).transpose(0, 2, 5, 4, 1, 3)
    return a.reshape(g, seq_len, RW_VP, LANES)


def _from_scan_vals(y, n_seq, seq_len):
    nb = RW_CHAINS // RW_HEADS
    g = n_seq // nb
    y = y.reshape(g, seq_len, RW_VP, 2, nb, RW_HEADS).transpose(0, 4, 1, 5, 3, 2)
    return y.reshape(n_seq * seq_len, MIX_W)


def _state_to_scan(s, n_seq):
    nb = RW_CHAINS // RW_HEADS
    g = n_seq // nb
    s = s.reshape(g, nb, RW_HEADS, 2, RW_VP, RW_HD).transpose(0, 4, 5, 3, 1, 2)
    return s.reshape(g, RW_VP, RW_HD, LANES)


def _state_from_scan(s, n_seq):
    nb = RW_CHAINS // RW_HEADS
    g = n_seq // nb
    s = s.reshape(g, RW_VP, RW_HD, 2, nb, RW_HEADS).transpose(0, 4, 5, 3, 1, 2)
    return s.reshape(n_seq, RW_HEADS, RW_HD, RW_HD)


def _rwkv_branch(z, shift0, s0, p, row0, n_seq, seq_len, bb, lt, tc):
    r, w, k, kk, b, v, g, bonus = _rwkv_prep(z, shift0, p, row0, n_seq, seq_len, bb, lt)
    keys = [_to_scan_keys(a, n_seq, seq_len) for a in (r, w, k, kk, b)]
    y, s_new = _rwkv_scan(*keys, _to_scan_vals(v, n_seq, seq_len), _state_to_scan(s0, n_seq), tc)
    y = _from_scan_vals(y, n_seq, seq_len)
    return _rwkv_post(y, g, bonus, p['gn_g'], p['gn_b']), _state_from_scan(s_new, n_seq)


def _seq_masks(rows, lc):
    ri = lax.broadcasted_iota(jnp.int32, (rows, rows), 0)
    ci = lax.broadcasted_iota(jnp.int32, (rows, rows), 1)
    same = (ri // lc) == (ci // lc)
    return ri, ci, same


def _rows_from_seq(x, bb, lc):
    return jnp.broadcast_to(x, (bb, lc, x.shape[-1])).reshape(bb * lc, x.shape[-1])


def _last_of_seq(x, bb, lc):
    x3 = x.reshape(bb, lc, x.shape[-1])
    pick = (lax.broadcasted_iota(jnp.int32, (1, lc, 1), 1) == lc - 1).astype(F32)
    return jnp.sum(x3 * pick, axis=1, keepdims=True)


def _mlstm_kernel(bb, lc, q_ref, k_ref, v_ref, o_ref, gt_ref, gb_ref, ng_ref, c0_ref, n0_ref, m0_ref,
                  y_ref, c_ref, n_ref, m_ref):
    hd = pl.program_id(1)
    rows = bb * lc

    @pl.when(pl.program_id(2) == 0)
    def _():
        c_ref[...] = c0_ref[...]
        n_ref[...] = n0_ref[...]
        m_ref[...] = m0_ref[...]

    c0 = c_ref[:, 0]
    n0 = n_ref[:, 0]
    m0 = m_ref[:, 0]
    q = q_ref[...]
    k = k_ref[...] * (ML_HD ** -0.5)
    v = v_ref[...]
    gt = ML_GATE_CAP * jnp.tanh((gt_ref[...] + gb_ref[...]) * (1.0 / ML_GATE_CAP))
    lane = lax.broadcasted_iota(jnp.int32, (rows, LANES), 1)
    ri, ci, same = _seq_masks(rows, lc)
    causal = same & (ci <= ri)
    lf_all = -_softplus(-gt)
    cum = _dot_exact_lhs(causal.astype(F32), lf_all)
    bcol = jnp.sum(jnp.where(lane == ML_HEADS + hd, cum, 0.0), axis=1, keepdims=True)
    igcol = jnp.sum(jnp.where(lane == hd, gt, 0.0), axis=1, keepdims=True)
    eye = (ri == ci).astype(F32)
    rowb = _dot_exact_lhs(jnp.ones((rows, rows), F32), (igcol - bcol) * eye)
    dlog = jnp.where(causal, bcol + rowb, -jnp.inf)
    m_inter = bcol + _rows_from_seq(m0, bb, lc)
    m_t = jnp.maximum(m_inter, jnp.max(dlog, axis=-1, keepdims=True))
    qk = lax.dot_general(q.astype(BF16), k.astype(BF16), (((1,), (1,)), ((), ())), preferred_element_type=F32)
    wts = jnp.exp(dlog - m_t) * qk
    s_inter = jnp.exp(m_inter - m_t)
    q3 = q.reshape(bb, lc, ML_HD).astype(BF16)
    qc = jnp.einsum('btd,bde->bte', q3, c0.astype(BF16), preferred_element_type=F32).reshape(rows, ML_HD)
    num = _bdot(wts, v) + s_inter * qc
    qn = jnp.sum(q * _rows_from_seq(n0, bb, lc), axis=-1, keepdims=True)
    den = jnp.sum(wts, axis=-1, keepdims=True) + s_inter * qn
    hh = num / jnp.maximum(jnp.abs(den), jnp.exp(-m_t))
    hh = hh * lax.rsqrt(jnp.mean(hh * hh, axis=-1, keepdims=True) + EPS) * ng_ref[...]
    y_ref[...] = (_sigmoid(o_ref[...]) * hh).astype(y_ref.dtype)

    stat = jnp.where(lane == 0, bcol, jnp.where(lane == 1, m_t, 0.0))
    last = _last_of_seq(stat, bb, lc)
    last_rows = _rows_from_seq(last, bb, lc)
    b_last, m_new = last_rows[:, 0:1], last_rows[:, 1:2]
    w_end = jnp.exp(b_last - bcol + igcol - m_new)
    f_end = jnp.exp(last[:, :, 0:1] + m0 - last[:, :, 1:2])
    kw = k * w_end
    kw3 = kw.reshape(bb, lc, ML_HD)
    upd = jnp.einsum('bsd,bse->bde', kw3.astype(BF16), v.reshape(bb, lc, ML_HD).astype(BF16),
                     preferred_element_type=F32)
    c_ref[:, 0] = f_end * c0 + upd
    n_ref[:, 0] = f_end * n0 + jnp.sum(kw3, axis=1, keepdims=True)
    m_ref[:, 0] = last[:, :, 1:2]


def _mlstm_branch(z, c0, n0, m0, gate_bias, norm_g, row0, n_seq, seq_len, bb, lc):
    rows = bb * lc
    nc = seq_len // lc
    rb0 = row0 // rows
    m = n_seq * seq_len
    col0 = Z_ML // ML_HD
    zspec = lambda part: pl.BlockSpec((rows, ML_HD),
                                      lambda b, h, c: (rb0 + b * nc + c, col0 + part * ML_HEADS + h))
    smap = lambda b, h, c: (b, h, 0, 0)
    return pl.pallas_call(
        functools.partial(_mlstm_kernel, bb, lc),
        grid=(n_seq // bb, ML_HEADS, nc),
        in_specs=[zspec(0), zspec(1), zspec(2), zspec(3),
                  pl.BlockSpec((rows, LANES), lambda b, h, c: (rb0 + b * nc + c, Z_MLG // LANES)),
                  pl.BlockSpec((1, LANES), lambda b, h, c: (0, 0)),
                  pl.BlockSpec((1, ML_HD), lambda b, h, c: (0, h)),
                  pl.BlockSpec((bb, 1, ML_HD, ML_HD), smap), pl.BlockSpec((bb, 1, 1, ML_HD), smap),
                  pl.BlockSpec((bb, 1, 1, 1), smap)],
        out_specs=[pl.BlockSpec((rows, ML_HD), lambda b, h, c: (b * nc + c, h)),
                   pl.BlockSpec((bb, 1, ML_HD, ML_HD), smap), pl.BlockSpec((bb, 1, 1, ML_HD), smap),
                   pl.BlockSpec((bb, 1, 1, 1), smap)],
        out_shape=[jax.ShapeDtypeStruct((m, MIX_W), BF16),
                   jax.ShapeDtypeStruct((n_seq, ML_HEADS, ML_HD, ML_HD), F32),
                   jax.ShapeDtypeStruct((n_seq, ML_HEADS, 1, ML_HD), F32),
                   jax.ShapeDtypeStruct((n_seq, ML_HEADS, 1, 1), F32)],
        compiler_params=_cparams("parallel", "parallel", "arbitrary"),
        name="mlstm_chunk",
    )(z, z, z, z, z, gate_bias, norm_g, c0, n0, m0)


def _rope_rows(x, cos, sin):
    half = RT_HD // 2
    x1, x2 = x[:, :half], x[:, half:]
    return jnp.concatenate([x1 * cos - x2 * sin, x1 * sin + x2 * cos], axis=1)


def _ret_kernel(bb, lc, q_ref, k_ref, v_ref, g_ref, cs_ref, dec_ref, cd_ref, s0_ref, y_ref, s_ref):
    rows = bb * lc

    @pl.when(pl.program_id(2) == 0)
    def _():
        s_ref[...] = s0_ref[...]

    s0 = s_ref[:, 0]
    cos, sin = cs_ref[:, :RT_HD // 2], cs_ref[:, RT_HD // 2:]
    q = _rope_rows(q_ref[...], cos, sin)
    k = _rope_rows(k_ref[...], cos, sin) * (RT_HD ** -0.5)
    v = v_ref[...]
    cd = cd_ref[0]
    qk = lax.dot_general(q.astype(BF16), k.astype(BF16), (((1,), (1,)), ((), ())), preferred_element_type=F32)
    inner = _bdot(qk * dec_ref[0], v)
    q3 = q.reshape(bb, lc, RT_HD).astype(BF16)
    cross = jnp.einsum('btd,bde->bte', q3, s0.astype(BF16), preferred_element_type=F32).reshape(rows, RT_HD)
    y = inner + cross * cd[:, 0:1]
    y = y * lax.rsqrt(jnp.mean(y * y, axis=-1, keepdims=True) + EPS)
    gt = g_ref[...]
    y_ref[...] = (gt * _sigmoid(gt) * y).astype(y_ref.dtype)
    kd = (k * cd[:, 1:2]).reshape(bb, lc, RT_HD).astype(BF16)
    upd = jnp.einsum('bsd,bse->bde', kd, v.reshape(bb, lc, RT_HD).astype(BF16), preferred_element_type=F32)
    s_ref[:, 0] = cd[0:1, 2:3] * s0 + upd


def _ret_branch(z, s0, row0, n_seq, seq_len, bb, lc, pos0):
    rows = bb * lc
    nc = seq_len // lc
    rb0 = row0 // rows
    m = n_seq * seq_len
    half = RT_HD // 2
    inv = ROPE_BASE ** (-jnp.arange(half, dtype=F32) / half)
    pos = jnp.broadcast_to((jnp.arange(nc * lc, dtype=F32) + float(pos0)).reshape(nc, 1, lc), (nc, bb, lc))
    ang = pos.reshape(nc * rows)[:, None] * inv[None, :]
    cs = jnp.concatenate([jnp.cos(ang), jnp.sin(ang)], axis=1)
    log_g = jnp.log(1.0 - jnp.exp(jnp.linspace(math.log(1.0 / 32), math.log(1.0 / 512), RT_HEADS)))
    idx = jnp.arange(rows)
    t = (idx % lc).astype(F32)
    diff = t[:, None] - t[None, :]
    same = (idx[:, None] // lc) == (idx[None, :] // lc)
    dec = jnp.where(same & (diff >= 0), jnp.exp(log_g[:, None, None] * jnp.maximum(diff, 0.0)), 0.0)
    cd = jnp.zeros((RT_HEADS, rows, LANES), F32)
    cd = cd.at[:, :, 0].set(jnp.exp(log_g[:, None] * (t[None, :] + 1.0)))
    cd = cd.at[:, :, 1].set(jnp.exp(log_g[:, None] * (lc - 1.0 - t)[None, :]))
    cd = cd.at[:, :, 2].set(jnp.broadcast_to(jnp.exp(log_g * lc)[:, None], (RT_HEADS, rows)))
    col0 = Z_RT // RT_HD
    zspec = lambda part: pl.BlockSpec((rows, RT_HD),
                                      lambda b, h, c: (rb0 + b * nc + c, col0 + part * RT_HEADS + h))
    smap = lambda b, h, c: (b, h, 0, 0)
    return pl.pallas_call(
        functools.partial(_ret_kernel, bb, lc),
        grid=(n_seq // bb, RT_HEADS, nc),
        in_specs=[zspec(0), zspec(1), zspec(2), zspec(3),
                  pl.BlockSpec((rows, RT_HD), lambda b, h, c: (c, 0)),
                  pl.BlockSpec((1, rows, rows), lambda b, h, c: (h, 0, 0)),
                  pl.BlockSpec((1, rows, LANES), lambda b, h, c: (h, 0, 0)),
                  pl.BlockSpec((bb, 1, RT_HD, RT_HD), smap)],
        out_specs=[pl.BlockSpec((rows, RT_HD), lambda b, h, c: (b * nc + c, h)),
                   pl.BlockSpec((bb, 1, RT_HD, RT_HD), smap)],
        out_shape=[jax.ShapeDtypeStruct((m, MIX_W), BF16),
                   jax.ShapeDtypeStruct((n_seq, RT_HEADS, RT_HD, RT_HD), F32)],
        compiler_params=_cparams("parallel", "parallel", "arbitrary"),
        name="retention_chunk",
    )(z, z, z, z, cs, dec, cd, s0)


def _pack_w_in(w_in):
    d = w_in.shape[0]
    ml0 = RW_COLS
    mlg0 = ml0 + 4 * MIX_W
    rt0 = mlg0 + 2 * ML_HEADS
    pad = jnp.zeros((d, Z_ML - Z_MLG - 2 * ML_HEADS), w_in.dtype)
    packed = jnp.concatenate([w_in[:, :RW_COLS], w_in[:, mlg0:rt0], pad, w_in[:, ml0:mlg0], w_in[:, rt0:]], axis=1)
    return packed.astype(BF16)


def _group_cfg(n_seq, seq_len):
    if seq_len % CHUNK == 0:
        return dict(rw_bb=1, rw_lt=256, rw_tc=32, ch_bb=1, ch_lc=CHUNK, at_bb=1, at_lq=512)
    return dict(rw_bb=32, rw_lt=seq_len, rw_tc=seq_len, ch_bb=16, ch_lc=seq_len, at_bb=8, at_lq=seq_len)


def kernel(x_prompt, x_sample, mem_prompt, state_rwkv_shift, state_rwkv, state_mlstm_c, state_mlstm_n,
           state_mlstm_m, state_ret, cache_mem_k, cache_mem_v, g_pre_mix, g_post_mix, g_pre_x, g_post_x,
           g_pre_ff, g_post_ff, g_mem, w_in, rw_mu, rw_w0, rw_w_up, rw_a0, rw_a_up, rw_g_up, rw_k_k, rw_k_a,
           rw_r_k, rw_gn_g, rw_gn_b, ml_i_b, ml_f_b, ml_norm_g, w_br, w_out, x_wq, x_wkv, x_wo, ff_w1, ff_w2):
    bp, lp, d = x_prompt.shape
    bs, ls, _ = x_sample.shape
    depth = w_in.shape[0]
    n_mem = mem_prompt.shape[1]
    tp, ts = bp * lp, bs * ls
    groups = [(0, bp, lp, 0), (tp, bs, ls, PAST_LEN)]

    x = jnp.concatenate([x_prompt.reshape(tp, d), x_sample.reshape(ts, d)], axis=0)
    mem = mem_prompt.reshape(bp * n_mem, d)
    h = _rmsnorm(x, g_pre_mix[0])

    zeros = lambda *s: jnp.zeros(s, F32)
    new_states = [[], []]
    mem_k, mem_v = [], []
    for l in range(depth):
        kv = _mm(_rmsnorm(mem, g_mem[l], tm=256), x_wkv[l].astype(BF16), tm=256, tn=512)
        mk_p, mv_p = kv[:, :X_W].reshape(bp, n_mem, X_W), kv[:, X_W:].reshape(bp, n_mem, X_W)
        mem_k.append(mk_p.reshape(bp, n_mem, X_HEADS, X_HD))
        mem_v.append(mv_p.reshape(bp, n_mem, X_HEADS, X_HD))

        z = _mm(h, _pack_w_in(w_in[l]), tm=1024, tn=512)

        rw_p = dict(mu=rw_mu[l].reshape(1, -1), w0=rw_w0[l].reshape(1, -1), w_up=rw_w_up[l].astype(BF16),
                    a0=rw_a0[l].reshape(1, -1), a_up=rw_a_up[l].astype(BF16), g_up=rw_g_up[l].astype(BF16),
                    k_k=rw_k_k[l].reshape(1, -1), k_a=rw_k_a[l].reshape(1, -1), r_k=rw_r_k[l].reshape(1, -1),
                    gn_g=rw_gn_g[l].reshape(1, -1), gn_b=rw_gn_b[l].reshape(1, -1))
        gate_bias = jnp.concatenate([ml_i_b[l], ml_f_b[l], zeros(LANES - 2 * ML_HEADS)]).reshape(1, LANES)
        norm_g = ml_norm_g[l].reshape(1, MIX_W)

        ys = [[], [], []]
        for gi, (row0, n_seq, seq_len, pos0) in enumerate(groups):
            cfg = _group_cfg(n_seq, seq_len)
            if gi == 0:
                init = (zeros(n_seq, 1, RW_COLS), zeros(n_seq, RW_HEADS, RW_HD, RW_HD),
                        zeros(n_seq, ML_HEADS, ML_HD, ML_HD), zeros(n_seq, ML_HEADS, 1, ML_HD),
                        zeros(n_seq, ML_HEADS, 1, 1), zeros(n_seq, RT_HEADS, RT_HD, RT_HD))
            else:
                init = (state_rwkv_shift[l].reshape(n_seq, 1, RW_COLS), state_rwkv[l], state_mlstm_c[l],
                        state_mlstm_n[l].reshape(n_seq, ML_HEADS, 1, ML_HD),
                        state_mlstm_m[l].reshape(n_seq, ML_HEADS, 1, 1), state_ret[l])
            y_rw, s_rw = _rwkv_branch(z, init[0], init[1], rw_p, row0, n_seq, seq_len,
                                      cfg['rw_bb'], cfg['rw_lt'], cfg['rw_tc'])
            y_ml, c_new, n_new, m_new = _mlstm_branch(z, init[2], init[3], init[4], gate_bias, norm_g,
                                                      row0, n_seq, seq_len, cfg['ch_bb'], cfg['ch_lc'])
            y_rt, s_rt = _ret_branch(z, init[5], row0, n_seq, seq_len, cfg['ch_bb'], cfg['ch_lc'], pos0)
            shift = z[row0:row0 + n_seq * seq_len].reshape(n_seq, seq_len, Z_COLS)[:, -1, :RW_COLS]
            new_states[gi].append((shift, s_rw, c_new, n_new.reshape(n_seq, ML_HEADS, ML_HD),
                                   m_new.reshape(n_seq, ML_HEADS), s_rt))
            for acc, y in zip(ys, (y_rw, y_ml, y_rt)):
                acc.append(y)
        y_rw, y_ml, y_rt = (jnp.concatenate(parts, axis=0) for parts in ys)

        merged = _merge(y_rw, y_ml, y_rt, w_br[l].astype(BF16), z)
        x, h = _proj_res(merged, w_out[l].astype(BF16), x, g_post_mix[l], g_pre_x[l])

        wq, wo = x_wq[l].astype(BF16), x_wo[l].astype(BF16)
        xs, hs = [], []
        for gi, (row0, n_seq, seq_len, pos0) in enumerate(groups):
            cfg = _group_cfg(n_seq, seq_len)
            if gi == 0:
                mk, mv = mk_p, mv_p
            else:
                mk, mv = cache_mem_k[l].reshape(n_seq, n_mem, X_W), cache_mem_v[l].reshape(n_seq, n_mem, X_W)
            xg, hg = _attn(h, x, mk, mv, wq, wo, g_post_x[l], g_pre_ff[l], row0, n_seq, seq_len,
                           cfg['at_bb'], cfg['at_lq'])
            xs.append(xg)
            hs.append(hg)
        x, h = jnp.concatenate(xs, axis=0), jnp.concatenate(hs, axis=0)

        g_next = g_pre_mix[(l + 1) % depth]
        x, h = _mlp(h, ff_w1[l].astype(BF16), ff_w2[l].astype(BF16), x, g_post_ff[l], g_next)

    stack = lambda gi, i: jnp.stack([s[i] for s in new_states[gi]])
    return (x[:tp].reshape(bp, lp, d), x[tp:].reshape(bs, ls, d),
            stack(0, 0), stack(0, 1), stack(0, 2), stack(0, 3), stack(0, 4), stack(0, 5),
            jnp.stack(mem_k), jnp.stack(mem_v),
            stack(1, 0), stack(1, 1), stack(1, 2), stack(1, 3), stack(1, 4), stack(1, 5))
```

```python
import functools
import math

import jax
import jax.numpy as jnp
from jax import lax
from jax.experimental import pallas as pl
from jax.experimental.pallas import tpu as pltpu

F32 = jnp.float32
BF16 = jnp.bfloat16

D_MODEL = 2048
MIX_W = D_MODEL // 2
RW_HD = 64
RW_HEADS = MIX_W // RW_HD
RW_LORA = (64, 64, 128)
RW_COLS = 3 * MIX_W + sum(RW_LORA)
RW_GN_EPS = 64e-5
ML_HEADS = 8
ML_HD = MIX_W // ML_HEADS
ML_GATE_CAP = 15.0
RT_HEADS = 4
RT_HD = MIX_W // RT_HEADS
ROPE_BASE = 10000.0
X_HEADS = 4
X_HD = 128
X_W = X_HEADS * X_HD
D_FF = 4 * D_MODEL
CHUNK = 64
EPS = 1e-6
PAST_LEN = 16384

Z_ML = 0
Z_RT = Z_ML + 4 * MIX_W
Z_GATE = Z_RT + 4 * MIX_W
Z_RW = Z_GATE + 3 * D_MODEL
Z_MLG = Z_RW + RW_COLS
Z_RWB = RW_COLS + 256
Z_COLS = Z_MLG + 256

LANES = 128
RW_VP = RW_HD // 2
RW_CHAINS = LANES // 2


def _cparams(*sem):
    return pltpu.CompilerParams(dimension_semantics=sem, vmem_limit_bytes=56 * 1024 * 1024)


def _carry_through(kernel_fn, n_in, prev):
    if not prev:
        return kernel_fn, [], {}

    def wrapped(*refs):
        return kernel_fn(*refs[:n_in], *refs[n_in + len(prev):])

    return wrapped, [pl.BlockSpec(memory_space=pl.ANY)] * len(prev), {n_in + i: i for i in range(len(prev))}


def _sigmoid(x):
    return 1.0 / (1.0 + jnp.exp(-x))


def _softplus(x):
    return jnp.maximum(x, 0.0) + jnp.log(1.0 + jnp.exp(-jnp.abs(x)))


def _bdot(a, b):
    return jnp.dot(a.astype(BF16), b.astype(BF16), preferred_element_type=F32)


def _split3(x):
    hi = x.astype(BF16)
    r1 = x - hi.astype(F32)
    mid = r1.astype(BF16)
    lo = (r1 - mid.astype(F32)).astype(BF16)
    return hi, mid, lo


def _dot_exact_lhs(m, x):
    mb = m.astype(BF16)
    hi, mid, lo = _split3(x)
    return (jnp.dot(mb, hi, preferred_element_type=F32) + jnp.dot(mb, mid, preferred_element_type=F32)
            + jnp.dot(mb, lo, preferred_element_type=F32))


def _dot_exact_rhs(x, m):
    mb = m.astype(BF16)
    hi, mid, lo = _split3(x)
    return (jnp.dot(hi, mb, preferred_element_type=F32) + jnp.dot(mid, mb, preferred_element_type=F32)
            + jnp.dot(lo, mb, preferred_element_type=F32))


def _segsum(x, seg):
    blk = 256
    ri = lax.broadcasted_iota(jnp.int32, (blk, blk), 0) // seg
    ci = lax.broadcasted_iota(jnp.int32, (blk, blk), 1) // seg
    ones = (ri == ci).astype(F32)
    parts = [_dot_exact_rhs(x[:, c:c + blk], ones) for c in range(0, x.shape[1], blk)]
    return parts[0] if len(parts) == 1 else jnp.concatenate(parts, axis=1)


def _rms_rows(x, g):
    return x * lax.rsqrt(jnp.mean(x * x, axis=-1, keepdims=True) + EPS) * g


def _rmsnorm_kernel(x_ref, g_ref, o_ref):
    o_ref[...] = _rms_rows(x_ref[...], g_ref[...]).astype(o_ref.dtype)


def _rmsnorm(x, g, tm=512):
    m, d = x.shape
    return pl.pallas_call(
        _rmsnorm_kernel,
        grid=(m // tm,),
        in_specs=[pl.BlockSpec((tm, d), lambda i: (i, 0)), pl.BlockSpec((1, d), lambda i: (0, 0))],
        out_specs=pl.BlockSpec((tm, d), lambda i: (i, 0)),
        out_shape=jax.ShapeDtypeStruct((m, d), BF16),
        compiler_params=_cparams("parallel"),
        name="rmsnorm",
    )(x, g.reshape(1, d))


def _mm_kernel(x_ref, w_ref, o_ref):
    o_ref[...] = jnp.dot(x_ref[...], w_ref[...], preferred_element_type=F32)


def _mm(x, w, tm, tn):
    m, k = x.shape
    n = w.shape[1]
    return pl.pallas_call(
        _mm_kernel,
        grid=(n // tn, m // tm),
        in_specs=[pl.BlockSpec((tm, k), lambda j, i: (i, 0)), pl.BlockSpec((k, tn), lambda j, i: (0, j))],
        out_specs=pl.BlockSpec((tm, tn), lambda j, i: (i, j)),
        out_shape=jax.ShapeDtypeStruct((m, n), F32),
        compiler_params=_cparams("parallel", "parallel"),
        name="matmul",
    )(x, w)


def _merge_kernel(y0_ref, y1_ref, y2_ref, w_ref, g0_ref, g1_ref, g2_ref, o_ref):
    acc = _sigmoid(g0_ref[...]) * jnp.dot(y0_ref[...], w_ref[0], preferred_element_type=F32)
    acc += _sigmoid(g1_ref[...]) * jnp.dot(y1_ref[...], w_ref[1], preferred_element_type=F32)
    acc += _sigmoid(g2_ref[...]) * jnp.dot(y2_ref[...], w_ref[2], preferred_element_type=F32)
    o_ref[...] = acc.astype(o_ref.dtype)


def _merge(y_rw, y_ml, y_rt, w_br, z, tm=512, tn=512):
    m, kw = y_rw.shape
    d = w_br.shape[2]
    gb = Z_GATE // tn
    gspec = lambda c: pl.BlockSpec((tm, tn), lambda j, i: (i, gb + c * (d // tn) + j))
    yspec = pl.BlockSpec((tm, kw), lambda j, i: (i, 0))
    return pl.pallas_call(
        _merge_kernel,
        grid=(d // tn, m // tm),
        in_specs=[yspec, yspec, yspec, pl.BlockSpec((3, kw, tn), lambda j, i: (0, 0, j)),
                  gspec(0), gspec(1), gspec(2)],
        out_specs=pl.BlockSpec((tm, tn), lambda j, i: (i, j)),
        out_shape=jax.ShapeDtypeStruct((m, d), BF16),
        compiler_params=_cparams("parallel", "parallel"),
        name="branch_merge",
    )(y_rw, y_ml, y_rt, w_br, z, z, z)


def _residual_epilogue(acc, x_ref, gpost_ref, gnext_ref, xo_ref, ho_ref):
    xn = x_ref[...] + _rms_rows(acc, gpost_ref[...])
    xo_ref[...] = xn
    ho_ref[...] = _rms_rows(xn, gnext_ref[...]).astype(ho_ref.dtype)


def _proj_res_kernel(a_ref, w_ref, x_ref, gpost_ref, gnext_ref, xo_ref, ho_ref):
    acc = jnp.dot(a_ref[...], w_ref[...], preferred_element_type=F32)
    _residual_epilogue(acc, x_ref, gpost_ref, gnext_ref, xo_ref, ho_ref)


def _proj_res(a, w, x, g_post, g_next, tm=512):
    m, k = a.shape
    d = w.shape[1]
    row = lambda i: (i, 0)
    fixed = lambda i: (0, 0)
    return pl.pallas_call(
        _proj_res_kernel,
        grid=(m // tm,),
        in_specs=[pl.BlockSpec((tm, k), row), pl.BlockSpec((k, d), fixed), pl.BlockSpec((tm, d), row),
                  pl.BlockSpec((1, d), fixed), pl.BlockSpec((1, d), fixed)],
        out_specs=[pl.BlockSpec((tm, d), row), pl.BlockSpec((tm, d), row)],
        out_shape=[jax.ShapeDtypeStruct((m, d), F32), jax.ShapeDtypeStruct((m, d), BF16)],
        compiler_params=_cparams("parallel"),
        name="out_proj_residual",
    )(a, w, x, g_post.reshape(1, d), g_next.reshape(1, d))


def _mlp_kernel(h_ref, w1_ref, w2_ref, x_ref, gpost_ref, gnext_ref, xo_ref, ho_ref, acc_ref):
    k = pl.program_id(1)
    u = jnp.dot(h_ref[...], w1_ref[...], preferred_element_type=F32)
    u = jnp.square(jnp.maximum(u, 0.0)).astype(BF16)
    part = jnp.dot(u, w2_ref[...], preferred_element_type=F32)

    @pl.when(k == 0)
    def _():
        acc_ref[...] = part

    @pl.when(k > 0)
    def _():
        acc_ref[...] += part

    @pl.when(k == pl.num_programs(1) - 1)
    def _():
        _residual_epilogue(acc_ref[...], x_ref, gpost_ref, gnext_ref, xo_ref, ho_ref)


def _mlp(h, w1, w2, x, g_post, g_next, tm=512, tf=512):
    m, d = h.shape
    f = w1.shape[1]
    row = lambda i, k: (i, 0)
    fixed = lambda i, k: (0, 0)
    return pl.pallas_call(
        _mlp_kernel,
        grid=(m // tm, f // tf),
        in_specs=[pl.BlockSpec((tm, d), row), pl.BlockSpec((d, tf), lambda i, k: (0, k)),
                  pl.BlockSpec((tf, d), lambda i, k: (k, 0)), pl.BlockSpec((tm, d), row),
                  pl.BlockSpec((1, d), fixed), pl.BlockSpec((1, d), fixed)],
        out_specs=[pl.BlockSpec((tm, d), row), pl.BlockSpec((tm, d), row)],
        out_shape=[jax.ShapeDtypeStruct((m, d), F32), jax.ShapeDtypeStruct((m, d), BF16)],
        scratch_shapes=[pltpu.VMEM((tm, d), F32)],
        compiler_params=_cparams("parallel", "arbitrary"),
        name="relu2_mlp",
    )(h, w1, w2, x, g_post.reshape(1, d), g_next.reshape(1, d))


def _attn_kernel(bb, lq, h_ref, wq_ref, mk_ref, mv_ref, wo_ref, x_ref, gpost_ref, gnext_ref, xo_ref, ho_ref):
    q = jnp.dot(h_ref[...], wq_ref[...], preferred_element_type=F32)
    outs = []
    for hd in range(X_HEADS):
        sl = slice(hd * X_HD, (hd + 1) * X_HD)
        q3 = q[:, sl].reshape(bb, lq, X_HD).astype(BF16)
        k3 = mk_ref[:, :, sl].astype(BF16)
        v3 = mv_ref[:, :, sl].astype(BF16)
        s = jnp.einsum('bqd,bmd->bqm', q3, k3, preferred_element_type=F32) * (X_HD ** -0.5)
        s = s - jnp.max(s, axis=-1, keepdims=True)
        e = jnp.exp(s)
        p = (e / jnp.sum(e, axis=-1, keepdims=True)).astype(BF16)
        o3 = jnp.einsum('bqm,bmd->bqd', p, v3, preferred_element_type=F32)
        outs.append(o3.reshape(bb * lq, X_HD))
    o = jnp.concatenate(outs, axis=1).astype(BF16)
    acc = jnp.dot(o, wo_ref[...], preferred_element_type=F32)
    _residual_epilogue(acc, x_ref, gpost_ref, gnext_ref, xo_ref, ho_ref)


def _attn(h, x, mk, mv, wq, wo, g_post, g_next, row0, n_seq, seq_len, bb, lq):
    t, d = h.shape
    rows = bb * lq
    nl = seq_len // lq
    n_mem = mk.shape[1]
    rb0 = row0 // rows
    rowmap = lambda b, l: (rb0 + b * nl + l, 0)
    fixed = lambda b, l: (0, 0)
    kvspec = pl.BlockSpec((bb, n_mem, X_W), lambda b, l: (b, 0, 0))
    return pl.pallas_call(
        functools.partial(_attn_kernel, bb, lq),
        grid=(n_seq // bb, nl),
        in_specs=[pl.BlockSpec((rows, d), rowmap), pl.BlockSpec((d, X_W), fixed), kvspec, kvspec,
                  pl.BlockSpec((X_W, d), fixed), pl.BlockSpec((rows, d), rowmap),
                  pl.BlockSpec((1, d), fixed), pl.BlockSpec((1, d), fixed)],
        out_specs=[pl.BlockSpec((rows, d), rowmap), pl.BlockSpec((rows, d), rowmap)],
        out_shape=[jax.ShapeDtypeStruct((t, d), F32), jax.ShapeDtypeStruct((t, d), BF16)],
        input_output_aliases={5: 0, 0: 1},
        compiler_params=_cparams("parallel", "parallel"),
        name="mem_cross_attention",
    )(h, wq, mk, mv, wo, x, g_post.reshape(1, d), g_next.reshape(1, d))


def _rwkv_prep_kernel(bb, lt, u_ref, s0_ref, mu_ref, w0_ref, wup_ref, a0_ref, aup_ref, gup_ref, kk_ref, ka_ref,
                      rk_ref, r_o, w_o, k_o, kk_o, b_o, v_o, g_o, bonus_o, carry_ref):
    l = pl.program_id(1)
    rows = bb * lt
    u = u_ref[:, :RW_COLS]
    if bb == 1:
        first = jnp.where(l == 0, s0_ref[0], carry_ref[...])
        first_rows = jnp.broadcast_to(first, (rows, RW_COLS))
    else:
        first_rows = jnp.broadcast_to(s0_ref[...], (bb, lt, RW_COLS)).reshape(rows, RW_COLS)
    rowid = lax.broadcasted_iota(jnp.int32, (rows, 1), 0) % lt
    prev = jnp.where(rowid == 0, first_rows, pltpu.roll(u, 1, 0))
    if bb == 1:
        carry_ref[...] = u[rows - 1:rows, :]
    z = u + (prev - u) * mu_ref[...]
    r = z[:, 0:MIX_W]
    k = z[:, MIX_W:2 * MIX_W]
    v = z[:, 2 * MIX_W:3 * MIX_W]
    o = 3 * MIX_W
    wd = z[:, o:o + RW_LORA[0]]
    ad = z[:, o + RW_LORA[0]:o + RW_LORA[0] + RW_LORA[1]]
    gd = z[:, o + RW_LORA[0] + RW_LORA[1]:RW_COLS]
    w_log = -_softplus(-(w0_ref[...] + _bdot(jnp.tanh(wd), wup_ref[...]))) - 0.5
    decay = jnp.exp(-jnp.exp(w_log))
    a = _sigmoid(a0_ref[...] + _bdot(ad, aup_ref[...]))
    g = _bdot(_sigmoid(gd), gup_ref[...])
    kk = k * kk_ref[...]
    kk = kk / jnp.maximum(jnp.sqrt(_segsum(kk * kk, RW_HD)), 1e-12)
    k = k * (1.0 + (a - 1.0) * ka_ref[...])
    r_o[...] = r
    w_o[...] = decay
    k_o[...] = k
    kk_o[...] = kk
    b_o[...] = kk * a
    v_o[...] = v
    g_o[...] = g
    bonus_o[...] = _segsum(r * k * rk_ref[...], RW_HD) * v


def _rwkv_prep(z, shift0, p, row0, n_seq, seq_len, bb, lt):
    rows = bb * lt
    nl = seq_len // lt
    assert bb == 1 or nl == 1
    rb0 = row0 // rows
    m = n_seq * seq_len
    fixed = lambda b, l: (0, 0)
    outmap = lambda b, l: (b * nl + l, 0)
    vec = lambda width: pl.BlockSpec((1, width), fixed)
    ospec = pl.BlockSpec((rows, MIX_W), outmap)
    oshape = jax.ShapeDtypeStruct((m, MIX_W), F32)
    return pl.pallas_call(
        functools.partial(_rwkv_prep_kernel, bb, lt),
        grid=(n_seq // bb, nl),
        in_specs=[pl.BlockSpec((rows, Z_RWB), lambda b, l: (rb0 + b * nl + l, Z_RW // Z_RWB)),
                  pl.BlockSpec((bb, 1, RW_COLS), lambda b, l: (b, 0, 0)),
                  vec(RW_COLS), vec(MIX_W), pl.BlockSpec((RW_LORA[0], MIX_W), fixed), vec(MIX_W),
                  pl.BlockSpec((RW_LORA[1], MIX_W), fixed), pl.BlockSpec((RW_LORA[2], MIX_W), fixed),
                  vec(MIX_W), vec(MIX_W), vec(MIX_W)],
        out_specs=[ospec] * 8,
        out_shape=[oshape] * 8,
        scratch_shapes=[pltpu.VMEM((1, RW_COLS), F32)],
        compiler_params=_cparams("parallel", "arbitrary"),
        name="rwkv_prep",
    )(z, shift0, p['mu'], p['w0'], p['w_up'], p['a0'], p['a_up'], p['g_up'], p['k_k'], p['k_a'], p['r_k'])


def _rwkv_scan_kernel(tc, r_ref, w_ref, k_ref, kk_ref, b_ref, v_ref, s0_ref, y_ref, s_ref, ops_ref):
    @pl.when(pl.program_id(1) == 0)
    def _():
        s_ref[...] = s0_ref[...]

    for i, ref in enumerate((kk_ref, w_ref, b_ref, k_ref, r_ref)):
        x = ref[0]
        ops_ref[i] = jnp.concatenate([x, x], axis=-1)

    def step(t, carry):
        for vp in range(RW_VP):
            s = s_ref[0, vp]
            sa = jnp.sum(s * ops_ref[0, t], axis=0, keepdims=True)
            sn = s * ops_ref[1, t] - sa * ops_ref[2, t] + v_ref[0, t, vp:vp + 1, :] * ops_ref[3, t]
            s_ref[0, vp] = sn
            y_ref[0, t, vp:vp + 1, :] = jnp.sum(sn * ops_ref[4, t], axis=0, keepdims=True)
        return carry

    lax.fori_loop(0, tc, step, 0)


def _rwkv_scan(r, w, k, kk, b, v, s0, tc):
    g, l = r.shape[0], r.shape[1]
    kspec = pl.BlockSpec((1, tc, RW_HD, RW_CHAINS), lambda i, c: (i, c, 0, 0))
    vspec = pl.BlockSpec((1, tc, RW_VP, LANES), lambda i, c: (i, c, 0, 0))
    sspec = pl.BlockSpec((1, RW_VP, RW_HD, LANES), lambda i, c: (i, 0, 0, 0))
    return pl.pallas_call(
        functools.partial(_rwkv_scan_kernel, tc),
        grid=(g, l // tc),
        in_specs=[kspec] * 5 + [vspec, sspec],
        out_specs=[vspec, sspec],
        out_shape=[jax.ShapeDtypeStruct((g, l, RW_VP, LANES), F32),
                   jax.ShapeDtypeStruct((g, RW_VP, RW_HD, LANES), F32)],
        scratch_shapes=[pltpu.VMEM((5, tc, RW_HD, LANES), F32)],
        compiler_params=_cparams("parallel", "arbitrary"),
        name="rwkv_scan",
    )(r, w, k, kk, b, v, s0)


def _rwkv_post_kernel(y_ref, g_ref, bonus_ref, gng_ref, gnb_ref, o_ref):
    y = y_ref[...]
    mean = _segsum(y, RW_HD) * (1.0 / RW_HD)
    yc = y - mean
    var = _segsum(yc * yc, RW_HD) * (1.0 / RW_HD)
    yn = yc * lax.rsqrt(var + RW_GN_EPS) * gng_ref[...] + gnb_ref[...]
    o_ref[...] = ((yn + bonus_ref[...]) * g_ref[...]).astype(o_ref.dtype)


def _rwkv_post(y, g, bonus, gn_g, gn_b, row0, total_rows, prev, tm=256):
    m = y.shape[0]
    rb0 = row0 // tm
    fixed = lambda i: (0, 0)
    spec = pl.BlockSpec((tm, MIX_W), lambda i: (i, 0))
    kern, prev_specs, aliases = _carry_through(_rwkv_post_kernel, 5, prev)
    return pl.pallas_call(
        kern,
        grid=(m // tm,),
        in_specs=[spec, spec, spec, pl.BlockSpec((1, MIX_W), fixed), pl.BlockSpec((1, MIX_W), fixed)] + prev_specs,
        out_specs=pl.BlockSpec((tm, MIX_W), lambda i: (rb0 + i, 0)),
        out_shape=jax.ShapeDtypeStruct((total_rows, MIX_W), BF16),
        input_output_aliases=aliases,
        compiler_params=_cparams("parallel"),
        name="rwkv_post",
    )(y, g, bonus, gn_g, gn_b, *prev)


def _to_scan_keys(a, n_seq, seq_len):
    nb = RW_CHAINS // RW_HEADS
    g = n_seq // nb
    return a.reshape(g, nb, seq_len, RW_HEADS, RW_HD).transpose(0, 2, 4, 1, 3).reshape(g, seq_len, RW_HD, RW_CHAINS)


def _to_scan_vals(a, n_seq, seq_len):
    nb = RW_CHAINS // RW_HEADS
    g = n_seq // nb
    a = a.reshape(g, nb, seq_len, RW_HEADS, 2, RW_VP).transpose(0, 2, 5, 4, 1, 3)
    return a.reshape(g, seq_len, RW_VP, LANES)


def _from_scan_vals(y, n_seq, seq_len):
    nb = RW_CHAINS // RW_HEADS
    g = n_seq // nb
    y = y.reshape(g, seq_len, RW_VP, 2, nb, RW_HEADS).transpose(0, 4, 1, 5, 3, 2)
    return y.reshape(n_seq * seq_len, MIX_W)


def _state_to_scan(s, n_seq):
    nb = RW_CHAINS // RW_HEADS
    g = n_seq // nb
    s = s.reshape(g, nb, RW_HEADS, 2, RW_VP, RW_HD).transpose(0, 4, 5, 3, 1, 2)
    return s.reshape(g, RW_VP, RW_HD, LANES)


def _state_from_scan(s, n_seq):
    nb = RW_CHAINS // RW_HEADS
    g = n_seq // nb
    s = s.reshape(g, RW_VP, RW_HD, 2, nb, RW_HEADS).transpose(0, 4, 5, 3, 1, 2)
    return s.reshape(n_seq, RW_HEADS, RW_HD, RW_HD)


def _rwkv_branch(z, shift0, s0, p, row0, n_seq, seq_len, bb, lt, tc, prev):
    r, w, k, kk, b, v, g, bonus = _rwkv_prep(z, shift0, p, row0, n_seq, seq_len, bb, lt)
    keys = [_to_scan_keys(a, n_seq, seq_len) for a in (r, w, k, kk, b)]
    y, s_new = _rwkv_scan(*keys, _to_scan_vals(v, n_seq, seq_len), _state_to_scan(s0, n_seq), tc)
    y = _from_scan_vals(y, n_seq, seq_len)
    y = _rwkv_post(y, g, bonus, p['gn_g'], p['gn_b'], row0, z.shape[0], prev)
    return y, _state_from_scan(s_new, n_seq)


def _seq_masks(rows, lc):
    ri = lax.broadcasted_iota(jnp.int32, (rows, rows), 0)
    ci = lax.broadcasted_iota(jnp.int32, (rows, rows), 1)
    same = (ri // lc) == (ci // lc)
    return ri, ci, same


def _rows_from_seq(x, bb, lc):
    return jnp.broadcast_to(x, (bb, lc, x.shape[-1])).reshape(bb * lc, x.shape[-1])


def _last_of_seq(x, bb, lc):
    x3 = x.reshape(bb, lc, x.shape[-1])
    pick = (lax.broadcasted_iota(jnp.int32, (1, lc, 1), 1) == lc - 1).astype(F32)
    return jnp.sum(x3 * pick, axis=1, keepdims=True)


def _transpose_rows(x):
    rows = x.shape[0]
    if rows < LANES:
        x = jnp.concatenate([x, jnp.zeros((LANES - rows, LANES), x.dtype)], axis=0)
    return x.T[:, :rows]


def _mlstm_kernel(bb, lc, q_ref, k_ref, v_ref, o_ref, gt_ref, gb_ref, ng_ref, c0_ref, n0_ref, m0_ref,
                  y_ref, c_ref, n_ref, m_ref):
    rows = bb * lc

    @pl.when(pl.program_id(1) == 0)
    def _():
        c_ref[...] = c0_ref[...]
        n_ref[...] = n0_ref[...]
        m_ref[...] = m0_ref[...]

    gt = ML_GATE_CAP * jnp.tanh((gt_ref[...] + gb_ref[...]) * (1.0 / ML_GATE_CAP))
    lane = lax.broadcasted_iota(jnp.int32, (rows, LANES), 1)
    ri, ci, same = _seq_masks(rows, lc)
    causal = same & (ci <= ri)
    cum = _dot_exact_lhs(causal.astype(F32), -_softplus(-gt))
    gt_t = _transpose_rows(gt)
    cum_t = _transpose_rows(cum)
    for hd in range(ML_HEADS):
        sl = slice(hd * ML_HD, (hd + 1) * ML_HD)
        c0 = c_ref[:, hd]
        n0 = n_ref[:, hd]
        m0 = m_ref[:, hd]
        q = q_ref[:, sl]
        k = k_ref[:, sl] * (ML_HD ** -0.5)
        v = v_ref[:, sl]
        bcol = cum[:, ML_HEADS + hd:ML_HEADS + hd + 1]
        igcol = gt[:, hd:hd + 1]
        rowb = gt_t[hd:hd + 1, :] - cum_t[ML_HEADS + hd:ML_HEADS + hd + 1, :]
        dlog = jnp.where(causal, bcol + rowb, -jnp.inf)
        m_inter = bcol + _rows_from_seq(m0, bb, lc)
        m_t = jnp.maximum(m_inter, jnp.max(dlog, axis=-1, keepdims=True))
        qk = lax.dot_general(q.astype(BF16), k.astype(BF16), (((1,), (1,)), ((), ())),
                             preferred_element_type=F32)
        wts = jnp.exp(dlog - m_t) * qk
        s_inter = jnp.exp(m_inter - m_t)
        q3 = q.reshape(bb, lc, ML_HD).astype(BF16)
        qc = jnp.einsum('btd,bde->bte', q3, c0.astype(BF16), preferred_element_type=F32).reshape(rows, ML_HD)
        num = _bdot(wts, v) + s_inter * qc
        qn = jnp.sum(q * _rows_from_seq(n0, bb, lc), axis=-1, keepdims=True)
        den = jnp.sum(wts, axis=-1, keepdims=True) + s_inter * qn
        hh = num / jnp.maximum(jnp.abs(den), jnp.exp(-m_t))
        hh = hh * lax.rsqrt(jnp.mean(hh * hh, axis=-1, keepdims=True) + EPS) * ng_ref[:, sl]
        y_ref[:, sl] = (_sigmoid(o_ref[:, sl]) * hh).astype(y_ref.dtype)

        stat = jnp.where(lane == 0, bcol, jnp.where(lane == 1, m_t, 0.0))
        last = _last_of_seq(stat, bb, lc)
        last_rows = _rows_from_seq(last, bb, lc)
        b_last, m_new = last_rows[:, 0:1], last_rows[:, 1:2]
        w_end = jnp.exp(b_last - bcol + igcol - m_new)
        f_end = jnp.exp(last[:, :, 0:1] + m0 - last[:, :, 1:2])
        kw3 = (k * w_end).reshape(bb, lc, ML_HD)
        upd = jnp.einsum('bsd,bse->bde', kw3.astype(BF16), v.reshape(bb, lc, ML_HD).astype(BF16),
                         preferred_element_type=F32)
        c_ref[:, hd] = f_end * c0 + upd
        n_ref[:, hd] = f_end * n0 + jnp.sum(kw3, axis=1, keepdims=True)
        m_ref[:, hd] = last[:, :, 1:2]


def _mlstm_branch(z, c0, n0, m0, gate_bias, norm_g, row0, n_seq, seq_len, bb, lc, prev):
    rows = bb * lc
    nc = seq_len // lc
    rb0 = row0 // rows
    rowmap = lambda part: (lambda b, c: (rb0 + b * nc + c, Z_ML // MIX_W + part))
    zspec = lambda part: pl.BlockSpec((rows, MIX_W), rowmap(part))
    smap = lambda b, c: (b, 0, 0, 0)
    cspec = pl.BlockSpec((bb, ML_HEADS, ML_HD, ML_HD), smap)
    nspec = pl.BlockSpec((bb, ML_HEADS, 1, ML_HD), smap)
    mspec = pl.BlockSpec((bb, ML_HEADS, 1, 1), smap)
    kern, prev_specs, aliases = _carry_through(functools.partial(_mlstm_kernel, bb, lc), 10, prev)
    return pl.pallas_call(
        kern,
        grid=(n_seq // bb, nc),
        in_specs=[zspec(0), zspec(1), zspec(2), zspec(3),
                  pl.BlockSpec((rows, LANES), lambda b, c: (rb0 + b * nc + c, Z_MLG // LANES)),
                  pl.BlockSpec((1, LANES), lambda b, c: (0, 0)), pl.BlockSpec((1, MIX_W), lambda b, c: (0, 0)),
                  cspec, nspec, mspec] + prev_specs,
        out_specs=[pl.BlockSpec((rows, MIX_W), lambda b, c: (rb0 + b * nc + c, 0)), cspec, nspec, mspec],
        out_shape=[jax.ShapeDtypeStruct((z.shape[0], MIX_W), BF16),
                   jax.ShapeDtypeStruct((n_seq, ML_HEADS, ML_HD, ML_HD), F32),
                   jax.ShapeDtypeStruct((n_seq, ML_HEADS, 1, ML_HD), F32),
                   jax.ShapeDtypeStruct((n_seq, ML_HEADS, 1, 1), F32)],
        input_output_aliases=aliases,
        compiler_params=_cparams("parallel", "arbitrary"),
        name="mlstm_chunk",
    )(z, z, z, z, z, gate_bias, norm_g, c0, n0, m0, *prev)


def _rope_rows(x, cos, sin):
    half = RT_HD // 2
    x1, x2 = x[:, :half], x[:, half:]
    return jnp.concatenate([x1 * cos - x2 * sin, x1 * sin + x2 * cos], axis=1)


def _ret_kernel(bb, lc, q_ref, k_ref, v_ref, g_ref, cs_ref, dec_ref, cd_ref, s0_ref, y_ref, s_ref):
    rows = bb * lc

    @pl.when(pl.program_id(1) == 0)
    def _():
        s_ref[...] = s0_ref[...]

    cos, sin = cs_ref[:, :RT_HD // 2], cs_ref[:, RT_HD // 2:]
    for hd in range(RT_HEADS):
        sl = slice(hd * RT_HD, (hd + 1) * RT_HD)
        s0 = s_ref[:, hd]
        q = _rope_rows(q_ref[:, sl], cos, sin)
        k = _rope_rows(k_ref[:, sl], cos, sin) * (RT_HD ** -0.5)
        v = v_ref[:, sl]
        cd = cd_ref[hd]
        qk = lax.dot_general(q.astype(BF16), k.astype(BF16), (((1,), (1,)), ((), ())),
                             preferred_element_type=F32)
        inner = _bdot(qk * dec_ref[hd], v)
        q3 = q.reshape(bb, lc, RT_HD).astype(BF16)
        cross = jnp.einsum('btd,bde->bte', q3, s0.astype(BF16), preferred_element_type=F32).reshape(rows, RT_HD)
        y = inner + cross * cd[:, 0:1]
        y = y * lax.rsqrt(jnp.mean(y * y, axis=-1, keepdims=True) + EPS)
        gt = g_ref[:, sl]
        y_ref[:, sl] = (gt * _sigmoid(gt) * y).astype(y_ref.dtype)
        kd = (k * cd[:, 1:2]).reshape(bb, lc, RT_HD).astype(BF16)
        upd = jnp.einsum('bsd,bse->bde', kd, v.reshape(bb, lc, RT_HD).astype(BF16), preferred_element_type=F32)
        s_ref[:, hd] = cd[0:1, 2:3] * s0 + upd


def _ret_branch(z, s0, row0, n_seq, seq_len, bb, lc, pos0, prev):
    rows = bb * lc
    nc = seq_len // lc
    rb0 = row0 // rows
    half = RT_HD // 2
    inv = ROPE_BASE ** (-jnp.arange(half, dtype=F32) / half)
    pos = jnp.broadcast_to((jnp.arange(nc * lc, dtype=F32) + float(pos0)).reshape(nc, 1, lc), (nc, bb, lc))
    ang = pos.reshape(nc * rows)[:, None] * inv[None, :]
    cs = jnp.concatenate([jnp.cos(ang), jnp.sin(ang)], axis=1)
    log_g = jnp.log(1.0 - jnp.exp(jnp.linspace(math.log(1.0 / 32), math.log(1.0 / 512), RT_HEADS)))
    idx = jnp.arange(rows)
    t = (idx % lc).astype(F32)
    diff = t[:, None] - t[None, :]
    same = (idx[:, None] // lc) == (idx[None, :] // lc)
    dec = jnp.where(same & (diff >= 0), jnp.exp(log_g[:, None, None] * jnp.maximum(diff, 0.0)), 0.0)
    cd = jnp.zeros((RT_HEADS, rows, LANES), F32)
    cd = cd.at[:, :, 0].set(jnp.exp(log_g[:, None] * (t[None, :] + 1.0)))
    cd = cd.at[:, :, 1].set(jnp.exp(log_g[:, None] * (lc - 1.0 - t)[None, :]))
    cd = cd.at[:, :, 2].set(jnp.broadcast_to(jnp.exp(log_g * lc)[:, None], (RT_HEADS, rows)))
    rowmap = lambda part: (lambda b, c: (rb0 + b * nc + c, Z_RT // MIX_W + part))
    zspec = lambda part: pl.BlockSpec((rows, MIX_W), rowmap(part))
    sspec = pl.BlockSpec((bb, RT_HEADS, RT_HD, RT_HD), lambda b, c: (b, 0, 0, 0))
    kern, prev_specs, aliases = _carry_through(functools.partial(_ret_kernel, bb, lc), 8, prev)
    return pl.pallas_call(
        kern,
        grid=(n_seq // bb, nc),
        in_specs=[zspec(0), zspec(1), zspec(2), zspec(3),
                  pl.BlockSpec((rows, RT_HD), lambda b, c: (c, 0)),
                  pl.BlockSpec((RT_HEADS, rows, rows), lambda b, c: (0, 0, 0)),
                  pl.BlockSpec((RT_HEADS, rows, LANES), lambda b, c: (0, 0, 0)),
                  sspec] + prev_specs,
        out_specs=[pl.BlockSpec((rows, MIX_W), lambda b, c: (rb0 + b * nc + c, 0)), sspec],
        out_shape=[jax.ShapeDtypeStruct((z.shape[0], MIX_W), BF16),
                   jax.ShapeDtypeStruct((n_seq, RT_HEADS, RT_HD, RT_HD), F32)],
        input_output_aliases=aliases,
        compiler_params=_cparams("parallel", "arbitrary"),
        name="retention_chunk",
    )(z, z, z, z, cs, dec, cd, s0, *prev)


def _pack_w_in(w_in):
    d = w_in.shape[0]
    ml0 = RW_COLS
    mlg0 = ml0 + 4 * MIX_W
    rt0 = mlg0 + 2 * ML_HEADS
    pad = jnp.zeros((d, Z_COLS - Z_MLG - 2 * ML_HEADS), w_in.dtype)
    packed = jnp.concatenate([w_in[:, ml0:mlg0], w_in[:, rt0:], w_in[:, :RW_COLS], w_in[:, mlg0:rt0], pad], axis=1)
    return packed.astype(BF16)


def _group_cfg(n_seq, seq_len):
    if seq_len % CHUNK == 0:
        return dict(rw_bb=1, rw_lt=256, rw_tc=32, ch_bb=1, ch_lc=CHUNK, at_bb=1, at_lq=512)
    return dict(rw_bb=32, rw_lt=seq_len, rw_tc=seq_len, ch_bb=4, ch_lc=seq_len, at_bb=8, at_lq=seq_len)


def kernel(x_prompt, x_sample, mem_prompt, state_rwkv_shift, state_rwkv, state_mlstm_c, state_mlstm_n,
           state_mlstm_m, state_ret, cache_mem_k, cache_mem_v, g_pre_mix, g_post_mix, g_pre_x, g_post_x,
           g_pre_ff, g_post_ff, g_mem, w_in, rw_mu, rw_w0, rw_w_up, rw_a0, rw_a_up, rw_g_up, rw_k_k, rw_k_a,
           rw_r_k, rw_gn_g, rw_gn_b, ml_i_b, ml_f_b, ml_norm_g, w_br, w_out, x_wq, x_wkv, x_wo, ff_w1, ff_w2):
    bp, lp, d = x_prompt.shape
    bs, ls, _ = x_sample.shape
    depth = w_in.shape[0]
    n_mem = mem_prompt.shape[1]
    tp, ts = bp * lp, bs * ls
    groups = [(0, bp, lp, 0), (tp, bs, ls, PAST_LEN)]

    x = jnp.concatenate([x_prompt.reshape(tp, d), x_sample.reshape(ts, d)], axis=0)
    mem = mem_prompt.reshape(bp * n_mem, d)
    h = _rmsnorm(x, g_pre_mix[0])

    zeros = lambda *s: jnp.zeros(s, F32)
    new_states = [[], []]
    mem_k, mem_v = [], []
    for l in range(depth):
        kv = _mm(_rmsnorm(mem, g_mem[l], tm=256), x_wkv[l].astype(BF16), tm=256, tn=512)
        mk_p, mv_p = kv[:, :X_W].reshape(bp, n_mem, X_W), kv[:, X_W:].reshape(bp, n_mem, X_W)
        mem_k.append(mk_p.reshape(bp, n_mem, X_HEADS, X_HD))
        mem_v.append(mv_p.reshape(bp, n_mem, X_HEADS, X_HD))

        z = _mm(h, _pack_w_in(w_in[l]), tm=1024, tn=512)

        rw_p = dict(mu=rw_mu[l].reshape(1, -1), w0=rw_w0[l].reshape(1, -1), w_up=rw_w_up[l].astype(BF16),
                    a0=rw_a0[l].reshape(1, -1), a_up=rw_a_up[l].astype(BF16), g_up=rw_g_up[l].astype(BF16),
                    k_k=rw_k_k[l].reshape(1, -1), k_a=rw_k_a[l].reshape(1, -1), r_k=rw_r_k[l].reshape(1, -1),
                    gn_g=rw_gn_g[l].reshape(1, -1), gn_b=rw_gn_b[l].reshape(1, -1))
        gate_bias = jnp.concatenate([ml_i_b[l], ml_f_b[l], zeros(LANES - 2 * ML_HEADS)]).reshape(1, LANES)
        norm_g = ml_norm_g[l].reshape(1, MIX_W)

        y_rw = y_ml = y_rt = None
        for gi, (row0, n_seq, seq_len, pos0) in enumerate(groups):
            prev = (lambda y: () if y is None else (y,))
            cfg = _group_cfg(n_seq, seq_len)
            if gi == 0:
                init = (zeros(n_seq, 1, RW_COLS), zeros(n_seq, RW_HEADS, RW_HD, RW_HD),
                        zeros(n_seq, ML_HEADS, ML_HD, ML_HD), zeros(n_seq, ML_HEADS, 1, ML_HD),
                        zeros(n_seq, ML_HEADS, 1, 1), zeros(n_seq, RT_HEADS, RT_HD, RT_HD))
            else:
                init = (state_rwkv_shift[l].reshape(n_seq, 1, RW_COLS), state_rwkv[l], state_mlstm_c[l],
                        state_mlstm_n[l].reshape(n_seq, ML_HEADS, 1, ML_HD),
                        state_mlstm_m[l].reshape(n_seq, ML_HEADS, 1, 1), state_ret[l])
            y_rw, s_rw = _rwkv_branch(z, init[0], init[1], rw_p, row0, n_seq, seq_len,
                                      cfg['rw_bb'], cfg['rw_lt'], cfg['rw_tc'], prev(y_rw))
            y_ml, c_new, n_new, m_new = _mlstm_branch(z, init[2], init[3], init[4], gate_bias, norm_g, row0,
                                                      n_seq, seq_len, cfg['ch_bb'], cfg['ch_lc'], prev(y_ml))
            y_rt, s_rt = _ret_branch(z, init[5], row0, n_seq, seq_len, cfg['ch_bb'], cfg['ch_lc'], pos0,
                                     prev(y_rt))
            last = row0 + seq_len - 1
            shift = lax.slice(z, (last, Z_RW), (last + (n_seq - 1) * seq_len + 1, Z_RW + RW_COLS), (seq_len, 1))
            new_states[gi].append((shift, s_rw, c_new, n_new.reshape(n_seq, ML_HEADS, ML_HD),
                                   m_new.reshape(n_seq, ML_HEADS), s_rt))

        merged = _merge(y_rw, y_ml, y_rt, w_br[l].astype(BF16), z)
        x, h = _proj_res(merged, w_out[l].astype(BF16), x, g_post_mix[l], g_pre_x[l])

        wq, wo = x_wq[l].astype(BF16), x_wo[l].astype(BF16)
        for gi, (row0, n_seq, seq_len, pos0) in enumerate(groups):
            cfg = _group_cfg(n_seq, seq_len)
            if gi == 0:
                mk, mv = mk_p, mv_p
            else:
                mk, mv = cache_mem_k[l].reshape(n_seq, n_mem, X_W), cache_mem_v[l].reshape(n_seq, n_mem, X_W)
            x, h = _attn(h, x, mk, mv, wq, wo, g_post_x[l], g_pre_ff[l], row0, n_seq, seq_len,
                         cfg['at_bb'], cfg['at_lq'])

        g_next = g_pre_mix[(l + 1) % depth]
        x, h = _mlp(h, ff_w1[l].astype(BF16), ff_w2[l].astype(BF16), x, g_post_ff[l], g_next)

    stack = lambda gi, i: jnp.stack([s[i] for s in new_states[gi]])
    return (x[:tp].reshape(bp, lp, d), x[tp:].reshape(bs, ls, d),
            stack(0, 0), stack(0, 1), stack(0, 2), stack(0, 3), stack(0, 4), stack(0, 5),
            jnp.stack(mem_k), jnp.stack(mem_v),
            stack(1, 0), stack(1, 1), stack(1, 2), stack(1, 3), stack(1, 4), stack(1, 5))
```

```python
import functools
import math

import jax
import jax.numpy as jnp
from jax import lax
from jax.experimental import pallas as pl
from jax.experimental.pallas import tpu as pltpu

F32 = jnp.float32
BF16 = jnp.bfloat16

D_MODEL = 2048
MIX_W = D_MODEL // 2
RW_HD = 64
RW_HEADS = MIX_W // RW_HD
RW_LORA = (64, 64, 128)
RW_COLS = 3 * MIX_W + sum(RW_LORA)
RW_GN_EPS = 64e-5
ML_HEADS = 8
ML_HD = MIX_W // ML_HEADS
ML_GATE_CAP = 15.0
RT_HEADS = 4
RT_HD = MIX_W // RT_HEADS
ROPE_BASE = 10000.0
X_HEADS = 4
X_HD = 128
X_W = X_HEADS * X_HD
D_FF = 4 * D_MODEL
CHUNK = 64
EPS = 1e-6
PAST_LEN = 16384

Z_ML = 0
Z_RT = Z_ML + 4 * MIX_W
Z_GATE = Z_RT + 4 * MIX_W
Z_RW = Z_GATE + 3 * D_MODEL
Z_MLG = Z_RW + RW_COLS
Z_RWB = RW_COLS + 256
Z_COLS = Z_MLG + 256

LANES = 128
RW_VP = RW_HD // 2
RW_CHAINS = LANES // 2


def _cparams(*sem):
    return pltpu.CompilerParams(dimension_semantics=sem, vmem_limit_bytes=56 * 1024 * 1024)


def _carry_through(kernel_fn, n_in, prev):
    carried = [(i, p) for i, p in enumerate(prev) if p is not None]
    if not carried:
        return kernel_fn, [], [], {}

    def wrapped(*refs):
        return kernel_fn(*refs[:n_in], *refs[n_in + len(carried):])

    return (wrapped, [pl.BlockSpec(memory_space=pl.ANY)] * len(carried), [p for _, p in carried],
            {n_in + j: i for j, (i, _) in enumerate(carried)})


def _sigmoid(x):
    return 1.0 / (1.0 + jnp.exp(-x))


def _softplus(x):
    return jnp.maximum(x, 0.0) + jnp.log(1.0 + jnp.exp(-jnp.abs(x)))


def _bdot(a, b):
    return jnp.dot(a.astype(BF16), b.astype(BF16), preferred_element_type=F32)


def _split3(x):
    hi = x.astype(BF16)
    r1 = x - hi.astype(F32)
    mid = r1.astype(BF16)
    lo = (r1 - mid.astype(F32)).astype(BF16)
    return hi, mid, lo


def _dot_exact_lhs(m, x):
    mb = m.astype(BF16)
    hi, mid, lo = _split3(x)
    return (jnp.dot(mb, hi, preferred_element_type=F32) + jnp.dot(mb, mid, preferred_element_type=F32)
            + jnp.dot(mb, lo, preferred_element_type=F32))


def _dot_exact_rhs(x, m):
    mb = m.astype(BF16)
    hi, mid, lo = _split3(x)
    return (jnp.dot(hi, mb, preferred_element_type=F32) + jnp.dot(mid, mb, preferred_element_type=F32)
            + jnp.dot(lo, mb, preferred_element_type=F32))


def _segsum(x, seg):
    blk = 256
    ri = lax.broadcasted_iota(jnp.int32, (blk, blk), 0) // seg
    ci = lax.broadcasted_iota(jnp.int32, (blk, blk), 1) // seg
    ones = (ri == ci).astype(F32)
    parts = [_dot_exact_rhs(x[:, c:c + blk], ones) for c in range(0, x.shape[1], blk)]
    return parts[0] if len(parts) == 1 else jnp.concatenate(parts, axis=1)


def _rms_rows(x, g):
    return x * lax.rsqrt(jnp.mean(x * x, axis=-1, keepdims=True) + EPS) * g


def _rmsnorm_kernel(x_ref, g_ref, o_ref):
    o_ref[...] = _rms_rows(x_ref[...], g_ref[...]).astype(o_ref.dtype)


def _rmsnorm(x, g, tm=512):
    m, d = x.shape
    return pl.pallas_call(
        _rmsnorm_kernel,
        grid=(m // tm,),
        in_specs=[pl.BlockSpec((tm, d), lambda i: (i, 0)), pl.BlockSpec((1, d), lambda i: (0, 0))],
        out_specs=pl.BlockSpec((tm, d), lambda i: (i, 0)),
        out_shape=jax.ShapeDtypeStruct((m, d), BF16),
        compiler_params=_cparams("parallel"),
        name="rmsnorm",
    )(x, g.reshape(1, d))


def _mm_kernel(x_ref, w_ref, o_ref):
    o_ref[...] = jnp.dot(x_ref[...], w_ref[...], preferred_element_type=F32)


def _mm(x, w, tm, tn):
    m, k = x.shape
    n = w.shape[1]
    return pl.pallas_call(
        _mm_kernel,
        grid=(n // tn, m // tm),
        in_specs=[pl.BlockSpec((tm, k), lambda j, i: (i, 0)), pl.BlockSpec((k, tn), lambda j, i: (0, j))],
        out_specs=pl.BlockSpec((tm, tn), lambda j, i: (i, j)),
        out_shape=jax.ShapeDtypeStruct((m, n), F32),
        compiler_params=_cparams("parallel", "parallel"),
        name="matmul",
    )(x, w)


def _merge_kernel(y0_ref, y1_ref, y2_ref, w_ref, g0_ref, g1_ref, g2_ref, o_ref):
    acc = _sigmoid(g0_ref[...]) * jnp.dot(y0_ref[...], w_ref[0], preferred_element_type=F32)
    acc += _sigmoid(g1_ref[...]) * jnp.dot(y1_ref[...], w_ref[1], preferred_element_type=F32)
    acc += _sigmoid(g2_ref[...]) * jnp.dot(y2_ref[...], w_ref[2], preferred_element_type=F32)
    o_ref[...] = acc.astype(o_ref.dtype)


def _merge(y_rw, y_ml, y_rt, w_br, z, tm=512, tn=512):
    m, kw = y_rw.shape
    d = w_br.shape[2]
    gb = Z_GATE // tn
    gspec = lambda c: pl.BlockSpec((tm, tn), lambda j, i: (i, gb + c * (d // tn) + j))
    yspec = pl.BlockSpec((tm, kw), lambda j, i: (i, 0))
    return pl.pallas_call(
        _merge_kernel,
        grid=(d // tn, m // tm),
        in_specs=[yspec, yspec, yspec, pl.BlockSpec((3, kw, tn), lambda j, i: (0, 0, j)),
                  gspec(0), gspec(1), gspec(2)],
        out_specs=pl.BlockSpec((tm, tn), lambda j, i: (i, j)),
        out_shape=jax.ShapeDtypeStruct((m, d), BF16),
        compiler_params=_cparams("parallel", "parallel"),
        name="branch_merge",
    )(y_rw, y_ml, y_rt, w_br, z, z, z)


def _residual_epilogue(acc, x_ref, gpost_ref, gnext_ref, xo_ref, ho_ref):
    xn = x_ref[...] + _rms_rows(acc, gpost_ref[...])
    xo_ref[...] = xn
    ho_ref[...] = _rms_rows(xn, gnext_ref[...]).astype(ho_ref.dtype)


def _proj_res_kernel(a_ref, w_ref, x_ref, gpost_ref, gnext_ref, xo_ref, ho_ref):
    acc = jnp.dot(a_ref[...], w_ref[...], preferred_element_type=F32)
    _residual_epilogue(acc, x_ref, gpost_ref, gnext_ref, xo_ref, ho_ref)


def _proj_res(a, w, x, g_post, g_next, tm=512):
    m, k = a.shape
    d = w.shape[1]
    row = lambda i: (i, 0)
    fixed = lambda i: (0, 0)
    return pl.pallas_call(
        _proj_res_kernel,
        grid=(m // tm,),
        in_specs=[pl.BlockSpec((tm, k), row), pl.BlockSpec((k, d), fixed), pl.BlockSpec((tm, d), row),
                  pl.BlockSpec((1, d), fixed), pl.BlockSpec((1, d), fixed)],
        out_specs=[pl.BlockSpec((tm, d), row), pl.BlockSpec((tm, d), row)],
        out_shape=[jax.ShapeDtypeStruct((m, d), F32), jax.ShapeDtypeStruct((m, d), BF16)],
        compiler_params=_cparams("parallel"),
        name="out_proj_residual",
    )(a, w, x, g_post.reshape(1, d), g_next.reshape(1, d))


def _mlp_kernel(h_ref, w1_ref, w2_ref, x_ref, gpost_ref, gnext_ref, xo_ref, ho_ref, acc_ref):
    k = pl.program_id(1)

    @pl.when(k == 0)
    def _():
        acc_ref[...] = jnp.zeros_like(acc_ref)

    u = jnp.dot(h_ref[...], w1_ref[...], preferred_element_type=F32)
    u = jnp.square(jnp.maximum(u, 0.0)).astype(BF16)
    acc_ref[...] += jnp.dot(u, w2_ref[...], preferred_element_type=F32)

    @pl.when(k == pl.num_programs(1) - 1)
    def _():
        _residual_epilogue(acc_ref[...], x_ref, gpost_ref, gnext_ref, xo_ref, ho_ref)


def _mlp(h, w1, w2, x, g_post, g_next, tm=512, tf=1024):
    m, d = h.shape
    f = w1.shape[1]
    row = lambda i, k: (i, 0)
    fixed = lambda i, k: (0, 0)
    return pl.pallas_call(
        _mlp_kernel,
        grid=(m // tm, f // tf),
        in_specs=[pl.BlockSpec((tm, d), row), pl.BlockSpec((d, tf), lambda i, k: (0, k)),
                  pl.BlockSpec((tf, d), lambda i, k: (k, 0)), pl.BlockSpec((tm, d), row),
                  pl.BlockSpec((1, d), fixed), pl.BlockSpec((1, d), fixed)],
        out_specs=[pl.BlockSpec((tm, d), row), pl.BlockSpec((tm, d), row)],
        out_shape=[jax.ShapeDtypeStruct((m, d), F32), jax.ShapeDtypeStruct((m, d), BF16)],
        scratch_shapes=[pltpu.VMEM((tm, d), F32)],
        compiler_params=_cparams("parallel", "arbitrary"),
        name="relu2_mlp",
    )(h, w1, w2, x, g_post.reshape(1, d), g_next.reshape(1, d))


def _attn_kernel(bb, lq, h_ref, wq_ref, mk_ref, mv_ref, wo_ref, x_ref, gpost_ref, gnext_ref, xo_ref, ho_ref):
    q = jnp.dot(h_ref[...], wq_ref[...], preferred_element_type=F32)
    outs = []
    for hd in range(X_HEADS):
        sl = slice(hd * X_HD, (hd + 1) * X_HD)
        q3 = q[:, sl].reshape(bb, lq, X_HD).astype(BF16)
        k3 = mk_ref[0, :, :, hd, :].astype(BF16)
        v3 = mv_ref[0, :, :, hd, :].astype(BF16)
        s = jnp.einsum('bqd,bmd->bqm', q3, k3, preferred_element_type=F32) * (X_HD ** -0.5)
        s = s - jnp.max(s, axis=-1, keepdims=True)
        e = jnp.exp(s)
        p = (e / jnp.sum(e, axis=-1, keepdims=True)).astype(BF16)
        o3 = jnp.einsum('bqm,bmd->bqd', p, v3, preferred_element_type=F32)
        outs.append(o3.reshape(bb * lq, X_HD))
    o = jnp.concatenate(outs, axis=1).astype(BF16)
    acc = jnp.dot(o, wo_ref[...], preferred_element_type=F32)
    _residual_epilogue(acc, x_ref, gpost_ref, gnext_ref, xo_ref, ho_ref)


def _attn(h, x, mk, mv, lyr, wq, wo, g_post, g_next, row0, n_seq, seq_len, bb, lq):
    t, d = h.shape
    rows = bb * lq
    nl = seq_len // lq
    n_mem = mk.shape[2]
    rb0 = row0 // rows
    rowmap = lambda b, l: (rb0 + b * nl + l, 0)
    fixed = lambda b, l: (0, 0)
    kvspec = pl.BlockSpec((1, bb, n_mem, X_HEADS, X_HD), lambda b, l: (lyr, b, 0, 0, 0))
    return pl.pallas_call(
        functools.partial(_attn_kernel, bb, lq),
        grid=(n_seq // bb, nl),
        in_specs=[pl.BlockSpec((rows, d), rowmap), pl.BlockSpec((d, X_W), fixed), kvspec, kvspec,
                  pl.BlockSpec((X_W, d), fixed), pl.BlockSpec((rows, d), rowmap),
                  pl.BlockSpec((1, d), fixed), pl.BlockSpec((1, d), fixed)],
        out_specs=[pl.BlockSpec((rows, d), rowmap), pl.BlockSpec((rows, d), rowmap)],
        out_shape=[jax.ShapeDtypeStruct((t, d), F32), jax.ShapeDtypeStruct((t, d), BF16)],
        input_output_aliases={5: 0, 0: 1},
        compiler_params=_cparams("parallel", "parallel"),
        name="mem_cross_attention",
    )(h, wq, mk, mv, wo, x, g_post.reshape(1, d), g_next.reshape(1, d))


def _rwkv_prep_kernel(bb, lt, u_ref, s0_ref, mu_ref, w0_ref, wup_ref, a0_ref, aup_ref, gup_ref, kk_ref, ka_ref,
                      rk_ref, r_o, w_o, k_o, kk_o, b_o, v_o, g_o, bonus_o, carry_ref):
    l = pl.program_id(1)
    rows = bb * lt
    u = u_ref[:, :RW_COLS]
    if bb == 1:
        first = jnp.where(l == 0, s0_ref[0], carry_ref[...])
        first_rows = jnp.broadcast_to(first, (rows, RW_COLS))
    else:
        first_rows = jnp.broadcast_to(s0_ref[...], (bb, lt, RW_COLS)).reshape(rows, RW_COLS)
    rowid = lax.broadcasted_iota(jnp.int32, (rows, 1), 0) % lt
    prev = jnp.where(rowid == 0, first_rows, pltpu.roll(u, 1, 0))
    if bb == 1:
        carry_ref[...] = u[rows - 1:rows, :]
    z = u + (prev - u) * mu_ref[...]
    r = z[:, 0:MIX_W]
    k = z[:, MIX_W:2 * MIX_W]
    v = z[:, 2 * MIX_W:3 * MIX_W]
    o = 3 * MIX_W
    wd = z[:, o:o + RW_LORA[0]]
    ad = z[:, o + RW_LORA[0]:o + RW_LORA[0] + RW_LORA[1]]
    gd = z[:, o + RW_LORA[0] + RW_LORA[1]:RW_COLS]
    w_log = -_softplus(-(w0_ref[...] + _bdot(jnp.tanh(wd), wup_ref[...]))) - 0.5
    decay = jnp.exp(-jnp.exp(w_log))
    a = _sigmoid(a0_ref[...] + _bdot(ad, aup_ref[...]))
    g = _bdot(_sigmoid(gd), gup_ref[...])
    kk = k * kk_ref[...]
    kk = kk / jnp.maximum(jnp.sqrt(_segsum(kk * kk, RW_HD)), 1e-12)
    k = k * (1.0 + (a - 1.0) * ka_ref[...])
    r_o[...] = r
    w_o[...] = decay
    k_o[...] = k
    kk_o[...] = kk
    b_o[...] = kk * a
    v_o[...] = v
    g_o[...] = g
    bonus_o[...] = _segsum(r * k * rk_ref[...], RW_HD) * v


def _rwkv_prep(z, shift0, p, row0, n_seq, seq_len, bb, lt):
    rows = bb * lt
    nl = seq_len // lt
    assert bb == 1 or nl == 1
    rb0 = row0 // rows
    m = n_seq * seq_len
    fixed = lambda b, l: (0, 0)
    outmap = lambda b, l: (b * nl + l, 0)
    vec = lambda width: pl.BlockSpec((1, width), fixed)
    ospec = pl.BlockSpec((rows, MIX_W), outmap)
    oshape = jax.ShapeDtypeStruct((m, MIX_W), F32)
    return pl.pallas_call(
        functools.partial(_rwkv_prep_kernel, bb, lt),
        grid=(n_seq // bb, nl),
        in_specs=[pl.BlockSpec((rows, Z_RWB), lambda b, l: (rb0 + b * nl + l, Z_RW // Z_RWB)),
                  pl.BlockSpec((bb, 1, RW_COLS), lambda b, l: (b, 0, 0)),
                  vec(RW_COLS), vec(MIX_W), pl.BlockSpec((RW_LORA[0], MIX_W), fixed), vec(MIX_W),
                  pl.BlockSpec((RW_LORA[1], MIX_W), fixed), pl.BlockSpec((RW_LORA[2], MIX_W), fixed),
                  vec(MIX_W), vec(MIX_W), vec(MIX_W)],
        out_specs=[ospec] * 8,
        out_shape=[oshape] * 8,
        scratch_shapes=[pltpu.VMEM((1, RW_COLS), F32)],
        compiler_params=_cparams("parallel", "arbitrary"),
        name="rwkv_prep",
    )(z, shift0, p['mu'], p['w0'], p['w_up'], p['a0'], p['a_up'], p['g_up'], p['k_k'], p['k_a'], p['r_k'])


def _rwkv_scan_kernel(tc, r_ref, w_ref, k_ref, kk_ref, b_ref, v_ref, s0_ref, y_ref, s_ref, ops_ref):
    @pl.when(pl.program_id(1) == 0)
    def _():
        s_ref[...] = s0_ref[...]

    for i, ref in enumerate((kk_ref, w_ref, b_ref, k_ref, r_ref)):
        x = ref[0]
        ops_ref[i] = jnp.concatenate([x, x], axis=-1)

    def step(t, carry):
        for vp in range(RW_VP):
            s = s_ref[0, vp]
            sa = jnp.sum(s * ops_ref[0, t], axis=0, keepdims=True)
            sn = s * ops_ref[1, t] - sa * ops_ref[2, t] + v_ref[0, t, vp:vp + 1, :] * ops_ref[3, t]
            s_ref[0, vp] = sn
            y_ref[0, t, vp:vp + 1, :] = jnp.sum(sn * ops_ref[4, t], axis=0, keepdims=True)
        return carry

    lax.fori_loop(0, tc, step, 0)


def _rwkv_scan(r, w, k, kk, b, v, s0, tc):
    g, l = r.shape[0], r.shape[1]
    kspec = pl.BlockSpec((1, tc, RW_HD, RW_CHAINS), lambda i, c: (i, c, 0, 0))
    vspec = pl.BlockSpec((1, tc, RW_VP, LANES), lambda i, c: (i, c, 0, 0))
    sspec = pl.BlockSpec((1, RW_VP, RW_HD, LANES), lambda i, c: (i, 0, 0, 0))
    return pl.pallas_call(
        functools.partial(_rwkv_scan_kernel, tc),
        grid=(g, l // tc),
        in_specs=[kspec] * 5 + [vspec, sspec],
        out_specs=[vspec, sspec],
        out_shape=[jax.ShapeDtypeStruct((g, l, RW_VP, LANES), F32),
                   jax.ShapeDtypeStruct((g, RW_VP, RW_HD, LANES), F32)],
        scratch_shapes=[pltpu.VMEM((5, tc, RW_HD, LANES), F32)],
        compiler_params=_cparams("parallel", "arbitrary"),
        name="rwkv_scan",
    )(r, w, k, kk, b, v, s0)


def _rwkv_post_kernel(y_ref, g_ref, bonus_ref, gng_ref, gnb_ref, o_ref):
    y = y_ref[...]
    mean = _segsum(y, RW_HD) * (1.0 / RW_HD)
    yc = y - mean
    var = _segsum(yc * yc, RW_HD) * (1.0 / RW_HD)
    yn = yc * lax.rsqrt(var + RW_GN_EPS) * gng_ref[...] + gnb_ref[...]
    o_ref[...] = ((yn + bonus_ref[...]) * g_ref[...]).astype(o_ref.dtype)


def _rwkv_post(y, g, bonus, gn_g, gn_b, row0, total_rows, prev, tm=256):
    m = y.shape[0]
    rb0 = row0 // tm
    fixed = lambda i: (0, 0)
    spec = pl.BlockSpec((tm, MIX_W), lambda i: (i, 0))
    kern, prev_specs, prev_args, aliases = _carry_through(_rwkv_post_kernel, 5, [prev])
    return pl.pallas_call(
        kern,
        grid=(m // tm,),
        in_specs=[spec, spec, spec, pl.BlockSpec((1, MIX_W), fixed), pl.BlockSpec((1, MIX_W), fixed)] + prev_specs,
        out_specs=pl.BlockSpec((tm, MIX_W), lambda i: (rb0 + i, 0)),
        out_shape=jax.ShapeDtypeStruct((total_rows, MIX_W), BF16),
        input_output_aliases=aliases,
        compiler_params=_cparams("parallel"),
        name="rwkv_post",
    )(y, g, bonus, gn_g, gn_b, *prev_args)


def _to_scan_keys(a, n_seq, seq_len):
    nb = RW_CHAINS // RW_HEADS
    g = n_seq // nb
    return a.reshape(g, nb, seq_len, RW_HEADS, RW_HD).transpose(0, 2, 4, 1, 3).reshape(g, seq_len, RW_HD, RW_CHAINS)


def _to_scan_vals(a, n_seq, seq_len):
    nb = RW_CHAINS // RW_HEADS
    g = n_seq // nb
    a = a.reshape(g, nb, seq_len, RW_HEADS, 2, RW_VP).transpose(0, 2, 5, 4, 1, 3)
    return a.reshape(g, seq_len, RW_VP, LANES)


def _from_scan_vals(y, n_seq, seq_len):
    nb = RW_CHAINS // RW_HEADS
    g = n_seq // nb
    y = y.reshape(g, seq_len, RW_VP, 2, nb, RW_HEADS).transpose(0, 4, 1, 5, 3, 2)
    return y.reshape(n_seq * seq_len, MIX_W)


def _state_to_scan(s, n_seq):
    nb = RW_CHAINS // RW_HEADS
    g = n_seq // nb
    s = s.reshape(g, nb, RW_HEADS, 2, RW_VP, RW_HD).transpose(0, 4, 5, 3, 1, 2)
    return s.reshape(g, RW_VP, RW_HD, LANES)


def _state_from_scan(s, n_seq):
    nb = RW_CHAINS // RW_HEADS
    g = n_seq // nb
    s = s.reshape(g, RW_VP, RW_HD, 2, nb, RW_HEADS).transpose(0, 4, 5, 3, 1, 2)
    return s.reshape(n_seq, RW_HEADS, RW_HD, RW_HD)


def _rwkv_branch(z, shift0, s0, p, row0, n_seq, seq_len, bb, lt, tc, prev):
    r, w, k, kk, b, v, g, bonus = _rwkv_prep(z, shift0, p, row0, n_seq, seq_len, bb, lt)
    keys = [_to_scan_keys(a, n_seq, seq_len) for a in (r, w, k, kk, b)]
    y, s_new = _rwkv_scan(*keys, _to_scan_vals(v, n_seq, seq_len), _state_to_scan(s0, n_seq), tc)
    y = _from_scan_vals(y, n_seq, seq_len)
    y = _rwkv_post(y, g, bonus, p['gn_g'], p['gn_b'], row0, z.shape[0], prev)
    return y, _state_from_scan(s_new, n_seq)


def _seq_masks(rows, lc):
    ri = lax.broadcasted_iota(jnp.int32, (rows, rows), 0)
    ci = lax.broadcasted_iota(jnp.int32, (rows, rows), 1)
    same = (ri // lc) == (ci // lc)
    return ri, ci, same


def _rows_from_seq(x, bb, lc):
    return jnp.broadcast_to(x, (bb, lc, x.shape[-1])).reshape(bb * lc, x.shape[-1])


def _last_of_seq(x, bb, lc):
    x3 = x.reshape(bb, lc, x.shape[-1])
    pick = (lax.broadcasted_iota(jnp.int32, (1, lc, 1), 1) == lc - 1).astype(F32)
    return jnp.sum(x3 * pick, axis=1, keepdims=True)


def _transpose_rows(x):
    rows = x.shape[0]
    if rows < LANES:
        x = jnp.concatenate([x, jnp.zeros((LANES - rows, LANES), x.dtype)], axis=0)
    return x.T[:, :rows]


def _mlstm_kernel(bb, lc, q_ref, k_ref, v_ref, o_ref, gt_ref, gb_ref, ng_ref, c0_ref, n0_ref, m0_ref,
                  y_ref, c_ref, n_ref, m_ref):
    rows = bb * lc

    @pl.when(pl.program_id(1) == 0)
    def _():
        c_ref[...] = c0_ref[...]
        n_ref[...] = n0_ref[...]
        m_ref[...] = m0_ref[...]

    gt = ML_GATE_CAP * jnp.tanh((gt_ref[...] + gb_ref[...]) * (1.0 / ML_GATE_CAP))
    lane = lax.broadcasted_iota(jnp.int32, (rows, LANES), 1)
    ri, ci, same = _seq_masks(rows, lc)
    causal = same & (ci <= ri)
    cum = _dot_exact_lhs(causal.astype(F32), -_softplus(-gt))
    gt_t = _transpose_rows(gt)
    cum_t = _transpose_rows(cum)
    for hd in range(ML_HEADS):
        sl = slice(hd * ML_HD, (hd + 1) * ML_HD)
        c0 = c_ref[0, :, hd]
        n0 = n_ref[0, :, hd:hd + 1, :]
        m0 = m_ref[0, :, :, hd:hd + 1]
        q = q_ref[:, sl]
        k = k_ref[:, sl] * (ML_HD ** -0.5)
        v = v_ref[:, sl]
        bcol = cum[:, ML_HEADS + hd:ML_HEADS + hd + 1]
        igcol = gt[:, hd:hd + 1]
        rowb = gt_t[hd:hd + 1, :] - cum_t[ML_HEADS + hd:ML_HEADS + hd + 1, :]
        dlog = jnp.where(causal, bcol + rowb, -jnp.inf)
        m_inter = bcol + _rows_from_seq(m0, bb, lc)
        m_t = jnp.maximum(m_inter, jnp.max(dlog, axis=-1, keepdims=True))
        qk = lax.dot_general(q.astype(BF16), k.astype(BF16), (((1,), (1,)), ((), ())),
                             preferred_element_type=F32)
        wts = jnp.exp(dlog - m_t) * qk
        s_inter = jnp.exp(m_inter - m_t)
        q3 = q.reshape(bb, lc, ML_HD).astype(BF16)
        qc = jnp.einsum('btd,bde->bte', q3, c0.astype(BF16), preferred_element_type=F32).reshape(rows, ML_HD)
        num = _bdot(wts, v) + s_inter * qc
        qn = jnp.sum(q * _rows_from_seq(n0, bb, lc), axis=-1, keepdims=True)
        den = jnp.sum(wts, axis=-1, keepdims=True) + s_inter * qn
        hh = num / jnp.maximum(jnp.abs(den), jnp.exp(-m_t))
        hh = hh * lax.rsqrt(jnp.mean(hh * hh, axis=-1, keepdims=True) + EPS) * ng_ref[:, sl]
        y_ref[:, sl] = (_sigmoid(o_ref[:, sl]) * hh).astype(y_ref.dtype)

        stat = jnp.where(lane == 0, bcol, jnp.where(lane == 1, m_t, 0.0))
        last = _last_of_seq(stat, bb, lc)
        last_rows = _rows_from_seq(last, bb, lc)
        b_last, m_new = last_rows[:, 0:1], last_rows[:, 1:2]
        w_end = jnp.exp(b_last - bcol + igcol - m_new)
        f_end = jnp.exp(last[:, :, 0:1] + m0 - last[:, :, 1:2])
        kw3 = (k * w_end).reshape(bb, lc, ML_HD)
        upd = jnp.einsum('bsd,bse->bde', kw3.astype(BF16), v.reshape(bb, lc, ML_HD).astype(BF16),
                         preferred_element_type=F32)
        c_ref[0, :, hd] = f_end * c0 + upd
        n_ref[0, :, hd:hd + 1, :] = f_end * n0 + jnp.sum(kw3, axis=1, keepdims=True)
        m_ref[0, :, :, hd:hd + 1] = last[:, :, 1:2]


def _mlstm_branch(z, init, lyr_in, gate_bias, norm_g, row0, n_seq, seq_len, bb, lc, depth, lyr, prev):
    rows = bb * lc
    nc = seq_len // lc
    rb0 = row0 // rows
    rowmap = lambda part: (lambda b, c: (rb0 + b * nc + c, Z_ML // MIX_W + part))
    zspec = lambda part: pl.BlockSpec((rows, MIX_W), rowmap(part))
    state_specs = lambda at: [pl.BlockSpec((1, bb, ML_HEADS, ML_HD, ML_HD), lambda b, c: (at, b, 0, 0, 0)),
                              pl.BlockSpec((1, bb, ML_HEADS, ML_HD), lambda b, c: (at, b, 0, 0)),
                              pl.BlockSpec((1, bb, 1, ML_HEADS), lambda b, c: (at, b, 0, 0))]
    kern, prev_specs, prev_args, aliases = _carry_through(functools.partial(_mlstm_kernel, bb, lc), 10, prev)
    return pl.pallas_call(
        kern,
        grid=(n_seq // bb, nc),
        in_specs=[zspec(0), zspec(1), zspec(2), zspec(3),
                  pl.BlockSpec((rows, LANES), lambda b, c: (rb0 + b * nc + c, Z_MLG // LANES)),
                  pl.BlockSpec((1, LANES), lambda b, c: (0, 0)), pl.BlockSpec((1, MIX_W), lambda b, c: (0, 0))]
                 + state_specs(lyr_in) + prev_specs,
        out_specs=[pl.BlockSpec((rows, MIX_W), lambda b, c: (rb0 + b * nc + c, 0))] + state_specs(lyr),
        out_shape=[jax.ShapeDtypeStruct((z.shape[0], MIX_W), BF16),
                   jax.ShapeDtypeStruct((depth, n_seq, ML_HEADS, ML_HD, ML_HD), F32),
                   jax.ShapeDtypeStruct((depth, n_seq, ML_HEADS, ML_HD), F32),
                   jax.ShapeDtypeStruct((depth, n_seq, 1, ML_HEADS), F32)],
        input_output_aliases=aliases,
        compiler_params=_cparams("parallel", "arbitrary"),
        name="mlstm_chunk",
    )(z, z, z, z, z, gate_bias, norm_g, *init, *prev_args)


def _rope_rows(x, cos, sin):
    half = RT_HD // 2
    x1, x2 = x[:, :half], x[:, half:]
    return jnp.concatenate([x1 * cos - x2 * sin, x1 * sin + x2 * cos], axis=1)


def _ret_kernel(bb, lc, q_ref, k_ref, v_ref, g_ref, cs_ref, dec_ref, cd_ref, s0_ref, y_ref, s_ref):
    rows = bb * lc

    @pl.when(pl.program_id(1) == 0)
    def _():
        s_ref[...] = s0_ref[...]

    cos, sin = cs_ref[:, :RT_HD // 2], cs_ref[:, RT_HD // 2:]
    for hd in range(RT_HEADS):
        sl = slice(hd * RT_HD, (hd + 1) * RT_HD)
        s0 = s_ref[0, :, hd]
        q = _rope_rows(q_ref[:, sl], cos, sin)
        k = _rope_rows(k_ref[:, sl], cos, sin) * (RT_HD ** -0.5)
        v = v_ref[:, sl]
        cd = cd_ref[hd]
        qk = lax.dot_general(q.astype(BF16), k.astype(BF16), (((1,), (1,)), ((), ())),
                             preferred_element_type=F32)
        inner = _bdot(qk * dec_ref[hd], v)
        q3 = q.reshape(bb, lc, RT_HD).astype(BF16)
        cross = jnp.einsum('btd,bde->bte', q3, s0.astype(BF16), preferred_element_type=F32).reshape(rows, RT_HD)
        y = inner + cross * cd[:, 0:1]
        y = y * lax.rsqrt(jnp.mean(y * y, axis=-1, keepdims=True) + EPS)
        gt = g_ref[:, sl]
        y_ref[:, sl] = (gt * _sigmoid(gt) * y).astype(y_ref.dtype)
        kd = (k * cd[:, 1:2]).reshape(bb, lc, RT_HD).astype(BF16)
        upd = jnp.einsum('bsd,bse->bde', kd, v.reshape(bb, lc, RT_HD).astype(BF16), preferred_element_type=F32)
        s_ref[0, :, hd] = cd[0:1, 2:3] * s0 + upd


def _ret_branch(z, s0, lyr_in, row0, n_seq, seq_len, bb, lc, pos0, depth, lyr, prev):
    rows = bb * lc
    nc = seq_len // lc
    rb0 = row0 // rows
    half = RT_HD // 2
    inv = ROPE_BASE ** (-jnp.arange(half, dtype=F32) / half)
    pos = jnp.broadcast_to((jnp.arange(nc * lc, dtype=F32) + float(pos0)).reshape(nc, 1, lc), (nc, bb, lc))
    ang = pos.reshape(nc * rows)[:, None] * inv[None, :]
    cs = jnp.concatenate([jnp.cos(ang), jnp.sin(ang)], axis=1)
    log_g = jnp.log(1.0 - jnp.exp(jnp.linspace(math.log(1.0 / 32), math.log(1.0 / 512), RT_HEADS)))
    idx = jnp.arange(rows)
    t = (idx % lc).astype(F32)
    diff = t[:, None] - t[None, :]
    same = (idx[:, None] // lc) == (idx[None, :] // lc)
    dec = jnp.where(same & (diff >= 0), jnp.exp(log_g[:, None, None] * jnp.maximum(diff, 0.0)), 0.0)
    cd = jnp.zeros((RT_HEADS, rows, LANES), F32)
    cd = cd.at[:, :, 0].set(jnp.exp(log_g[:, None] * (t[None, :] + 1.0)))
    cd = cd.at[:, :, 1].set(jnp.exp(log_g[:, None] * (lc - 1.0 - t)[None, :]))
    cd = cd.at[:, :, 2].set(jnp.broadcast_to(jnp.exp(log_g * lc)[:, None], (RT_HEADS, rows)))
    rowmap = lambda part: (lambda b, c: (rb0 + b * nc + c, Z_RT // MIX_W + part))
    zspec = lambda part: pl.BlockSpec((rows, MIX_W), rowmap(part))
    sspec = lambda at: pl.BlockSpec((1, bb, RT_HEADS, RT_HD, RT_HD), lambda b, c: (at, b, 0, 0, 0))
    kern, prev_specs, prev_args, aliases = _carry_through(functools.partial(_ret_kernel, bb, lc), 8, prev)
    return pl.pallas_call(
        kern,
        grid=(n_seq // bb, nc),
        in_specs=[zspec(0), zspec(1), zspec(2), zspec(3),
                  pl.BlockSpec((rows, RT_HD), lambda b, c: (c, 0)),
                  pl.BlockSpec((RT_HEADS, rows, rows), lambda b, c: (0, 0, 0)),
                  pl.BlockSpec((RT_HEADS, rows, LANES), lambda b, c: (0, 0, 0)),
                  sspec(lyr_in)] + prev_specs,
        out_specs=[pl.BlockSpec((rows, MIX_W), lambda b, c: (rb0 + b * nc + c, 0)), sspec(lyr)],
        out_shape=[jax.ShapeDtypeStruct((z.shape[0], MIX_W), BF16),
                   jax.ShapeDtypeStruct((depth, n_seq, RT_HEADS, RT_HD, RT_HD), F32)],
        input_output_aliases=aliases,
        compiler_params=_cparams("parallel", "arbitrary"),
        name="retention_chunk",
    )(z, z, z, z, cs, dec, cd, s0, *prev_args)


def _pack_w_in(w_in):
    d = w_in.shape[0]
    ml0 = RW_COLS
    mlg0 = ml0 + 4 * MIX_W
    rt0 = mlg0 + 2 * ML_HEADS
    pad = jnp.zeros((d, Z_COLS - Z_MLG - 2 * ML_HEADS), w_in.dtype)
    packed = jnp.concatenate([w_in[:, ml0:mlg0], w_in[:, rt0:], w_in[:, :RW_COLS], w_in[:, mlg0:rt0], pad], axis=1)
    return packed.astype(BF16)


def _group_cfg(n_seq, seq_len):
    if seq_len % CHUNK == 0:
        return dict(rw_bb=1, rw_lt=256, rw_tc=32, ch_bb=1, ch_lc=CHUNK, at_bb=1, at_lq=512)
    return dict(rw_bb=32, rw_lt=seq_len, rw_tc=seq_len, ch_bb=4, ch_lc=seq_len, at_bb=8, at_lq=seq_len)


def kernel(x_prompt, x_sample, mem_prompt, state_rwkv_shift, state_rwkv, state_mlstm_c, state_mlstm_n,
           state_mlstm_m, state_ret, cache_mem_k, cache_mem_v, g_pre_mix, g_post_mix, g_pre_x, g_post_x,
           g_pre_ff, g_post_ff, g_mem, w_in, rw_mu, rw_w0, rw_w_up, rw_a0, rw_a_up, rw_g_up, rw_k_k, rw_k_a,
           rw_r_k, rw_gn_g, rw_gn_b, ml_i_b, ml_f_b, ml_norm_g, w_br, w_out, x_wq, x_wkv, x_wo, ff_w1, ff_w2):
    bp, lp, d = x_prompt.shape
    bs, ls, _ = x_sample.shape
    depth = w_in.shape[0]
    n_mem = mem_prompt.shape[1]
    tp, ts = bp * lp, bs * ls
    groups = [(0, bp, lp, 0), (tp, bs, ls, PAST_LEN)]

    x = jnp.concatenate([x_prompt.reshape(tp, d), x_sample.reshape(ts, d)], axis=0)
    mem = mem_prompt.reshape(bp * n_mem, d)
    h = _rmsnorm(x, g_pre_mix[0])

    zeros = lambda *s: jnp.zeros(s, F32)
    inits = [
        lambda l: (zeros(bp, 1, RW_COLS), zeros(bp, RW_HEADS, RW_HD, RW_HD),
                   (zeros(1, bp, ML_HEADS, ML_HD, ML_HD), zeros(1, bp, ML_HEADS, ML_HD), zeros(1, bp, 1, ML_HEADS)),
                   zeros(1, bp, RT_HEADS, RT_HD, RT_HD), 0),
        lambda l: (state_rwkv_shift[l].reshape(bs, 1, RW_COLS), state_rwkv[l],
                   (state_mlstm_c, state_mlstm_n, state_mlstm_m.reshape(depth, bs, 1, ML_HEADS)), state_ret, l),
    ]
    rw_states = [[], []]
    ml_states = [[None] * 3, [None] * 3]
    rt_states = [None, None]
    mem_k, mem_v = [], []
    for l in range(depth):
        kv = _mm(_rmsnorm(mem, g_mem[l], tm=256), x_wkv[l].astype(BF16), tm=256, tn=512)
        mem_k.append(kv[:, :X_W].reshape(bp, n_mem, X_HEADS, X_HD))
        mem_v.append(kv[:, X_W:].reshape(bp, n_mem, X_HEADS, X_HD))

        z = _mm(h, _pack_w_in(w_in[l]), tm=1024, tn=1280)

        rw_p = dict(mu=rw_mu[l].reshape(1, -1), w0=rw_w0[l].reshape(1, -1), w_up=rw_w_up[l].astype(BF16),
                    a0=rw_a0[l].reshape(1, -1), a_up=rw_a_up[l].astype(BF16), g_up=rw_g_up[l].astype(BF16),
                    k_k=rw_k_k[l].reshape(1, -1), k_a=rw_k_a[l].reshape(1, -1), r_k=rw_r_k[l].reshape(1, -1),
                    gn_g=rw_gn_g[l].reshape(1, -1), gn_b=rw_gn_b[l].reshape(1, -1))
        gate_bias = jnp.concatenate([ml_i_b[l], ml_f_b[l], zeros(LANES - 2 * ML_HEADS)]).reshape(1, LANES)
        norm_g = ml_norm_g[l].reshape(1, MIX_W)

        y_rw = y_ml = y_rt = None
        for gi, (row0, n_seq, seq_len, pos0) in enumerate(groups):
            cfg = _group_cfg(n_seq, seq_len)
            shift0, s_rw0, ml0, rt0, lyr_in = inits[gi](l)
            y_rw, s_rw = _rwkv_branch(z, shift0, s_rw0, rw_p, row0, n_seq, seq_len,
                                      cfg['rw_bb'], cfg['rw_lt'], cfg['rw_tc'], y_rw)
            y_ml, *ml_states[gi] = _mlstm_branch(z, ml0, lyr_in, gate_bias, norm_g, row0, n_seq, seq_len,
                                                 cfg['ch_bb'], cfg['ch_lc'], depth, l, [y_ml] + ml_states[gi])
            y_rt, rt_states[gi] = _ret_branch(z, rt0, lyr_in, row0, n_seq, seq_len, cfg['ch_bb'], cfg['ch_lc'],
                                              pos0, depth, l, [y_rt, rt_states[gi]])
            last = row0 + seq_len - 1
            shift = lax.slice(z, (last, Z_RW), (last + (n_seq - 1) * seq_len + 1, Z_RW + RW_COLS), (seq_len, 1))
            rw_states[gi].append((shift, s_rw))

        merged = _merge(y_rw, y_ml, y_rt, w_br[l].astype(BF16), z)
        x, h = _proj_res(merged, w_out[l].astype(BF16), x, g_post_mix[l], g_pre_x[l])

        wq, wo = x_wq[l].astype(BF16), x_wo[l].astype(BF16)
        for gi, (row0, n_seq, seq_len, pos0) in enumerate(groups):
            cfg = _group_cfg(n_seq, seq_len)
            if gi == 0:
                mk, mv, lyr_in = mem_k[l][None], mem_v[l][None], 0
            else:
                mk, mv, lyr_in = cache_mem_k, cache_mem_v, l
            x, h = _attn(h, x, mk, mv, lyr_in, wq, wo, g_post_x[l], g_pre_ff[l], row0, n_seq, seq_len,
                         cfg['at_bb'], cfg['at_lq'])

        g_next = g_pre_mix[(l + 1) % depth]
        x, h = _mlp(h, ff_w1[l].astype(BF16), ff_w2[l].astype(BF16), x, g_post_ff[l], g_next)

    def states(gi, n_seq):
        c_new, n_new, m_new = ml_states[gi]
        return (jnp.stack([s[0] for s in rw_states[gi]]), jnp.stack([s[1] for s in rw_states[gi]]),
                c_new, n_new, m_new.reshape(depth, n_seq, ML_HEADS), rt_states[gi])

    return (x[:tp].reshape(bp, lp, d), x[tp:].reshape(bs, ls, d), *states(0, bp),
            jnp.stack(mem_k), jnp.stack(mem_v), *states(1, bs))
```

```python
import functools
import math

import jax
import jax.numpy as jnp
from jax import lax
from jax.experimental import pallas as pl
from jax.experimental.pallas import tpu as pltpu

F32 = jnp.float32
BF16 = jnp.bfloat16

D_MODEL = 2048
MIX_W = D_MODEL // 2
RW_HD = 64
RW_HEADS = MIX_W // RW_HD
RW_LORA = (64, 64, 128)
RW_COLS = 3 * MIX_W + sum(RW_LORA)
RW_GN_EPS = 64e-5
ML_HEADS = 8
ML_HD = MIX_W // ML_HEADS
ML_GATE_CAP = 15.0
RT_HEADS = 4
RT_HD = MIX_W // RT_HEADS
ROPE_BASE = 10000.0
X_HEADS = 4
X_HD = 128
X_W = X_HEADS * X_HD
D_FF = 4 * D_MODEL
CHUNK = 64
EPS = 1e-6
PAST_LEN = 16384

Z_ML = 0
Z_RT = Z_ML + 4 * MIX_W
Z_GATE = Z_RT + 4 * MIX_W
Z_RW = Z_GATE + 3 * D_MODEL
Z_MLG = Z_RW + RW_COLS
Z_RWB = RW_COLS + 256
Z_COLS = Z_MLG + 256

LANES = 128
RW_VP = RW_HD // 2
RW_CHAINS = LANES // 2


def _cparams(*sem):
    return pltpu.CompilerParams(dimension_semantics=sem, vmem_limit_bytes=56 * 1024 * 1024)


def _carry_through(kernel_fn, n_in, prev):
    carried = [(i, p) for i, p in enumerate(prev) if p is not None]
    if not carried:
        return kernel_fn, [], [], {}

    def wrapped(*refs):
        return kernel_fn(*refs[:n_in], *refs[n_in + len(carried):])

    return (wrapped, [pl.BlockSpec(memory_space=pl.ANY)] * len(carried), [p for _, p in carried],
            {n_in + j: i for j, (i, _) in enumerate(carried)})


def _sigmoid(x):
    return 1.0 / (1.0 + jnp.exp(-x))


def _softplus(x):
    return jnp.maximum(x, 0.0) + jnp.log(1.0 + jnp.exp(-jnp.abs(x)))


def _bdot(a, b):
    return jnp.dot(a.astype(BF16), b.astype(BF16), preferred_element_type=F32)


def _split3(x):
    hi = x.astype(BF16)
    r1 = x - hi.astype(F32)
    mid = r1.astype(BF16)
    lo = (r1 - mid.astype(F32)).astype(BF16)
    return hi, mid, lo


def _dot_exact_lhs(m, x):
    mb = m.astype(BF16)
    hi, mid, lo = _split3(x)
    return (jnp.dot(mb, hi, preferred_element_type=F32) + jnp.dot(mb, mid, preferred_element_type=F32)
            + jnp.dot(mb, lo, preferred_element_type=F32))


def _dot_exact_rhs(x, m):
    mb = m.astype(BF16)
    hi, mid, lo = _split3(x)
    return (jnp.dot(hi, mb, preferred_element_type=F32) + jnp.dot(mid, mb, preferred_element_type=F32)
            + jnp.dot(lo, mb, preferred_element_type=F32))


def _segsum(x, seg):
    blk = 256
    ri = lax.broadcasted_iota(jnp.int32, (blk, blk), 0) // seg
    ci = lax.broadcasted_iota(jnp.int32, (blk, blk), 1) // seg
    ones = (ri == ci).astype(F32)
    parts = [_dot_exact_rhs(x[:, c:c + blk], ones) for c in range(0, x.shape[1], blk)]
    return parts[0] if len(parts) == 1 else jnp.concatenate(parts, axis=1)


def _rms_rows(x, g):
    return x * lax.rsqrt(jnp.mean(x * x, axis=-1, keepdims=True) + EPS) * g


def _rmsnorm_kernel(x_ref, g_ref, o_ref):
    o_ref[...] = _rms_rows(x_ref[...], g_ref[...]).astype(o_ref.dtype)


def _rmsnorm(x, g, tm=512):
    m, d = x.shape
    return pl.pallas_call(
        _rmsnorm_kernel,
        grid=(m // tm,),
        in_specs=[pl.BlockSpec((tm, d), lambda i: (i, 0)), pl.BlockSpec((1, d), lambda i: (0, 0))],
        out_specs=pl.BlockSpec((tm, d), lambda i: (i, 0)),
        out_shape=jax.ShapeDtypeStruct((m, d), BF16),
        compiler_params=_cparams("parallel"),
        name="rmsnorm",
    )(x, g.reshape(1, d))


def _mm_kernel(x_ref, w_ref, o_ref):
    o_ref[...] = jnp.dot(x_ref[...], w_ref[...], preferred_element_type=F32)


def _mm(x, w, tm, tn):
    m, k = x.shape
    n = w.shape[1]
    return pl.pallas_call(
        _mm_kernel,
        grid=(n // tn, m // tm),
        in_specs=[pl.BlockSpec((tm, k), lambda j, i: (i, 0)), pl.BlockSpec((k, tn), lambda j, i: (0, j))],
        out_specs=pl.BlockSpec((tm, tn), lambda j, i: (i, j)),
        out_shape=jax.ShapeDtypeStruct((m, n), F32),
        compiler_params=_cparams("parallel", "parallel"),
        name="matmul",
    )(x, w)


def _merge_kernel(y0_ref, y1_ref, y2_ref, w_ref, g0_ref, g1_ref, g2_ref, o_ref):
    acc = _sigmoid(g0_ref[...]) * jnp.dot(y0_ref[...], w_ref[0], preferred_element_type=F32)
    acc += _sigmoid(g1_ref[...]) * jnp.dot(y1_ref[...], w_ref[1], preferred_element_type=F32)
    acc += _sigmoid(g2_ref[...]) * jnp.dot(y2_ref[...], w_ref[2], preferred_element_type=F32)
    o_ref[...] = acc.astype(o_ref.dtype)


def _merge(y_rw, y_ml, y_rt, w_br, z, tm=512, tn=512):
    m, kw = y_rw.shape
    d = w_br.shape[2]
    gb = Z_GATE // tn
    gspec = lambda c: pl.BlockSpec((tm, tn), lambda j, i: (i, gb + c * (d // tn) + j))
    yspec = pl.BlockSpec((tm, kw), lambda j, i: (i, 0))
    return pl.pallas_call(
        _merge_kernel,
        grid=(d // tn, m // tm),
        in_specs=[yspec, yspec, yspec, pl.BlockSpec((3, kw, tn), lambda j, i: (0, 0, j)),
                  gspec(0), gspec(1), gspec(2)],
        out_specs=pl.BlockSpec((tm, tn), lambda j, i: (i, j)),
        out_shape=jax.ShapeDtypeStruct((m, d), BF16),
        compiler_params=_cparams("parallel", "parallel"),
        name="branch_merge",
    )(y_rw, y_ml, y_rt, w_br, z, z, z)


def _residual_epilogue(acc, x_ref, gpost_ref, gnext_ref, xo_ref, ho_ref):
    xn = x_ref[...] + _rms_rows(acc, gpost_ref[...])
    xo_ref[...] = xn
    ho_ref[...] = _rms_rows(xn, gnext_ref[...]).astype(ho_ref.dtype)


def _proj_res_kernel(a_ref, w_ref, x_ref, gpost_ref, gnext_ref, xo_ref, ho_ref):
    acc = jnp.dot(a_ref[...], w_ref[...], preferred_element_type=F32)
    _residual_epilogue(acc, x_ref, gpost_ref, gnext_ref, xo_ref, ho_ref)


def _proj_res(a, w, x, g_post, g_next, tm=512):
    m, k = a.shape
    d = w.shape[1]
    row = lambda i: (i, 0)
    fixed = lambda i: (0, 0)
    return pl.pallas_call(
        _proj_res_kernel,
        grid=(m // tm,),
        in_specs=[pl.BlockSpec((tm, k), row), pl.BlockSpec((k, d), fixed), pl.BlockSpec((tm, d), row),
                  pl.BlockSpec((1, d), fixed), pl.BlockSpec((1, d), fixed)],
        out_specs=[pl.BlockSpec((tm, d), row), pl.BlockSpec((tm, d), row)],
        out_shape=[jax.ShapeDtypeStruct((m, d), F32), jax.ShapeDtypeStruct((m, d), BF16)],
        compiler_params=_cparams("parallel"),
        name="out_proj_residual",
    )(a, w, x, g_post.reshape(1, d), g_next.reshape(1, d))


def _mlp_kernel(h_ref, w1_ref, w2_ref, x_ref, gpost_ref, gnext_ref, xo_ref, ho_ref, acc_ref):
    k = pl.program_id(1)

    @pl.when(k == 0)
    def _():
        acc_ref[...] = jnp.zeros_like(acc_ref)

    u = jnp.dot(h_ref[...], w1_ref[...], preferred_element_type=F32)
    u = jnp.square(jnp.maximum(u, 0.0)).astype(BF16)
    acc_ref[...] += jnp.dot(u, w2_ref[...], preferred_element_type=F32)

    @pl.when(k == pl.num_programs(1) - 1)
    def _():
        _residual_epilogue(acc_ref[...], x_ref, gpost_ref, gnext_ref, xo_ref, ho_ref)


def _mlp(h, w1, w2, x, g_post, g_next, tm=512, tf=1024):
    m, d = h.shape
    f = w1.shape[1]
    row = lambda i, k: (i, 0)
    fixed = lambda i, k: (0, 0)
    return pl.pallas_call(
        _mlp_kernel,
        grid=(m // tm, f // tf),
        in_specs=[pl.BlockSpec((tm, d), row), pl.BlockSpec((d, tf), lambda i, k: (0, k)),
                  pl.BlockSpec((tf, d), lambda i, k: (k, 0)), pl.BlockSpec((tm, d), row),
                  pl.BlockSpec((1, d), fixed), pl.BlockSpec((1, d), fixed)],
        out_specs=[pl.BlockSpec((tm, d), row), pl.BlockSpec((tm, d), row)],
        out_shape=[jax.ShapeDtypeStruct((m, d), F32), jax.ShapeDtypeStruct((m, d), BF16)],
        scratch_shapes=[pltpu.VMEM((tm, d), F32)],
        compiler_params=_cparams("parallel", "arbitrary"),
        name="relu2_mlp",
    )(h, w1, w2, x, g_post.reshape(1, d), g_next.reshape(1, d))


def _attn_kernel(bb, lq, h_ref, wq_ref, mk_ref, mv_ref, wo_ref, x_ref, gpost_ref, gnext_ref, xo_ref, ho_ref):
    q = jnp.dot(h_ref[...], wq_ref[...], preferred_element_type=F32)
    outs = []
    for hd in range(X_HEADS):
        sl = slice(hd * X_HD, (hd + 1) * X_HD)
        q3 = q[:, sl].reshape(bb, lq, X_HD).astype(BF16)
        k3 = mk_ref[0, :, :, hd, :].astype(BF16)
        v3 = mv_ref[0, :, :, hd, :].astype(BF16)
        s = jnp.einsum('bqd,bmd->bqm', q3, k3, preferred_element_type=F32) * (X_HD ** -0.5)
        s = s - jnp.max(s, axis=-1, keepdims=True)
        e = jnp.exp(s)
        p = (e / jnp.sum(e, axis=-1, keepdims=True)).astype(BF16)
        o3 = jnp.einsum('bqm,bmd->bqd', p, v3, preferred_element_type=F32)
        outs.append(o3.reshape(bb * lq, X_HD))
    o = jnp.concatenate(outs, axis=1).astype(BF16)
    acc = jnp.dot(o, wo_ref[...], preferred_element_type=F32)
    _residual_epilogue(acc, x_ref, gpost_ref, gnext_ref, xo_ref, ho_ref)


def _attn(h, x, mk, mv, lyr, wq, wo, g_post, g_next, row0, n_seq, seq_len, bb, lq):
    t, d = h.shape
    rows = bb * lq
    nl = seq_len // lq
    n_mem = mk.shape[2]
    rb0 = row0 // rows
    rowmap = lambda b, l: (rb0 + b * nl + l, 0)
    fixed = lambda b, l: (0, 0)
    kvspec = pl.BlockSpec((1, bb, n_mem, X_HEADS, X_HD), lambda b, l: (lyr, b, 0, 0, 0))
    return pl.pallas_call(
        functools.partial(_attn_kernel, bb, lq),
        grid=(n_seq // bb, nl),
        in_specs=[pl.BlockSpec((rows, d), rowmap), pl.BlockSpec((d, X_W), fixed), kvspec, kvspec,
                  pl.BlockSpec((X_W, d), fixed), pl.BlockSpec((rows, d), rowmap),
                  pl.BlockSpec((1, d), fixed), pl.BlockSpec((1, d), fixed)],
        out_specs=[pl.BlockSpec((rows, d), rowmap), pl.BlockSpec((rows, d), rowmap)],
        out_shape=[jax.ShapeDtypeStruct((t, d), F32), jax.ShapeDtypeStruct((t, d), BF16)],
        input_output_aliases={5: 0, 0: 1},
        compiler_params=_cparams("parallel", "parallel"),
        name="mem_cross_attention",
    )(h, wq, mk, mv, wo, x, g_post.reshape(1, d), g_next.reshape(1, d))


def _rwkv_prep_kernel(bb, lt, u_ref, s0_ref, mu_ref, w0_ref, wup_ref, a0_ref, aup_ref, gup_ref, kk_ref, ka_ref,
                      rk_ref, r_o, w_o, k_o, kk_o, b_o, v_o, g_o, bonus_o, carry_ref):
    l = pl.program_id(1)
    rows = bb * lt
    u = u_ref[:, :RW_COLS]
    if bb == 1:
        first = jnp.where(l == 0, s0_ref[0], carry_ref[...])
        first_rows = jnp.broadcast_to(first, (rows, RW_COLS))
    else:
        first_rows = jnp.broadcast_to(s0_ref[...], (bb, lt, RW_COLS)).reshape(rows, RW_COLS)
    rowid = lax.broadcasted_iota(jnp.int32, (rows, 1), 0) % lt
    prev = jnp.where(rowid == 0, first_rows, pltpu.roll(u, 1, 0))
    if bb == 1:
        carry_ref[...] = u[rows - 1:rows, :]
    z = u + (prev - u) * mu_ref[...]
    r = z[:, 0:MIX_W]
    k = z[:, MIX_W:2 * MIX_W]
    v = z[:, 2 * MIX_W:3 * MIX_W]
    o = 3 * MIX_W
    wd = z[:, o:o + RW_LORA[0]]
    ad = z[:, o + RW_LORA[0]:o + RW_LORA[0] + RW_LORA[1]]
    gd = z[:, o + RW_LORA[0] + RW_LORA[1]:RW_COLS]
    w_log = -_softplus(-(w0_ref[...] + _bdot(jnp.tanh(wd), wup_ref[...]))) - 0.5
    decay = jnp.exp(-jnp.exp(w_log))
    a = _sigmoid(a0_ref[...] + _bdot(ad, aup_ref[...]))
    g = _bdot(_sigmoid(gd), gup_ref[...])
    kk = k * kk_ref[...]
    kk = kk / jnp.maximum(jnp.sqrt(_segsum(kk * kk, RW_HD)), 1e-12)
    k = k * (1.0 + (a - 1.0) * ka_ref[...])
    r_o[...] = r
    w_o[...] = decay
    k_o[...] = k
    kk_o[...] = kk
    b_o[...] = kk * a
    v_o[...] = v
    g_o[...] = g
    bonus_o[...] = _segsum(r * k * rk_ref[...], RW_HD) * v


def _rwkv_prep(z, shift0, p, row0, n_seq, seq_len, bb, lt):
    rows = bb * lt
    nl = seq_len // lt
    assert bb == 1 or nl == 1
    rb0 = row0 // rows
    m = n_seq * seq_len
    fixed = lambda b, l: (0, 0)
    outmap = lambda b, l: (b * nl + l, 0)
    vec = lambda width: pl.BlockSpec((1, width), fixed)
    ospec = pl.BlockSpec((rows, MIX_W), outmap)
    oshape = jax.ShapeDtypeStruct((m, MIX_W), F32)
    return pl.pallas_call(
        functools.partial(_rwkv_prep_kernel, bb, lt),
        grid=(n_seq // bb, nl),
        in_specs=[pl.BlockSpec((rows, Z_RWB), lambda b, l: (rb0 + b * nl + l, Z_RW // Z_RWB)),
                  pl.BlockSpec((bb, 1, RW_COLS), lambda b, l: (b, 0, 0)),
                  vec(RW_COLS), vec(MIX_W), pl.BlockSpec((RW_LORA[0], MIX_W), fixed), vec(MIX_W),
                  pl.BlockSpec((RW_LORA[1], MIX_W), fixed), pl.BlockSpec((RW_LORA[2], MIX_W), fixed),
                  vec(MIX_W), vec(MIX_W), vec(MIX_W)],
        out_specs=[ospec] * 8,
        out_shape=[oshape] * 8,
        scratch_shapes=[pltpu.VMEM((1, RW_COLS), F32)],
        compiler_params=_cparams("parallel", "arbitrary"),
        name="rwkv_prep",
    )(z, shift0, p['mu'], p['w0'], p['w_up'], p['a0'], p['a_up'], p['g_up'], p['k_k'], p['k_a'], p['r_k'])


def _rwkv_scan_kernel(tc, r_ref, w_ref, k_ref, kk_ref, b_ref, v_ref, s0_ref, y_ref, s_ref):
    @pl.when(pl.program_id(1) == 0)
    def _():
        s_ref[...] = s0_ref[...]

    y_ref[0, :, RW_VP:, :] = jnp.zeros((tc, RW_VR - RW_VP, LANES), F32)
    keys = lambda ref, t: ref[0, t, :RW_HD, :]

    def step(t, carry):
        for vp in range(RW_VP):
            s = s_ref[0, vp]
            sa = jnp.sum(s * keys(kk_ref, t), axis=0, keepdims=True)
            sn = s * keys(w_ref, t) - sa * keys(b_ref, t) + v_ref[0, t, vp:vp + 1, :] * keys(k_ref, t)
            s_ref[0, vp] = sn
            y_ref[0, t, vp:vp + 1, :] = jnp.sum(sn * keys(r_ref, t), axis=0, keepdims=True)
        return carry

    lax.fori_loop(0, tc, step, 0)


def _rwkv_scan(r, w, k, kk, b, v, s0, tc):
    g, l = r.shape[0], r.shape[1]
    kspec = pl.BlockSpec((1, tc, RW_KR, LANES), lambda i, c: (i, c, 0, 0))
    vspec = pl.BlockSpec((1, tc, RW_VR, LANES), lambda i, c: (i, c, 0, 0))
    sspec = pl.BlockSpec((1, RW_VP, RW_HD, LANES), lambda i, c: (i, 0, 0, 0))
    return pl.pallas_call(
        functools.partial(_rwkv_scan_kernel, tc),
        grid=(g, l // tc),
        in_specs=[kspec] * 5 + [vspec, sspec],
        out_specs=[vspec, sspec],
        out_shape=[jax.ShapeDtypeStruct((g, l, RW_VR, LANES), F32),
                   jax.ShapeDtypeStruct((g, RW_VP, RW_HD, LANES), F32)],
        compiler_params=_cparams("parallel", "arbitrary"),
        name="rwkv_scan",
    )(r, w, k, kk, b, v, s0)


def _rwkv_post_kernel(y_ref, g_ref, bonus_ref, gng_ref, gnb_ref, o_ref):
    y = y_ref[...]
    mean = _segsum(y, RW_HD) * (1.0 / RW_HD)
    yc = y - mean
    var = _segsum(yc * yc, RW_HD) * (1.0 / RW_HD)
    yn = yc * lax.rsqrt(var + RW_GN_EPS) * gng_ref[...] + gnb_ref[...]
    o_ref[...] = ((yn + bonus_ref[...]) * g_ref[...]).astype(o_ref.dtype)


def _rwkv_post(y, g, bonus, gn_g, gn_b, row0, total_rows, prev, tm=256):
    m = y.shape[0]
    rb0 = row0 // tm
    fixed = lambda i: (0, 0)
    spec = pl.BlockSpec((tm, MIX_W), lambda i: (i, 0))
    kern, prev_specs, prev_args, aliases = _carry_through(_rwkv_post_kernel, 5, [prev])
    return pl.pallas_call(
        kern,
        grid=(m // tm,),
        in_specs=[spec, spec, spec, pl.BlockSpec((1, MIX_W), fixed), pl.BlockSpec((1, MIX_W), fixed)] + prev_specs,
        out_specs=pl.BlockSpec((tm, MIX_W), lambda i: (rb0 + i, 0)),
        out_shape=jax.ShapeDtypeStruct((total_rows, MIX_W), BF16),
        input_output_aliases=aliases,
        compiler_params=_cparams("parallel"),
        name="rwkv_post",
    )(y, g, bonus, gn_g, gn_b, *prev_args)


RW_NB = RW_CHAINS // RW_HEADS
RW_TT = LANES


RW_KR = RW_HD + 8
RW_VR = RW_VP + 8


def _natural_to_chain_rows(x_ref, a_ref):
    for bl in range(RW_NB):
        xt = x_ref[bl].T
        for h in range(RW_HEADS):
            c = bl * RW_HEADS + h
            a_ref[c * RW_KR:c * RW_KR + RW_HD, :] = xt[h * RW_HD:(h + 1) * RW_HD]


def _keys_to_scan_kernel(x_ref, o_ref, a_ref):
    _natural_to_chain_rows(x_ref, a_ref)
    o_ref[...] = jnp.zeros_like(o_ref)
    for k in range(RW_HD):
        tile = a_ref[pl.ds(k, RW_CHAINS, stride=RW_KR), :]
        o_ref[0, pl.ds(k, RW_TT, stride=RW_KR), :] = jnp.concatenate([tile, tile], axis=0).T


def _vals_to_scan_kernel(x_ref, o_ref, a_ref):
    _natural_to_chain_rows(x_ref, a_ref)
    o_ref[...] = jnp.zeros_like(o_ref)
    for vp in range(RW_VP):
        halves = [a_ref[pl.ds(vh * RW_VP + vp, RW_CHAINS, stride=RW_KR), :] for vh in range(2)]
        o_ref[0, pl.ds(vp, RW_TT, stride=RW_VR), :] = jnp.concatenate(halves, axis=0).T


def _vals_from_scan_kernel(y_ref, o_ref, a_ref):
    for vp in range(RW_VP):
        tile = y_ref[0, pl.ds(vp, RW_TT, stride=RW_VR), :].T
        for vh in range(2):
            a_ref[pl.ds(vh * RW_VP + vp, RW_CHAINS, stride=RW_KR), :] = tile[vh * RW_CHAINS:(vh + 1) * RW_CHAINS]
    for bl in range(RW_NB):
        rows = [a_ref[c * RW_KR:c * RW_KR + RW_HD, :] for c in range(bl * RW_HEADS, (bl + 1) * RW_HEADS)]
        o_ref[bl] = jnp.concatenate(rows, axis=0).T


def _scan_relayout(kernel_fn, x, n_seq, seq_len, rows_per_t, to_scan, name):
    g, nl = n_seq // RW_NB, seq_len // RW_TT
    nat_spec = pl.BlockSpec((RW_NB, RW_TT, MIX_W), lambda i, l: (i, l, 0))
    scan_spec = pl.BlockSpec((1, RW_TT * rows_per_t, LANES), lambda i, l: (i, l, 0))
    nat_shape = jax.ShapeDtypeStruct((n_seq, seq_len, MIX_W), F32)
    scan_shape = jax.ShapeDtypeStruct((g, seq_len * rows_per_t, LANES), F32)
    x = x.reshape(n_seq, seq_len, MIX_W) if to_scan else x.reshape(g, seq_len * rows_per_t, LANES)
    out = pl.pallas_call(
        kernel_fn,
        grid=(g, nl),
        in_specs=[nat_spec if to_scan else scan_spec],
        out_specs=scan_spec if to_scan else nat_spec,
        out_shape=scan_shape if to_scan else nat_shape,
        scratch_shapes=[pltpu.VMEM((RW_CHAINS * RW_KR, RW_TT), F32)],
        compiler_params=_cparams("parallel", "parallel"),
        name=name,
    )(x)
    return out.reshape(g, seq_len, rows_per_t, LANES) if to_scan else out.reshape(n_seq * seq_len, MIX_W)


def _pad_rows(a, rows):
    return jnp.pad(a, ((0, 0), (0, 0), (0, rows - a.shape[2]), (0, 0)))


def _to_scan_keys(a, n_seq, seq_len):
    if seq_len % RW_TT == 0:
        return _scan_relayout(_keys_to_scan_kernel, a, n_seq, seq_len, RW_KR, True, "rwkv_keys_to_scan")
    g = n_seq // RW_NB
    a = a.reshape(g, RW_NB, seq_len, RW_HEADS, RW_HD).transpose(0, 2, 4, 1, 3).reshape(g, seq_len, RW_HD, RW_CHAINS)
    return _pad_rows(jnp.concatenate([a, a], axis=-1), RW_KR)


def _to_scan_vals(a, n_seq, seq_len):
    if seq_len % RW_TT == 0:
        return _scan_relayout(_vals_to_scan_kernel, a, n_seq, seq_len, RW_VR, True, "rwkv_vals_to_scan")
    g = n_seq // RW_NB
    a = a.reshape(g, RW_NB, seq_len, RW_HEADS, 2, RW_VP).transpose(0, 2, 5, 4, 1, 3)
    return _pad_rows(a.reshape(g, seq_len, RW_VP, LANES), RW_VR)


def _from_scan_vals(y, n_seq, seq_len):
    if seq_len % RW_TT == 0:
        return _scan_relayout(_vals_from_scan_kernel, y, n_seq, seq_len, RW_VR, False, "rwkv_vals_from_scan")
    g = n_seq // RW_NB
    y = y[:, :, :RW_VP].reshape(g, seq_len, RW_VP, 2, RW_NB, RW_HEADS).transpose(0, 4, 1, 5, 3, 2)
    return y.reshape(n_seq * seq_len, MIX_W)


def _state_to_scan(s, n_seq):
    nb = RW_CHAINS // RW_HEADS
    g = n_seq // nb
    s = s.reshape(g, nb, RW_HEADS, 2, RW_VP, RW_HD).transpose(0, 4, 5, 3, 1, 2)
    return s.reshape(g, RW_VP, RW_HD, LANES)


def _state_from_scan(s, n_seq):
    nb = RW_CHAINS // RW_HEADS
    g = n_seq // nb
    s = s.reshape(g, RW_VP, RW_HD, 2, nb, RW_HEADS).transpose(0, 4, 5, 3, 1, 2)
    return s.reshape(n_seq, RW_HEADS, RW_HD, RW_HD)


def _rwkv_branch(z, shift0, s0, p, row0, n_seq, seq_len, bb, lt, tc, prev):
    r, w, k, kk, b, v, g, bonus = _rwkv_prep(z, shift0, p, row0, n_seq, seq_len, bb, lt)
    keys = [_to_scan_keys(a, n_seq, seq_len) for a in (r, w, k, kk, b)]
    y, s_new = _rwkv_scan(*keys, _to_scan_vals(v, n_seq, seq_len), _state_to_scan(s0, n_seq), tc)
    y = _from_scan_vals(y, n_seq, seq_len)
    y = _rwkv_post(y, g, bonus, p['gn_g'], p['gn_b'], row0, z.shape[0], prev)
    return y, _state_from_scan(s_new, n_seq)


def _seq_masks(rows, lc):
    ri = lax.broadcasted_iota(jnp.int32, (rows, rows), 0)
    ci = lax.broadcasted_iota(jnp.int32, (rows, rows), 1)
    same = (ri // lc) == (ci // lc)
    return ri, ci, same


def _rows_from_seq(x, bb, lc):
    return jnp.broadcast_to(x, (bb, lc, x.shape[-1])).reshape(bb * lc, x.shape[-1])


def _last_of_seq(x, bb, lc):
    x3 = x.reshape(bb, lc, x.shape[-1])
    pick = (lax.broadcasted_iota(jnp.int32, (1, lc, 1), 1) == lc - 1).astype(F32)
    return jnp.sum(x3 * pick, axis=1, keepdims=True)


def _transpose_rows(x):
    rows = x.shape[0]
    if rows < LANES:
        x = jnp.concatenate([x, jnp.zeros((LANES - rows, LANES), x.dtype)], axis=0)
    return x.T[:, :rows]


def _mlstm_kernel(bb, lc, q_ref, k_ref, v_ref, o_ref, gt_ref, gb_ref, ng_ref, c0_ref, n0_ref, m0_ref,
                  y_ref, c_ref, n_ref, m_ref):
    rows = bb * lc

    @pl.when(pl.program_id(1) == 0)
    def _():
        c_ref[...] = c0_ref[...]
        n_ref[...] = n0_ref[...]
        m_ref[...] = m0_ref[...]

    gt = ML_GATE_CAP * jnp.tanh((gt_ref[...] + gb_ref[...]) * (1.0 / ML_GATE_CAP))
    lane = lax.broadcasted_iota(jnp.int32, (rows, LANES), 1)
    ri, ci, same = _seq_masks(rows, lc)
    causal = same & (ci <= ri)
    cum = _dot_exact_lhs(causal.astype(F32), -_softplus(-gt))
    gt_t = _transpose_rows(gt)
    cum_t = _transpose_rows(cum)
    for hd in range(ML_HEADS):
        sl = slice(hd * ML_HD, (hd + 1) * ML_HD)
        c0 = c_ref[0, :, hd]
        n0 = n_ref[0, :, hd:hd + 1, :]
        m0 = m_ref[0, :, :, hd:hd + 1]
        q = q_ref[:, sl]
        k = k_ref[:, sl] * (ML_HD ** -0.5)
        v = v_ref[:, sl]
        bcol = cum[:, ML_HEADS + hd:ML_HEADS + hd + 1]
        igcol = gt[:, hd:hd + 1]
        rowb = gt_t[hd:hd + 1, :] - cum_t[ML_HEADS + hd:ML_HEADS + hd + 1, :]
        dlog = jnp.where(causal, bcol + rowb, -jnp.inf)
        m_inter = bcol + _rows_from_seq(m0, bb, lc)
        m_t = jnp.maximum(m_inter, jnp.max(dlog, axis=-1, keepdims=True))
        qk = lax.dot_general(q.astype(BF16), k.astype(BF16), (((1,), (1,)), ((), ())),
                             preferred_element_type=F32)
        wts = jnp.exp(dlog - m_t) * qk
        s_inter = jnp.exp(m_inter - m_t)
        q3 = q.reshape(bb, lc, ML_HD).astype(BF16)
        qc = jnp.einsum('btd,bde->bte', q3, c0.astype(BF16), preferred_element_type=F32).reshape(rows, ML_HD)
        num = _bdot(wts, v) + s_inter * qc
        qn = jnp.sum(q * _rows_from_seq(n0, bb, lc), axis=-1, keepdims=True)
        den = jnp.sum(wts, axis=-1, keepdims=True) + s_inter * qn
        hh = num / jnp.maximum(jnp.abs(den), jnp.exp(-m_t))
        hh = hh * lax.rsqrt(jnp.mean(hh * hh, axis=-1, keepdims=True) + EPS) * ng_ref[:, sl]
        y_ref[:, sl] = (_sigmoid(o_ref[:, sl]) * hh).astype(y_ref.dtype)

        stat = jnp.where(lane == 0, bcol, jnp.where(lane == 1, m_t, 0.0))
        last = _last_of_seq(stat, bb, lc)
        last_rows = _rows_from_seq(last, bb, lc)
        b_last, m_new = last_rows[:, 0:1], last_rows[:, 1:2]
        w_end = jnp.exp(b_last - bcol + igcol - m_new)
        f_end = jnp.exp(last[:, :, 0:1] + m0 - last[:, :, 1:2])
        kw3 = (k * w_end).reshape(bb, lc, ML_HD)
        upd = jnp.einsum('bsd,bse->bde', kw3.astype(BF16), v.reshape(bb, lc, ML_HD).astype(BF16),
                         preferred_element_type=F32)
        c_ref[0, :, hd] = f_end * c0 + upd
        n_ref[0, :, hd:hd + 1, :] = f_end * n0 + jnp.sum(kw3, axis=1, keepdims=True)
        m_ref[0, :, :, hd:hd + 1] = last[:, :, 1:2]


def _mlstm_branch(z, init, lyr_in, gate_bias, norm_g, row0, n_seq, seq_len, bb, lc, depth, lyr, prev):
    rows = bb * lc
    nc = seq_len // lc
    rb0 = row0 // rows
    rowmap = lambda part: (lambda b, c: (rb0 + b * nc + c, Z_ML // MIX_W + part))
    zspec = lambda part: pl.BlockSpec((rows, MIX_W), rowmap(part))
    state_specs = lambda at: [pl.BlockSpec((1, bb, ML_HEADS, ML_HD, ML_HD), lambda b, c: (at, b, 0, 0, 0)),
                              pl.BlockSpec((1, bb, ML_HEADS, ML_HD), lambda b, c: (at, b, 0, 0)),
                              pl.BlockSpec((1, bb, 1, ML_HEADS), lambda b, c: (at, b, 0, 0))]
    kern, prev_specs, prev_args, aliases = _carry_through(functools.partial(_mlstm_kernel, bb, lc), 10, prev)
    return pl.pallas_call(
        kern,
        grid=(n_seq // bb, nc),
        in_specs=[zspec(0), zspec(1), zspec(2), zspec(3),
                  pl.BlockSpec((rows, LANES), lambda b, c: (rb0 + b * nc + c, Z_MLG // LANES)),
                  pl.BlockSpec((1, LANES), lambda b, c: (0, 0)), pl.BlockSpec((1, MIX_W), lambda b, c: (0, 0))]
                 + state_specs(lyr_in) + prev_specs,
        out_specs=[pl.BlockSpec((rows, MIX_W), lambda b, c: (rb0 + b * nc + c, 0))] + state_specs(lyr),
        out_shape=[jax.ShapeDtypeStruct((z.shape[0], MIX_W), BF16),
                   jax.ShapeDtypeStruct((depth, n_seq, ML_HEADS, ML_HD, ML_HD), F32),
                   jax.ShapeDtypeStruct((depth, n_seq, ML_HEADS, ML_HD), F32),
                   jax.ShapeDtypeStruct((depth, n_seq, 1, ML_HEADS), F32)],
        input_output_aliases=aliases,
        compiler_params=_cparams("parallel", "arbitrary"),
        name="mlstm_chunk",
    )(z, z, z, z, z, gate_bias, norm_g, *init, *prev_args)


def _rope_rows(x, cos, sin):
    half = RT_HD // 2
    x1, x2 = x[:, :half], x[:, half:]
    return jnp.concatenate([x1 * cos - x2 * sin, x1 * sin + x2 * cos], axis=1)


def _ret_kernel(bb, lc, q_ref, k_ref, v_ref, g_ref, cs_ref, dec_ref, cd_ref, s0_ref, y_ref, s_ref):
    rows = bb * lc

    @pl.when(pl.program_id(1) == 0)
    def _():
        s_ref[...] = s0_ref[...]

    cos, sin = cs_ref[:, :RT_HD // 2], cs_ref[:, RT_HD // 2:]
    for hd in range(RT_HEADS):
        sl = slice(hd * RT_HD, (hd + 1) * RT_HD)
        s0 = s_ref[0, :, hd]
        q = _rope_rows(q_ref[:, sl], cos, sin)
        k = _rope_rows(k_ref[:, sl], cos, sin) * (RT_HD ** -0.5)
        v = v_ref[:, sl]
        cd = cd_ref[hd]
        qk = lax.dot_general(q.astype(BF16), k.astype(BF16), (((1,), (1,)), ((), ())),
                             preferred_element_type=F32)
        inner = _bdot(qk * dec_ref[hd], v)
        q3 = q.reshape(bb, lc, RT_HD).astype(BF16)
        cross = jnp.einsum('btd,bde->bte', q3, s0.astype(BF16), preferred_element_type=F32).reshape(rows, RT_HD)
        y = inner + cross * cd[:, 0:1]
        y = y * lax.rsqrt(jnp.mean(y * y, axis=-1, keepdims=True) + EPS)
        gt = g_ref[:, sl]
        y_ref[:, sl] = (gt * _sigmoid(gt) * y).astype(y_ref.dtype)
        kd = (k * cd[:, 1:2]).reshape(bb, lc, RT_HD).astype(BF16)
        upd = jnp.einsum('bsd,bse->bde', kd, v.reshape(bb, lc, RT_HD).astype(BF16), preferred_element_type=F32)
        s_ref[0, :, hd] = cd[0:1, 2:3] * s0 + upd


def _ret_branch(z, s0, lyr_in, row0, n_seq, seq_len, bb, lc, pos0, depth, lyr, prev):
    rows = bb * lc
    nc = seq_len // lc
    rb0 = row0 // rows
    half = RT_HD // 2
    inv = ROPE_BASE ** (-jnp.arange(half, dtype=F32) / half)
    pos = jnp.broadcast_to((jnp.arange(nc * lc, dtype=F32) + float(pos0)).reshape(nc, 1, lc), (nc, bb, lc))
    ang = pos.reshape(nc * rows)[:, None] * inv[None, :]
    cs = jnp.concatenate([jnp.cos(ang), jnp.sin(ang)], axis=1)
    log_g = jnp.log(1.0 - jnp.exp(jnp.linspace(math.log(1.0 / 32), math.log(1.0 / 512), RT_HEADS)))
    idx = jnp.arange(rows)
    t = (idx % lc).astype(F32)
    diff = t[:, None] - t[None, :]
    same = (idx[:, None] // lc) == (idx[None, :] // lc)
    dec = jnp.where(same & (diff >= 0), jnp.exp(log_g[:, None, None] * jnp.maximum(diff, 0.0)), 0.0)
    cd = jnp.zeros((RT_HEADS, rows, LANES), F32)
    cd = cd.at[:, :, 0].set(jnp.exp(log_g[:, None] * (t[None, :] + 1.0)))
    cd = cd.at[:, :, 1].set(jnp.exp(log_g[:, None] * (lc - 1.0 - t)[None, :]))
    cd = cd.at[:, :, 2].set(jnp.broadcast_to(jnp.exp(log_g * lc)[:, None], (RT_HEADS, rows)))
    rowmap = lambda part: (lambda b, c: (rb0 + b * nc + c, Z_RT // MIX_W + part))
    zspec = lambda part: pl.BlockSpec((rows, MIX_W), rowmap(part))
    sspec = lambda at: pl.BlockSpec((1, bb, RT_HEADS, RT_HD, RT_HD), lambda b, c: (at, b, 0, 0, 0))
    kern, prev_specs, prev_args, aliases = _carry_through(functools.partial(_ret_kernel, bb, lc), 8, prev)
    return pl.pallas_call(
        kern,
        grid=(n_seq // bb, nc),
        in_specs=[zspec(0), zspec(1), zspec(2), zspec(3),
                  pl.BlockSpec((rows, RT_HD), lambda b, c: (c, 0)),
                  pl.BlockSpec((RT_HEADS, rows, rows), lambda b, c: (0, 0, 0)),
                  pl.BlockSpec((RT_HEADS, rows, LANES), lambda b, c: (0, 0, 0)),
                  sspec(lyr_in)] + prev_specs,
        out_specs=[pl.BlockSpec((rows, MIX_W), lambda b, c: (rb0 + b * nc + c, 0)), sspec(lyr)],
        out_shape=[jax.ShapeDtypeStruct((z.shape[0], MIX_W), BF16),
                   jax.ShapeDtypeStruct((depth, n_seq, RT_HEADS, RT_HD, RT_HD), F32)],
        input_output_aliases=aliases,
        compiler_params=_cparams("parallel", "arbitrary"),
        name="retention_chunk",
    )(z, z, z, z, cs, dec, cd, s0, *prev_args)


def _pack_w_in(w_in):
    d = w_in.shape[0]
    ml0 = RW_COLS
    mlg0 = ml0 + 4 * MIX_W
    rt0 = mlg0 + 2 * ML_HEADS
    pad = jnp.zeros((d, Z_COLS - Z_MLG - 2 * ML_HEADS), w_in.dtype)
    packed = jnp.concatenate([w_in[:, ml0:mlg0], w_in[:, rt0:], w_in[:, :RW_COLS], w_in[:, mlg0:rt0], pad], axis=1)
    return packed.astype(BF16)


def _group_cfg(n_seq, seq_len):
    if seq_len % CHUNK == 0:
        return dict(rw_bb=1, rw_lt=256, rw_tc=32, ch_bb=1, ch_lc=CHUNK, at_bb=1, at_lq=512)
    return dict(rw_bb=32, rw_lt=seq_len, rw_tc=seq_len, ch_bb=4, ch_lc=seq_len, at_bb=8, at_lq=seq_len)


def kernel(x_prompt, x_sample, mem_prompt, state_rwkv_shift, state_rwkv, state_mlstm_c, state_mlstm_n,
           state_mlstm_m, state_ret, cache_mem_k, cache_mem_v, g_pre_mix, g_post_mix, g_pre_x, g_post_x,
           g_pre_ff, g_post_ff, g_mem, w_in, rw_mu, rw_w0, rw_w_up, rw_a0, rw_a_up, rw_g_up, rw_k_k, rw_k_a,
           rw_r_k, rw_gn_g, rw_gn_b, ml_i_b, ml_f_b, ml_norm_g, w_br, w_out, x_wq, x_wkv, x_wo, ff_w1, ff_w2):
    bp, lp, d = x_prompt.shape
    bs, ls, _ = x_sample.shape
    depth = w_in.shape[0]
    n_mem = mem_prompt.shape[1]
    tp, ts = bp * lp, bs * ls
    groups = [(0, bp, lp, 0), (tp, bs, ls, PAST_LEN)]

    x = jnp.concatenate([x_prompt.reshape(tp, d), x_sample.reshape(ts, d)], axis=0)
    mem = mem_prompt.reshape(bp * n_mem, d)
    h = _rmsnorm(x, g_pre_mix[0])

    zeros = lambda *s: jnp.zeros(s, F32)
    inits = [
        lambda l: (zeros(bp, 1, RW_COLS), zeros(bp, RW_HEADS, RW_HD, RW_HD),
                   (zeros(1, bp, ML_HEADS, ML_HD, ML_HD), zeros(1, bp, ML_HEADS, ML_HD), zeros(1, bp, 1, ML_HEADS)),
                   zeros(1, bp, RT_HEADS, RT_HD, RT_HD), 0),
        lambda l: (state_rwkv_shift[l].reshape(bs, 1, RW_COLS), state_rwkv[l],
                   (state_mlstm_c, state_mlstm_n, state_mlstm_m.reshape(depth, bs, 1, ML_HEADS)), state_ret, l),
    ]
    rw_states = [[], []]
    ml_states = [[None] * 3, [None] * 3]
    rt_states = [None, None]
    mem_k, mem_v = [], []
    for l in range(depth):
        kv = _mm(_rmsnorm(mem, g_mem[l], tm=256), x_wkv[l].astype(BF16), tm=256, tn=512)
        mem_k.append(kv[:, :X_W].reshape(bp, n_mem, X_HEADS, X_HD))
        mem_v.append(kv[:, X_W:].reshape(bp, n_mem, X_HEADS, X_HD))

        z = _mm(h, _pack_w_in(w_in[l]), tm=1024, tn=1280)

        rw_p = dict(mu=rw_mu[l].reshape(1, -1), w0=rw_w0[l].reshape(1, -1), w_up=rw_w_up[l].astype(BF16),
                    a0=rw_a0[l].reshape(1, -1), a_up=rw_a_up[l].astype(BF16), g_up=rw_g_up[l].astype(BF16),
                    k_k=rw_k_k[l].reshape(1, -1), k_a=rw_k_a[l].reshape(1, -1), r_k=rw_r_k[l].reshape(1, -1),
                    gn_g=rw_gn_g[l].reshape(1, -1), gn_b=rw_gn_b[l].reshape(1, -1))
        gate_bias = jnp.concatenate([ml_i_b[l], ml_f_b[l], zeros(LANES - 2 * ML_HEADS)]).reshape(1, LANES)
        norm_g = ml_norm_g[l].reshape(1, MIX_W)

        y_rw = y_ml = y_rt = None
        for gi, (row0, n_seq, seq_len, pos0) in enumerate(groups):
            cfg = _group_cfg(n_seq, seq_len)
            shift0, s_rw0, ml0, rt0, lyr_in = inits[gi](l)
            y_rw, s_rw = _rwkv_branch(z, shift0, s_rw0, rw_p, row0, n_seq, seq_len,
                                      cfg['rw_bb'], cfg['rw_lt'], cfg['rw_tc'], y_rw)
            y_ml, *ml_states[gi] = _mlstm_branch(z, ml0, lyr_in, gate_bias, norm_g, row0, n_seq, seq_len,
                                                 cfg['ch_bb'], cfg['ch_lc'], depth, l, [y_ml] + ml_states[gi])
            y_rt, rt_states[gi] = _ret_branch(z, rt0, lyr_in, row0, n_seq, seq_len, cfg['ch_bb'], cfg['ch_lc'],
                                              pos0, depth, l, [y_rt, rt_states[gi]])
            last = row0 + seq_len - 1
            shift = lax.slice(z, (last, Z_RW), (last + (n_seq - 1) * seq_len + 1, Z_RW + RW_COLS), (seq_len, 1))
            rw_states[gi].append((shift, s_rw))

        merged = _merge(y_rw, y_ml, y_rt, w_br[l].astype(BF16), z)
        x, h = _proj_res(merged, w_out[l].astype(BF16), x, g_post_mix[l], g_pre_x[l])

        wq, wo = x_wq[l].astype(BF16), x_wo[l].astype(BF16)
        for gi, (row0, n_seq, seq_len, pos0) in enumerate(groups):
            cfg = _group_cfg(n_seq, seq_len)
            if gi == 0:
                mk, mv, lyr_in = mem_k[l][None], mem_v[l][None], 0
            else:
                mk, mv, lyr_in = cache_mem_k, cache_mem_v, l
            x, h = _attn(h, x, mk, mv, lyr_in, wq, wo, g_post_x[l], g_pre_ff[l], row0, n_seq, seq_len,
                         cfg['at_bb'], cfg['at_lq'])

        g_next = g_pre_mix[(l + 1) % depth]
        x, h = _mlp(h, ff_w1[l].astype(BF16), ff_w2[l].astype(BF16), x, g_post_ff[l], g_next)

    def states(gi, n_seq):
        c_new, n_new, m_new = ml_states[gi]
        return (jnp.stack([s[0] for s in rw_states[gi]]), jnp.stack([s[1] for s in rw_states[gi]]),
                c_new, n_new, m_new.reshape(depth, n_seq, ML_HEADS), rt_states[gi])

    return (x[:tp].reshape(bp, lp, d), x[tp:].reshape(bs, ls, d), *states(0, bp),
            jnp.stack(mem_k), jnp.stack(mem_v), *states(1, bs))
```

```python
import functools
import math

import jax
import jax.numpy as jnp
from jax import lax
from jax.experimental import pallas as pl
from jax.experimental.pallas import tpu as pltpu

F32 = jnp.float32
BF16 = jnp.bfloat16

D_MODEL = 2048
MIX_W = D_MODEL // 2
RW_HD = 64
RW_HEADS = MIX_W // RW_HD
RW_LORA = (64, 64, 128)
RW_COLS = 3 * MIX_W + sum(RW_LORA)
RW_GN_EPS = 64e-5
ML_HEADS = 8
ML_HD = MIX_W // ML_HEADS
ML_GATE_CAP = 15.0
RT_HEADS = 4
RT_HD = MIX_W // RT_HEADS
ROPE_BASE = 10000.0
X_HEADS = 4
X_HD = 128
X_W = X_HEADS * X_HD
D_FF = 4 * D_MODEL
CHUNK = 64
EPS = 1e-6
PAST_LEN = 16384

IN_ML = RW_COLS
IN_MLG = IN_ML + 4 * MIX_W
IN_RT = IN_MLG + 2 * ML_HEADS
Z_MLG = RW_COLS
Z_RWB = RW_COLS + 256
Z_GATE = 4 * MIX_W

LANES = 128
RW_VP = RW_HD // 2
RW_CHAINS = LANES // 2


def _cparams(*sem):
    return pltpu.CompilerParams(dimension_semantics=sem, vmem_limit_bytes=56 * 1024 * 1024)


def _carry_through(kernel_fn, n_in, prev):
    carried = [(i, p) for i, p in enumerate(prev) if p is not None]
    if not carried:
        return kernel_fn, [], [], {}

    def wrapped(*refs):
        return kernel_fn(*refs[:n_in], *refs[n_in + len(carried):])

    return (wrapped, [pl.BlockSpec(memory_space=pl.ANY)] * len(carried), [p for _, p in carried],
            {n_in + j: i for j, (i, _) in enumerate(carried)})


def _sigmoid(x):
    return 1.0 / (1.0 + jnp.exp(-x))


def _softplus(x):
    return jnp.maximum(x, 0.0) + jnp.log(1.0 + jnp.exp(-jnp.abs(x)))


def _bdot(a, b):
    return jnp.dot(a.astype(BF16), b.astype(BF16), preferred_element_type=F32)


def _split3(x):
    hi = x.astype(BF16)
    r1 = x - hi.astype(F32)
    mid = r1.astype(BF16)
    lo = (r1 - mid.astype(F32)).astype(BF16)
    return hi, mid, lo


def _dot_exact_lhs(m, x):
    mb = m.astype(BF16)
    hi, mid, lo = _split3(x)
    return (jnp.dot(mb, hi, preferred_element_type=F32) + jnp.dot(mb, mid, preferred_element_type=F32)
            + jnp.dot(mb, lo, preferred_element_type=F32))


def _dot_exact_rhs(x, m):
    mb = m.astype(BF16)
    hi, mid, lo = _split3(x)
    return (jnp.dot(hi, mb, preferred_element_type=F32) + jnp.dot(mid, mb, preferred_element_type=F32)
            + jnp.dot(lo, mb, preferred_element_type=F32))


def _segsum(x, seg):
    blk = 256
    ri = lax.broadcasted_iota(jnp.int32, (blk, blk), 0) // seg
    ci = lax.broadcasted_iota(jnp.int32, (blk, blk), 1) // seg
    ones = (ri == ci).astype(F32)
    parts = [_dot_exact_rhs(x[:, c:c + blk], ones) for c in range(0, x.shape[1], blk)]
    return parts[0] if len(parts) == 1 else jnp.concatenate(parts, axis=1)


def _rms_rows(x, g):
    return x * lax.rsqrt(jnp.mean(x * x, axis=-1, keepdims=True) + EPS) * g


def _rmsnorm_kernel(x_ref, g_ref, o_ref):
    o_ref[...] = _rms_rows(x_ref[...], g_ref[...]).astype(o_ref.dtype)


def _rmsnorm(x, g, tm=512):
    m, d = x.shape
    return pl.pallas_call(
        _rmsnorm_kernel,
        grid=(m // tm,),
        in_specs=[pl.BlockSpec((tm, d), lambda i: (i, 0)), pl.BlockSpec((1, d), lambda i: (0, 0))],
        out_specs=pl.BlockSpec((tm, d), lambda i: (i, 0)),
        out_shape=jax.ShapeDtypeStruct((m, d), BF16),
        compiler_params=_cparams("parallel"),
        name="rmsnorm",
    )(x, g.reshape(1, d))


def _mm_kernel(x_ref, w_ref, o_ref):
    o_ref[...] = jnp.dot(x_ref[...], w_ref[0], preferred_element_type=F32)


def _mm(x, w, lyr, tm, tn):
    m, k = x.shape
    n = w.shape[2]
    return pl.pallas_call(
        _mm_kernel,
        grid=(n // tn, m // tm),
        in_specs=[pl.BlockSpec((tm, k), lambda j, i: (i, 0)), pl.BlockSpec((1, k, tn), lambda j, i: (lyr, 0, j))],
        out_specs=pl.BlockSpec((tm, tn), lambda j, i: (i, j)),
        out_shape=jax.ShapeDtypeStruct((m, n), F32),
        compiler_params=_cparams("parallel", "parallel"),
        name="matmul",
    )(x, w)


def _merge_kernel(y0_ref, y1_ref, y2_ref, w_ref, g0_ref, g1_ref, g2_ref, o_ref):
    acc = _sigmoid(g0_ref[...]) * jnp.dot(y0_ref[...], w_ref[0, 0], preferred_element_type=F32)
    acc += _sigmoid(g1_ref[...]) * jnp.dot(y1_ref[...], w_ref[0, 1], preferred_element_type=F32)
    acc += _sigmoid(g2_ref[...]) * jnp.dot(y2_ref[...], w_ref[0, 2], preferred_element_type=F32)
    o_ref[...] = acc.astype(o_ref.dtype)


def _merge(y_rw, y_ml, y_rt, w_br, lyr, z, gate_col0, tm=512, tn=1024):
    m, kw = y_rw.shape
    d = w_br.shape[3]
    gb = gate_col0 // tn
    gspec = lambda c: pl.BlockSpec((tm, tn), lambda j, i: (i, gb + c * (d // tn) + j))
    yspec = pl.BlockSpec((tm, kw), lambda j, i: (i, 0))
    return pl.pallas_call(
        _merge_kernel,
        grid=(d // tn, m // tm),
        in_specs=[yspec, yspec, yspec, pl.BlockSpec((1, 3, kw, tn), lambda j, i: (lyr, 0, 0, j)),
                  gspec(0), gspec(1), gspec(2)],
        out_specs=pl.BlockSpec((tm, tn), lambda j, i: (i, j)),
        out_shape=jax.ShapeDtypeStruct((m, d), BF16),
        compiler_params=_cparams("parallel", "parallel"),
        name="branch_merge",
    )(y_rw, y_ml, y_rt, w_br, z, z, z)


def _residual_epilogue(acc, x_ref, gpost_ref, gnext_ref, xo_ref, ho_ref):
    xn = x_ref[...] + _rms_rows(acc, gpost_ref[...])
    xo_ref[...] = xn
    ho_ref[...] = _rms_rows(xn, gnext_ref[...]).astype(ho_ref.dtype)


def _proj_res_kernel(a_ref, w_ref, x_ref, gpost_ref, gnext_ref, xo_ref, ho_ref):
    acc = jnp.dot(a_ref[...], w_ref[0], preferred_element_type=F32)
    _residual_epilogue(acc, x_ref, gpost_ref, gnext_ref, xo_ref, ho_ref)


def _proj_res(a, w, lyr, x, g_post, g_next, tm=512):
    m, k = a.shape
    d = w.shape[2]
    row = lambda i: (i, 0)
    fixed = lambda i: (0, 0)
    return pl.pallas_call(
        _proj_res_kernel,
        grid=(m // tm,),
        in_specs=[pl.BlockSpec((tm, k), row), pl.BlockSpec((1, k, d), lambda i: (lyr, 0, 0)),
                  pl.BlockSpec((tm, d), row),
                  pl.BlockSpec((1, d), fixed), pl.BlockSpec((1, d), fixed)],
        out_specs=[pl.BlockSpec((tm, d), row), pl.BlockSpec((tm, d), row)],
        out_shape=[jax.ShapeDtypeStruct((m, d), F32), jax.ShapeDtypeStruct((m, d), BF16)],
        compiler_params=_cparams("parallel"),
        name="out_proj_residual",
    )(a, w, x, g_post.reshape(1, d), g_next.reshape(1, d))


def _mlp_kernel(h_ref, w1_ref, w2_ref, x_ref, gpost_ref, gnext_ref, xo_ref, ho_ref, acc_ref):
    k = pl.program_id(1)

    @pl.when(k == 0)
    def _():
        acc_ref[...] = jnp.zeros_like(acc_ref)

    u = jnp.dot(h_ref[...], w1_ref[0], preferred_element_type=F32)
    u = jnp.square(jnp.maximum(u, 0.0)).astype(BF16)
    acc_ref[...] += jnp.dot(u, w2_ref[0], preferred_element_type=F32)

    @pl.when(k == pl.num_programs(1) - 1)
    def _():
        _residual_epilogue(acc_ref[...], x_ref, gpost_ref, gnext_ref, xo_ref, ho_ref)


def _mlp(h, w1, w2, lyr, x, g_post, g_next, tm=512, tf=1024):
    m, d = h.shape
    f = w1.shape[2]
    row = lambda i, k: (i, 0)
    fixed = lambda i, k: (0, 0)
    return pl.pallas_call(
        _mlp_kernel,
        grid=(m // tm, f // tf),
        in_specs=[pl.BlockSpec((tm, d), row), pl.BlockSpec((1, d, tf), lambda i, k: (lyr, 0, k)),
                  pl.BlockSpec((1, tf, d), lambda i, k: (lyr, k, 0)), pl.BlockSpec((tm, d), row),
                  pl.BlockSpec((1, d), fixed), pl.BlockSpec((1, d), fixed)],
        out_specs=[pl.BlockSpec((tm, d), row), pl.BlockSpec((tm, d), row)],
        out_shape=[jax.ShapeDtypeStruct((m, d), F32), jax.ShapeDtypeStruct((m, d), BF16)],
        scratch_shapes=[pltpu.VMEM((tm, d), F32)],
        compiler_params=_cparams("parallel", "arbitrary"),
        name="relu2_mlp",
    )(h, w1, w2, x, g_post.reshape(1, d), g_next.reshape(1, d))


def _attn_kernel(bb, lq, h_ref, wq_ref, mk_ref, mv_ref, wo_ref, x_ref, gpost_ref, gnext_ref, xo_ref, ho_ref):
    q = jnp.dot(h_ref[...], wq_ref[0], preferred_element_type=F32)
    outs = []
    for hd in range(X_HEADS):
        sl = slice(hd * X_HD, (hd + 1) * X_HD)
        q3 = q[:, sl].reshape(bb, lq, X_HD).astype(BF16)
        k3 = mk_ref[0, :, :, hd, :].astype(BF16)
        v3 = mv_ref[0, :, :, hd, :].astype(BF16)
        s = jnp.einsum('bqd,bmd->bqm', q3, k3, preferred_element_type=F32) * (X_HD ** -0.5)
        s = s - jnp.max(s, axis=-1, keepdims=True)
        e = jnp.exp(s)
        p = (e / jnp.sum(e, axis=-1, keepdims=True)).astype(BF16)
        o3 = jnp.einsum('bqm,bmd->bqd', p, v3, preferred_element_type=F32)
        outs.append(o3.reshape(bb * lq, X_HD))
    o = jnp.concatenate(outs, axis=1).astype(BF16)
    acc = jnp.dot(o, wo_ref[0], preferred_element_type=F32)
    _residual_epilogue(acc, x_ref, gpost_ref, gnext_ref, xo_ref, ho_ref)


def _attn(h, x, mk, mv, lyr, wq, wo, wl, g_post, g_next, row0, n_seq, seq_len, bb, lq):
    t, d = h.shape
    rows = bb * lq
    nl = seq_len // lq
    n_mem = mk.shape[2]
    rb0 = row0 // rows
    rowmap = lambda b, l: (rb0 + b * nl + l, 0)
    fixed = lambda b, l: (0, 0)
    kvspec = pl.BlockSpec((1, bb, n_mem, X_HEADS, X_HD), lambda b, l: (lyr, b, 0, 0, 0))
    return pl.pallas_call(
        functools.partial(_attn_kernel, bb, lq),
        grid=(n_seq // bb, nl),
        in_specs=[pl.BlockSpec((rows, d), rowmap), pl.BlockSpec((1, d, X_W), lambda b, l: (wl, 0, 0)), kvspec,
                  kvspec, pl.BlockSpec((1, X_W, d), lambda b, l: (wl, 0, 0)), pl.BlockSpec((rows, d), rowmap),
                  pl.BlockSpec((1, d), fixed), pl.BlockSpec((1, d), fixed)],
        out_specs=[pl.BlockSpec((rows, d), rowmap), pl.BlockSpec((rows, d), rowmap)],
        out_shape=[jax.ShapeDtypeStruct((t, d), F32), jax.ShapeDtypeStruct((t, d), BF16)],
        input_output_aliases={5: 0, 0: 1},
        compiler_params=_cparams("parallel", "parallel"),
        name="mem_cross_attention",
    )(h, wq, mk, mv, wo, x, g_post.reshape(1, d), g_next.reshape(1, d))


def _rwkv_prep_kernel(bb, lt, u_ref, s0_ref, mu_ref, w0_ref, wup_ref, a0_ref, aup_ref, gup_ref, kk_ref, ka_ref,
                      rk_ref, r_o, w_o, k_o, kk_o, b_o, v_o, g_o, bonus_o, carry_ref):
    l = pl.program_id(1)
    rows = bb * lt
    u = u_ref[:, :RW_COLS]
    if bb == 1:
        first = jnp.where(l == 0, s0_ref[0], carry_ref[...])
        first_rows = jnp.broadcast_to(first, (rows, RW_COLS))
    else:
        first_rows = jnp.broadcast_to(s0_ref[...], (bb, lt, RW_COLS)).reshape(rows, RW_COLS)
    rowid = lax.broadcasted_iota(jnp.int32, (rows, 1), 0) % lt
    prev = jnp.where(rowid == 0, first_rows, pltpu.roll(u, 1, 0))
    if bb == 1:
        carry_ref[...] = u[rows - 1:rows, :]
    z = u + (prev - u) * mu_ref[...]
    r = z[:, 0:MIX_W]
    k = z[:, MIX_W:2 * MIX_W]
    v = z[:, 2 * MIX_W:3 * MIX_W]
    o = 3 * MIX_W
    wd = z[:, o:o + RW_LORA[0]]
    ad = z[:, o + RW_LORA[0]:o + RW_LORA[0] + RW_LORA[1]]
    gd = z[:, o + RW_LORA[0] + RW_LORA[1]:RW_COLS]
    w_log = -_softplus(-(w0_ref[...] + _bdot(jnp.tanh(wd), wup_ref[...]))) - 0.5
    decay = jnp.exp(-jnp.exp(w_log))
    a = _sigmoid(a0_ref[...] + _bdot(ad, aup_ref[...]))
    g = _bdot(_sigmoid(gd), gup_ref[...])
    kk = k * kk_ref[...]
    kk = kk / jnp.maximum(jnp.sqrt(_segsum(kk * kk, RW_HD)), 1e-12)
    k = k * (1.0 + (a - 1.0) * ka_ref[...])
    r_o[...] = r
    w_o[...] = decay
    k_o[...] = k
    kk_o[...] = kk
    b_o[...] = kk * a
    v_o[...] = v
    g_o[...] = g
    bonus_o[...] = _segsum(r * k * rk_ref[...], RW_HD) * v


def _rwkv_prep(z, shift0, p, row0, n_seq, seq_len, bb, lt):
    rows = bb * lt
    nl = seq_len // lt
    assert bb == 1 or nl == 1
    rb0 = row0 // rows
    m = n_seq * seq_len
    fixed = lambda b, l: (0, 0)
    outmap = lambda b, l: (b * nl + l, 0)
    vec = lambda width: pl.BlockSpec((1, width), fixed)
    ospec = pl.BlockSpec((rows, MIX_W), outmap)
    oshape = jax.ShapeDtypeStruct((m, MIX_W), F32)
    return pl.pallas_call(
        functools.partial(_rwkv_prep_kernel, bb, lt),
        grid=(n_seq // bb, nl),
        in_specs=[pl.BlockSpec((rows, Z_RWB), lambda b, l: (rb0 + b * nl + l, 0)),
                  pl.BlockSpec((bb, 1, RW_COLS), lambda b, l: (b, 0, 0)),
                  vec(RW_COLS), vec(MIX_W), pl.BlockSpec((RW_LORA[0], MIX_W), fixed), vec(MIX_W),
                  pl.BlockSpec((RW_LORA[1], MIX_W), fixed), pl.BlockSpec((RW_LORA[2], MIX_W), fixed),
                  vec(MIX_W), vec(MIX_W), vec(MIX_W)],
        out_specs=[ospec] * 8,
        out_shape=[oshape] * 8,
        scratch_shapes=[pltpu.VMEM((1, RW_COLS), F32)],
        compiler_params=_cparams("parallel", "arbitrary"),
        name="rwkv_prep",
    )(z, shift0, p['mu'], p['w0'], p['w_up'], p['a0'], p['a_up'], p['g_up'], p['k_k'], p['k_a'], p['r_k'])


def _rwkv_scan_kernel(tc, r_ref, w_ref, k_ref, kk_ref, b_ref, v_ref, s0_ref, y_ref, s_ref):
    @pl.when(pl.program_id(1) == 0)
    def _():
        s_ref[...] = s0_ref[...]

    y_ref[0, :, RW_VP:, :] = jnp.zeros((tc, RW_VR - RW_VP, LANES), F32)
    keys = lambda ref, t: ref[0, t, :RW_HD, :]

    def step(t, carry):
        for vp in range(RW_VP):
            s = s_ref[0, vp]
            sa = jnp.sum(s * keys(kk_ref, t), axis=0, keepdims=True)
            sn = s * keys(w_ref, t) - sa * keys(b_ref, t) + v_ref[0, t, vp:vp + 1, :] * keys(k_ref, t)
            s_ref[0, vp] = sn
            y_ref[0, t, vp:vp + 1, :] = jnp.sum(sn * keys(r_ref, t), axis=0, keepdims=True)
        return carry

    lax.fori_loop(0, tc, step, 0)


def _rwkv_scan(r, w, k, kk, b, v, s0, tc):
    g, l = r.shape[0], r.shape[1]
    kspec = pl.BlockSpec((1, tc, RW_KR, LANES), lambda i, c: (i, c, 0, 0))
    vspec = pl.BlockSpec((1, tc, RW_VR, LANES), lambda i, c: (i, c, 0, 0))
    sspec = pl.BlockSpec((1, RW_VP, RW_HD, LANES), lambda i, c: (i, 0, 0, 0))
    return pl.pallas_call(
        functools.partial(_rwkv_scan_kernel, tc),
        grid=(g, l // tc),
        in_specs=[kspec] * 5 + [vspec, sspec],
        out_specs=[vspec, sspec],
        out_shape=[jax.ShapeDtypeStruct((g, l, RW_VR, LANES), F32),
                   jax.ShapeDtypeStruct((g, RW_VP, RW_HD, LANES), F32)],
        compiler_params=_cparams("parallel", "arbitrary"),
        name="rwkv_scan",
    )(r, w, k, kk, b, v, s0)


def _rwkv_post_kernel(y_ref, g_ref, bonus_ref, gng_ref, gnb_ref, o_ref):
    y = y_ref[...]
    mean = _segsum(y, RW_HD) * (1.0 / RW_HD)
    yc = y - mean
    var = _segsum(yc * yc, RW_HD) * (1.0 / RW_HD)
    yn = yc * lax.rsqrt(var + RW_GN_EPS) * gng_ref[...] + gnb_ref[...]
    o_ref[...] = ((yn + bonus_ref[...]) * g_ref[...]).astype(o_ref.dtype)


def _rwkv_post(y, g, bonus, gn_g, gn_b, row0, total_rows, prev, tm=256):
    m = y.shape[0]
    rb0 = row0 // tm
    fixed = lambda i: (0, 0)
    spec = pl.BlockSpec((tm, MIX_W), lambda i: (i, 0))
    kern, prev_specs, prev_args, aliases = _carry_through(_rwkv_post_kernel, 5, [prev])
    return pl.pallas_call(
        kern,
        grid=(m // tm,),
        in_specs=[spec, spec, spec, pl.BlockSpec((1, MIX_W), fixed), pl.BlockSpec((1, MIX_W), fixed)] + prev_specs,
        out_specs=pl.BlockSpec((tm, MIX_W), lambda i: (rb0 + i, 0)),
        out_shape=jax.ShapeDtypeStruct((total_rows, MIX_W), BF16),
        input_output_aliases=aliases,
        compiler_params=_cparams("parallel"),
        name="rwkv_post",
    )(y, g, bonus, gn_g, gn_b, *prev_args)


RW_NB = RW_CHAINS // RW_HEADS
RW_TT = LANES


RW_KR = RW_HD + 8
RW_VR = RW_VP + 8


def _natural_to_chain_rows(x_ref, a_ref):
    for bl in range(RW_NB):
        xt = x_ref[bl].T
        for h in range(RW_HEADS):
            c = bl * RW_HEADS + h
            a_ref[c * RW_KR:c * RW_KR + RW_HD, :] = xt[h * RW_HD:(h + 1) * RW_HD]


def _keys_to_scan_kernel(x_ref, o_ref, a_ref):
    _natural_to_chain_rows(x_ref, a_ref)
    o_ref[...] = jnp.zeros_like(o_ref)
    for k in range(RW_HD):
        tile = a_ref[pl.ds(k, RW_CHAINS, stride=RW_KR), :]
        o_ref[0, pl.ds(k, RW_TT, stride=RW_KR), :] = jnp.concatenate([tile, tile], axis=0).T


def _vals_to_scan_kernel(x_ref, o_ref, a_ref):
    _natural_to_chain_rows(x_ref, a_ref)
    o_ref[...] = jnp.zeros_like(o_ref)
    for vp in range(RW_VP):
        halves = [a_ref[pl.ds(vh * RW_VP + vp, RW_CHAINS, stride=RW_KR), :] for vh in range(2)]
        o_ref[0, pl.ds(vp, RW_TT, stride=RW_VR), :] = jnp.concatenate(halves, axis=0).T


def _vals_from_scan_kernel(y_ref, o_ref, a_ref):
    for vp in range(RW_VP):
        tile = y_ref[0, pl.ds(vp, RW_TT, stride=RW_VR), :].T
        for vh in range(2):
            a_ref[pl.ds(vh * RW_VP + vp, RW_CHAINS, stride=RW_KR), :] = tile[vh * RW_CHAINS:(vh + 1) * RW_CHAINS]
    for bl in range(RW_NB):
        rows = [a_ref[c * RW_KR:c * RW_KR + RW_HD, :] for c in range(bl * RW_HEADS, (bl + 1) * RW_HEADS)]
        o_ref[bl] = jnp.concatenate(rows, axis=0).T


def _scan_relayout(kernel_fn, x, n_seq, seq_len, rows_per_t, to_scan, name):
    g, nl = n_seq // RW_NB, seq_len // RW_TT
    nat_spec = pl.BlockSpec((RW_NB, RW_TT, MIX_W), lambda i, l: (i, l, 0))
    scan_spec = pl.BlockSpec((1, RW_TT * rows_per_t, LANES), lambda i, l: (i, l, 0))
    nat_shape = jax.ShapeDtypeStruct((n_seq, seq_len, MIX_W), F32)
    scan_shape = jax.ShapeDtypeStruct((g, seq_len * rows_per_t, LANES), F32)
    x = x.reshape(n_seq, seq_len, MIX_W) if to_scan else x.reshape(g, seq_len * rows_per_t, LANES)
    out = pl.pallas_call(
        kernel_fn,
        grid=(g, nl),
        in_specs=[nat_spec if to_scan else scan_spec],
        out_specs=scan_spec if to_scan else nat_spec,
        out_shape=scan_shape if to_scan else nat_shape,
        scratch_shapes=[pltpu.VMEM((RW_CHAINS * RW_KR, RW_TT), F32)],
        compiler_params=_cparams("parallel", "parallel"),
        name=name,
    )(x)
    return out.reshape(g, seq_len, rows_per_t, LANES) if to_scan else out.reshape(n_seq * seq_len, MIX_W)


def _pad_rows(a, rows):
    return jnp.pad(a, ((0, 0), (0, 0), (0, rows - a.shape[2]), (0, 0)))


def _to_scan_keys(a, n_seq, seq_len):
    if seq_len % RW_TT == 0:
        return _scan_relayout(_keys_to_scan_kernel, a, n_seq, seq_len, RW_KR, True, "rwkv_keys_to_scan")
    g = n_seq // RW_NB
    a = a.reshape(g, RW_NB, seq_len, RW_HEADS, RW_HD).transpose(0, 2, 4, 1, 3).reshape(g, seq_len, RW_HD, RW_CHAINS)
    return _pad_rows(jnp.concatenate([a, a], axis=-1), RW_KR)


def _to_scan_vals(a, n_seq, seq_len):
    if seq_len % RW_TT == 0:
        return _scan_relayout(_vals_to_scan_kernel, a, n_seq, seq_len, RW_VR, True, "rwkv_vals_to_scan")
    g = n_seq // RW_NB
    a = a.reshape(g, RW_NB, seq_len, RW_HEADS, 2, RW_VP).transpose(0, 2, 5, 4, 1, 3)
    return _pad_rows(a.reshape(g, seq_len, RW_VP, LANES), RW_VR)


def _from_scan_vals(y, n_seq, seq_len):
    if seq_len % RW_TT == 0:
        return _scan_relayout(_vals_from_scan_kernel, y, n_seq, seq_len, RW_VR, False, "rwkv_vals_from_scan")
    g = n_seq // RW_NB
    y = y[:, :, :RW_VP].reshape(g, seq_len, RW_VP, 2, RW_NB, RW_HEADS).transpose(0, 4, 1, 5, 3, 2)
    return y.reshape(n_seq * seq_len, MIX_W)


def _state_to_scan(s, n_seq):
    nb = RW_CHAINS // RW_HEADS
    g = n_seq // nb
    s = s.reshape(g, nb, RW_HEADS, 2, RW_VP, RW_HD).transpose(0, 4, 5, 3, 1, 2)
    return s.reshape(g, RW_VP, RW_HD, LANES)


def _state_from_scan(s, n_seq):
    nb = RW_CHAINS // RW_HEADS
    g = n_seq // nb
    s = s.reshape(g, RW_VP, RW_HD, 2, nb, RW_HEADS).transpose(0, 4, 5, 3, 1, 2)
    return s.reshape(n_seq, RW_HEADS, RW_HD, RW_HD)


def _rwkv_branch(z, shift0, s0, p, row0, n_seq, seq_len, bb, lt, tc, prev):
    r, w, k, kk, b, v, g, bonus = _rwkv_prep(z, shift0, p, row0, n_seq, seq_len, bb, lt)
    keys = [_to_scan_keys(a, n_seq, seq_len) for a in (r, w, k, kk, b)]
    y, s_new = _rwkv_scan(*keys, _to_scan_vals(v, n_seq, seq_len), _state_to_scan(s0, n_seq), tc)
    y = _from_scan_vals(y, n_seq, seq_len)
    y = _rwkv_post(y, g, bonus, p['gn_g'], p['gn_b'], row0, z.shape[0], prev)
    return y, _state_from_scan(s_new, n_seq)


def _seq_masks(rows, lc):
    ri = lax.broadcasted_iota(jnp.int32, (rows, rows), 0)
    ci = lax.broadcasted_iota(jnp.int32, (rows, rows), 1)
    same = (ri // lc) == (ci // lc)
    return ri, ci, same


def _rows_from_seq(x, bb, lc):
    return jnp.broadcast_to(x, (bb, lc, x.shape[-1])).reshape(bb * lc, x.shape[-1])


def _last_of_seq(x, bb, lc):
    x3 = x.reshape(bb, lc, x.shape[-1])
    pick = (lax.broadcasted_iota(jnp.int32, (1, lc, 1), 1) == lc - 1).astype(F32)
    return jnp.sum(x3 * pick, axis=1, keepdims=True)


def _transpose_rows(x):
    rows = x.shape[0]
    if rows < LANES:
        x = jnp.concatenate([x, jnp.zeros((LANES - rows, LANES), x.dtype)], axis=0)
    return x.T[:, :rows]


def _mlstm_kernel(bb, lc, q_ref, k_ref, v_ref, o_ref, gt_ref, gb_ref, ng_ref, c0_ref, n0_ref, m0_ref,
                  y_ref, c_ref, n_ref, m_ref):
    rows = bb * lc

    @pl.when(pl.program_id(1) == 0)
    def _():
        c_ref[...] = c0_ref[...]
        n_ref[...] = n0_ref[...]
        m_ref[...] = m0_ref[...]

    gt = ML_GATE_CAP * jnp.tanh((gt_ref[...] + gb_ref[...]) * (1.0 / ML_GATE_CAP))
    lane = lax.broadcasted_iota(jnp.int32, (rows, LANES), 1)
    ri, ci, same = _seq_masks(rows, lc)
    causal = same & (ci <= ri)
    cum = _dot_exact_lhs(causal.astype(F32), -_softplus(-gt))
    gt_t = _transpose_rows(gt)
    cum_t = _transpose_rows(cum)
    for hd in range(ML_HEADS):
        sl = slice(hd * ML_HD, (hd + 1) * ML_HD)
        c0 = c_ref[0, :, hd]
        n0 = n_ref[0, :, hd:hd + 1, :]
        m0 = m_ref[0, :, :, hd:hd + 1]
        q = q_ref[:, sl]
        k = k_ref[:, sl] * (ML_HD ** -0.5)
        v = v_ref[:, sl]
        bcol = cum[:, ML_HEADS + hd:ML_HEADS + hd + 1]
        igcol = gt[:, hd:hd + 1]
        rowb = gt_t[hd:hd + 1, :] - cum_t[ML_HEADS + hd:ML_HEADS + hd + 1, :]
        dlog = jnp.where(causal, bcol + rowb, -jnp.inf)
        m_inter = bcol + _rows_from_seq(m0, bb, lc)
        m_t = jnp.maximum(m_inter, jnp.max(dlog, axis=-1, keepdims=True))
        qk = lax.dot_general(q.astype(BF16), k.astype(BF16), (((1,), (1,)), ((), ())),
                             preferred_element_type=F32)
        wts = jnp.exp(dlog - m_t) * qk
        s_inter = jnp.exp(m_inter - m_t)
        q3 = q.reshape(bb, lc, ML_HD).astype(BF16)
        qc = jnp.einsum('btd,bde->bte', q3, c0.astype(BF16), preferred_element_type=F32).reshape(rows, ML_HD)
        num = _bdot(wts, v) + s_inter * qc
        qn = jnp.sum(q * _rows_from_seq(n0, bb, lc), axis=-1, keepdims=True)
        den = jnp.sum(wts, axis=-1, keepdims=True) + s_inter * qn
        hh = num / jnp.maximum(jnp.abs(den), jnp.exp(-m_t))
        hh = hh * lax.rsqrt(jnp.mean(hh * hh, axis=-1, keepdims=True) + EPS) * ng_ref[:, sl]
        y_ref[:, sl] = (_sigmoid(o_ref[:, sl]) * hh).astype(y_ref.dtype)

        stat = jnp.where(lane == 0, bcol, jnp.where(lane == 1, m_t, 0.0))
        last = _last_of_seq(stat, bb, lc)
        last_rows = _rows_from_seq(last, bb, lc)
        b_last, m_new = last_rows[:, 0:1], last_rows[:, 1:2]
        w_end = jnp.exp(b_last - bcol + igcol - m_new)
        f_end = jnp.exp(last[:, :, 0:1] + m0 - last[:, :, 1:2])
        kw3 = (k * w_end).reshape(bb, lc, ML_HD)
        upd = jnp.einsum('bsd,bse->bde', kw3.astype(BF16), v.reshape(bb, lc, ML_HD).astype(BF16),
                         preferred_element_type=F32)
        c_ref[0, :, hd] = f_end * c0 + upd
        n_ref[0, :, hd:hd + 1, :] = f_end * n0 + jnp.sum(kw3, axis=1, keepdims=True)
        m_ref[0, :, :, hd:hd + 1] = last[:, :, 1:2]


def _mlstm_branch(z, z_gates, init, lyr_in, gate_bias, norm_g, row0, n_seq, seq_len, bb, lc, depth, lyr, prev):
    rows = bb * lc
    nc = seq_len // lc
    rb0 = row0 // rows
    rowmap = lambda part: (lambda b, c: (rb0 + b * nc + c, part))
    zspec = lambda part: pl.BlockSpec((rows, MIX_W), rowmap(part))
    state_specs = lambda at: [pl.BlockSpec((1, bb, ML_HEADS, ML_HD, ML_HD), lambda b, c: (at, b, 0, 0, 0)),
                              pl.BlockSpec((1, bb, ML_HEADS, ML_HD), lambda b, c: (at, b, 0, 0)),
                              pl.BlockSpec((1, bb, 1, ML_HEADS), lambda b, c: (at, b, 0, 0))]
    kern, prev_specs, prev_args, aliases = _carry_through(functools.partial(_mlstm_kernel, bb, lc), 10, prev)
    return pl.pallas_call(
        kern,
        grid=(n_seq // bb, nc),
        in_specs=[zspec(0), zspec(1), zspec(2), zspec(3),
                  pl.BlockSpec((rows, LANES), lambda b, c: (rb0 + b * nc + c, Z_MLG // LANES)),
                  pl.BlockSpec((1, LANES), lambda b, c: (0, 0)), pl.BlockSpec((1, MIX_W), lambda b, c: (0, 0))]
                 + state_specs(lyr_in) + prev_specs,
        out_specs=[pl.BlockSpec((rows, MIX_W), lambda b, c: (rb0 + b * nc + c, 0))] + state_specs(lyr),
        out_shape=[jax.ShapeDtypeStruct((z.shape[0], MIX_W), BF16),
                   jax.ShapeDtypeStruct((depth, n_seq, ML_HEADS, ML_HD, ML_HD), F32),
                   jax.ShapeDtypeStruct((depth, n_seq, ML_HEADS, ML_HD), F32),
                   jax.ShapeDtypeStruct((depth, n_seq, 1, ML_HEADS), F32)],
        input_output_aliases=aliases,
        compiler_params=_cparams("parallel", "arbitrary"),
        name="mlstm_chunk",
    )(z, z, z, z, z_gates, gate_bias, norm_g, *init, *prev_args)


def _rope_rows(x, cos, sin):
    half = RT_HD // 2
    x1, x2 = x[:, :half], x[:, half:]
    return jnp.concatenate([x1 * cos - x2 * sin, x1 * sin + x2 * cos], axis=1)


def _ret_kernel(bb, lc, q_ref, k_ref, v_ref, g_ref, cs_ref, dec_ref, cd_ref, s0_ref, y_ref, s_ref):
    rows = bb * lc

    @pl.when(pl.program_id(1) == 0)
    def _():
        s_ref[...] = s0_ref[...]

    cos, sin = cs_ref[:, :RT_HD // 2], cs_ref[:, RT_HD // 2:]
    for hd in range(RT_HEADS):
        sl = slice(hd * RT_HD, (hd + 1) * RT_HD)
        s0 = s_ref[0, :, hd]
        q = _rope_rows(q_ref[:, sl], cos, sin)
        k = _rope_rows(k_ref[:, sl], cos, sin) * (RT_HD ** -0.5)
        v = v_ref[:, sl]
        cd = cd_ref[hd]
        qk = lax.dot_general(q.astype(BF16), k.astype(BF16), (((1,), (1,)), ((), ())),
                             preferred_element_type=F32)
        inner = _bdot(qk * dec_ref[hd], v)
        q3 = q.reshape(bb, lc, RT_HD).astype(BF16)
        cross = jnp.einsum('btd,bde->bte', q3, s0.astype(BF16), preferred_element_type=F32).reshape(rows, RT_HD)
        y = inner + cross * cd[:, 0:1]
        y = y * lax.rsqrt(jnp.mean(y * y, axis=-1, keepdims=True) + EPS)
        gt = g_ref[:, sl]
        y_ref[:, sl] = (gt * _sigmoid(gt) * y).astype(y_ref.dtype)
        kd = (k * cd[:, 1:2]).reshape(bb, lc, RT_HD).astype(BF16)
        upd = jnp.einsum('bsd,bse->bde', kd, v.reshape(bb, lc, RT_HD).astype(BF16), preferred_element_type=F32)
        s_ref[0, :, hd] = cd[0:1, 2:3] * s0 + upd


def _ret_branch(z, s0, lyr_in, row0, n_seq, seq_len, bb, lc, pos0, depth, lyr, prev):
    rows = bb * lc
    nc = seq_len // lc
    rb0 = row0 // rows
    half = RT_HD // 2
    inv = ROPE_BASE ** (-jnp.arange(half, dtype=F32) / half)
    pos = jnp.broadcast_to((jnp.arange(nc * lc, dtype=F32) + float(pos0)).reshape(nc, 1, lc), (nc, bb, lc))
    ang = pos.reshape(nc * rows)[:, None] * inv[None, :]
    cs = jnp.concatenate([jnp.cos(ang), jnp.sin(ang)], axis=1)
    log_g = jnp.log(1.0 - jnp.exp(jnp.linspace(math.log(1.0 / 32), math.log(1.0 / 512), RT_HEADS)))
    idx = jnp.arange(rows)
    t = (idx % lc).astype(F32)
    diff = t[:, None] - t[None, :]
    same = (idx[:, None] // lc) == (idx[None, :] // lc)
    dec = jnp.where(same & (diff >= 0), jnp.exp(log_g[:, None, None] * jnp.maximum(diff, 0.0)), 0.0)
    cd = jnp.zeros((RT_HEADS, rows, LANES), F32)
    cd = cd.at[:, :, 0].set(jnp.exp(log_g[:, None] * (t[None, :] + 1.0)))
    cd = cd.at[:, :, 1].set(jnp.exp(log_g[:, None] * (lc - 1.0 - t)[None, :]))
    cd = cd.at[:, :, 2].set(jnp.broadcast_to(jnp.exp(log_g * lc)[:, None], (RT_HEADS, rows)))
    rowmap = lambda part: (lambda b, c: (rb0 + b * nc + c, part))
    zspec = lambda part: pl.BlockSpec((rows, MIX_W), rowmap(part))
    sspec = lambda at: pl.BlockSpec((1, bb, RT_HEADS, RT_HD, RT_HD), lambda b, c: (at, b, 0, 0, 0))
    kern, prev_specs, prev_args, aliases = _carry_through(functools.partial(_ret_kernel, bb, lc), 8, prev)
    return pl.pallas_call(
        kern,
        grid=(n_seq // bb, nc),
        in_specs=[zspec(0), zspec(1), zspec(2), zspec(3),
                  pl.BlockSpec((rows, RT_HD), lambda b, c: (c, 0)),
                  pl.BlockSpec((RT_HEADS, rows, rows), lambda b, c: (0, 0, 0)),
                  pl.BlockSpec((RT_HEADS, rows, LANES), lambda b, c: (0, 0, 0)),
                  sspec(lyr_in)] + prev_specs,
        out_specs=[pl.BlockSpec((rows, MIX_W), lambda b, c: (rb0 + b * nc + c, 0)), sspec(lyr)],
        out_shape=[jax.ShapeDtypeStruct((z.shape[0], MIX_W), BF16),
                   jax.ShapeDtypeStruct((depth, n_seq, RT_HEADS, RT_HD, RT_HD), F32)],
        input_output_aliases=aliases,
        compiler_params=_cparams("parallel", "arbitrary"),
        name="retention_chunk",
    )(z, z, z, z, cs, dec, cd, s0, *prev_args)


def _split_w_in(w_in):
    depth, d, _ = w_in.shape
    pad = jnp.zeros((depth, d, Z_RWB - RW_COLS - 2 * ML_HEADS), BF16)
    w_rw = jnp.concatenate([w_in[:, :, :RW_COLS].astype(BF16), w_in[:, :, IN_MLG:IN_RT].astype(BF16), pad], axis=2)
    return w_rw, w_in[:, :, IN_ML:IN_MLG].astype(BF16), w_in[:, :, IN_RT:].astype(BF16)


def _group_cfg(n_seq, seq_len):
    if seq_len % CHUNK == 0:
        return dict(rw_bb=1, rw_lt=256, rw_tc=32, ch_bb=1, ch_lc=CHUNK, at_bb=1, at_lq=512)
    return dict(rw_bb=32, rw_lt=seq_len, rw_tc=seq_len, ch_bb=4, ch_lc=seq_len, at_bb=8, at_lq=seq_len)


def kernel(x_prompt, x_sample, mem_prompt, state_rwkv_shift, state_rwkv, state_mlstm_c, state_mlstm_n,
           state_mlstm_m, state_ret, cache_mem_k, cache_mem_v, g_pre_mix, g_post_mix, g_pre_x, g_post_x,
           g_pre_ff, g_post_ff, g_mem, w_in, rw_mu, rw_w0, rw_w_up, rw_a0, rw_a_up, rw_g_up, rw_k_k, rw_k_a,
           rw_r_k, rw_gn_g, rw_gn_b, ml_i_b, ml_f_b, ml_norm_g, w_br, w_out, x_wq, x_wkv, x_wo, ff_w1, ff_w2):
    bp, lp, d = x_prompt.shape
    bs, ls, _ = x_sample.shape
    depth = w_in.shape[0]
    n_mem = mem_prompt.shape[1]
    tp, ts = bp * lp, bs * ls
    groups = [(0, bp, lp, 0), (tp, bs, ls, PAST_LEN)]

    x = jnp.concatenate([x_prompt.reshape(tp, d), x_sample.reshape(ts, d)], axis=0)
    mem = mem_prompt.reshape(bp * n_mem, d)
    h = _rmsnorm(x, g_pre_mix[0])

    zeros = lambda *s: jnp.zeros(s, F32)
    inits = [
        lambda l: (zeros(bp, 1, RW_COLS), zeros(bp, RW_HEADS, RW_HD, RW_HD),
                   (zeros(1, bp, ML_HEADS, ML_HD, ML_HD), zeros(1, bp, ML_HEADS, ML_HD), zeros(1, bp, 1, ML_HEADS)),
                   zeros(1, bp, RT_HEADS, RT_HD, RT_HD), 0),
        lambda l: (state_rwkv_shift[l].reshape(bs, 1, RW_COLS), state_rwkv[l],
                   (state_mlstm_c, state_mlstm_n, state_mlstm_m.reshape(depth, bs, 1, ML_HEADS)), state_ret, l),
    ]
    rw_states = [[], []]
    ml_states = [[None] * 3, [None] * 3]
    rt_states = [None, None]
    mem_k, mem_v = [], []
    w_rw, w_ml, w_rg = _split_w_in(w_in)
    w_kv, w_brb, w_outb, w_q, w_o, w_f1, w_f2 = (w.astype(BF16) for w in (x_wkv, w_br, w_out, x_wq, x_wo,
                                                                         ff_w1, ff_w2))
    for l in range(depth):
        kv = _mm(_rmsnorm(mem, g_mem[l], tm=256), w_kv, l, tm=256, tn=512)
        mem_k.append(kv[:, :X_W].reshape(bp, n_mem, X_HEADS, X_HD))
        mem_v.append(kv[:, X_W:].reshape(bp, n_mem, X_HEADS, X_HD))

        z_rw = _mm(h, w_rw, l, tm=1024, tn=Z_RWB // 2)
        z_ml = _mm(h, w_ml, l, tm=1024, tn=1024)
        z_rg = _mm(h, w_rg, l, tm=1024, tn=1280)

        rw_p = dict(mu=rw_mu[l].reshape(1, -1), w0=rw_w0[l].reshape(1, -1), w_up=rw_w_up[l].astype(BF16),
                    a0=rw_a0[l].reshape(1, -1), a_up=rw_a_up[l].astype(BF16), g_up=rw_g_up[l].astype(BF16),
                    k_k=rw_k_k[l].reshape(1, -1), k_a=rw_k_a[l].reshape(1, -1), r_k=rw_r_k[l].reshape(1, -1),
                    gn_g=rw_gn_g[l].reshape(1, -1), gn_b=rw_gn_b[l].reshape(1, -1))
        gate_bias = jnp.concatenate([ml_i_b[l], ml_f_b[l], zeros(LANES - 2 * ML_HEADS)]).reshape(1, LANES)
        norm_g = ml_norm_g[l].reshape(1, MIX_W)

        y_rw = y_ml = y_rt = None
        for gi, (row0, n_seq, seq_len, pos0) in enumerate(groups):
            cfg = _group_cfg(n_seq, seq_len)
            shift0, s_rw0, ml0, rt0, lyr_in = inits[gi](l)
            y_rw, s_rw = _rwkv_branch(z_rw, shift0, s_rw0, rw_p, row0, n_seq, seq_len,
                                      cfg['rw_bb'], cfg['rw_lt'], cfg['rw_tc'], y_rw)
            y_ml, *ml_states[gi] = _mlstm_branch(z_ml, z_rw, ml0, lyr_in, gate_bias, norm_g, row0, n_seq, seq_len,
                                                 cfg['ch_bb'], cfg['ch_lc'], depth, l, [y_ml] + ml_states[gi])
            y_rt, rt_states[gi] = _ret_branch(z_rg, rt0, lyr_in, row0, n_seq, seq_len, cfg['ch_bb'],
                                              cfg['ch_lc'], pos0, depth, l, [y_rt, rt_states[gi]])
            last = row0 + seq_len - 1
            shift = lax.slice(z_rw, (last, 0), (last + (n_seq - 1) * seq_len + 1, RW_COLS), (seq_len, 1))
            rw_states[gi].append((shift, s_rw))

        merged = _merge(y_rw, y_ml, y_rt, w_brb, l, z_rg, Z_GATE)
        x, h = _proj_res(merged, w_outb, l, x, g_post_mix[l], g_pre_x[l])

        for gi, (row0, n_seq, seq_len, pos0) in enumerate(groups):
            cfg = _group_cfg(n_seq, seq_len)
            if gi == 0:
                mk, mv, lyr_in = mem_k[l][None], mem_v[l][None], 0
            else:
                mk, mv, lyr_in = cache_mem_k, cache_mem_v, l
            x, h = _attn(h, x, mk, mv, lyr_in, w_q, w_o, l, g_post_x[l], g_pre_ff[l], row0, n_seq, seq_len,
                         cfg['at_bb'], cfg['at_lq'])

        g_next = g_pre_mix[(l + 1) % depth]
        x, h = _mlp(h, w_f1, w_f2, l, x, g_post_ff[l], g_next)

    def states(gi, n_seq):
        c_new, n_new, m_new = ml_states[gi]
        return (jnp.stack([s[0] for s in rw_states[gi]]), jnp.stack([s[1] for s in rw_states[gi]]),
                c_new, n_new, m_new.reshape(depth, n_seq, ML_HEADS), rt_states[gi])

    return (x[:tp].reshape(bp, lp, d), x[tp:].reshape(bs, ls, d), *states(0, bp),
            jnp.stack(mem_k), jnp.stack(mem_v), *states(1, bs))
```

```python
import functools
import math

import jax
import jax.numpy as jnp
from jax import lax
from jax.experimental import pallas as pl
from jax.experimental.pallas import tpu as pltpu

F32 = jnp.float32
BF16 = jnp.bfloat16

D_MODEL = 2048
MIX_W = D_MODEL // 2
RW_HD = 64
RW_HEADS = MIX_W // RW_HD
RW_LORA = (64, 64, 128)
RW_COLS = 3 * MIX_W + sum(RW_LORA)
RW_GN_EPS = 64e-5
ML_HEADS = 8
ML_HD = MIX_W // ML_HEADS
ML_GATE_CAP = 15.0
RT_HEADS = 4
RT_HD = MIX_W // RT_HEADS
ROPE_BASE = 10000.0
X_HEADS = 4
X_HD = 128
X_W = X_HEADS * X_HD
D_FF = 4 * D_MODEL
CHUNK = 64
EPS = 1e-6
PAST_LEN = 16384

IN_ML = RW_COLS
IN_MLG = IN_ML + 4 * MIX_W
IN_RT = IN_MLG + 2 * ML_HEADS
Z_MLG = RW_COLS
Z_RWB = RW_COLS + 256
Z_GATE = 4 * MIX_W

LANES = 128
RW_VP = RW_HD // 2
RW_CHAINS = LANES // 2


def _cparams(*sem):
    return pltpu.CompilerParams(dimension_semantics=sem, vmem_limit_bytes=56 * 1024 * 1024)


def _carry_through(kernel_fn, n_in, prev):
    carried = [(i, p) for i, p in enumerate(prev) if p is not None]
    if not carried:
        return kernel_fn, [], [], {}

    def wrapped(*refs):
        return kernel_fn(*refs[:n_in], *refs[n_in + len(carried):])

    return (wrapped, [pl.BlockSpec(memory_space=pl.ANY)] * len(carried), [p for _, p in carried],
            {n_in + j: i for j, (i, _) in enumerate(carried)})


def _sigmoid(x):
    return 1.0 / (1.0 + jnp.exp(-x))


def _softplus(x):
    return jnp.maximum(x, 0.0) + jnp.log(1.0 + jnp.exp(-jnp.abs(x)))


def _bdot(a, b):
    return jnp.dot(a.astype(BF16), b.astype(BF16), preferred_element_type=F32)


def _split3(x):
    hi = x.astype(BF16)
    r1 = x - hi.astype(F32)
    mid = r1.astype(BF16)
    lo = (r1 - mid.astype(F32)).astype(BF16)
    return hi, mid, lo


def _dot_exact_lhs(m, x):
    mb = m.astype(BF16)
    hi, mid, lo = _split3(x)
    return (jnp.dot(mb, hi, preferred_element_type=F32) + jnp.dot(mb, mid, preferred_element_type=F32)
            + jnp.dot(mb, lo, preferred_element_type=F32))


def _dot_exact_rhs(x, m):
    mb = m.astype(BF16)
    hi, mid, lo = _split3(x)
    return (jnp.dot(hi, mb, preferred_element_type=F32) + jnp.dot(mid, mb, preferred_element_type=F32)
            + jnp.dot(lo, mb, preferred_element_type=F32))


def _segsum(x, seg):
    blk = 256
    ri = lax.broadcasted_iota(jnp.int32, (blk, blk), 0) // seg
    ci = lax.broadcasted_iota(jnp.int32, (blk, blk), 1) // seg
    ones = (ri == ci).astype(F32)
    parts = [_dot_exact_rhs(x[:, c:c + blk], ones) for c in range(0, x.shape[1], blk)]
    return parts[0] if len(parts) == 1 else jnp.concatenate(parts, axis=1)


def _rms_rows(x, g):
    return x * lax.rsqrt(jnp.mean(x * x, axis=-1, keepdims=True) + EPS) * g


def _rmsnorm_kernel(x_ref, g_ref, o_ref):
    o_ref[...] = _rms_rows(x_ref[...], g_ref[...]).astype(o_ref.dtype)


def _rmsnorm(x, g, tm=512):
    m, d = x.shape
    return pl.pallas_call(
        _rmsnorm_kernel,
        grid=(m // tm,),
        in_specs=[pl.BlockSpec((tm, d), lambda i: (i, 0)), pl.BlockSpec((1, d), lambda i: (0, 0))],
        out_specs=pl.BlockSpec((tm, d), lambda i: (i, 0)),
        out_shape=jax.ShapeDtypeStruct((m, d), BF16),
        compiler_params=_cparams("parallel"),
        name="rmsnorm",
    )(x, g.reshape(1, d))


def _mm_kernel(x_ref, w_ref, o_ref):
    o_ref[...] = jnp.dot(x_ref[...], w_ref[0], preferred_element_type=F32)


def _mm(x, w, lyr, tm, tn):
    m, k = x.shape
    n = w.shape[2]
    return pl.pallas_call(
        _mm_kernel,
        grid=(n // tn, m // tm),
        in_specs=[pl.BlockSpec((tm, k), lambda j, i: (i, 0)), pl.BlockSpec((1, k, tn), lambda j, i: (lyr, 0, j))],
        out_specs=pl.BlockSpec((tm, tn), lambda j, i: (i, j)),
        out_shape=jax.ShapeDtypeStruct((m, n), F32),
        compiler_params=_cparams("parallel", "parallel"),
        name="matmul",
    )(x, w)


def _merge_kernel(y0_ref, y1_ref, y2_ref, w_ref, g0_ref, g1_ref, g2_ref, o_ref):
    acc = _sigmoid(g0_ref[...]) * jnp.dot(y0_ref[...], w_ref[0, 0], preferred_element_type=F32)
    acc += _sigmoid(g1_ref[...]) * jnp.dot(y1_ref[...], w_ref[0, 1], preferred_element_type=F32)
    acc += _sigmoid(g2_ref[...]) * jnp.dot(y2_ref[...], w_ref[0, 2], preferred_element_type=F32)
    o_ref[...] = acc.astype(o_ref.dtype)


def _merge(y_rw, y_ml, y_rt, w_br, lyr, z, gate_col0, tm=512, tn=1024):
    m, kw = y_rw.shape
    d = w_br.shape[3]
    gb = gate_col0 // tn
    gspec = lambda c: pl.BlockSpec((tm, tn), lambda j, i: (i, gb + c * (d // tn) + j))
    yspec = pl.BlockSpec((tm, kw), lambda j, i: (i, 0))
    return pl.pallas_call(
        _merge_kernel,
        grid=(d // tn, m // tm),
        in_specs=[yspec, yspec, yspec, pl.BlockSpec((1, 3, kw, tn), lambda j, i: (lyr, 0, 0, j)),
                  gspec(0), gspec(1), gspec(2)],
        out_specs=pl.BlockSpec((tm, tn), lambda j, i: (i, j)),
        out_shape=jax.ShapeDtypeStruct((m, d), BF16),
        compiler_params=_cparams("parallel", "parallel"),
        name="branch_merge",
    )(y_rw, y_ml, y_rt, w_br, z, z, z)


def _residual_epilogue(acc, x_ref, gpost_ref, gnext_ref, xo_ref, ho_ref):
    xn = x_ref[...] + _rms_rows(acc, gpost_ref[...])
    xo_ref[...] = xn
    ho_ref[...] = _rms_rows(xn, gnext_ref[...]).astype(ho_ref.dtype)


def _proj_res_kernel(a_ref, w_ref, x_ref, gpost_ref, gnext_ref, xo_ref, ho_ref):
    acc = jnp.dot(a_ref[...], w_ref[0], preferred_element_type=F32)
    _residual_epilogue(acc, x_ref, gpost_ref, gnext_ref, xo_ref, ho_ref)


def _proj_res(a, w, lyr, x, g_post, g_next, tm=512):
    m, k = a.shape
    d = w.shape[2]
    row = lambda i: (i, 0)
    fixed = lambda i: (0, 0)
    return pl.pallas_call(
        _proj_res_kernel,
        grid=(m // tm,),
        in_specs=[pl.BlockSpec((tm, k), row), pl.BlockSpec((1, k, d), lambda i: (lyr, 0, 0)),
                  pl.BlockSpec((tm, d), row),
                  pl.BlockSpec((1, d), fixed), pl.BlockSpec((1, d), fixed)],
        out_specs=[pl.BlockSpec((tm, d), row), pl.BlockSpec((tm, d), row)],
        out_shape=[jax.ShapeDtypeStruct((m, d), F32), jax.ShapeDtypeStruct((m, d), BF16)],
        compiler_params=_cparams("parallel"),
        name="out_proj_residual",
    )(a, w, x, g_post.reshape(1, d), g_next.reshape(1, d))


def _mlp_kernel(h_ref, w1_ref, w2_ref, x_ref, gpost_ref, gnext_ref, xo_ref, ho_ref, acc_ref):
    k = pl.program_id(1)

    @pl.when(k == 0)
    def _():
        acc_ref[...] = jnp.zeros_like(acc_ref)

    u = jnp.dot(h_ref[...], w1_ref[0], preferred_element_type=F32)
    u = jnp.square(jnp.maximum(u, 0.0)).astype(BF16)
    acc_ref[...] += jnp.dot(u, w2_ref[0], preferred_element_type=F32)

    @pl.when(k == pl.num_programs(1) - 1)
    def _():
        _residual_epilogue(acc_ref[...], x_ref, gpost_ref, gnext_ref, xo_ref, ho_ref)


def _mlp(h, w1, w2, lyr, x, g_post, g_next, tm=512, tf=1024):
    m, d = h.shape
    f = w1.shape[2]
    row = lambda i, k: (i, 0)
    fixed = lambda i, k: (0, 0)
    return pl.pallas_call(
        _mlp_kernel,
        grid=(m // tm, f // tf),
        in_specs=[pl.BlockSpec((tm, d), row), pl.BlockSpec((1, d, tf), lambda i, k: (lyr, 0, k)),
                  pl.BlockSpec((1, tf, d), lambda i, k: (lyr, k, 0)), pl.BlockSpec((tm, d), row),
                  pl.BlockSpec((1, d), fixed), pl.BlockSpec((1, d), fixed)],
        out_specs=[pl.BlockSpec((tm, d), row), pl.BlockSpec((tm, d), row)],
        out_shape=[jax.ShapeDtypeStruct((m, d), F32), jax.ShapeDtypeStruct((m, d), BF16)],
        scratch_shapes=[pltpu.VMEM((tm, d), F32)],
        compiler_params=_cparams("parallel", "arbitrary"),
        name="relu2_mlp",
    )(h, w1, w2, x, g_post.reshape(1, d), g_next.reshape(1, d))


def _attn_kernel(bb, lq, h_ref, wq_ref, mk_ref, mv_ref, wo_ref, x_ref, gpost_ref, gnext_ref, xo_ref, ho_ref):
    q = jnp.dot(h_ref[...], wq_ref[0], preferred_element_type=F32)
    outs = []
    for hd in range(X_HEADS):
        sl = slice(hd * X_HD, (hd + 1) * X_HD)
        q3 = q[:, sl].reshape(bb, lq, X_HD).astype(BF16)
        k3 = mk_ref[0, :, :, hd, :].astype(BF16)
        v3 = mv_ref[0, :, :, hd, :].astype(BF16)
        s = jnp.einsum('bqd,bmd->bqm', q3, k3, preferred_element_type=F32) * (X_HD ** -0.5)
        s = s - jnp.max(s, axis=-1, keepdims=True)
        e = jnp.exp(s)
        p = (e / jnp.sum(e, axis=-1, keepdims=True)).astype(BF16)
        o3 = jnp.einsum('bqm,bmd->bqd', p, v3, preferred_element_type=F32)
        outs.append(o3.reshape(bb * lq, X_HD))
    o = jnp.concatenate(outs, axis=1).astype(BF16)
    acc = jnp.dot(o, wo_ref[0], preferred_element_type=F32)
    _residual_epilogue(acc, x_ref, gpost_ref, gnext_ref, xo_ref, ho_ref)


def _attn(h, x, mk, mv, lyr, wq, wo, wl, g_post, g_next, row0, n_seq, seq_len, bb, lq):
    t, d = h.shape
    rows = bb * lq
    nl = seq_len // lq
    n_mem = mk.shape[2]
    rb0 = row0 // rows
    rowmap = lambda b, l: (rb0 + b * nl + l, 0)
    fixed = lambda b, l: (0, 0)
    kvspec = pl.BlockSpec((1, bb, n_mem, X_HEADS, X_HD), lambda b, l: (lyr, b, 0, 0, 0))
    return pl.pallas_call(
        functools.partial(_attn_kernel, bb, lq),
        grid=(n_seq // bb, nl),
        in_specs=[pl.BlockSpec((rows, d), rowmap), pl.BlockSpec((1, d, X_W), lambda b, l: (wl, 0, 0)), kvspec,
                  kvspec, pl.BlockSpec((1, X_W, d), lambda b, l: (wl, 0, 0)), pl.BlockSpec((rows, d), rowmap),
                  pl.BlockSpec((1, d), fixed), pl.BlockSpec((1, d), fixed)],
        out_specs=[pl.BlockSpec((rows, d), rowmap), pl.BlockSpec((rows, d), rowmap)],
        out_shape=[jax.ShapeDtypeStruct((t, d), F32), jax.ShapeDtypeStruct((t, d), BF16)],
        input_output_aliases={5: 0, 0: 1},
        compiler_params=_cparams("parallel", "parallel"),
        name="mem_cross_attention",
    )(h, wq, mk, mv, wo, x, g_post.reshape(1, d), g_next.reshape(1, d))


def _rwkv_prep_kernel(bb, lt, u_ref, s0_ref, mu_ref, w0_ref, wup_ref, a0_ref, aup_ref, gup_ref, kk_ref, ka_ref,
                      rk_ref, r_o, w_o, k_o, kk_o, b_o, v_o, g_o, bonus_o, carry_ref):
    l = pl.program_id(1)
    rows = bb * lt
    u = u_ref[:, :RW_COLS]
    if bb == 1:
        first = jnp.where(l == 0, s0_ref[0], carry_ref[...])
        first_rows = jnp.broadcast_to(first, (rows, RW_COLS))
    else:
        first_rows = jnp.broadcast_to(s0_ref[...], (bb, lt, RW_COLS)).reshape(rows, RW_COLS)
    rowid = lax.broadcasted_iota(jnp.int32, (rows, 1), 0) % lt
    prev = jnp.where(rowid == 0, first_rows, pltpu.roll(u, 1, 0))
    if bb == 1:
        carry_ref[...] = u[rows - 1:rows, :]
    z = u + (prev - u) * mu_ref[...]
    r = z[:, 0:MIX_W]
    k = z[:, MIX_W:2 * MIX_W]
    v = z[:, 2 * MIX_W:3 * MIX_W]
    o = 3 * MIX_W
    wd = z[:, o:o + RW_LORA[0]]
    ad = z[:, o + RW_LORA[0]:o + RW_LORA[0] + RW_LORA[1]]
    gd = z[:, o + RW_LORA[0] + RW_LORA[1]:RW_COLS]
    w_log = -_softplus(-(w0_ref[...] + _bdot(jnp.tanh(wd), wup_ref[...]))) - 0.5
    decay = jnp.exp(-jnp.exp(w_log))
    a = _sigmoid(a0_ref[...] + _bdot(ad, aup_ref[...]))
    g = _bdot(_sigmoid(gd), gup_ref[...])
    kk = k * kk_ref[...]
    kk = kk / jnp.maximum(jnp.sqrt(_segsum(kk * kk, RW_HD)), 1e-12)
    k = k * (1.0 + (a - 1.0) * ka_ref[...])
    r_o[...] = r
    w_o[...] = decay
    k_o[...] = k
    kk_o[...] = kk
    b_o[...] = kk * a
    v_o[...] = v
    g_o[...] = g
    bonus_o[...] = _segsum(r * k * rk_ref[...], RW_HD) * v


def _rwkv_prep(z, shift0, p, row0, n_seq, seq_len, bb, lt):
    rows = bb * lt
    nl = seq_len // lt
    assert bb == 1 or nl == 1
    rb0 = row0 // rows
    m = n_seq * seq_len
    fixed = lambda b, l: (0, 0)
    outmap = lambda b, l: (b * nl + l, 0)
    vec = lambda width: pl.BlockSpec((1, width), fixed)
    ospec = pl.BlockSpec((rows, MIX_W), outmap)
    oshape = jax.ShapeDtypeStruct((m, MIX_W), F32)
    return pl.pallas_call(
        functools.partial(_rwkv_prep_kernel, bb, lt),
        grid=(n_seq // bb, nl),
        in_specs=[pl.BlockSpec((rows, Z_RWB), lambda b, l: (rb0 + b * nl + l, 0)),
                  pl.BlockSpec((bb, 1, RW_COLS), lambda b, l: (b, 0, 0)),
                  vec(RW_COLS), vec(MIX_W), pl.BlockSpec((RW_LORA[0], MIX_W), fixed), vec(MIX_W),
                  pl.BlockSpec((RW_LORA[1], MIX_W), fixed), pl.BlockSpec((RW_LORA[2], MIX_W), fixed),
                  vec(MIX_W), vec(MIX_W), vec(MIX_W)],
        out_specs=[ospec] * 8,
        out_shape=[oshape] * 8,
        scratch_shapes=[pltpu.VMEM((1, RW_COLS), F32)],
        compiler_params=_cparams("parallel", "arbitrary"),
        name="rwkv_prep",
    )(z, shift0, p['mu'], p['w0'], p['w_up'], p['a0'], p['a_up'], p['g_up'], p['k_k'], p['k_a'], p['r_k'])


def _rwkv_scan_kernel(tc, r_ref, w_ref, k_ref, kk_ref, b_ref, v_ref, s0_ref, y_ref, s_ref):
    @pl.when(pl.program_id(1) == 0)
    def _():
        s_ref[...] = s0_ref[...]

    y_ref[0, :, RW_VP:, :] = jnp.zeros((tc, RW_VR - RW_VP, LANES), F32)
    keys = lambda ref, t: ref[0, t, :RW_HD, :]

    def step(t, carry):
        for vp in range(RW_VP):
            s = s_ref[0, vp]
            sa = jnp.sum(s * keys(kk_ref, t), axis=0, keepdims=True)
            sn = s * keys(w_ref, t) - sa * keys(b_ref, t) + v_ref[0, t, vp:vp + 1, :] * keys(k_ref, t)
            s_ref[0, vp] = sn
            y_ref[0, t, vp:vp + 1, :] = jnp.sum(sn * keys(r_ref, t), axis=0, keepdims=True)
        return carry

    lax.fori_loop(0, tc, step, 0)


def _rwkv_scan(r, w, k, kk, b, v, s0, lyr_in, tc, depth, lyr, s_prev):
    g, l = r.shape[0], r.shape[1]
    kspec = pl.BlockSpec((1, tc, RW_KR, LANES), lambda i, c: (i, c, 0, 0))
    vspec = pl.BlockSpec((1, tc, RW_VR, LANES), lambda i, c: (i, c, 0, 0))
    sspec = lambda at: pl.BlockSpec((1, RW_VP, RW_HD, LANES), lambda i, c: (at * g + i, 0, 0, 0))
    kern, prev_specs, prev_args, aliases = _carry_through(functools.partial(_rwkv_scan_kernel, tc), 7,
                                                          [None, s_prev])
    return pl.pallas_call(
        kern,
        grid=(g, l // tc),
        in_specs=[kspec] * 5 + [vspec, sspec(lyr_in)] + prev_specs,
        out_specs=[vspec, sspec(lyr)],
        out_shape=[jax.ShapeDtypeStruct((g, l, RW_VR, LANES), F32),
                   jax.ShapeDtypeStruct((depth * g, RW_VP, RW_HD, LANES), F32)],
        input_output_aliases=aliases,
        compiler_params=_cparams("parallel", "arbitrary"),
        name="rwkv_scan",
    )(r, w, k, kk, b, v, s0, *prev_args)


def _rwkv_post_kernel(y_ref, g_ref, bonus_ref, gng_ref, gnb_ref, o_ref):
    y = y_ref[...]
    mean = _segsum(y, RW_HD) * (1.0 / RW_HD)
    yc = y - mean
    var = _segsum(yc * yc, RW_HD) * (1.0 / RW_HD)
    yn = yc * lax.rsqrt(var + RW_GN_EPS) * gng_ref[...] + gnb_ref[...]
    o_ref[...] = ((yn + bonus_ref[...]) * g_ref[...]).astype(o_ref.dtype)


def _rwkv_post(y, g, bonus, gn_g, gn_b, row0, total_rows, prev, tm=256):
    m = y.shape[0]
    rb0 = row0 // tm
    fixed = lambda i: (0, 0)
    spec = pl.BlockSpec((tm, MIX_W), lambda i: (i, 0))
    kern, prev_specs, prev_args, aliases = _carry_through(_rwkv_post_kernel, 5, [prev])
    return pl.pallas_call(
        kern,
        grid=(m // tm,),
        in_specs=[spec, spec, spec, pl.BlockSpec((1, MIX_W), fixed), pl.BlockSpec((1, MIX_W), fixed)] + prev_specs,
        out_specs=pl.BlockSpec((tm, MIX_W), lambda i: (rb0 + i, 0)),
        out_shape=jax.ShapeDtypeStruct((total_rows, MIX_W), BF16),
        input_output_aliases=aliases,
        compiler_params=_cparams("parallel"),
        name="rwkv_post",
    )(y, g, bonus, gn_g, gn_b, *prev_args)


RW_NB = RW_CHAINS // RW_HEADS
RW_TT = LANES


RW_KR = RW_HD + 8
RW_VR = RW_VP + 8


def _natural_to_chain_rows(x_ref, a_ref):
    for bl in range(RW_NB):
        xt = x_ref[bl].T
        for h in range(RW_HEADS):
            c = bl * RW_HEADS + h
            a_ref[c * RW_KR:c * RW_KR + RW_HD, :] = xt[h * RW_HD:(h + 1) * RW_HD]


def _keys_to_scan_kernel(x_ref, o_ref, a_ref):
    _natural_to_chain_rows(x_ref, a_ref)
    o_ref[...] = jnp.zeros_like(o_ref)
    for k in range(RW_HD):
        tile = a_ref[pl.ds(k, RW_CHAINS, stride=RW_KR), :]
        o_ref[0, pl.ds(k, RW_TT, stride=RW_KR), :] = jnp.concatenate([tile, tile], axis=0).T


def _vals_to_scan_kernel(x_ref, o_ref, a_ref):
    _natural_to_chain_rows(x_ref, a_ref)
    o_ref[...] = jnp.zeros_like(o_ref)
    for vp in range(RW_VP):
        halves = [a_ref[pl.ds(vh * RW_VP + vp, RW_CHAINS, stride=RW_KR), :] for vh in range(2)]
        o_ref[0, pl.ds(vp, RW_TT, stride=RW_VR), :] = jnp.concatenate(halves, axis=0).T


def _vals_from_scan_kernel(y_ref, o_ref, a_ref):
    for vp in range(RW_VP):
        tile = y_ref[0, pl.ds(vp, RW_TT, stride=RW_VR), :].T
        for vh in range(2):
            a_ref[pl.ds(vh * RW_VP + vp, RW_CHAINS, stride=RW_KR), :] = tile[vh * RW_CHAINS:(vh + 1) * RW_CHAINS]
    for bl in range(RW_NB):
        rows = [a_ref[c * RW_KR:c * RW_KR + RW_HD, :] for c in range(bl * RW_HEADS, (bl + 1) * RW_HEADS)]
        o_ref[bl] = jnp.concatenate(rows, axis=0).T


def _scan_relayout(kernel_fn, x, n_seq, seq_len, rows_per_t, to_scan, name):
    g, nl = n_seq // RW_NB, seq_len // RW_TT
    nat_spec = pl.BlockSpec((RW_NB, RW_TT, MIX_W), lambda i, l: (i, l, 0))
    scan_spec = pl.BlockSpec((1, RW_TT * rows_per_t, LANES), lambda i, l: (i, l, 0))
    nat_shape = jax.ShapeDtypeStruct((n_seq, seq_len, MIX_W), F32)
    scan_shape = jax.ShapeDtypeStruct((g, seq_len * rows_per_t, LANES), F32)
    x = x.reshape(n_seq, seq_len, MIX_W) if to_scan else x.reshape(g, seq_len * rows_per_t, LANES)
    out = pl.pallas_call(
        kernel_fn,
        grid=(g, nl),
        in_specs=[nat_spec if to_scan else scan_spec],
        out_specs=scan_spec if to_scan else nat_spec,
        out_shape=scan_shape if to_scan else nat_shape,
        scratch_shapes=[pltpu.VMEM((RW_CHAINS * RW_KR, RW_TT), F32)],
        compiler_params=_cparams("parallel", "parallel"),
        name=name,
    )(x)
    return out.reshape(g, seq_len, rows_per_t, LANES) if to_scan else out.reshape(n_seq * seq_len, MIX_W)


def _pad_rows(a, rows):
    return jnp.pad(a, ((0, 0), (0, 0), (0, rows - a.shape[2]), (0, 0)))


def _to_scan_keys(a, n_seq, seq_len):
    if seq_len % RW_TT == 0:
        return _scan_relayout(_keys_to_scan_kernel, a, n_seq, seq_len, RW_KR, True, "rwkv_keys_to_scan")
    g = n_seq // RW_NB
    a = a.reshape(g, RW_NB, seq_len, RW_HEADS, RW_HD).transpose(0, 2, 4, 1, 3).reshape(g, seq_len, RW_HD, RW_CHAINS)
    return _pad_rows(jnp.concatenate([a, a], axis=-1), RW_KR)


def _to_scan_vals(a, n_seq, seq_len):
    if seq_len % RW_TT == 0:
        return _scan_relayout(_vals_to_scan_kernel, a, n_seq, seq_len, RW_VR, True, "rwkv_vals_to_scan")
    g = n_seq // RW_NB
    a = a.reshape(g, RW_NB, seq_len, RW_HEADS, 2, RW_VP).transpose(0, 2, 5, 4, 1, 3)
    return _pad_rows(a.reshape(g, seq_len, RW_VP, LANES), RW_VR)


def _from_scan_vals(y, n_seq, seq_len):
    if seq_len % RW_TT == 0:
        return _scan_relayout(_vals_from_scan_kernel, y, n_seq, seq_len, RW_VR, False, "rwkv_vals_from_scan")
    g = n_seq // RW_NB
    y = y[:, :, :RW_VP].reshape(g, seq_len, RW_VP, 2, RW_NB, RW_HEADS).transpose(0, 4, 1, 5, 3, 2)
    return y.reshape(n_seq * seq_len, MIX_W)


def _state_to_scan(s, n_seq):
    nb = RW_CHAINS // RW_HEADS
    g = n_seq // nb
    s = s.reshape(g, nb, RW_HEADS, 2, RW_VP, RW_HD).transpose(0, 4, 5, 3, 1, 2)
    return s.reshape(g, RW_VP, RW_HD, LANES)


def _state_from_scan(s, n_seq):
    nb = RW_CHAINS // RW_HEADS
    g = n_seq // nb
    s = s.reshape(g, RW_VP, RW_HD, 2, nb, RW_HEADS).transpose(0, 4, 5, 3, 1, 2)
    return s.reshape(n_seq, RW_HEADS, RW_HD, RW_HD)


def _rwkv_branch(z, shift0, s0, lyr_in, p, row0, n_seq, seq_len, bb, lt, tc, depth, lyr, prev):
    r, w, k, kk, b, v, g, bonus = _rwkv_prep(z, shift0, p, row0, n_seq, seq_len, bb, lt)
    keys = [_to_scan_keys(a, n_seq, seq_len) for a in (r, w, k, kk, b)]
    y, s_new = _rwkv_scan(*keys, _to_scan_vals(v, n_seq, seq_len), s0, lyr_in, tc, depth, lyr, prev[1])
    y = _from_scan_vals(y, n_seq, seq_len)
    return _rwkv_post(y, g, bonus, p['gn_g'], p['gn_b'], row0, z.shape[0], prev[0]), s_new


def _seq_masks(rows, lc):
    ri = lax.broadcasted_iota(jnp.int32, (rows, rows), 0)
    ci = lax.broadcasted_iota(jnp.int32, (rows, rows), 1)
    same = (ri // lc) == (ci // lc)
    return ri, ci, same


def _rows_from_seq(x, bb, lc):
    return jnp.broadcast_to(x, (bb, lc, x.shape[-1])).reshape(bb * lc, x.shape[-1])


def _last_of_seq(x, bb, lc):
    x3 = x.reshape(bb, lc, x.shape[-1])
    pick = (lax.broadcasted_iota(jnp.int32, (1, lc, 1), 1) == lc - 1).astype(F32)
    return jnp.sum(x3 * pick, axis=1, keepdims=True)


def _transpose_rows(x):
    rows = x.shape[0]
    if rows < LANES:
        x = jnp.concatenate([x, jnp.zeros((LANES - rows, LANES), x.dtype)], axis=0)
    return x.T[:, :rows]


def _mlstm_kernel(bb, lc, q_ref, k_ref, v_ref, o_ref, gt_ref, gb_ref, ng_ref, c0_ref, n0_ref, m0_ref,
                  y_ref, c_ref, n_ref, m_ref):
    rows = bb * lc

    @pl.when(pl.program_id(1) == 0)
    def _():
        c_ref[...] = c0_ref[...]
        n_ref[...] = n0_ref[...]
        m_ref[...] = m0_ref[...]

    gt = ML_GATE_CAP * jnp.tanh((gt_ref[...] + gb_ref[...]) * (1.0 / ML_GATE_CAP))
    lane = lax.broadcasted_iota(jnp.int32, (rows, LANES), 1)
    ri, ci, same = _seq_masks(rows, lc)
    causal = same & (ci <= ri)
    cum = _dot_exact_lhs(causal.astype(F32), -_softplus(-gt))
    gt_t = _transpose_rows(gt)
    cum_t = _transpose_rows(cum)
    for hd in range(ML_HEADS):
        sl = slice(hd * ML_HD, (hd + 1) * ML_HD)
        c0 = c_ref[0, :, hd]
        n0 = n_ref[0, :, hd:hd + 1, :]
        m0 = m_ref[0, :, :, hd:hd + 1]
        q = q_ref[:, sl]
        k = k_ref[:, sl] * (ML_HD ** -0.5)
        v = v_ref[:, sl]
        bcol = cum[:, ML_HEADS + hd:ML_HEADS + hd + 1]
        igcol = gt[:, hd:hd + 1]
        rowb = gt_t[hd:hd + 1, :] - cum_t[ML_HEADS + hd:ML_HEADS + hd + 1, :]
        dlog = jnp.where(causal, bcol + rowb, -jnp.inf)
        m_inter = bcol + _rows_from_seq(m0, bb, lc)
        m_t = jnp.maximum(m_inter, jnp.max(dlog, axis=-1, keepdims=True))
        qk = lax.dot_general(q.astype(BF16), k.astype(BF16), (((1,), (1,)), ((), ())),
                             preferred_element_type=F32)
        wts = jnp.exp(dlog - m_t) * qk
        s_inter = jnp.exp(m_inter - m_t)
        q3 = q.reshape(bb, lc, ML_HD).astype(BF16)
        qc = jnp.einsum('btd,bde->bte', q3, c0.astype(BF16), preferred_element_type=F32).reshape(rows, ML_HD)
        num = _bdot(wts, v) + s_inter * qc
        qn = jnp.sum(q * _rows_from_seq(n0, bb, lc), axis=-1, keepdims=True)
        den = jnp.sum(wts, axis=-1, keepdims=True) + s_inter * qn
        hh = num / jnp.maximum(jnp.abs(den), jnp.exp(-m_t))
        hh = hh * lax.rsqrt(jnp.mean(hh * hh, axis=-1, keepdims=True) + EPS) * ng_ref[:, sl]
        y_ref[:, sl] = (_sigmoid(o_ref[:, sl]) * hh).astype(y_ref.dtype)

        stat = jnp.where(lane == 0, bcol, jnp.where(lane == 1, m_t, 0.0))
        last = _last_of_seq(stat, bb, lc)
        last_rows = _rows_from_seq(last, bb, lc)
        b_last, m_new = last_rows[:, 0:1], last_rows[:, 1:2]
        w_end = jnp.exp(b_last - bcol + igcol - m_new)
        f_end = jnp.exp(last[:, :, 0:1] + m0 - last[:, :, 1:2])
        kw3 = (k * w_end).reshape(bb, lc, ML_HD)
        upd = jnp.einsum('bsd,bse->bde', kw3.astype(BF16), v.reshape(bb, lc, ML_HD).astype(BF16),
                         preferred_element_type=F32)
        c_ref[0, :, hd] = f_end * c0 + upd
        n_ref[0, :, hd:hd + 1, :] = f_end * n0 + jnp.sum(kw3, axis=1, keepdims=True)
        m_ref[0, :, :, hd:hd + 1] = last[:, :, 1:2]


def _mlstm_branch(z, z_gates, init, lyr_in, gate_bias, norm_g, row0, n_seq, seq_len, bb, lc, depth, lyr, prev):
    rows = bb * lc
    nc = seq_len // lc
    rb0 = row0 // rows
    rowmap = lambda part: (lambda b, c: (rb0 + b * nc + c, part))
    zspec = lambda part: pl.BlockSpec((rows, MIX_W), rowmap(part))
    state_specs = lambda at: [pl.BlockSpec((1, bb, ML_HEADS, ML_HD, ML_HD), lambda b, c: (at, b, 0, 0, 0)),
                              pl.BlockSpec((1, bb, ML_HEADS, ML_HD), lambda b, c: (at, b, 0, 0)),
                              pl.BlockSpec((1, bb, 1, ML_HEADS), lambda b, c: (at, b, 0, 0))]
    kern, prev_specs, prev_args, aliases = _carry_through(functools.partial(_mlstm_kernel, bb, lc), 10, prev)
    return pl.pallas_call(
        kern,
        grid=(n_seq // bb, nc),
        in_specs=[zspec(0), zspec(1), zspec(2), zspec(3),
                  pl.BlockSpec((rows, LANES), lambda b, c: (rb0 + b * nc + c, Z_MLG // LANES)),
                  pl.BlockSpec((1, LANES), lambda b, c: (0, 0)), pl.BlockSpec((1, MIX_W), lambda b, c: (0, 0))]
                 + state_specs(lyr_in) + prev_specs,
        out_specs=[pl.BlockSpec((rows, MIX_W), lambda b, c: (rb0 + b * nc + c, 0))] + state_specs(lyr),
        out_shape=[jax.ShapeDtypeStruct((z.shape[0], MIX_W), BF16),
                   jax.ShapeDtypeStruct((depth, n_seq, ML_HEADS, ML_HD, ML_HD), F32),
                   jax.ShapeDtypeStruct((depth, n_seq, ML_HEADS, ML_HD), F32),
                   jax.ShapeDtypeStruct((depth, n_seq, 1, ML_HEADS), F32)],
        input_output_aliases=aliases,
        compiler_params=_cparams("parallel", "arbitrary"),
        name="mlstm_chunk",
    )(z, z, z, z, z_gates, gate_bias, norm_g, *init, *prev_args)


def _rope_rows(x, cos, sin):
    half = RT_HD // 2
    x1, x2 = x[:, :half], x[:, half:]
    return jnp.concatenate([x1 * cos - x2 * sin, x1 * sin + x2 * cos], axis=1)


def _ret_kernel(bb, lc, q_ref, k_ref, v_ref, g_ref, cs_ref, dec_ref, cd_ref, s0_ref, y_ref, s_ref):
    rows = bb * lc

    @pl.when(pl.program_id(1) == 0)
    def _():
        s_ref[...] = s0_ref[...]

    cos, sin = cs_ref[:, :RT_HD // 2], cs_ref[:, RT_HD // 2:]
    for hd in range(RT_HEADS):
        sl = slice(hd * RT_HD, (hd + 1) * RT_HD)
        s0 = s_ref[0, :, hd]
        q = _rope_rows(q_ref[:, sl], cos, sin)
        k = _rope_rows(k_ref[:, sl], cos, sin) * (RT_HD ** -0.5)
        v = v_ref[:, sl]
        cd = cd_ref[hd]
        qk = lax.dot_general(q.astype(BF16), k.astype(BF16), (((1,), (1,)), ((), ())),
                             preferred_element_type=F32)
        inner = _bdot(qk * dec_ref[hd], v)
        q3 = q.reshape(bb, lc, RT_HD).astype(BF16)
        cross = jnp.einsum('btd,bde->bte', q3, s0.astype(BF16), preferred_element_type=F32).reshape(rows, RT_HD)
        y = inner + cross * cd[:, 0:1]
        y = y * lax.rsqrt(jnp.mean(y * y, axis=-1, keepdims=True) + EPS)
        gt = g_ref[:, sl]
        y_ref[:, sl] = (gt * _sigmoid(gt) * y).astype(y_ref.dtype)
        kd = (k * cd[:, 1:2]).reshape(bb, lc, RT_HD).astype(BF16)
        upd = jnp.einsum('bsd,bse->bde', kd, v.reshape(bb, lc, RT_HD).astype(BF16), preferred_element_type=F32)
        s_ref[0, :, hd] = cd[0:1, 2:3] * s0 + upd


def _ret_branch(z, s0, lyr_in, row0, n_seq, seq_len, bb, lc, pos0, depth, lyr, prev):
    rows = bb * lc
    nc = seq_len // lc
    rb0 = row0 // rows
    half = RT_HD // 2
    inv = ROPE_BASE ** (-jnp.arange(half, dtype=F32) / half)
    pos = jnp.broadcast_to((jnp.arange(nc * lc, dtype=F32) + float(pos0)).reshape(nc, 1, lc), (nc, bb, lc))
    ang = pos.reshape(nc * rows)[:, None] * inv[None, :]
    cs = jnp.concatenate([jnp.cos(ang), jnp.sin(ang)], axis=1)
    log_g = jnp.log(1.0 - jnp.exp(jnp.linspace(math.log(1.0 / 32), math.log(1.0 / 512), RT_HEADS)))
    idx = jnp.arange(rows)
    t = (idx % lc).astype(F32)
    diff = t[:, None] - t[None, :]
    same = (idx[:, None] // lc) == (idx[None, :] // lc)
    dec = jnp.where(same & (diff >= 0), jnp.exp(log_g[:, None, None] * jnp.maximum(diff, 0.0)), 0.0)
    cd = jnp.zeros((RT_HEADS, rows, LANES), F32)
    cd = cd.at[:, :, 0].set(jnp.exp(log_g[:, None] * (t[None, :] + 1.0)))
    cd = cd.at[:, :, 1].set(jnp.exp(log_g[:, None] * (lc - 1.0 - t)[None, :]))
    cd = cd.at[:, :, 2].set(jnp.broadcast_to(jnp.exp(log_g * lc)[:, None], (RT_HEADS, rows)))
    rowmap = lambda part: (lambda b, c: (rb0 + b * nc + c, part))
    zspec = lambda part: pl.BlockSpec((rows, MIX_W), rowmap(part))
    sspec = lambda at: pl.BlockSpec((1, bb, RT_HEADS, RT_HD, RT_HD), lambda b, c: (at, b, 0, 0, 0))
    kern, prev_specs, prev_args, aliases = _carry_through(functools.partial(_ret_kernel, bb, lc), 8, prev)
    return pl.pallas_call(
        kern,
        grid=(n_seq // bb, nc),
        in_specs=[zspec(0), zspec(1), zspec(2), zspec(3),
                  pl.BlockSpec((rows, RT_HD), lambda b, c: (c, 0)),
                  pl.BlockSpec((RT_HEADS, rows, rows), lambda b, c: (0, 0, 0)),
                  pl.BlockSpec((RT_HEADS, rows, LANES), lambda b, c: (0, 0, 0)),
                  sspec(lyr_in)] + prev_specs,
        out_specs=[pl.BlockSpec((rows, MIX_W), lambda b, c: (rb0 + b * nc + c, 0)), sspec(lyr)],
        out_shape=[jax.ShapeDtypeStruct((z.shape[0], MIX_W), BF16),
                   jax.ShapeDtypeStruct((depth, n_seq, RT_HEADS, RT_HD, RT_HD), F32)],
        input_output_aliases=aliases,
        compiler_params=_cparams("parallel", "arbitrary"),
        name="retention_chunk",
    )(z, z, z, z, cs, dec, cd, s0, *prev_args)


W_CAST_TN = 256


def _cast_rw_kernel(w_ref, o_ref):
    lane = lax.broadcasted_iota(jnp.int32, w_ref.shape[1:], 1)
    keep = (pl.program_id(1) < RW_COLS // W_CAST_TN) | (lane < 2 * ML_HEADS)
    o_ref[0] = jnp.where(keep, w_ref[0], 0.0).astype(o_ref.dtype)


def _cast_kernel(w_ref, o_ref):
    o_ref[...] = w_ref[...].astype(o_ref.dtype)


def _cast_shifted_kernel(off, a_ref, b_ref, o_ref):
    both = jnp.concatenate([a_ref[0], b_ref[0]], axis=1)
    o_ref[0] = both[:, off:off + W_CAST_TN].astype(o_ref.dtype)


def _split_w_in(w_in):
    depth, d, _ = w_in.shape
    tn = W_CAST_TN
    blk = lambda cmap: pl.BlockSpec((1, d, tn), cmap)
    out = lambda: blk(lambda l, j: (l, 0, j))
    call = lambda kern, n_out, in_specs, name: pl.pallas_call(
        kern, grid=(depth, n_out // tn), in_specs=in_specs, out_specs=out(),
        out_shape=jax.ShapeDtypeStruct((depth, d, n_out), BF16),
        compiler_params=_cparams("parallel", "parallel"), name=name)
    n_rw = RW_COLS // tn
    w_rw = call(_cast_rw_kernel, Z_RWB, [blk(lambda l, j: (l, 0, jnp.where(j < n_rw, j, IN_MLG // tn)))],
                "cast_w_rw")(w_in)
    w_ml = call(_cast_kernel, 4 * MIX_W, [blk(lambda l, j: (l, 0, IN_ML // tn + j))], "cast_w_ml")(w_in)
    b0, off = IN_RT // tn, IN_RT % tn
    w_rg = call(functools.partial(_cast_shifted_kernel, off), w_in.shape[2] - IN_RT,
                [blk(lambda l, j: (l, 0, b0 + j)), blk(lambda l, j: (l, 0, b0 + j + 1))], "cast_w_rg")(w_in, w_in)
    return w_rw, w_ml, w_rg


def _group_cfg(n_seq, seq_len):
    if seq_len % CHUNK == 0:
        return dict(rw_bb=1, rw_lt=256, rw_tc=32, ml_bb=1, rt_bb=1, ch_lc=CHUNK, at_bb=1, at_lq=512)
    return dict(rw_bb=32, rw_lt=seq_len, rw_tc=seq_len, ml_bb=8, rt_bb=8, ch_lc=seq_len, at_bb=8, at_lq=seq_len)


def kernel(x_prompt, x_sample, mem_prompt, state_rwkv_shift, state_rwkv, state_mlstm_c, state_mlstm_n,
           state_mlstm_m, state_ret, cache_mem_k, cache_mem_v, g_pre_mix, g_post_mix, g_pre_x, g_post_x,
           g_pre_ff, g_post_ff, g_mem, w_in, rw_mu, rw_w0, rw_w_up, rw_a0, rw_a_up, rw_g_up, rw_k_k, rw_k_a,
           rw_r_k, rw_gn_g, rw_gn_b, ml_i_b, ml_f_b, ml_norm_g, w_br, w_out, x_wq, x_wkv, x_wo, ff_w1, ff_w2):
    bp, lp, d = x_prompt.shape
    bs, ls, _ = x_sample.shape
    depth = w_in.shape[0]
    n_mem = mem_prompt.shape[1]
    tp, ts = bp * lp, bs * ls
    groups = [(0, bp, lp, 0), (tp, bs, ls, PAST_LEN)]

    x = jnp.concatenate([x_prompt.reshape(tp, d), x_sample.reshape(ts, d)], axis=0)
    mem = mem_prompt.reshape(bp * n_mem, d)
    h = _rmsnorm(x, g_pre_mix[0])

    zeros = lambda *s: jnp.zeros(s, F32)
    rw_scan0 = _state_to_scan(state_rwkv.reshape(depth * bs, RW_HEADS, RW_HD, RW_HD), depth * bs)
    inits = [
        lambda l: (zeros(bp, 1, RW_COLS), zeros(bp // RW_NB, RW_VP, RW_HD, LANES),
                   (zeros(1, bp, ML_HEADS, ML_HD, ML_HD), zeros(1, bp, ML_HEADS, ML_HD), zeros(1, bp, 1, ML_HEADS)),
                   zeros(1, bp, RT_HEADS, RT_HD, RT_HD), 0),
        lambda l: (state_rwkv_shift[l].reshape(bs, 1, RW_COLS), rw_scan0,
                   (state_mlstm_c, state_mlstm_n, state_mlstm_m.reshape(depth, bs, 1, ML_HEADS)), state_ret, l),
    ]
    shifts = [[], []]
    rw_states = [None, None]
    ml_states = [[None] * 3, [None] * 3]
    rt_states = [None, None]
    mem_k, mem_v = [], []
    w_rw, w_ml, w_rg = _split_w_in(w_in)
    w_kv, w_brb, w_outb, w_q, w_o, w_f1, w_f2 = (w.astype(BF16) for w in (x_wkv, w_br, w_out, x_wq, x_wo,
                                                                         ff_w1, ff_w2))
    for l in range(depth):
        kv = _mm(_rmsnorm(mem, g_mem[l], tm=256), w_kv, l, tm=256, tn=512)
        mem_k.append(kv[:, :X_W].reshape(bp, n_mem, X_HEADS, X_HD))
        mem_v.append(kv[:, X_W:].reshape(bp, n_mem, X_HEADS, X_HD))

        z_rw = _mm(h, w_rw, l, tm=1024, tn=Z_RWB // 2)
        z_ml = _mm(h, w_ml, l, tm=1024, tn=1024)
        z_rg = _mm(h, w_rg, l, tm=1024, tn=1280)

        rw_p = dict(mu=rw_mu[l].reshape(1, -1), w0=rw_w0[l].reshape(1, -1), w_up=rw_w_up[l].astype(BF16),
                    a0=rw_a0[l].reshape(1, -1), a_up=rw_a_up[l].astype(BF16), g_up=rw_g_up[l].astype(BF16),
                    k_k=rw_k_k[l].reshape(1, -1), k_a=rw_k_a[l].reshape(1, -1), r_k=rw_r_k[l].reshape(1, -1),
                    gn_g=rw_gn_g[l].reshape(1, -1), gn_b=rw_gn_b[l].reshape(1, -1))
        gate_bias = jnp.concatenate([ml_i_b[l], ml_f_b[l], zeros(LANES - 2 * ML_HEADS)]).reshape(1, LANES)
        norm_g = ml_norm_g[l].reshape(1, MIX_W)

        y_rw = y_ml = y_rt = None
        for gi, (row0, n_seq, seq_len, pos0) in enumerate(groups):
            cfg = _group_cfg(n_seq, seq_len)
            shift0, s_rw0, ml0, rt0, lyr_in = inits[gi](l)
            y_rw, rw_states[gi] = _rwkv_branch(z_rw, shift0, s_rw0, lyr_in, rw_p, row0, n_seq, seq_len,
                                               cfg['rw_bb'], cfg['rw_lt'], cfg['rw_tc'], depth, l,
                                               (y_rw, rw_states[gi]))
            y_ml, *ml_states[gi] = _mlstm_branch(z_ml, z_rw, ml0, lyr_in, gate_bias, norm_g, row0, n_seq, seq_len,
                                                 cfg['ml_bb'], cfg['ch_lc'], depth, l, [y_ml] + ml_states[gi])
            y_rt, rt_states[gi] = _ret_branch(z_rg, rt0, lyr_in, row0, n_seq, seq_len, cfg['rt_bb'],
                                              cfg['ch_lc'], pos0, depth, l, [y_rt, rt_states[gi]])
            last = row0 + seq_len - 1
            shifts[gi].append(lax.slice(z_rw, (last, 0), (last + (n_seq - 1) * seq_len + 1, RW_COLS),
                                        (seq_len, 1)))

        merged = _merge(y_rw, y_ml, y_rt, w_brb, l, z_rg, Z_GATE)
        x, h = _proj_res(merged, w_outb, l, x, g_post_mix[l], g_pre_x[l])

        for gi, (row0, n_seq, seq_len, pos0) in enumerate(groups):
            cfg = _group_cfg(n_seq, seq_len)
            if gi == 0:
                mk, mv, lyr_in = mem_k[l][None], mem_v[l][None], 0
            else:
                mk, mv, lyr_in = cache_mem_k, cache_mem_v, l
            x, h = _attn(h, x, mk, mv, lyr_in, w_q, w_o, l, g_post_x[l], g_pre_ff[l], row0, n_seq, seq_len,
                         cfg['at_bb'], cfg['at_lq'])

        g_next = g_pre_mix[(l + 1) % depth]
        x, h = _mlp(h, w_f1, w_f2, l, x, g_post_ff[l], g_next)

    def states(gi, n_seq):
        c_new, n_new, m_new = ml_states[gi]
        s_rw = _state_from_scan(rw_states[gi], depth * n_seq).reshape(depth, n_seq, RW_HEADS, RW_HD, RW_HD)
        return (jnp.stack(shifts[gi]), s_rw, c_new, n_new, m_new.reshape(depth, n_seq, ML_HEADS), rt_states[gi])

    return (x[:tp].reshape(bp, lp, d), x[tp:].reshape(bs, ls, d), *states(0, bp),
            jnp.stack(mem_k), jnp.stack(mem_v), *states(1, bs))
```

```python
import functools
import math

import jax
import jax.numpy as jnp
from jax import lax
from jax.experimental import pallas as pl
from jax.experimental.pallas import tpu as pltpu

F32 = jnp.float32
BF16 = jnp.bfloat16

D_MODEL = 2048
MIX_W = D_MODEL // 2
RW_HD = 64
RW_HEADS = MIX_W // RW_HD
RW_LORA = (64, 64, 128)
RW_COLS = 3 * MIX_W + sum(RW_LORA)
RW_GN_EPS = 64e-5
ML_HEADS = 8
ML_HD = MIX_W // ML_HEADS
ML_GATE_CAP = 15.0
RT_HEADS = 4
RT_HD = MIX_W // RT_HEADS
ROPE_BASE = 10000.0
X_HEADS = 4
X_HD = 128
X_W = X_HEADS * X_HD
D_FF = 4 * D_MODEL
CHUNK = 64
EPS = 1e-6
PAST_LEN = 16384

IN_ML = RW_COLS
IN_MLG = IN_ML + 4 * MIX_W
IN_RT = IN_MLG + 2 * ML_HEADS
Z_GATE = 4 * MIX_W

LANES = 128
RW_VP = RW_HD // 2
RW_CHAINS = LANES // 2


def _cparams(*sem):
    return pltpu.CompilerParams(dimension_semantics=sem, vmem_limit_bytes=56 * 1024 * 1024)


def _carry_through(kernel_fn, n_in, prev):
    carried = [(i, p) for i, p in enumerate(prev) if p is not None]
    if not carried:
        return kernel_fn, [], [], {}

    def wrapped(*refs):
        return kernel_fn(*refs[:n_in], *refs[n_in + len(carried):])

    return (wrapped, [pl.BlockSpec(memory_space=pl.ANY)] * len(carried), [p for _, p in carried],
            {n_in + j: i for j, (i, _) in enumerate(carried)})


def _sigmoid(x):
    return 1.0 / (1.0 + jnp.exp(-x))


def _softplus(x):
    return jnp.maximum(x, 0.0) + jnp.log(1.0 + jnp.exp(-jnp.abs(x)))


def _bdot(a, b):
    return jnp.dot(a.astype(BF16), b.astype(BF16), preferred_element_type=F32)


def _split3(x):
    hi = x.astype(BF16)
    r1 = x - hi.astype(F32)
    mid = r1.astype(BF16)
    lo = (r1 - mid.astype(F32)).astype(BF16)
    return hi, mid, lo


def _dot_exact_lhs(m, x):
    mb = m.astype(BF16)
    hi, mid, lo = _split3(x)
    return (jnp.dot(mb, hi, preferred_element_type=F32) + jnp.dot(mb, mid, preferred_element_type=F32)
            + jnp.dot(mb, lo, preferred_element_type=F32))


def _dot_exact_rhs(x, m):
    mb = m.astype(BF16)
    hi, mid, lo = _split3(x)
    return (jnp.dot(hi, mb, preferred_element_type=F32) + jnp.dot(mid, mb, preferred_element_type=F32)
            + jnp.dot(lo, mb, preferred_element_type=F32))


def _segsum(x, seg):
    blk = 256
    ri = lax.broadcasted_iota(jnp.int32, (blk, blk), 0) // seg
    ci = lax.broadcasted_iota(jnp.int32, (blk, blk), 1) // seg
    ones = (ri == ci).astype(F32)
    parts = [_dot_exact_rhs(x[:, c:c + blk], ones) for c in range(0, x.shape[1], blk)]
    return parts[0] if len(parts) == 1 else jnp.concatenate(parts, axis=1)


def _rms_rows(x, g):
    return x * lax.rsqrt(jnp.mean(x * x, axis=-1, keepdims=True) + EPS) * g


def _rmsnorm_kernel(x_ref, g_ref, o_ref):
    o_ref[...] = _rms_rows(x_ref[...], g_ref[...]).astype(o_ref.dtype)


def _rmsnorm(x, g, tm=512):
    m, d = x.shape
    return pl.pallas_call(
        _rmsnorm_kernel,
        grid=(m // tm,),
        in_specs=[pl.BlockSpec((tm, d), lambda i: (i, 0)), pl.BlockSpec((1, d), lambda i: (0, 0))],
        out_specs=pl.BlockSpec((tm, d), lambda i: (i, 0)),
        out_shape=jax.ShapeDtypeStruct((m, d), BF16),
        compiler_params=_cparams("parallel"),
        name="rmsnorm",
    )(x, g.reshape(1, d))


def _mm_kernel(x_ref, w_ref, o_ref):
    o_ref[...] = jnp.dot(x_ref[...], w_ref[0], preferred_element_type=F32)


def _mm(x, w, lyr, tm, tn):
    m, k = x.shape
    n = w.shape[2]
    return pl.pallas_call(
        _mm_kernel,
        grid=(n // tn, m // tm),
        in_specs=[pl.BlockSpec((tm, k), lambda j, i: (i, 0)), pl.BlockSpec((1, k, tn), lambda j, i: (lyr, 0, j))],
        out_specs=pl.BlockSpec((tm, tn), lambda j, i: (i, j)),
        out_shape=jax.ShapeDtypeStruct((m, n), F32),
        compiler_params=_cparams("parallel", "parallel"),
        name="matmul",
    )(x, w)


def _merge_kernel(y0_ref, y1_ref, y2_ref, w_ref, g0_ref, g1_ref, g2_ref, o_ref):
    acc = _sigmoid(g0_ref[...]) * jnp.dot(y0_ref[...], w_ref[0, 0], preferred_element_type=F32)
    acc += _sigmoid(g1_ref[...]) * jnp.dot(y1_ref[...], w_ref[0, 1], preferred_element_type=F32)
    acc += _sigmoid(g2_ref[...]) * jnp.dot(y2_ref[...], w_ref[0, 2], preferred_element_type=F32)
    o_ref[...] = acc.astype(o_ref.dtype)


def _merge(y_rw, y_ml, y_rt, w_br, lyr, z, gate_col0, tm=512, tn=1024):
    m, kw = y_rw.shape
    d = w_br.shape[3]
    gb = gate_col0 // tn
    gspec = lambda c: pl.BlockSpec((tm, tn), lambda j, i: (i, gb + c * (d // tn) + j))
    yspec = pl.BlockSpec((tm, kw), lambda j, i: (i, 0))
    return pl.pallas_call(
        _merge_kernel,
        grid=(d // tn, m // tm),
        in_specs=[yspec, yspec, yspec, pl.BlockSpec((1, 3, kw, tn), lambda j, i: (lyr, 0, 0, j)),
                  gspec(0), gspec(1), gspec(2)],
        out_specs=pl.BlockSpec((tm, tn), lambda j, i: (i, j)),
        out_shape=jax.ShapeDtypeStruct((m, d), BF16),
        compiler_params=_cparams("parallel", "parallel"),
        name="branch_merge",
    )(y_rw, y_ml, y_rt, w_br, z, z, z)


def _residual_epilogue(acc, x_ref, gpost_ref, gnext_ref, xo_ref, ho_ref):
    xn = x_ref[...] + _rms_rows(acc, gpost_ref[...])
    xo_ref[...] = xn
    ho_ref[...] = _rms_rows(xn, gnext_ref[...]).astype(ho_ref.dtype)


def _proj_res_kernel(a_ref, w_ref, x_ref, gpost_ref, gnext_ref, xo_ref, ho_ref):
    acc = jnp.dot(a_ref[...], w_ref[0], preferred_element_type=F32)
    _residual_epilogue(acc, x_ref, gpost_ref, gnext_ref, xo_ref, ho_ref)


def _proj_res(a, w, lyr, x, g_post, g_next, tm=512):
    m, k = a.shape
    d = w.shape[2]
    row = lambda i: (i, 0)
    fixed = lambda i: (0, 0)
    return pl.pallas_call(
        _proj_res_kernel,
        grid=(m // tm,),
        in_specs=[pl.BlockSpec((tm, k), row), pl.BlockSpec((1, k, d), lambda i: (lyr, 0, 0)),
                  pl.BlockSpec((tm, d), row),
                  pl.BlockSpec((1, d), fixed), pl.BlockSpec((1, d), fixed)],
        out_specs=[pl.BlockSpec((tm, d), row), pl.BlockSpec((tm, d), row)],
        out_shape=[jax.ShapeDtypeStruct((m, d), F32), jax.ShapeDtypeStruct((m, d), BF16)],
        compiler_params=_cparams("parallel"),
        name="out_proj_residual",
    )(a, w, x, g_post.reshape(1, d), g_next.reshape(1, d))


def _mlp_kernel(h_ref, w1_ref, w2_ref, x_ref, gpost_ref, gnext_ref, xo_ref, ho_ref, acc_ref):
    k = pl.program_id(1)

    @pl.when(k == 0)
    def _():
        acc_ref[...] = jnp.zeros_like(acc_ref)

    u = jnp.dot(h_ref[...], w1_ref[0], preferred_element_type=F32)
    u = jnp.square(jnp.maximum(u, 0.0)).astype(BF16)
    acc_ref[...] += jnp.dot(u, w2_ref[0], preferred_element_type=F32)

    @pl.when(k == pl.num_programs(1) - 1)
    def _():
        _residual_epilogue(acc_ref[...], x_ref, gpost_ref, gnext_ref, xo_ref, ho_ref)


def _mlp(h, w1, w2, lyr, x, g_post, g_next, tm=512, tf=1024):
    m, d = h.shape
    f = w1.shape[2]
    row = lambda i, k: (i, 0)
    fixed = lambda i, k: (0, 0)
    return pl.pallas_call(
        _mlp_kernel,
        grid=(m // tm, f // tf),
        in_specs=[pl.BlockSpec((tm, d), row), pl.BlockSpec((1, d, tf), lambda i, k: (lyr, 0, k)),
                  pl.BlockSpec((1, tf, d), lambda i, k: (lyr, k, 0)), pl.BlockSpec((tm, d), row),
                  pl.BlockSpec((1, d), fixed), pl.BlockSpec((1, d), fixed)],
        out_specs=[pl.BlockSpec((tm, d), row), pl.BlockSpec((tm, d), row)],
        out_shape=[jax.ShapeDtypeStruct((m, d), F32), jax.ShapeDtypeStruct((m, d), BF16)],
        scratch_shapes=[pltpu.VMEM((tm, d), F32)],
        compiler_params=_cparams("parallel", "arbitrary"),
        name="relu2_mlp",
    )(h, w1, w2, x, g_post.reshape(1, d), g_next.reshape(1, d))


def _attn_kernel(bb, lq, h_ref, wq_ref, mk_ref, mv_ref, wo_ref, x_ref, gpost_ref, gnext_ref, xo_ref, ho_ref):
    q = jnp.dot(h_ref[...], wq_ref[0], preferred_element_type=F32)
    outs = []
    for hd in range(X_HEADS):
        sl = slice(hd * X_HD, (hd + 1) * X_HD)
        q3 = q[:, sl].reshape(bb, lq, X_HD).astype(BF16)
        k3 = mk_ref[0, :, :, hd, :].astype(BF16)
        v3 = mv_ref[0, :, :, hd, :].astype(BF16)
        s = jnp.einsum('bqd,bmd->bqm', q3, k3, preferred_element_type=F32) * (X_HD ** -0.5)
        s = s - jnp.max(s, axis=-1, keepdims=True)
        e = jnp.exp(s)
        p = (e / jnp.sum(e, axis=-1, keepdims=True)).astype(BF16)
        o3 = jnp.einsum('bqm,bmd->bqd', p, v3, preferred_element_type=F32)
        outs.append(o3.reshape(bb * lq, X_HD))
    o = jnp.concatenate(outs, axis=1).astype(BF16)
    acc = jnp.dot(o, wo_ref[0], preferred_element_type=F32)
    _residual_epilogue(acc, x_ref, gpost_ref, gnext_ref, xo_ref, ho_ref)


def _attn(h, x, mk, mv, lyr, wq, wo, wl, g_post, g_next, row0, n_seq, seq_len, bb, lq):
    t, d = h.shape
    rows = bb * lq
    nl = seq_len // lq
    n_mem = mk.shape[2]
    rb0 = row0 // rows
    rowmap = lambda b, l: (rb0 + b * nl + l, 0)
    fixed = lambda b, l: (0, 0)
    kvspec = pl.BlockSpec((1, bb, n_mem, X_HEADS, X_HD), lambda b, l: (lyr, b, 0, 0, 0))
    return pl.pallas_call(
        functools.partial(_attn_kernel, bb, lq),
        grid=(n_seq // bb, nl),
        in_specs=[pl.BlockSpec((rows, d), rowmap), pl.BlockSpec((1, d, X_W), lambda b, l: (wl, 0, 0)), kvspec,
                  kvspec, pl.BlockSpec((1, X_W, d), lambda b, l: (wl, 0, 0)), pl.BlockSpec((rows, d), rowmap),
                  pl.BlockSpec((1, d), fixed), pl.BlockSpec((1, d), fixed)],
        out_specs=[pl.BlockSpec((rows, d), rowmap), pl.BlockSpec((rows, d), rowmap)],
        out_shape=[jax.ShapeDtypeStruct((t, d), F32), jax.ShapeDtypeStruct((t, d), BF16)],
        input_output_aliases={5: 0, 0: 1},
        compiler_params=_cparams("parallel", "parallel"),
        name="mem_cross_attention",
    )(h, wq, mk, mv, wo, x, g_post.reshape(1, d), g_next.reshape(1, d))


def _rwkv_prep_kernel(bb, lt, u_ref, s0_ref, mu_ref, w0_ref, wup_ref, a0_ref, aup_ref, gup_ref, kk_ref, ka_ref,
                      rk_ref, r_o, w_o, k_o, kk_o, b_o, v_o, g_o, bonus_o, carry_ref):
    l = pl.program_id(1)
    rows = bb * lt
    u = u_ref[:, :RW_COLS]
    if bb == 1:
        first = jnp.where(l == 0, s0_ref[0], carry_ref[...])
        first_rows = jnp.broadcast_to(first, (rows, RW_COLS))
    else:
        first_rows = jnp.broadcast_to(s0_ref[...], (bb, lt, RW_COLS)).reshape(rows, RW_COLS)
    rowid = lax.broadcasted_iota(jnp.int32, (rows, 1), 0) % lt
    prev = jnp.where(rowid == 0, first_rows, pltpu.roll(u, 1, 0))
    if bb == 1:
        carry_ref[...] = u[rows - 1:rows, :]
    z = u + (prev - u) * mu_ref[...]
    r = z[:, 0:MIX_W]
    k = z[:, MIX_W:2 * MIX_W]
    v = z[:, 2 * MIX_W:3 * MIX_W]
    o = 3 * MIX_W
    wd = z[:, o:o + RW_LORA[0]]
    ad = z[:, o + RW_LORA[0]:o + RW_LORA[0] + RW_LORA[1]]
    gd = z[:, o + RW_LORA[0] + RW_LORA[1]:RW_COLS]
    w_log = -_softplus(-(w0_ref[...] + _bdot(jnp.tanh(wd), wup_ref[...]))) - 0.5
    decay = jnp.exp(-jnp.exp(w_log))
    a = _sigmoid(a0_ref[...] + _bdot(ad, aup_ref[...]))
    g = _bdot(_sigmoid(gd), gup_ref[...])
    kk = k * kk_ref[...]
    kk = kk / jnp.maximum(jnp.sqrt(_segsum(kk * kk, RW_HD)), 1e-12)
    k = k * (1.0 + (a - 1.0) * ka_ref[...])
    r_o[...] = r
    w_o[...] = decay
    k_o[...] = k
    kk_o[...] = kk
    b_o[...] = kk * a
    v_o[...] = v
    g_o[...] = g
    bonus_o[...] = _segsum(r * k * rk_ref[...], RW_HD) * v


def _rwkv_prep(z, shift0, p, row0, n_seq, seq_len, bb, lt):
    rows = bb * lt
    nl = seq_len // lt
    assert bb == 1 or nl == 1
    rb0 = row0 // rows
    m = n_seq * seq_len
    fixed = lambda b, l: (0, 0)
    outmap = lambda b, l: (b * nl + l, 0)
    vec = lambda width: pl.BlockSpec((1, width), fixed)
    ospec = pl.BlockSpec((rows, MIX_W), outmap)
    oshape = jax.ShapeDtypeStruct((m, MIX_W), F32)
    return pl.pallas_call(
        functools.partial(_rwkv_prep_kernel, bb, lt),
        grid=(n_seq // bb, nl),
        in_specs=[pl.BlockSpec((rows, RW_COLS), lambda b, l: (rb0 + b * nl + l, 0)),
                  pl.BlockSpec((bb, 1, RW_COLS), lambda b, l: (b, 0, 0)),
                  vec(RW_COLS), vec(MIX_W), pl.BlockSpec((RW_LORA[0], MIX_W), fixed), vec(MIX_W),
                  pl.BlockSpec((RW_LORA[1], MIX_W), fixed), pl.BlockSpec((RW_LORA[2], MIX_W), fixed),
                  vec(MIX_W), vec(MIX_W), vec(MIX_W)],
        out_specs=[ospec] * 8,
        out_shape=[oshape] * 8,
        scratch_shapes=[pltpu.VMEM((1, RW_COLS), F32)],
        compiler_params=_cparams("parallel", "arbitrary"),
        name="rwkv_prep",
    )(z, shift0, p['mu'], p['w0'], p['w_up'], p['a0'], p['a_up'], p['g_up'], p['k_k'], p['k_a'], p['r_k'])


def _rwkv_scan_kernel(tc, r_ref, w_ref, k_ref, kk_ref, b_ref, v_ref, s0_ref, y_ref, s_ref):
    @pl.when(pl.program_id(1) == 0)
    def _():
        s_ref[...] = s0_ref[...]

    y_ref[0, :, RW_VP:, :] = jnp.zeros((tc, RW_VR - RW_VP, LANES), F32)
    keys = lambda ref, t: ref[0, t, :RW_HD, :]

    def step(t, carry):
        for vp in range(RW_VP):
            s = s_ref[0, vp]
            sa = jnp.sum(s * keys(kk_ref, t), axis=0, keepdims=True)
            sn = s * keys(w_ref, t) - sa * keys(b_ref, t) + v_ref[0, t, vp:vp + 1, :] * keys(k_ref, t)
            s_ref[0, vp] = sn
            y_ref[0, t, vp:vp + 1, :] = jnp.sum(sn * keys(r_ref, t), axis=0, keepdims=True)
        return carry

    lax.fori_loop(0, tc, step, 0)


def _rwkv_scan(r, w, k, kk, b, v, s0, lyr_in, tc, depth, lyr, s_prev):
    g, l = r.shape[0], r.shape[1]
    kspec = pl.BlockSpec((1, tc, RW_KR, LANES), lambda i, c: (i, c, 0, 0))
    vspec = pl.BlockSpec((1, tc, RW_VR, LANES), lambda i, c: (i, c, 0, 0))
    sspec = lambda at: pl.BlockSpec((1, RW_VP, RW_HD, LANES), lambda i, c: (at * g + i, 0, 0, 0))
    kern, prev_specs, prev_args, aliases = _carry_through(functools.partial(_rwkv_scan_kernel, tc), 7,
                                                          [None, s_prev])
    return pl.pallas_call(
        kern,
        grid=(g, l // tc),
        in_specs=[kspec] * 5 + [vspec, sspec(lyr_in)] + prev_specs,
        out_specs=[vspec, sspec(lyr)],
        out_shape=[jax.ShapeDtypeStruct((g, l, RW_VR, LANES), F32),
                   jax.ShapeDtypeStruct((depth * g, RW_VP, RW_HD, LANES), F32)],
        input_output_aliases=aliases,
        compiler_params=_cparams("parallel", "arbitrary"),
        name="rwkv_scan",
    )(r, w, k, kk, b, v, s0, *prev_args)


def _rwkv_post_kernel(y_ref, g_ref, bonus_ref, gng_ref, gnb_ref, o_ref):
    y = y_ref[...]
    mean = _segsum(y, RW_HD) * (1.0 / RW_HD)
    yc = y - mean
    var = _segsum(yc * yc, RW_HD) * (1.0 / RW_HD)
    yn = yc * lax.rsqrt(var + RW_GN_EPS) * gng_ref[...] + gnb_ref[...]
    o_ref[...] = ((yn + bonus_ref[...]) * g_ref[...]).astype(o_ref.dtype)


def _rwkv_post(y, g, bonus, gn_g, gn_b, row0, total_rows, prev, tm=256):
    m = y.shape[0]
    rb0 = row0 // tm
    fixed = lambda i: (0, 0)
    spec = pl.BlockSpec((tm, MIX_W), lambda i: (i, 0))
    kern, prev_specs, prev_args, aliases = _carry_through(_rwkv_post_kernel, 5, [prev])
    return pl.pallas_call(
        kern,
        grid=(m // tm,),
        in_specs=[spec, spec, spec, pl.BlockSpec((1, MIX_W), fixed), pl.BlockSpec((1, MIX_W), fixed)] + prev_specs,
        out_specs=pl.BlockSpec((tm, MIX_W), lambda i: (rb0 + i, 0)),
        out_shape=jax.ShapeDtypeStruct((total_rows, MIX_W), BF16),
        input_output_aliases=aliases,
        compiler_params=_cparams("parallel"),
        name="rwkv_post",
    )(y, g, bonus, gn_g, gn_b, *prev_args)


RW_NB = RW_CHAINS // RW_HEADS
RW_TT = LANES


RW_KR = RW_HD + 8
RW_VR = RW_VP + 8


def _natural_to_chain_rows(x_ref, a_ref):
    for bl in range(RW_NB):
        xt = x_ref[bl].T
        for h in range(RW_HEADS):
            c = bl * RW_HEADS + h
            a_ref[c * RW_KR:c * RW_KR + RW_HD, :] = xt[h * RW_HD:(h + 1) * RW_HD]


def _keys_to_scan_kernel(x_ref, o_ref, a_ref):
    _natural_to_chain_rows(x_ref, a_ref)
    o_ref[...] = jnp.zeros_like(o_ref)
    for k in range(RW_HD):
        tile = a_ref[pl.ds(k, RW_CHAINS, stride=RW_KR), :]
        o_ref[0, pl.ds(k, RW_TT, stride=RW_KR), :] = jnp.concatenate([tile, tile], axis=0).T


def _vals_to_scan_kernel(x_ref, o_ref, a_ref):
    _natural_to_chain_rows(x_ref, a_ref)
    o_ref[...] = jnp.zeros_like(o_ref)
    for vp in range(RW_VP):
        halves = [a_ref[pl.ds(vh * RW_VP + vp, RW_CHAINS, stride=RW_KR), :] for vh in range(2)]
        o_ref[0, pl.ds(vp, RW_TT, stride=RW_VR), :] = jnp.concatenate(halves, axis=0).T


def _vals_from_scan_kernel(y_ref, o_ref, a_ref):
    for vp in range(RW_VP):
        tile = y_ref[0, pl.ds(vp, RW_TT, stride=RW_VR), :].T
        for vh in range(2):
            a_ref[pl.ds(vh * RW_VP + vp, RW_CHAINS, stride=RW_KR), :] = tile[vh * RW_CHAINS:(vh + 1) * RW_CHAINS]
    for bl in range(RW_NB):
        rows = [a_ref[c * RW_KR:c * RW_KR + RW_HD, :] for c in range(bl * RW_HEADS, (bl + 1) * RW_HEADS)]
        o_ref[bl] = jnp.concatenate(rows, axis=0).T


def _scan_relayout(kernel_fn, x, n_seq, seq_len, rows_per_t, to_scan, name):
    g, nl = n_seq // RW_NB, seq_len // RW_TT
    nat_spec = pl.BlockSpec((RW_NB, RW_TT, MIX_W), lambda i, l: (i, l, 0))
    scan_spec = pl.BlockSpec((1, RW_TT * rows_per_t, LANES), lambda i, l: (i, l, 0))
    nat_shape = jax.ShapeDtypeStruct((n_seq, seq_len, MIX_W), F32)
    scan_shape = jax.ShapeDtypeStruct((g, seq_len * rows_per_t, LANES), F32)
    x = x.reshape(n_seq, seq_len, MIX_W) if to_scan else x.reshape(g, seq_len * rows_per_t, LANES)
    out = pl.pallas_call(
        kernel_fn,
        grid=(g, nl),
        in_specs=[nat_spec if to_scan else scan_spec],
        out_specs=scan_spec if to_scan else nat_spec,
        out_shape=scan_shape if to_scan else nat_shape,
        scratch_shapes=[pltpu.VMEM((RW_CHAINS * RW_KR, RW_TT), F32)],
        compiler_params=_cparams("parallel", "parallel"),
        name=name,
    )(x)
    return out.reshape(g, seq_len, rows_per_t, LANES) if to_scan else out.reshape(n_seq * seq_len, MIX_W)


def _pad_rows(a, rows):
    return jnp.pad(a, ((0, 0), (0, 0), (0, rows - a.shape[2]), (0, 0)))


def _to_scan_keys(a, n_seq, seq_len):
    if seq_len % RW_TT == 0:
        return _scan_relayout(_keys_to_scan_kernel, a, n_seq, seq_len, RW_KR, True, "rwkv_keys_to_scan")
    g = n_seq // RW_NB
    a = a.reshape(g, RW_NB, seq_len, RW_HEADS, RW_HD).transpose(0, 2, 4, 1, 3).reshape(g, seq_len, RW_HD, RW_CHAINS)
    return _pad_rows(jnp.concatenate([a, a], axis=-1), RW_KR)


def _to_scan_vals(a, n_seq, seq_len):
    if seq_len % RW_TT == 0:
        return _scan_relayout(_vals_to_scan_kernel, a, n_seq, seq_len, RW_VR, True, "rwkv_vals_to_scan")
    g = n_seq // RW_NB
    a = a.reshape(g, RW_NB, seq_len, RW_HEADS, 2, RW_VP).transpose(0, 2, 5, 4, 1, 3)
    return _pad_rows(a.reshape(g, seq_len, RW_VP, LANES), RW_VR)


def _from_scan_vals(y, n_seq, seq_len):
    if seq_len % RW_TT == 0:
        return _scan_relayout(_vals_from_scan_kernel, y, n_seq, seq_len, RW_VR, False, "rwkv_vals_from_scan")
    g = n_seq // RW_NB
    y = y[:, :, :RW_VP].reshape(g, seq_len, RW_VP, 2, RW_NB, RW_HEADS).transpose(0, 4, 1, 5, 3, 2)
    return y.reshape(n_seq * seq_len, MIX_W)


def _state_to_scan(s, n_seq):
    nb = RW_CHAINS // RW_HEADS
    g = n_seq // nb
    s = s.reshape(g, nb, RW_HEADS, 2, RW_VP, RW_HD).transpose(0, 4, 5, 3, 1, 2)
    return s.reshape(g, RW_VP, RW_HD, LANES)


def _state_from_scan(s, n_seq):
    nb = RW_CHAINS // RW_HEADS
    g = n_seq // nb
    s = s.reshape(g, RW_VP, RW_HD, 2, nb, RW_HEADS).transpose(0, 4, 5, 3, 1, 2)
    return s.reshape(n_seq, RW_HEADS, RW_HD, RW_HD)


def _rwkv_branch(z, shift0, s0, lyr_in, p, row0, n_seq, seq_len, bb, lt, tc, depth, lyr, prev):
    r, w, k, kk, b, v, g, bonus = _rwkv_prep(z, shift0, p, row0, n_seq, seq_len, bb, lt)
    keys = [_to_scan_keys(a, n_seq, seq_len) for a in (r, w, k, kk, b)]
    y, s_new = _rwkv_scan(*keys, _to_scan_vals(v, n_seq, seq_len), s0, lyr_in, tc, depth, lyr, prev[1])
    y = _from_scan_vals(y, n_seq, seq_len)
    return _rwkv_post(y, g, bonus, p['gn_g'], p['gn_b'], row0, z.shape[0], prev[0]), s_new


def _seq_masks(rows, lc):
    ri = lax.broadcasted_iota(jnp.int32, (rows, rows), 0)
    ci = lax.broadcasted_iota(jnp.int32, (rows, rows), 1)
    same = (ri // lc) == (ci // lc)
    return ri, ci, same


def _rows_from_seq(x, bb, lc):
    return jnp.broadcast_to(x, (bb, lc, x.shape[-1])).reshape(bb * lc, x.shape[-1])


def _last_of_seq(x, bb, lc):
    x3 = x.reshape(bb, lc, x.shape[-1])
    pick = (lax.broadcasted_iota(jnp.int32, (1, lc, 1), 1) == lc - 1).astype(F32)
    return jnp.sum(x3 * pick, axis=1, keepdims=True)


def _transpose_rows(x):
    rows = x.shape[0]
    if rows < LANES:
        x = jnp.concatenate([x, jnp.zeros((LANES - rows, LANES), x.dtype)], axis=0)
    return x.T[:, :rows]


def _mlstm_kernel(bb, lc, q_ref, k_ref, v_ref, o_ref, gt_ref, gb_ref, ng_ref, c0_ref, n0_ref, m0_ref,
                  y_ref, c_ref, n_ref, m_ref):
    rows = bb * lc

    @pl.when(pl.program_id(1) == 0)
    def _():
        c_ref[...] = c0_ref[...]
        n_ref[...] = n0_ref[...]
        m_ref[...] = m0_ref[...]

    gt = ML_GATE_CAP * jnp.tanh((gt_ref[...] + gb_ref[...]) * (1.0 / ML_GATE_CAP))
    lane = lax.broadcasted_iota(jnp.int32, (rows, LANES), 1)
    ri, ci, same = _seq_masks(rows, lc)
    causal = same & (ci <= ri)
    cum = _dot_exact_lhs(causal.astype(F32), -_softplus(-gt))
    gt_t = _transpose_rows(gt)
    cum_t = _transpose_rows(cum)
    for hd in range(ML_HEADS):
        sl = slice(hd * ML_HD, (hd + 1) * ML_HD)
        c0 = c_ref[0, :, hd]
        n0 = n_ref[0, :, hd:hd + 1, :]
        m0 = m_ref[0, :, :, hd:hd + 1]
        q = q_ref[:, sl]
        k = k_ref[:, sl] * (ML_HD ** -0.5)
        v = v_ref[:, sl]
        bcol = cum[:, ML_HEADS + hd:ML_HEADS + hd + 1]
        igcol = gt[:, hd:hd + 1]
        rowb = gt_t[hd:hd + 1, :] - cum_t[ML_HEADS + hd:ML_HEADS + hd + 1, :]
        dlog = jnp.where(causal, bcol + rowb, -jnp.inf)
        m_inter = bcol + _rows_from_seq(m0, bb, lc)
        m_t = jnp.maximum(m_inter, jnp.max(dlog, axis=-1, keepdims=True))
        qk = lax.dot_general(q.astype(BF16), k.astype(BF16), (((1,), (1,)), ((), ())),
                             preferred_element_type=F32)
        wts = jnp.exp(dlog - m_t) * qk
        s_inter = jnp.exp(m_inter - m_t)
        q3 = q.reshape(bb, lc, ML_HD).astype(BF16)
        qc = jnp.einsum('btd,bde->bte', q3, c0.astype(BF16), preferred_element_type=F32).reshape(rows, ML_HD)
        num = _bdot(wts, v) + s_inter * qc
        qn = jnp.sum(q * _rows_from_seq(n0, bb, lc), axis=-1, keepdims=True)
        den = jnp.sum(wts, axis=-1, keepdims=True) + s_inter * qn
        hh = num / jnp.maximum(jnp.abs(den), jnp.exp(-m_t))
        hh = hh * lax.rsqrt(jnp.mean(hh * hh, axis=-1, keepdims=True) + EPS) * ng_ref[:, sl]
        y_ref[:, sl] = (_sigmoid(o_ref[:, sl]) * hh).astype(y_ref.dtype)

        stat = jnp.where(lane == 0, bcol, jnp.where(lane == 1, m_t, 0.0))
        last = _last_of_seq(stat, bb, lc)
        last_rows = _rows_from_seq(last, bb, lc)
        b_last, m_new = last_rows[:, 0:1], last_rows[:, 1:2]
        w_end = jnp.exp(b_last - bcol + igcol - m_new)
        f_end = jnp.exp(last[:, :, 0:1] + m0 - last[:, :, 1:2])
        kw3 = (k * w_end).reshape(bb, lc, ML_HD)
        upd = jnp.einsum('bsd,bse->bde', kw3.astype(BF16), v.reshape(bb, lc, ML_HD).astype(BF16),
                         preferred_element_type=F32)
        c_ref[0, :, hd] = f_end * c0 + upd
        n_ref[0, :, hd:hd + 1, :] = f_end * n0 + jnp.sum(kw3, axis=1, keepdims=True)
        m_ref[0, :, :, hd:hd + 1] = last[:, :, 1:2]


def _mlstm_branch(z, z_gates, init, lyr_in, gate_bias, norm_g, row0, n_seq, seq_len, bb, lc, depth, lyr, prev):
    rows = bb * lc
    nc = seq_len // lc
    rb0 = row0 // rows
    rowmap = lambda part: (lambda b, c: (rb0 + b * nc + c, part))
    zspec = lambda part: pl.BlockSpec((rows, MIX_W), rowmap(part))
    state_specs = lambda at: [pl.BlockSpec((1, bb, ML_HEADS, ML_HD, ML_HD), lambda b, c: (at, b, 0, 0, 0)),
                              pl.BlockSpec((1, bb, ML_HEADS, ML_HD), lambda b, c: (at, b, 0, 0)),
                              pl.BlockSpec((1, bb, 1, ML_HEADS), lambda b, c: (at, b, 0, 0))]
    kern, prev_specs, prev_args, aliases = _carry_through(functools.partial(_mlstm_kernel, bb, lc), 10, prev)
    return pl.pallas_call(
        kern,
        grid=(n_seq // bb, nc),
        in_specs=[zspec(0), zspec(1), zspec(2), zspec(3),
                  pl.BlockSpec((rows, LANES), lambda b, c: (rb0 + b * nc + c, 0)),
                  pl.BlockSpec((1, LANES), lambda b, c: (0, 0)), pl.BlockSpec((1, MIX_W), lambda b, c: (0, 0))]
                 + state_specs(lyr_in) + prev_specs,
        out_specs=[pl.BlockSpec((rows, MIX_W), lambda b, c: (rb0 + b * nc + c, 0))] + state_specs(lyr),
        out_shape=[jax.ShapeDtypeStruct((z.shape[0], MIX_W), BF16),
                   jax.ShapeDtypeStruct((depth, n_seq, ML_HEADS, ML_HD, ML_HD), F32),
                   jax.ShapeDtypeStruct((depth, n_seq, ML_HEADS, ML_HD), F32),
                   jax.ShapeDtypeStruct((depth, n_seq, 1, ML_HEADS), F32)],
        input_output_aliases=aliases,
        compiler_params=_cparams("parallel", "arbitrary"),
        name="mlstm_chunk",
    )(z, z, z, z, z_gates, gate_bias, norm_g, *init, *prev_args)


def _rope_rows(x, cos, sin):
    half = RT_HD // 2
    x1, x2 = x[:, :half], x[:, half:]
    return jnp.concatenate([x1 * cos - x2 * sin, x1 * sin + x2 * cos], axis=1)


def _ret_kernel(bb, lc, q_ref, k_ref, v_ref, g_ref, cs_ref, dec_ref, cd_ref, s0_ref, y_ref, s_ref):
    rows = bb * lc

    @pl.when(pl.program_id(1) == 0)
    def _():
        s_ref[...] = s0_ref[...]

    cos, sin = cs_ref[:, :RT_HD // 2], cs_ref[:, RT_HD // 2:]
    for hd in range(RT_HEADS):
        sl = slice(hd * RT_HD, (hd + 1) * RT_HD)
        s0 = s_ref[0, :, hd]
        q = _rope_rows(q_ref[:, sl], cos, sin)
        k = _rope_rows(k_ref[:, sl], cos, sin) * (RT_HD ** -0.5)
        v = v_ref[:, sl]
        cd = cd_ref[hd]
        qk = lax.dot_general(q.astype(BF16), k.astype(BF16), (((1,), (1,)), ((), ())),
                             preferred_element_type=F32)
        inner = _bdot(qk * dec_ref[hd], v)
        q3 = q.reshape(bb, lc, RT_HD).astype(BF16)
        cross = jnp.einsum('btd,bde->bte', q3, s0.astype(BF16), preferred_element_type=F32).reshape(rows, RT_HD)
        y = inner + cross * cd[:, 0:1]
        y = y * lax.rsqrt(jnp.mean(y * y, axis=-1, keepdims=True) + EPS)
        gt = g_ref[:, sl]
        y_ref[:, sl] = (gt * _sigmoid(gt) * y).astype(y_ref.dtype)
        kd = (k * cd[:, 1:2]).reshape(bb, lc, RT_HD).astype(BF16)
        upd = jnp.einsum('bsd,bse->bde', kd, v.reshape(bb, lc, RT_HD).astype(BF16), preferred_element_type=F32)
        s_ref[0, :, hd] = cd[0:1, 2:3] * s0 + upd


def _ret_branch(z, s0, lyr_in, row0, n_seq, seq_len, bb, lc, pos0, depth, lyr, prev):
    rows = bb * lc
    nc = seq_len // lc
    rb0 = row0 // rows
    half = RT_HD // 2
    inv = ROPE_BASE ** (-jnp.arange(half, dtype=F32) / half)
    pos = jnp.broadcast_to((jnp.arange(nc * lc, dtype=F32) + float(pos0)).reshape(nc, 1, lc), (nc, bb, lc))
    ang = pos.reshape(nc * rows)[:, None] * inv[None, :]
    cs = jnp.concatenate([jnp.cos(ang), jnp.sin(ang)], axis=1)
    log_g = jnp.log(1.0 - jnp.exp(jnp.linspace(math.log(1.0 / 32), math.log(1.0 / 512), RT_HEADS)))
    idx = jnp.arange(rows)
    t = (idx % lc).astype(F32)
    diff = t[:, None] - t[None, :]
    same = (idx[:, None] // lc) == (idx[None, :] // lc)
    dec = jnp.where(same & (diff >= 0), jnp.exp(log_g[:, None, None] * jnp.maximum(diff, 0.0)), 0.0)
    cd = jnp.zeros((RT_HEADS, rows, LANES), F32)
    cd = cd.at[:, :, 0].set(jnp.exp(log_g[:, None] * (t[None, :] + 1.0)))
    cd = cd.at[:, :, 1].set(jnp.exp(log_g[:, None] * (lc - 1.0 - t)[None, :]))
    cd = cd.at[:, :, 2].set(jnp.broadcast_to(jnp.exp(log_g * lc)[:, None], (RT_HEADS, rows)))
    rowmap = lambda part: (lambda b, c: (rb0 + b * nc + c, part))
    zspec = lambda part: pl.BlockSpec((rows, MIX_W), rowmap(part))
    sspec = lambda at: pl.BlockSpec((1, bb, RT_HEADS, RT_HD, RT_HD), lambda b, c: (at, b, 0, 0, 0))
    kern, prev_specs, prev_args, aliases = _carry_through(functools.partial(_ret_kernel, bb, lc), 8, prev)
    return pl.pallas_call(
        kern,
        grid=(n_seq // bb, nc),
        in_specs=[zspec(0), zspec(1), zspec(2), zspec(3),
                  pl.BlockSpec((rows, RT_HD), lambda b, c: (c, 0)),
                  pl.BlockSpec((RT_HEADS, rows, rows), lambda b, c: (0, 0, 0)),
                  pl.BlockSpec((RT_HEADS, rows, LANES), lambda b, c: (0, 0, 0)),
                  sspec(lyr_in)] + prev_specs,
        out_specs=[pl.BlockSpec((rows, MIX_W), lambda b, c: (rb0 + b * nc + c, 0)), sspec(lyr)],
        out_shape=[jax.ShapeDtypeStruct((z.shape[0], MIX_W), BF16),
                   jax.ShapeDtypeStruct((depth, n_seq, RT_HEADS, RT_HD, RT_HD), F32)],
        input_output_aliases=aliases,
        compiler_params=_cparams("parallel", "arbitrary"),
        name="retention_chunk",
    )(z, z, z, z, cs, dec, cd, s0, *prev_args)


def _mm_wt_kernel(x_ref, wt_ref, o_ref, wb_ref):
    @pl.when(pl.program_id(1) == 0)
    def _():
        tn = wb_ref.shape[1]
        for c in range(0, tn, LANES):
            wb_ref[:, c:c + LANES] = wt_ref[0, c:c + LANES, :].T.astype(wb_ref.dtype)

    o_ref[...] = jnp.dot(x_ref[...], wb_ref[...], preferred_element_type=F32)


def _mm_wt(x, w_t, lyr, col0, n, tm, tn):
    m, k = x.shape
    return pl.pallas_call(
        _mm_wt_kernel,
        grid=(n // tn, m // tm),
        in_specs=[pl.BlockSpec((tm, k), lambda j, i: (i, 0)),
                  pl.BlockSpec((pl.Element(1), pl.Element(tn), pl.Element(k)),
                               lambda j, i: (lyr, pl.multiple_of(col0 + j * tn, 8), 0))],
        out_specs=pl.BlockSpec((tm, tn), lambda j, i: (i, j)),
        out_shape=jax.ShapeDtypeStruct((m, n), F32),
        scratch_shapes=[pltpu.VMEM((k, tn), BF16)],
        compiler_params=_cparams("parallel", "arbitrary"),
        name="matmul_wt",
    )(x, w_t)


def _group_cfg(n_seq, seq_len):
    if seq_len % CHUNK == 0:
        return dict(rw_bb=1, rw_lt=256, rw_tc=32, ml_bb=1, rt_bb=1, ch_lc=CHUNK, at_bb=1, at_lq=512)
    return dict(rw_bb=32, rw_lt=seq_len, rw_tc=seq_len, ml_bb=8, rt_bb=8, ch_lc=seq_len, at_bb=8, at_lq=seq_len)


def kernel(x_prompt, x_sample, mem_prompt, state_rwkv_shift, state_rwkv, state_mlstm_c, state_mlstm_n,
           state_mlstm_m, state_ret, cache_mem_k, cache_mem_v, g_pre_mix, g_post_mix, g_pre_x, g_post_x,
           g_pre_ff, g_post_ff, g_mem, w_in, rw_mu, rw_w0, rw_w_up, rw_a0, rw_a_up, rw_g_up, rw_k_k, rw_k_a,
           rw_r_k, rw_gn_g, rw_gn_b, ml_i_b, ml_f_b, ml_norm_g, w_br, w_out, x_wq, x_wkv, x_wo, ff_w1, ff_w2):
    bp, lp, d = x_prompt.shape
    bs, ls, _ = x_sample.shape
    depth = w_in.shape[0]
    n_mem = mem_prompt.shape[1]
    tp, ts = bp * lp, bs * ls
    groups = [(0, bp, lp, 0), (tp, bs, ls, PAST_LEN)]

    x = jnp.concatenate([x_prompt.reshape(tp, d), x_sample.reshape(ts, d)], axis=0)
    mem = mem_prompt.reshape(bp * n_mem, d)
    h = _rmsnorm(x, g_pre_mix[0])

    zeros = lambda *s: jnp.zeros(s, F32)
    rw_scan0 = _state_to_scan(state_rwkv.reshape(depth * bs, RW_HEADS, RW_HD, RW_HD), depth * bs)
    inits = [
        lambda l: (zeros(bp, 1, RW_COLS), zeros(bp // RW_NB, RW_VP, RW_HD, LANES),
                   (zeros(1, bp, ML_HEADS, ML_HD, ML_HD), zeros(1, bp, ML_HEADS, ML_HD), zeros(1, bp, 1, ML_HEADS)),
                   zeros(1, bp, RT_HEADS, RT_HD, RT_HD), 0),
        lambda l: (state_rwkv_shift[l].reshape(bs, 1, RW_COLS), rw_scan0,
                   (state_mlstm_c, state_mlstm_n, state_mlstm_m.reshape(depth, bs, 1, ML_HEADS)), state_ret, l),
    ]
    shifts = [[], []]
    rw_states = [None, None]
    ml_states = [[None] * 3, [None] * 3]
    rt_states = [None, None]
    mem_k, mem_v = [], []
    w_in_t = jnp.swapaxes(w_in, 1, 2)
    w_kv, w_brb, w_outb, w_q, w_o, w_f1, w_f2 = (w.astype(BF16) for w in (x_wkv, w_br, w_out, x_wq, x_wo,
                                                                         ff_w1, ff_w2))
    for l in range(depth):
        kv = _mm(_rmsnorm(mem, g_mem[l], tm=256), w_kv, l, tm=256, tn=512)
        mem_k.append(kv[:, :X_W].reshape(bp, n_mem, X_HEADS, X_HD))
        mem_v.append(kv[:, X_W:].reshape(bp, n_mem, X_HEADS, X_HD))

        z_rw = _mm_wt(h, w_in_t, l, 0, RW_COLS, tm=512, tn=RW_COLS // 2)
        z_ml = _mm_wt(h, w_in_t, l, IN_ML, 4 * MIX_W, tm=1024, tn=1024)
        z_g = _mm_wt(h, w_in_t, l, IN_MLG, LANES, tm=1024, tn=LANES)
        z_rg = _mm_wt(h, w_in_t, l, IN_RT, w_in.shape[2] - IN_RT, tm=1024, tn=1280)

        rw_p = dict(mu=rw_mu[l].reshape(1, -1), w0=rw_w0[l].reshape(1, -1), w_up=rw_w_up[l].astype(BF16),
                    a0=rw_a0[l].reshape(1, -1), a_up=rw_a_up[l].astype(BF16), g_up=rw_g_up[l].astype(BF16),
                    k_k=rw_k_k[l].reshape(1, -1), k_a=rw_k_a[l].reshape(1, -1), r_k=rw_r_k[l].reshape(1, -1),
                    gn_g=rw_gn_g[l].reshape(1, -1), gn_b=rw_gn_b[l].reshape(1, -1))
        gate_bias = jnp.concatenate([ml_i_b[l], ml_f_b[l], zeros(LANES - 2 * ML_HEADS)]).reshape(1, LANES)
        norm_g = ml_norm_g[l].reshape(1, MIX_W)

        y_rw = y_ml = y_rt = None
        for gi, (row0, n_seq, seq_len, pos0) in enumerate(groups):
            cfg = _group_cfg(n_seq, seq_len)
            shift0, s_rw0, ml0, rt0, lyr_in = inits[gi](l)
            y_rw, rw_states[gi] = _rwkv_branch(z_rw, shift0, s_rw0, lyr_in, rw_p, row0, n_seq, seq_len,
                                               cfg['rw_bb'], cfg['rw_lt'], cfg['rw_tc'], depth, l,
                                               (y_rw, rw_states[gi]))
            y_ml, *ml_states[gi] = _mlstm_branch(z_ml, z_g, ml0, lyr_in, gate_bias, norm_g, row0, n_seq, seq_len,
                                                 cfg['ml_bb'], cfg['ch_lc'], depth, l, [y_ml] + ml_states[gi])
            y_rt, rt_states[gi] = _ret_branch(z_rg, rt0, lyr_in, row0, n_seq, seq_len, cfg['rt_bb'],
                                              cfg['ch_lc'], pos0, depth, l, [y_rt, rt_states[gi]])
            last = row0 + seq_len - 1
            shifts[gi].append(lax.slice(z_rw, (last, 0), (last + (n_seq - 1) * seq_len + 1, RW_COLS),
                                        (seq_len, 1)))

        merged = _merge(y_rw, y_ml, y_rt, w_brb, l, z_rg, Z_GATE)
        x, h = _proj_res(merged, w_outb, l, x, g_post_mix[l], g_pre_x[l])

        for gi, (row0, n_seq, seq_len, pos0) in enumerate(groups):
            cfg = _group_cfg(n_seq, seq_len)
            if gi == 0:
                mk, mv, lyr_in = mem_k[l][None], mem_v[l][None], 0
            else:
                mk, mv, lyr_in = cache_mem_k, cache_mem_v, l
            x, h = _attn(h, x, mk, mv, lyr_in, w_q, w_o, l, g_post_x[l], g_pre_ff[l], row0, n_seq, seq_len,
                         cfg['at_bb'], cfg['at_lq'])

        g_next = g_pre_mix[(l + 1) % depth]
        x, h = _mlp(h, w_f1, w_f2, l, x, g_post_ff[l], g_next)

    def states(gi, n_seq):
        c_new, n_new, m_new = ml_states[gi]
        s_rw = _state_from_scan(rw_states[gi], depth * n_seq).reshape(depth, n_seq, RW_HEADS, RW_HD, RW_HD)
        return (jnp.stack(shifts[gi]), s_rw, c_new, n_new, m_new.reshape(depth, n_seq, ML_HEADS), rt_states[gi])

    return (x[:tp].reshape(bp, lp, d), x[tp:].reshape(bs, ls, d), *states(0, bp),
            jnp.stack(mem_k), jnp.stack(mem_v), *states(1, bs))
```

```python
import functools
import math

import jax
import jax.numpy as jnp
from jax import lax
from jax.experimental import pallas as pl
from jax.experimental.pallas import tpu as pltpu

F32 = jnp.float32
BF16 = jnp.bfloat16

D_MODEL = 2048
MIX_W = D_MODEL // 2
RW_HD = 64
RW_HEADS = MIX_W // RW_HD
RW_LORA = (64, 64, 128)
RW_COLS = 3 * MIX_W + sum(RW_LORA)
RW_GN_EPS = 64e-5
ML_HEADS = 8
ML_HD = MIX_W // ML_HEADS
ML_GATE_CAP = 15.0
RT_HEADS = 4
RT_HD = MIX_W // RT_HEADS
ROPE_BASE = 10000.0
X_HEADS = 4
X_HD = 128
X_W = X_HEADS * X_HD
D_FF = 4 * D_MODEL
CHUNK = 64
EPS = 1e-6
PAST_LEN = 16384

IN_ML = RW_COLS
IN_MLG = IN_ML + 4 * MIX_W
IN_RT = IN_MLG + 2 * ML_HEADS
Z_GATE = 4 * MIX_W

LANES = 128
RW_VP = RW_HD // 2
RW_CHAINS = LANES // 2


def _cparams(*sem):
    return pltpu.CompilerParams(dimension_semantics=sem, vmem_limit_bytes=56 * 1024 * 1024)


def _carry_through(kernel_fn, n_in, prev):
    carried = [(i, p) for i, p in enumerate(prev) if p is not None]
    if not carried:
        return kernel_fn, [], [], {}

    def wrapped(*refs):
        return kernel_fn(*refs[:n_in], *refs[n_in + len(carried):])

    return (wrapped, [pl.BlockSpec(memory_space=pl.ANY)] * len(carried), [p for _, p in carried],
            {n_in + j: i for j, (i, _) in enumerate(carried)})


def _sigmoid(x):
    return 1.0 / (1.0 + jnp.exp(-x))


def _softplus(x):
    return jnp.maximum(x, 0.0) + jnp.log(1.0 + jnp.exp(-jnp.abs(x)))


def _bdot(a, b):
    return jnp.dot(a.astype(BF16), b.astype(BF16), preferred_element_type=F32)


def _split3(x):
    hi = x.astype(BF16)
    r1 = x - hi.astype(F32)
    mid = r1.astype(BF16)
    lo = (r1 - mid.astype(F32)).astype(BF16)
    return hi, mid, lo


def _dot_exact_lhs(m, x):
    mb = m.astype(BF16)
    hi, mid, lo = _split3(x)
    return (jnp.dot(mb, hi, preferred_element_type=F32) + jnp.dot(mb, mid, preferred_element_type=F32)
            + jnp.dot(mb, lo, preferred_element_type=F32))


def _dot_exact_rhs(x, m):
    mb = m.astype(BF16)
    hi, mid, lo = _split3(x)
    return (jnp.dot(hi, mb, preferred_element_type=F32) + jnp.dot(mid, mb, preferred_element_type=F32)
            + jnp.dot(lo, mb, preferred_element_type=F32))


def _segsum(x, seg):
    blk = 256
    ri = lax.broadcasted_iota(jnp.int32, (blk, blk), 0) // seg
    ci = lax.broadcasted_iota(jnp.int32, (blk, blk), 1) // seg
    ones = (ri == ci).astype(F32)
    parts = [_dot_exact_rhs(x[:, c:c + blk], ones) for c in range(0, x.shape[1], blk)]
    return parts[0] if len(parts) == 1 else jnp.concatenate(parts, axis=1)


def _rms_rows(x, g):
    return x * lax.rsqrt(jnp.mean(x * x, axis=-1, keepdims=True) + EPS) * g


def _rmsnorm_kernel(x_ref, g_ref, o_ref):
    o_ref[...] = _rms_rows(x_ref[...], g_ref[...]).astype(o_ref.dtype)


def _rmsnorm(x, g, tm=512):
    m, d = x.shape
    return pl.pallas_call(
        _rmsnorm_kernel,
        grid=(m // tm,),
        in_specs=[pl.BlockSpec((tm, d), lambda i: (i, 0)), pl.BlockSpec((1, d), lambda i: (0, 0))],
        out_specs=pl.BlockSpec((tm, d), lambda i: (i, 0)),
        out_shape=jax.ShapeDtypeStruct((m, d), BF16),
        compiler_params=_cparams("parallel"),
        name="rmsnorm",
    )(x, g.reshape(1, d))


def _mm_kernel(x_ref, w_ref, o_ref):
    o_ref[...] = jnp.dot(x_ref[...], w_ref[0], preferred_element_type=F32)


def _mm(x, w, lyr, tm, tn):
    m, k = x.shape
    n = w.shape[2]
    return pl.pallas_call(
        _mm_kernel,
        grid=(n // tn, m // tm),
        in_specs=[pl.BlockSpec((tm, k), lambda j, i: (i, 0)), pl.BlockSpec((1, k, tn), lambda j, i: (lyr, 0, j))],
        out_specs=pl.BlockSpec((tm, tn), lambda j, i: (i, j)),
        out_shape=jax.ShapeDtypeStruct((m, n), F32),
        compiler_params=_cparams("parallel", "parallel"),
        name="matmul",
    )(x, w)


def _merge_kernel(y0_ref, y1_ref, y2_ref, w_ref, g0_ref, g1_ref, g2_ref, o_ref):
    acc = _sigmoid(g0_ref[...]) * jnp.dot(y0_ref[...], w_ref[0, 0], preferred_element_type=F32)
    acc += _sigmoid(g1_ref[...]) * jnp.dot(y1_ref[...], w_ref[0, 1], preferred_element_type=F32)
    acc += _sigmoid(g2_ref[...]) * jnp.dot(y2_ref[...], w_ref[0, 2], preferred_element_type=F32)
    o_ref[...] = acc.astype(o_ref.dtype)


def _merge(y_rw, y_ml, y_rt, w_br, lyr, z, gate_col0, tm=512, tn=1024):
    m, kw = y_rw.shape
    d = w_br.shape[3]
    gb = gate_col0 // tn
    gspec = lambda c: pl.BlockSpec((tm, tn), lambda j, i: (i, gb + c * (d // tn) + j))
    yspec = pl.BlockSpec((tm, kw), lambda j, i: (i, 0))
    return pl.pallas_call(
        _merge_kernel,
        grid=(d // tn, m // tm),
        in_specs=[yspec, yspec, yspec, pl.BlockSpec((1, 3, kw, tn), lambda j, i: (lyr, 0, 0, j)),
                  gspec(0), gspec(1), gspec(2)],
        out_specs=pl.BlockSpec((tm, tn), lambda j, i: (i, j)),
        out_shape=jax.ShapeDtypeStruct((m, d), BF16),
        compiler_params=_cparams("parallel", "parallel"),
        name="branch_merge",
    )(y_rw, y_ml, y_rt, w_br, z, z, z)


def _residual_epilogue(acc, x_ref, gpost_ref, gnext_ref, xo_ref, ho_ref):
    xn = x_ref[...] + _rms_rows(acc, gpost_ref[...])
    xo_ref[...] = xn
    ho_ref[...] = _rms_rows(xn, gnext_ref[...]).astype(ho_ref.dtype)


def _proj_res_kernel(a_ref, w_ref, x_ref, gpost_ref, gnext_ref, xo_ref, ho_ref):
    acc = jnp.dot(a_ref[...], w_ref[0], preferred_element_type=F32)
    _residual_epilogue(acc, x_ref, gpost_ref, gnext_ref, xo_ref, ho_ref)


def _proj_res(a, w, lyr, x, g_post, g_next, tm=512):
    m, k = a.shape
    d = w.shape[2]
    row = lambda i: (i, 0)
    fixed = lambda i: (0, 0)
    return pl.pallas_call(
        _proj_res_kernel,
        grid=(m // tm,),
        in_specs=[pl.BlockSpec((tm, k), row), pl.BlockSpec((1, k, d), lambda i: (lyr, 0, 0)),
                  pl.BlockSpec((tm, d), row),
                  pl.BlockSpec((1, d), fixed), pl.BlockSpec((1, d), fixed)],
        out_specs=[pl.BlockSpec((tm, d), row), pl.BlockSpec((tm, d), row)],
        out_shape=[jax.ShapeDtypeStruct((m, d), F32), jax.ShapeDtypeStruct((m, d), BF16)],
        compiler_params=_cparams("parallel"),
        name="out_proj_residual",
    )(a, w, x, g_post.reshape(1, d), g_next.reshape(1, d))


def _mlp_kernel(h_ref, w1_ref, w2_ref, x_ref, gpost_ref, gnext_ref, xo_ref, ho_ref, acc_ref):
    k = pl.program_id(1)

    @pl.when(k == 0)
    def _():
        acc_ref[...] = jnp.zeros_like(acc_ref)

    u = jnp.dot(h_ref[...], w1_ref[0], preferred_element_type=F32)
    u = jnp.square(jnp.maximum(u, 0.0)).astype(BF16)
    acc_ref[...] += jnp.dot(u, w2_ref[0], preferred_element_type=F32)

    @pl.when(k == pl.num_programs(1) - 1)
    def _():
        _residual_epilogue(acc_ref[...], x_ref, gpost_ref, gnext_ref, xo_ref, ho_ref)


def _mlp(h, w1, w2, lyr, x, g_post, g_next, tm=512, tf=1024):
    m, d = h.shape
    f = w1.shape[2]
    row = lambda i, k: (i, 0)
    fixed = lambda i, k: (0, 0)
    return pl.pallas_call(
        _mlp_kernel,
        grid=(m // tm, f // tf),
        in_specs=[pl.BlockSpec((tm, d), row), pl.BlockSpec((1, d, tf), lambda i, k: (lyr, 0, k)),
                  pl.BlockSpec((1, tf, d), lambda i, k: (lyr, k, 0)), pl.BlockSpec((tm, d), row),
                  pl.BlockSpec((1, d), fixed), pl.BlockSpec((1, d), fixed)],
        out_specs=[pl.BlockSpec((tm, d), row), pl.BlockSpec((tm, d), row)],
        out_shape=[jax.ShapeDtypeStruct((m, d), F32), jax.ShapeDtypeStruct((m, d), BF16)],
        scratch_shapes=[pltpu.VMEM((tm, d), F32)],
        compiler_params=_cparams("parallel", "arbitrary"),
        name="relu2_mlp",
    )(h, w1, w2, x, g_post.reshape(1, d), g_next.reshape(1, d))


def _attn_kernel(bb, lq, h_ref, wq_ref, mk_ref, mv_ref, wo_ref, x_ref, gpost_ref, gnext_ref, xo_ref, ho_ref):
    q = jnp.dot(h_ref[...], wq_ref[0], preferred_element_type=F32)
    outs = []
    for hd in range(X_HEADS):
        sl = slice(hd * X_HD, (hd + 1) * X_HD)
        q3 = q[:, sl].reshape(bb, lq, X_HD).astype(BF16)
        k3 = mk_ref[0, :, :, hd, :].astype(BF16)
        v3 = mv_ref[0, :, :, hd, :].astype(BF16)
        s = jnp.einsum('bqd,bmd->bqm', q3, k3, preferred_element_type=F32) * (X_HD ** -0.5)
        s = s - jnp.max(s, axis=-1, keepdims=True)
        e = jnp.exp(s)
        p = (e / jnp.sum(e, axis=-1, keepdims=True)).astype(BF16)
        o3 = jnp.einsum('bqm,bmd->bqd', p, v3, preferred_element_type=F32)
        outs.append(o3.reshape(bb * lq, X_HD))
    o = jnp.concatenate(outs, axis=1).astype(BF16)
    acc = jnp.dot(o, wo_ref[0], preferred_element_type=F32)
    _residual_epilogue(acc, x_ref, gpost_ref, gnext_ref, xo_ref, ho_ref)


def _attn(h, x, mk, mv, lyr, wq, wo, wl, g_post, g_next, row0, n_seq, seq_len, bb, lq):
    t, d = h.shape
    rows = bb * lq
    nl = seq_len // lq
    n_mem = mk.shape[2]
    rb0 = row0 // rows
    rowmap = lambda b, l: (rb0 + b * nl + l, 0)
    fixed = lambda b, l: (0, 0)
    kvspec = pl.BlockSpec((1, bb, n_mem, X_HEADS, X_HD), lambda b, l: (lyr, b, 0, 0, 0))
    return pl.pallas_call(
        functools.partial(_attn_kernel, bb, lq),
        grid=(n_seq // bb, nl),
        in_specs=[pl.BlockSpec((rows, d), rowmap), pl.BlockSpec((1, d, X_W), lambda b, l: (wl, 0, 0)), kvspec,
                  kvspec, pl.BlockSpec((1, X_W, d), lambda b, l: (wl, 0, 0)), pl.BlockSpec((rows, d), rowmap),
                  pl.BlockSpec((1, d), fixed), pl.BlockSpec((1, d), fixed)],
        out_specs=[pl.BlockSpec((rows, d), rowmap), pl.BlockSpec((rows, d), rowmap)],
        out_shape=[jax.ShapeDtypeStruct((t, d), F32), jax.ShapeDtypeStruct((t, d), BF16)],
        input_output_aliases={5: 0, 0: 1},
        compiler_params=_cparams("parallel", "parallel"),
        name="mem_cross_attention",
    )(h, wq, mk, mv, wo, x, g_post.reshape(1, d), g_next.reshape(1, d))


def _rwkv_prep_kernel(bb, lt, u_ref, s0_ref, mu_ref, w0_ref, wup_ref, a0_ref, aup_ref, gup_ref, kk_ref, ka_ref,
                      rk_ref, r_o, w_o, k_o, kk_o, b_o, v_o, g_o, bonus_o, carry_ref):
    l = pl.program_id(1)
    rows = bb * lt
    u = u_ref[:, :RW_COLS]
    if bb == 1:
        first = jnp.where(l == 0, s0_ref[0], carry_ref[...])
        first_rows = jnp.broadcast_to(first, (rows, RW_COLS))
    else:
        first_rows = jnp.broadcast_to(s0_ref[...], (bb, lt, RW_COLS)).reshape(rows, RW_COLS)
    rowid = lax.broadcasted_iota(jnp.int32, (rows, 1), 0) % lt
    prev = jnp.where(rowid == 0, first_rows, pltpu.roll(u, 1, 0))
    if bb == 1:
        carry_ref[...] = u[rows - 1:rows, :]
    z = u + (prev - u) * mu_ref[...]
    r = z[:, 0:MIX_W]
    k = z[:, MIX_W:2 * MIX_W]
    v = z[:, 2 * MIX_W:3 * MIX_W]
    o = 3 * MIX_W
    wd = z[:, o:o + RW_LORA[0]]
    ad = z[:, o + RW_LORA[0]:o + RW_LORA[0] + RW_LORA[1]]
    gd = z[:, o + RW_LORA[0] + RW_LORA[1]:RW_COLS]
    w_log = -_softplus(-(w0_ref[...] + _bdot(jnp.tanh(wd), wup_ref[...]))) - 0.5
    decay = jnp.exp(-jnp.exp(w_log))
    a = _sigmoid(a0_ref[...] + _bdot(ad, aup_ref[...]))
    g = _bdot(_sigmoid(gd), gup_ref[...])
    kk = k * kk_ref[...]
    kk = kk / jnp.maximum(jnp.sqrt(_segsum(kk * kk, RW_HD)), 1e-12)
    k = k * (1.0 + (a - 1.0) * ka_ref[...])
    r_o[...] = r
    w_o[...] = decay
    k_o[...] = k
    kk_o[...] = kk
    b_o[...] = kk * a
    v_o[...] = v
    g_o[...] = g
    bonus_o[...] = _segsum(r * k * rk_ref[...], RW_HD) * v


def _rwkv_prep(z, shift0, p, row0, n_seq, seq_len, bb, lt):
    rows = bb * lt
    nl = seq_len // lt
    assert bb == 1 or nl == 1
    rb0 = row0 // rows
    m = n_seq * seq_len
    fixed = lambda b, l: (0, 0)
    outmap = lambda b, l: (b * nl + l, 0)
    vec = lambda width: pl.BlockSpec((1, width), fixed)
    ospec = pl.BlockSpec((rows, MIX_W), outmap)
    oshape = jax.ShapeDtypeStruct((m, MIX_W), F32)
    return pl.pallas_call(
        functools.partial(_rwkv_prep_kernel, bb, lt),
        grid=(n_seq // bb, nl),
        in_specs=[pl.BlockSpec((rows, RW_COLS), lambda b, l: (rb0 + b * nl + l, 0)),
                  pl.BlockSpec((bb, 1, RW_COLS), lambda b, l: (b, 0, 0)),
                  vec(RW_COLS), vec(MIX_W), pl.BlockSpec((RW_LORA[0], MIX_W), fixed), vec(MIX_W),
                  pl.BlockSpec((RW_LORA[1], MIX_W), fixed), pl.BlockSpec((RW_LORA[2], MIX_W), fixed),
                  vec(MIX_W), vec(MIX_W), vec(MIX_W)],
        out_specs=[ospec] * 8,
        out_shape=[oshape] * 8,
        scratch_shapes=[pltpu.VMEM((1, RW_COLS), F32)],
        compiler_params=_cparams("parallel", "arbitrary"),
        name="rwkv_prep",
    )(z, shift0, p['mu'], p['w0'], p['w_up'], p['a0'], p['a_up'], p['g_up'], p['k_k'], p['k_a'], p['r_k'])


def _rwkv_scan_kernel(tc, r_ref, w_ref, k_ref, kk_ref, b_ref, v_ref, s0_ref, y_ref, s_ref):
    @pl.when(pl.program_id(1) == 0)
    def _():
        s_ref[...] = s0_ref[...]

    y_ref[0, :, RW_VP:, :] = jnp.zeros((tc, RW_VR - RW_VP, LANES), F32)
    keys = lambda ref, t: ref[0, t, :RW_HD, :]

    def step(t, carry):
        for vp in range(RW_VP):
            s = s_ref[0, vp]
            sa = jnp.sum(s * keys(kk_ref, t), axis=0, keepdims=True)
            sn = s * keys(w_ref, t) - sa * keys(b_ref, t) + v_ref[0, t, vp:vp + 1, :] * keys(k_ref, t)
            s_ref[0, vp] = sn
            y_ref[0, t, vp:vp + 1, :] = jnp.sum(sn * keys(r_ref, t), axis=0, keepdims=True)
        return carry

    lax.fori_loop(0, tc, step, 0)


def _rwkv_scan(r, w, k, kk, b, v, s0, lyr_in, tc, depth, lyr, s_prev):
    g, l = r.shape[0], r.shape[1]
    kspec = pl.BlockSpec((1, tc, RW_KR, LANES), lambda i, c: (i, c, 0, 0))
    vspec = pl.BlockSpec((1, tc, RW_VR, LANES), lambda i, c: (i, c, 0, 0))
    sspec = lambda at: pl.BlockSpec((1, RW_VP, RW_HD, LANES), lambda i, c: (at * g + i, 0, 0, 0))
    kern, prev_specs, prev_args, aliases = _carry_through(functools.partial(_rwkv_scan_kernel, tc), 7,
                                                          [None, s_prev])
    return pl.pallas_call(
        kern,
        grid=(g, l // tc),
        in_specs=[kspec] * 5 + [vspec, sspec(lyr_in)] + prev_specs,
        out_specs=[vspec, sspec(lyr)],
        out_shape=[jax.ShapeDtypeStruct((g, l, RW_VR, LANES), F32),
                   jax.ShapeDtypeStruct((depth * g, RW_VP, RW_HD, LANES), F32)],
        input_output_aliases=aliases,
        compiler_params=_cparams("parallel", "arbitrary"),
        name="rwkv_scan",
    )(r, w, k, kk, b, v, s0, *prev_args)


def _rwkv_post_kernel(y_ref, g_ref, bonus_ref, gng_ref, gnb_ref, o_ref):
    y = y_ref[...]
    mean = _segsum(y, RW_HD) * (1.0 / RW_HD)
    yc = y - mean
    var = _segsum(yc * yc, RW_HD) * (1.0 / RW_HD)
    yn = yc * lax.rsqrt(var + RW_GN_EPS) * gng_ref[...] + gnb_ref[...]
    o_ref[...] = ((yn + bonus_ref[...]) * g_ref[...]).astype(o_ref.dtype)


def _rwkv_post(y, g, bonus, gn_g, gn_b, row0, total_rows, prev, tm=256):
    m = y.shape[0]
    rb0 = row0 // tm
    fixed = lambda i: (0, 0)
    spec = pl.BlockSpec((tm, MIX_W), lambda i: (i, 0))
    kern, prev_specs, prev_args, aliases = _carry_through(_rwkv_post_kernel, 5, [prev])
    return pl.pallas_call(
        kern,
        grid=(m // tm,),
        in_specs=[spec, spec, spec, pl.BlockSpec((1, MIX_W), fixed), pl.BlockSpec((1, MIX_W), fixed)] + prev_specs,
        out_specs=pl.BlockSpec((tm, MIX_W), lambda i: (rb0 + i, 0)),
        out_shape=jax.ShapeDtypeStruct((total_rows, MIX_W), BF16),
        input_output_aliases=aliases,
        compiler_params=_cparams("parallel"),
        name="rwkv_post",
    )(y, g, bonus, gn_g, gn_b, *prev_args)


RW_NB = RW_CHAINS // RW_HEADS
RW_TT = LANES


RW_KR = RW_HD + 8
RW_VR = RW_VP + 8


def _natural_to_chain_rows(x_ref, a_ref):
    for bl in range(RW_NB):
        xt = x_ref[bl].T
        for h in range(RW_HEADS):
            c = bl * RW_HEADS + h
            a_ref[c * RW_KR:c * RW_KR + RW_HD, :] = xt[h * RW_HD:(h + 1) * RW_HD]


def _keys_to_scan_kernel(x_ref, o_ref, a_ref):
    _natural_to_chain_rows(x_ref, a_ref)
    o_ref[...] = jnp.zeros_like(o_ref)
    for k in range(RW_HD):
        tile = a_ref[pl.ds(k, RW_CHAINS, stride=RW_KR), :]
        o_ref[0, pl.ds(k, RW_TT, stride=RW_KR), :] = jnp.concatenate([tile, tile], axis=0).T


def _vals_to_scan_kernel(x_ref, o_ref, a_ref):
    _natural_to_chain_rows(x_ref, a_ref)
    o_ref[...] = jnp.zeros_like(o_ref)
    for vp in range(RW_VP):
        halves = [a_ref[pl.ds(vh * RW_VP + vp, RW_CHAINS, stride=RW_KR), :] for vh in range(2)]
        o_ref[0, pl.ds(vp, RW_TT, stride=RW_VR), :] = jnp.concatenate(halves, axis=0).T


def _vals_from_scan_kernel(y_ref, o_ref, a_ref):
    for vp in range(RW_VP):
        tile = y_ref[0, pl.ds(vp, RW_TT, stride=RW_VR), :].T
        for vh in range(2):
            a_ref[pl.ds(vh * RW_VP + vp, RW_CHAINS, stride=RW_KR), :] = tile[vh * RW_CHAINS:(vh + 1) * RW_CHAINS]
    for bl in range(RW_NB):
        rows = [a_ref[c * RW_KR:c * RW_KR + RW_HD, :] for c in range(bl * RW_HEADS, (bl + 1) * RW_HEADS)]
        o_ref[bl] = jnp.concatenate(rows, axis=0).T


def _scan_relayout(kernel_fn, x, n_seq, seq_len, rows_per_t, to_scan, name):
    g, nl = n_seq // RW_NB, seq_len // RW_TT
    nat_spec = pl.BlockSpec((RW_NB, RW_TT, MIX_W), lambda i, l: (i, l, 0))
    scan_spec = pl.BlockSpec((1, RW_TT * rows_per_t, LANES), lambda i, l: (i, l, 0))
    nat_shape = jax.ShapeDtypeStruct((n_seq, seq_len, MIX_W), F32)
    scan_shape = jax.ShapeDtypeStruct((g, seq_len * rows_per_t, LANES), F32)
    x = x.reshape(n_seq, seq_len, MIX_W) if to_scan else x.reshape(g, seq_len * rows_per_t, LANES)
    out = pl.pallas_call(
        kernel_fn,
        grid=(g, nl),
        in_specs=[nat_spec if to_scan else scan_spec],
        out_specs=scan_spec if to_scan else nat_spec,
        out_shape=scan_shape if to_scan else nat_shape,
        scratch_shapes=[pltpu.VMEM((RW_CHAINS * RW_KR, RW_TT), F32)],
        compiler_params=_cparams("parallel", "parallel"),
        name=name,
    )(x)
    return out.reshape(g, seq_len, rows_per_t, LANES) if to_scan else out.reshape(n_seq * seq_len, MIX_W)


def _pad_rows(a, rows):
    return jnp.pad(a, ((0, 0), (0, 0), (0, rows - a.shape[2]), (0, 0)))


def _to_scan_keys(a, n_seq, seq_len):
    if seq_len % RW_TT == 0:
        return _scan_relayout(_keys_to_scan_kernel, a, n_seq, seq_len, RW_KR, True, "rwkv_keys_to_scan")
    g = n_seq // RW_NB
    a = a.reshape(g, RW_NB, seq_len, RW_HEADS, RW_HD).transpose(0, 2, 4, 1, 3).reshape(g, seq_len, RW_HD, RW_CHAINS)
    return _pad_rows(jnp.concatenate([a, a], axis=-1), RW_KR)


def _to_scan_vals(a, n_seq, seq_len):
    if seq_len % RW_TT == 0:
        return _scan_relayout(_vals_to_scan_kernel, a, n_seq, seq_len, RW_VR, True, "rwkv_vals_to_scan")
    g = n_seq // RW_NB
    a = a.reshape(g, RW_NB, seq_len, RW_HEADS, 2, RW_VP).transpose(0, 2, 5, 4, 1, 3)
    return _pad_rows(a.reshape(g, seq_len, RW_VP, LANES), RW_VR)


def _from_scan_vals(y, n_seq, seq_len):
    if seq_len % RW_TT == 0:
        return _scan_relayout(_vals_from_scan_kernel, y, n_seq, seq_len, RW_VR, False, "rwkv_vals_from_scan")
    g = n_seq // RW_NB
    y = y[:, :, :RW_VP].reshape(g, seq_len, RW_VP, 2, RW_NB, RW_HEADS).transpose(0, 4, 1, 5, 3, 2)
    return y.reshape(n_seq * seq_len, MIX_W)


def _state_to_scan(s, n_seq):
    nb = RW_CHAINS // RW_HEADS
    g = n_seq // nb
    s = s.reshape(g, nb, RW_HEADS, 2, RW_VP, RW_HD).transpose(0, 4, 5, 3, 1, 2)
    return s.reshape(g, RW_VP, RW_HD, LANES)


def _state_from_scan(s, n_seq):
    nb = RW_CHAINS // RW_HEADS
    g = n_seq // nb
    s = s.reshape(g, RW_VP, RW_HD, 2, nb, RW_HEADS).transpose(0, 4, 5, 3, 1, 2)
    return s.reshape(n_seq, RW_HEADS, RW_HD, RW_HD)


def _rwkv_branch(z, shift0, s0, lyr_in, p, row0, n_seq, seq_len, bb, lt, tc, depth, lyr, prev):
    r, w, k, kk, b, v, g, bonus = _rwkv_prep(z, shift0, p, row0, n_seq, seq_len, bb, lt)
    keys = [_to_scan_keys(a, n_seq, seq_len) for a in (r, w, k, kk, b)]
    y, s_new = _rwkv_scan(*keys, _to_scan_vals(v, n_seq, seq_len), s0, lyr_in, tc, depth, lyr, prev[1])
    y = _from_scan_vals(y, n_seq, seq_len)
    return _rwkv_post(y, g, bonus, p['gn_g'], p['gn_b'], row0, z.shape[0], prev[0]), s_new


def _seq_masks(rows, lc):
    ri = lax.broadcasted_iota(jnp.int32, (rows, rows), 0)
    ci = lax.broadcasted_iota(jnp.int32, (rows, rows), 1)
    same = (ri // lc) == (ci // lc)
    return ri, ci, same


def _rows_from_seq(x, bb, lc):
    return jnp.broadcast_to(x, (bb, lc, x.shape[-1])).reshape(bb * lc, x.shape[-1])


def _last_of_seq(x, bb, lc):
    x3 = x.reshape(bb, lc, x.shape[-1])
    pick = (lax.broadcasted_iota(jnp.int32, (1, lc, 1), 1) == lc - 1).astype(F32)
    return jnp.sum(x3 * pick, axis=1, keepdims=True)


def _transpose_rows(x):
    rows = x.shape[0]
    if rows < LANES:
        x = jnp.concatenate([x, jnp.zeros((LANES - rows, LANES), x.dtype)], axis=0)
    return x.T[:, :rows]


def _mlstm_kernel(bb, lc, q_ref, k_ref, v_ref, o_ref, gt_ref, gb_ref, ng_ref, c0_ref, n0_ref, m0_ref,
                  y_ref, c_ref, n_ref, m_ref):
    rows = bb * lc

    @pl.when(pl.program_id(1) == 0)
    def _():
        c_ref[...] = c0_ref[...]
        n_ref[...] = n0_ref[...]
        m_ref[...] = m0_ref[...]

    gt = ML_GATE_CAP * jnp.tanh((gt_ref[...] + gb_ref[...]) * (1.0 / ML_GATE_CAP))
    lane = lax.broadcasted_iota(jnp.int32, (rows, LANES), 1)
    ri, ci, same = _seq_masks(rows, lc)
    causal = same & (ci <= ri)
    cum = _dot_exact_lhs(causal.astype(F32), -_softplus(-gt))
    gt_t = _transpose_rows(gt)
    cum_t = _transpose_rows(cum)
    for hd in range(ML_HEADS):
        sl = slice(hd * ML_HD, (hd + 1) * ML_HD)
        c0 = c_ref[0, :, hd]
        n0 = n_ref[0, :, hd:hd + 1, :]
        m0 = m_ref[0, :, :, hd:hd + 1]
        q = q_ref[:, sl]
        k = k_ref[:, sl] * (ML_HD ** -0.5)
        v = v_ref[:, sl]
        bcol = cum[:, ML_HEADS + hd:ML_HEADS + hd + 1]
        igcol = gt[:, hd:hd + 1]
        rowb = gt_t[hd:hd + 1, :] - cum_t[ML_HEADS + hd:ML_HEADS + hd + 1, :]
        dlog = jnp.where(causal, bcol + rowb, -jnp.inf)
        m_inter = bcol + _rows_from_seq(m0, bb, lc)
        m_t = jnp.maximum(m_inter, jnp.max(dlog, axis=-1, keepdims=True))
        qk = lax.dot_general(q.astype(BF16), k.astype(BF16), (((1,), (1,)), ((), ())),
                             preferred_element_type=F32)
        wts = jnp.exp(dlog - m_t) * qk
        s_inter = jnp.exp(m_inter - m_t)
        q3 = q.reshape(bb, lc, ML_HD).astype(BF16)
        qc = jnp.einsum('btd,bde->bte', q3, c0.astype(BF16), preferred_element_type=F32).reshape(rows, ML_HD)
        num = _bdot(wts, v) + s_inter * qc
        qn = jnp.sum(q * _rows_from_seq(n0, bb, lc), axis=-1, keepdims=True)
        den = jnp.sum(wts, axis=-1, keepdims=True) + s_inter * qn
        hh = num / jnp.maximum(jnp.abs(den), jnp.exp(-m_t))
        hh = hh * lax.rsqrt(jnp.mean(hh * hh, axis=-1, keepdims=True) + EPS) * ng_ref[:, sl]
        y_ref[:, sl] = (_sigmoid(o_ref[:, sl]) * hh).astype(y_ref.dtype)

        stat = jnp.where(lane == 0, bcol, jnp.where(lane == 1, m_t, 0.0))
        last = _last_of_seq(stat, bb, lc)
        last_rows = _rows_from_seq(last, bb, lc)
        b_last, m_new = last_rows[:, 0:1], last_rows[:, 1:2]
        w_end = jnp.exp(b_last - bcol + igcol - m_new)
        f_end = jnp.exp(last[:, :, 0:1] + m0 - last[:, :, 1:2])
        kw3 = (k * w_end).reshape(bb, lc, ML_HD)
        upd = jnp.einsum('bsd,bse->bde', kw3.astype(BF16), v.reshape(bb, lc, ML_HD).astype(BF16),
                         preferred_element_type=F32)
        c_ref[0, :, hd] = f_end * c0 + upd
        n_ref[0, :, hd:hd + 1, :] = f_end * n0 + jnp.sum(kw3, axis=1, keepdims=True)
        m_ref[0, :, :, hd:hd + 1] = last[:, :, 1:2]


def _mlstm_seq_kernel(lc, q_ref, k_ref, v_ref, o_ref, gt_ref, gb_ref, ng_ref, c0_ref, n0_ref, m0_ref,
                      y_ref, c_ref, n_ref, m_ref):
    heads = range(ML_HEADS)

    @pl.when(pl.program_id(1) == 0)
    def _():
        c_ref[...] = c0_ref[...]
        n_ref[...] = n0_ref[...]
        m_ref[...] = m0_ref[...]

    gt = ML_GATE_CAP * jnp.tanh((gt_ref[...] + gb_ref[...]) * (1.0 / ML_GATE_CAP))
    ri, ci, _ = _seq_masks(lc, lc)
    causal = ci <= ri
    cum = _dot_exact_lhs(causal.astype(F32), -_softplus(-gt))
    gt_t = _transpose_rows(gt)
    cum_t = _transpose_rows(cum)
    sl = lambda hd: slice(hd * ML_HD, (hd + 1) * ML_HD)
    stack = lambda f: jnp.stack([f(hd) for hd in heads])
    q = stack(lambda hd: q_ref[:, sl(hd)])
    k = stack(lambda hd: k_ref[:, sl(hd)]) * (ML_HD ** -0.5)
    v = stack(lambda hd: v_ref[:, sl(hd)])
    bcol = stack(lambda hd: cum[:, ML_HEADS + hd:ML_HEADS + hd + 1])
    igcol = stack(lambda hd: gt[:, hd:hd + 1])
    rowb = stack(lambda hd: gt_t[hd:hd + 1, :] - cum_t[ML_HEADS + hd:ML_HEADS + hd + 1, :])
    c0 = c_ref[0, 0]
    n0 = stack(lambda hd: n_ref[0, 0, hd:hd + 1, :])
    m0 = stack(lambda hd: m_ref[0, 0, :, hd:hd + 1])
    qb, kb, vb = q.astype(BF16), k.astype(BF16), v.astype(BF16)

    dlog = jnp.where(causal[None], bcol + rowb, -jnp.inf)
    m_inter = bcol + m0
    m_t = jnp.maximum(m_inter, jnp.max(dlog, axis=-1, keepdims=True))
    qk = jnp.einsum('hqd,hkd->hqk', qb, kb, preferred_element_type=F32)
    wts = jnp.exp(dlog - m_t) * qk
    s_inter = jnp.exp(m_inter - m_t)
    qc = jnp.einsum('hqd,hde->hqe', qb, c0.astype(BF16), preferred_element_type=F32)
    num = jnp.einsum('hqk,hke->hqe', wts.astype(BF16), vb, preferred_element_type=F32) + s_inter * qc
    den = jnp.sum(wts, axis=-1, keepdims=True) + s_inter * jnp.sum(q * n0, axis=-1, keepdims=True)
    hh = num / jnp.maximum(jnp.abs(den), jnp.exp(-m_t))
    hh = hh * lax.rsqrt(jnp.mean(hh * hh, axis=-1, keepdims=True) + EPS)
    for hd in heads:
        y_ref[:, sl(hd)] = (_sigmoid(o_ref[:, sl(hd)]) * hh[hd] * ng_ref[:, sl(hd)]).astype(y_ref.dtype)

    b_last, m_new = bcol[:, lc - 1:lc, :], m_t[:, lc - 1:lc, :]
    w_end = jnp.exp(b_last - bcol + igcol - m_new)
    f_end = jnp.exp(b_last + m0 - m_new)
    kw = k * w_end
    upd = jnp.einsum('hsd,hse->hde', kw.astype(BF16), vb, preferred_element_type=F32)
    c_ref[0, 0] = f_end * c0 + upd
    n_new = f_end * n0 + jnp.sum(kw, axis=1, keepdims=True)
    for hd in heads:
        n_ref[0, 0, hd:hd + 1, :] = n_new[hd]
        m_ref[0, 0, :, hd:hd + 1] = m_new[hd]


def _mlstm_branch(z, z_gates, init, lyr_in, gate_bias, norm_g, row0, n_seq, seq_len, bb, lc, depth, lyr, prev):
    rows = bb * lc
    nc = seq_len // lc
    rb0 = row0 // rows
    rowmap = lambda part: (lambda b, c: (rb0 + b * nc + c, part))
    zspec = lambda part: pl.BlockSpec((rows, MIX_W), rowmap(part))
    state_specs = lambda at: [pl.BlockSpec((1, bb, ML_HEADS, ML_HD, ML_HD), lambda b, c: (at, b, 0, 0, 0)),
                              pl.BlockSpec((1, bb, ML_HEADS, ML_HD), lambda b, c: (at, b, 0, 0)),
                              pl.BlockSpec((1, bb, 1, ML_HEADS), lambda b, c: (at, b, 0, 0))]
    body = functools.partial(_mlstm_seq_kernel, lc) if bb == 1 else functools.partial(_mlstm_kernel, bb, lc)
    kern, prev_specs, prev_args, aliases = _carry_through(body, 10, prev)
    return pl.pallas_call(
        kern,
        grid=(n_seq // bb, nc),
        in_specs=[zspec(0), zspec(1), zspec(2), zspec(3),
                  pl.BlockSpec((rows, LANES), lambda b, c: (rb0 + b * nc + c, 0)),
                  pl.BlockSpec((1, LANES), lambda b, c: (0, 0)), pl.BlockSpec((1, MIX_W), lambda b, c: (0, 0))]
                 + state_specs(lyr_in) + prev_specs,
        out_specs=[pl.BlockSpec((rows, MIX_W), lambda b, c: (rb0 + b * nc + c, 0))] + state_specs(lyr),
        out_shape=[jax.ShapeDtypeStruct((z.shape[0], MIX_W), BF16),
                   jax.ShapeDtypeStruct((depth, n_seq, ML_HEADS, ML_HD, ML_HD), F32),
                   jax.ShapeDtypeStruct((depth, n_seq, ML_HEADS, ML_HD), F32),
                   jax.ShapeDtypeStruct((depth, n_seq, 1, ML_HEADS), F32)],
        input_output_aliases=aliases,
        compiler_params=_cparams("parallel", "arbitrary"),
        name="mlstm_chunk",
    )(z, z, z, z, z_gates, gate_bias, norm_g, *init, *prev_args)


def _rope_rows(x, cos, sin):
    half = RT_HD // 2
    x1, x2 = x[:, :half], x[:, half:]
    return jnp.concatenate([x1 * cos - x2 * sin, x1 * sin + x2 * cos], axis=1)


def _ret_kernel(bb, lc, q_ref, k_ref, v_ref, g_ref, cs_ref, dec_ref, cd_ref, s0_ref, y_ref, s_ref):
    rows = bb * lc

    @pl.when(pl.program_id(1) == 0)
    def _():
        s_ref[...] = s0_ref[...]

    cos, sin = cs_ref[:, :RT_HD // 2], cs_ref[:, RT_HD // 2:]
    for hd in range(RT_HEADS):
        sl = slice(hd * RT_HD, (hd + 1) * RT_HD)
        s0 = s_ref[0, :, hd]
        q = _rope_rows(q_ref[:, sl], cos, sin)
        k = _rope_rows(k_ref[:, sl], cos, sin) * (RT_HD ** -0.5)
        v = v_ref[:, sl]
        cd = cd_ref[hd]
        qk = lax.dot_general(q.astype(BF16), k.astype(BF16), (((1,), (1,)), ((), ())),
                             preferred_element_type=F32)
        inner = _bdot(qk * dec_ref[hd], v)
        q3 = q.reshape(bb, lc, RT_HD).astype(BF16)
        cross = jnp.einsum('btd,bde->bte', q3, s0.astype(BF16), preferred_element_type=F32).reshape(rows, RT_HD)
        y = inner + cross * cd[:, 0:1]
        y = y * lax.rsqrt(jnp.mean(y * y, axis=-1, keepdims=True) + EPS)
        gt = g_ref[:, sl]
        y_ref[:, sl] = (gt * _sigmoid(gt) * y).astype(y_ref.dtype)
        kd = (k * cd[:, 1:2]).reshape(bb, lc, RT_HD).astype(BF16)
        upd = jnp.einsum('bsd,bse->bde', kd, v.reshape(bb, lc, RT_HD).astype(BF16), preferred_element_type=F32)
        s_ref[0, :, hd] = cd[0:1, 2:3] * s0 + upd


def _ret_branch(z, s0, lyr_in, row0, n_seq, seq_len, bb, lc, pos0, depth, lyr, prev):
    rows = bb * lc
    nc = seq_len // lc
    rb0 = row0 // rows
    half = RT_HD // 2
    inv = ROPE_BASE ** (-jnp.arange(half, dtype=F32) / half)
    pos = jnp.broadcast_to((jnp.arange(nc * lc, dtype=F32) + float(pos0)).reshape(nc, 1, lc), (nc, bb, lc))
    ang = pos.reshape(nc * rows)[:, None] * inv[None, :]
    cs = jnp.concatenate([jnp.cos(ang), jnp.sin(ang)], axis=1)
    log_g = jnp.log(1.0 - jnp.exp(jnp.linspace(math.log(1.0 / 32), math.log(1.0 / 512), RT_HEADS)))
    idx = jnp.arange(rows)
    t = (idx % lc).astype(F32)
    diff = t[:, None] - t[None, :]
    same = (idx[:, None] // lc) == (idx[None, :] // lc)
    dec = jnp.where(same & (diff >= 0), jnp.exp(log_g[:, None, None] * jnp.maximum(diff, 0.0)), 0.0)
    cd = jnp.zeros((RT_HEADS, rows, LANES), F32)
    cd = cd.at[:, :, 0].set(jnp.exp(log_g[:, None] * (t[None, :] + 1.0)))
    cd = cd.at[:, :, 1].set(jnp.exp(log_g[:, None] * (lc - 1.0 - t)[None, :]))
    cd = cd.at[:, :, 2].set(jnp.broadcast_to(jnp.exp(log_g * lc)[:, None], (RT_HEADS, rows)))
    rowmap = lambda part: (lambda b, c: (rb0 + b * nc + c, part))
    zspec = lambda part: pl.BlockSpec((rows, MIX_W), rowmap(part))
    sspec = lambda at: pl.BlockSpec((1, bb, RT_HEADS, RT_HD, RT_HD), lambda b, c: (at, b, 0, 0, 0))
    kern, prev_specs, prev_args, aliases = _carry_through(functools.partial(_ret_kernel, bb, lc), 8, prev)
    return pl.pallas_call(
        kern,
        grid=(n_seq // bb, nc),
        in_specs=[zspec(0), zspec(1), zspec(2), zspec(3),
                  pl.BlockSpec((rows, RT_HD), lambda b, c: (c, 0)),
                  pl.BlockSpec((RT_HEADS, rows, rows), lambda b, c: (0, 0, 0)),
                  pl.BlockSpec((RT_HEADS, rows, LANES), lambda b, c: (0, 0, 0)),
                  sspec(lyr_in)] + prev_specs,
        out_specs=[pl.BlockSpec((rows, MIX_W), lambda b, c: (rb0 + b * nc + c, 0)), sspec(lyr)],
        out_shape=[jax.ShapeDtypeStruct((z.shape[0], MIX_W), BF16),
                   jax.ShapeDtypeStruct((depth, n_seq, RT_HEADS, RT_HD, RT_HD), F32)],
        input_output_aliases=aliases,
        compiler_params=_cparams("parallel", "arbitrary"),
        name="retention_chunk",
    )(z, z, z, z, cs, dec, cd, s0, *prev_args)


def _mm_wt_kernel(x_ref, wt_ref, o_ref, wb_ref):
    @pl.when(pl.program_id(1) == 0)
    def _():
        tn = wb_ref.shape[1]
        for c in range(0, tn, LANES):
            wb_ref[:, c:c + LANES] = wt_ref[0, c:c + LANES, :].T.astype(wb_ref.dtype)

    o_ref[...] = jnp.dot(x_ref[...], wb_ref[...], preferred_element_type=F32)


def _mm_wt(x, w_t, lyr, col0, n, tm, tn):
    m, k = x.shape
    return pl.pallas_call(
        _mm_wt_kernel,
        grid=(n // tn, m // tm),
        in_specs=[pl.BlockSpec((tm, k), lambda j, i: (i, 0)),
                  pl.BlockSpec((pl.Element(1), pl.Element(tn), pl.Element(k)),
                               lambda j, i: (lyr, pl.multiple_of(col0 + j * tn, 8), 0))],
        out_specs=pl.BlockSpec((tm, tn), lambda j, i: (i, j)),
        out_shape=jax.ShapeDtypeStruct((m, n), F32),
        scratch_shapes=[pltpu.VMEM((k, tn), BF16)],
        compiler_params=_cparams("parallel", "arbitrary"),
        name="matmul_wt",
    )(x, w_t)


def _group_cfg(n_seq, seq_len):
    if seq_len % CHUNK == 0:
        return dict(rw_bb=1, rw_lt=256, rw_tc=32, ml_bb=1, rt_bb=1, ch_lc=CHUNK, at_bb=1, at_lq=512)
    return dict(rw_bb=32, rw_lt=seq_len, rw_tc=seq_len, ml_bb=8, rt_bb=8, ch_lc=seq_len, at_bb=8, at_lq=seq_len)


def kernel(x_prompt, x_sample, mem_prompt, state_rwkv_shift, state_rwkv, state_mlstm_c, state_mlstm_n,
           state_mlstm_m, state_ret, cache_mem_k, cache_mem_v, g_pre_mix, g_post_mix, g_pre_x, g_post_x,
           g_pre_ff, g_post_ff, g_mem, w_in, rw_mu, rw_w0, rw_w_up, rw_a0, rw_a_up, rw_g_up, rw_k_k, rw_k_a,
           rw_r_k, rw_gn_g, rw_gn_b, ml_i_b, ml_f_b, ml_norm_g, w_br, w_out, x_wq, x_wkv, x_wo, ff_w1, ff_w2):
    bp, lp, d = x_prompt.shape
    bs, ls, _ = x_sample.shape
    depth = w_in.shape[0]
    n_mem = mem_prompt.shape[1]
    tp, ts = bp * lp, bs * ls
    groups = [(0, bp, lp, 0), (tp, bs, ls, PAST_LEN)]

    x = jnp.concatenate([x_prompt.reshape(tp, d), x_sample.reshape(ts, d)], axis=0)
    mem = mem_prompt.reshape(bp * n_mem, d)
    h = _rmsnorm(x, g_pre_mix[0])

    zeros = lambda *s: jnp.zeros(s, F32)
    rw_scan0 = _state_to_scan(state_rwkv.reshape(depth * bs, RW_HEADS, RW_HD, RW_HD), depth * bs)
    inits = [
        lambda l: (zeros(bp, 1, RW_COLS), zeros(bp // RW_NB, RW_VP, RW_HD, LANES),
                   (zeros(1, bp, ML_HEADS, ML_HD, ML_HD), zeros(1, bp, ML_HEADS, ML_HD), zeros(1, bp, 1, ML_HEADS)),
                   zeros(1, bp, RT_HEADS, RT_HD, RT_HD), 0),
        lambda l: (state_rwkv_shift[l].reshape(bs, 1, RW_COLS), rw_scan0,
                   (state_mlstm_c, state_mlstm_n, state_mlstm_m.reshape(depth, bs, 1, ML_HEADS)), state_ret, l),
    ]
    shifts = [[], []]
    rw_states = [None, None]
    ml_states = [[None] * 3, [None] * 3]
    rt_states = [None, None]
    mem_k, mem_v = [], []
    w_in_t = jnp.swapaxes(w_in, 1, 2)
    w_kv, w_brb, w_outb, w_q, w_o, w_f1, w_f2 = (w.astype(BF16) for w in (x_wkv, w_br, w_out, x_wq, x_wo,
                                                                         ff_w1, ff_w2))
    for l in range(depth):
        kv = _mm(_rmsnorm(mem, g_mem[l], tm=256), w_kv, l, tm=256, tn=512)
        mem_k.append(kv[:, :X_W].reshape(bp, n_mem, X_HEADS, X_HD))
        mem_v.append(kv[:, X_W:].reshape(bp, n_mem, X_HEADS, X_HD))

        z_rw = _mm_wt(h, w_in_t, l, 0, RW_COLS, tm=512, tn=RW_COLS // 2)
        z_ml = _mm_wt(h, w_in_t, l, IN_ML, 4 * MIX_W, tm=1024, tn=1024)
        z_g = _mm_wt(h, w_in_t, l, IN_MLG, LANES, tm=1024, tn=LANES)
        z_rg = _mm_wt(h, w_in_t, l, IN_RT, w_in.shape[2] - IN_RT, tm=1024, tn=1280)

        rw_p = dict(mu=rw_mu[l].reshape(1, -1), w0=rw_w0[l].reshape(1, -1), w_up=rw_w_up[l].astype(BF16),
                    a0=rw_a0[l].reshape(1, -1), a_up=rw_a_up[l].astype(BF16), g_up=rw_g_up[l].astype(BF16),
                    k_k=rw_k_k[l].reshape(1, -1), k_a=rw_k_a[l].reshape(1, -1), r_k=rw_r_k[l].reshape(1, -1),
                    gn_g=rw_gn_g[l].reshape(1, -1), gn_b=rw_gn_b[l].reshape(1, -1))
        gate_bias = jnp.concatenate([ml_i_b[l], ml_f_b[l], zeros(LANES - 2 * ML_HEADS)]).reshape(1, LANES)
        norm_g = ml_norm_g[l].reshape(1, MIX_W)

        y_rw = y_ml = y_rt = None
        for gi, (row0, n_seq, seq_len, pos0) in enumerate(groups):
            cfg = _group_cfg(n_seq, seq_len)
            shift0, s_rw0, ml0, rt0, lyr_in = inits[gi](l)
            y_rw, rw_states[gi] = _rwkv_branch(z_rw, shift0, s_rw0, lyr_in, rw_p, row0, n_seq, seq_len,
                                               cfg['rw_bb'], cfg['rw_lt'], cfg['rw_tc'], depth, l,
                                               (y_rw, rw_states[gi]))
            y_ml, *ml_states[gi] = _mlstm_branch(z_ml, z_g, ml0, lyr_in, gate_bias, norm_g, row0, n_seq, seq_len,
                                                 cfg['ml_bb'], cfg['ch_lc'], depth, l, [y_ml] + ml_states[gi])
            y_rt, rt_states[gi] = _ret_branch(z_rg, rt0, lyr_in, row0, n_seq, seq_len, cfg['rt_bb'],
                                              cfg['ch_lc'], pos0, depth, l, [y_rt, rt_states[gi]])
            last = row0 + seq_len - 1
            shifts[gi].append(lax.slice(z_rw, (last, 0), (last + (n_seq - 1) * seq_len + 1, RW_COLS),
                                        (seq_len, 1)))

        merged = _merge(y_rw, y_ml, y_rt, w_brb, l, z_rg, Z_GATE)
        x, h = _proj_res(merged, w_outb, l, x, g_post_mix[l], g_pre_x[l])

        for gi, (row0, n_seq, seq_len, pos0) in enumerate(groups):
            cfg = _group_cfg(n_seq, seq_len)
            if gi == 0:
                mk, mv, lyr_in = mem_k[l][None], mem_v[l][None], 0
            else:
                mk, mv, lyr_in = cache_mem_k, cache_mem_v, l
            x, h = _attn(h, x, mk, mv, lyr_in, w_q, w_o, l, g_post_x[l], g_pre_ff[l], row0, n_seq, seq_len,
                         cfg['at_bb'], cfg['at_lq'])

        g_next = g_pre_mix[(l + 1) % depth]
        x, h = _mlp(h, w_f1, w_f2, l, x, g_post_ff[l], g_next)

    def states(gi, n_seq):
        c_new, n_new, m_new = ml_states[gi]
        s_rw = _state_from_scan(rw_states[gi], depth * n_seq).reshape(depth, n_seq, RW_HEADS, RW_HD, RW_HD)
        return (jnp.stack(shifts[gi]), s_rw, c_new, n_new, m_new.reshape(depth, n_seq, ML_HEADS), rt_states[gi])

    return (x[:tp].reshape(bp, lp, d), x[tp:].reshape(bs, ls, d), *states(0, bp),
            jnp.stack(mem_k), jnp.stack(mem_v), *states(1, bs))
```

```python
import functools
import math

import jax
import jax.numpy as jnp
from jax import lax
from jax.experimental import pallas as pl
from jax.experimental.pallas import tpu as pltpu

F32 = jnp.float32
BF16 = jnp.bfloat16

D_MODEL = 2048
MIX_W = D_MODEL // 2
RW_HD = 64
RW_HEADS = MIX_W // RW_HD
RW_LORA = (64, 64, 128)
RW_COLS = 3 * MIX_W + sum(RW_LORA)
RW_GN_EPS = 64e-5
ML_HEADS = 8
ML_HD = MIX_W // ML_HEADS
ML_GATE_CAP = 15.0
RT_HEADS = 4
RT_HD = MIX_W // RT_HEADS
ROPE_BASE = 10000.0
X_HEADS = 4
X_HD = 128
X_W = X_HEADS * X_HD
D_FF = 4 * D_MODEL
CHUNK = 64
EPS = 1e-6
PAST_LEN = 16384

IN_ML = RW_COLS
IN_MLG = IN_ML + 4 * MIX_W
IN_RT = IN_MLG + 2 * ML_HEADS
Z_GATE = 4 * MIX_W

LANES = 128
RW_VP = RW_HD // 2
RW_CHAINS = LANES // 2


def _cparams(*sem):
    return pltpu.CompilerParams(dimension_semantics=sem, vmem_limit_bytes=56 * 1024 * 1024)


def _carry_through(kernel_fn, n_in, prev):
    carried = [(i, p) for i, p in enumerate(prev) if p is not None]
    if not carried:
        return kernel_fn, [], [], {}

    def wrapped(*refs):
        return kernel_fn(*refs[:n_in], *refs[n_in + len(carried):])

    return (wrapped, [pl.BlockSpec(memory_space=pl.ANY)] * len(carried), [p for _, p in carried],
            {n_in + j: i for j, (i, _) in enumerate(carried)})


def _sigmoid(x):
    return 1.0 / (1.0 + jnp.exp(-x))


def _softplus(x):
    return jnp.maximum(x, 0.0) + jnp.log(1.0 + jnp.exp(-jnp.abs(x)))


def _bdot(a, b):
    return jnp.dot(a.astype(BF16), b.astype(BF16), preferred_element_type=F32)


def _split3(x):
    hi = x.astype(BF16)
    r1 = x - hi.astype(F32)
    mid = r1.astype(BF16)
    lo = (r1 - mid.astype(F32)).astype(BF16)
    return hi, mid, lo


def _dot_exact_lhs(m, x):
    mb = m.astype(BF16)
    hi, mid, lo = _split3(x)
    return (jnp.dot(mb, hi, preferred_element_type=F32) + jnp.dot(mb, mid, preferred_element_type=F32)
            + jnp.dot(mb, lo, preferred_element_type=F32))


def _dot_exact_rhs(x, m):
    mb = m.astype(BF16)
    hi, mid, lo = _split3(x)
    return (jnp.dot(hi, mb, preferred_element_type=F32) + jnp.dot(mid, mb, preferred_element_type=F32)
            + jnp.dot(lo, mb, preferred_element_type=F32))


def _segsum(x, seg):
    blk = 256
    ri = lax.broadcasted_iota(jnp.int32, (blk, blk), 0) // seg
    ci = lax.broadcasted_iota(jnp.int32, (blk, blk), 1) // seg
    ones = (ri == ci).astype(F32)
    parts = [_dot_exact_rhs(x[:, c:c + blk], ones) for c in range(0, x.shape[1], blk)]
    return parts[0] if len(parts) == 1 else jnp.concatenate(parts, axis=1)


def _rms_rows(x, g):
    return x * lax.rsqrt(jnp.mean(x * x, axis=-1, keepdims=True) + EPS) * g


def _rmsnorm_kernel(x_ref, g_ref, o_ref):
    o_ref[...] = _rms_rows(x_ref[...], g_ref[...]).astype(o_ref.dtype)


def _rmsnorm(x, g, tm=512, row0=0, total_rows=None, prev=None):
    m, d = x.shape
    rb0 = row0 // tm
    kern, prev_specs, prev_args, aliases = _carry_through(_rmsnorm_kernel, 2, [prev])
    return pl.pallas_call(
        kern,
        grid=(m // tm,),
        in_specs=[pl.BlockSpec((tm, d), lambda i: (i, 0)), pl.BlockSpec((1, d), lambda i: (0, 0))] + prev_specs,
        out_specs=pl.BlockSpec((tm, d), lambda i: (rb0 + i, 0)),
        out_shape=jax.ShapeDtypeStruct((total_rows or m, d), BF16),
        input_output_aliases=aliases,
        compiler_params=_cparams("parallel"),
        name="rmsnorm",
    )(x, g.reshape(1, d), *prev_args)


def _mm_kernel(x_ref, w_ref, o_ref):
    o_ref[...] = jnp.dot(x_ref[...], w_ref[0], preferred_element_type=F32)


def _mm(x, w, lyr, tm, tn):
    m, k = x.shape
    n = w.shape[2]
    return pl.pallas_call(
        _mm_kernel,
        grid=(n // tn, m // tm),
        in_specs=[pl.BlockSpec((tm, k), lambda j, i: (i, 0)), pl.BlockSpec((1, k, tn), lambda j, i: (lyr, 0, j))],
        out_specs=pl.BlockSpec((tm, tn), lambda j, i: (i, j)),
        out_shape=jax.ShapeDtypeStruct((m, n), F32),
        compiler_params=_cparams("parallel", "parallel"),
        name="matmul",
    )(x, w)


def _merge_kernel(y0_ref, y1_ref, y2_ref, w_ref, g0_ref, g1_ref, g2_ref, o_ref):
    acc = _sigmoid(g0_ref[...]) * jnp.dot(y0_ref[...], w_ref[0, 0], preferred_element_type=F32)
    acc += _sigmoid(g1_ref[...]) * jnp.dot(y1_ref[...], w_ref[0, 1], preferred_element_type=F32)
    acc += _sigmoid(g2_ref[...]) * jnp.dot(y2_ref[...], w_ref[0, 2], preferred_element_type=F32)
    o_ref[...] = acc.astype(o_ref.dtype)


def _merge(y_rw, y_ml, y_rt, w_br, lyr, z, gate_col0, tm=512, tn=1024):
    m, kw = y_rw.shape
    d = w_br.shape[3]
    gb = gate_col0 // tn
    gspec = lambda c: pl.BlockSpec((tm, tn), lambda j, i: (i, gb + c * (d // tn) + j))
    yspec = pl.BlockSpec((tm, kw), lambda j, i: (i, 0))
    return pl.pallas_call(
        _merge_kernel,
        grid=(d // tn, m // tm),
        in_specs=[yspec, yspec, yspec, pl.BlockSpec((1, 3, kw, tn), lambda j, i: (lyr, 0, 0, j)),
                  gspec(0), gspec(1), gspec(2)],
        out_specs=pl.BlockSpec((tm, tn), lambda j, i: (i, j)),
        out_shape=jax.ShapeDtypeStruct((m, d), BF16),
        compiler_params=_cparams("parallel", "parallel"),
        name="branch_merge",
    )(y_rw, y_ml, y_rt, w_br, z, z, z)


def _residual_epilogue(acc, x_ref, gpost_ref, gnext_ref, xo_ref, ho_ref):
    xn = x_ref[...] + _rms_rows(acc, gpost_ref[...])
    xo_ref[...] = xn
    ho_ref[...] = _rms_rows(xn, gnext_ref[...]).astype(ho_ref.dtype)


def _proj_res_kernel(a_ref, w_ref, x_ref, gpost_ref, gnext_ref, xo_ref, ho_ref):
    acc = jnp.dot(a_ref[...], w_ref[0], preferred_element_type=F32)
    _residual_epilogue(acc, x_ref, gpost_ref, gnext_ref, xo_ref, ho_ref)


def _proj_res(a, w, lyr, x, g_post, g_next, row0=0, prev=(None, None), tm=512):
    total, k = a.shape
    m, d = x.shape
    rb0 = row0 // tm
    at = lambda i: (rb0 + i, 0)
    fixed = lambda i: (0, 0)
    kern, prev_specs, prev_args, aliases = _carry_through(_proj_res_kernel, 5, list(prev))
    return pl.pallas_call(
        kern,
        grid=(m // tm,),
        in_specs=[pl.BlockSpec((tm, k), at), pl.BlockSpec((1, k, d), lambda i: (lyr, 0, 0)),
                  pl.BlockSpec((tm, d), lambda i: (i, 0)),
                  pl.BlockSpec((1, d), fixed), pl.BlockSpec((1, d), fixed)] + prev_specs,
        out_specs=[pl.BlockSpec((tm, d), at), pl.BlockSpec((tm, d), at)],
        out_shape=[jax.ShapeDtypeStruct((total, d), F32), jax.ShapeDtypeStruct((total, d), BF16)],
        input_output_aliases=aliases,
        compiler_params=_cparams("parallel"),
        name="out_proj_residual",
    )(a, w, x, g_post.reshape(1, d), g_next.reshape(1, d), *prev_args)


def _mlp_kernel(h_ref, w1_ref, w2_ref, x_ref, gpost_ref, gnext_ref, xo_ref, ho_ref, acc_ref):
    k = pl.program_id(1)

    @pl.when(k == 0)
    def _():
        acc_ref[...] = jnp.zeros_like(acc_ref)

    u = jnp.dot(h_ref[...], w1_ref[0], preferred_element_type=F32)
    u = jnp.square(jnp.maximum(u, 0.0)).astype(BF16)
    acc_ref[...] += jnp.dot(u, w2_ref[0], preferred_element_type=F32)

    @pl.when(k == pl.num_programs(1) - 1)
    def _():
        _residual_epilogue(acc_ref[...], x_ref, gpost_ref, gnext_ref, xo_ref, ho_ref)


def _mlp(h, w1, w2, lyr, x, g_post, g_next, row0=0, n_rows=None, tm=512, tf=1024):
    d = h.shape[1]
    m = n_rows or h.shape[0]
    f = w1.shape[2]
    rb0 = row0 // tm
    at = lambda i, k: (rb0 + i, 0)
    row = lambda i, k: (i, 0)
    fixed = lambda i, k: (0, 0)
    return pl.pallas_call(
        _mlp_kernel,
        grid=(m // tm, f // tf),
        in_specs=[pl.BlockSpec((tm, d), at), pl.BlockSpec((1, d, tf), lambda i, k: (lyr, 0, k)),
                  pl.BlockSpec((1, tf, d), lambda i, k: (lyr, k, 0)), pl.BlockSpec((tm, d), at),
                  pl.BlockSpec((1, d), fixed), pl.BlockSpec((1, d), fixed)],
        out_specs=[pl.BlockSpec((tm, d), row), pl.BlockSpec((tm, d), row)],
        out_shape=[jax.ShapeDtypeStruct((m, d), F32), jax.ShapeDtypeStruct((m, d), BF16)],
        scratch_shapes=[pltpu.VMEM((tm, d), F32)],
        compiler_params=_cparams("parallel", "arbitrary"),
        name="relu2_mlp",
    )(h, w1, w2, x, g_post.reshape(1, d), g_next.reshape(1, d))


def _attn_kernel(bb, lq, h_ref, wq_ref, mk_ref, mv_ref, wo_ref, x_ref, gpost_ref, gnext_ref, xo_ref, ho_ref):
    q = jnp.dot(h_ref[...], wq_ref[0], preferred_element_type=F32)
    outs = []
    for hd in range(X_HEADS):
        sl = slice(hd * X_HD, (hd + 1) * X_HD)
        q3 = q[:, sl].reshape(bb, lq, X_HD).astype(BF16)
        k3 = mk_ref[0, :, :, hd, :].astype(BF16)
        v3 = mv_ref[0, :, :, hd, :].astype(BF16)
        s = jnp.einsum('bqd,bmd->bqm', q3, k3, preferred_element_type=F32) * (X_HD ** -0.5)
        s = s - jnp.max(s, axis=-1, keepdims=True)
        e = jnp.exp(s)
        p = (e / jnp.sum(e, axis=-1, keepdims=True)).astype(BF16)
        o3 = jnp.einsum('bqm,bmd->bqd', p, v3, preferred_element_type=F32)
        outs.append(o3.reshape(bb * lq, X_HD))
    o = jnp.concatenate(outs, axis=1).astype(BF16)
    acc = jnp.dot(o, wo_ref[0], preferred_element_type=F32)
    _residual_epilogue(acc, x_ref, gpost_ref, gnext_ref, xo_ref, ho_ref)


def _attn(h, x, mk, mv, lyr, wq, wo, wl, g_post, g_next, row0, n_seq, seq_len, bb, lq):
    t, d = h.shape
    rows = bb * lq
    nl = seq_len // lq
    n_mem = mk.shape[2]
    rb0 = row0 // rows
    rowmap = lambda b, l: (rb0 + b * nl + l, 0)
    fixed = lambda b, l: (0, 0)
    kvspec = pl.BlockSpec((1, bb, n_mem, X_HEADS, X_HD), lambda b, l: (lyr, b, 0, 0, 0))
    return pl.pallas_call(
        functools.partial(_attn_kernel, bb, lq),
        grid=(n_seq // bb, nl),
        in_specs=[pl.BlockSpec((rows, d), rowmap), pl.BlockSpec((1, d, X_W), lambda b, l: (wl, 0, 0)), kvspec,
                  kvspec, pl.BlockSpec((1, X_W, d), lambda b, l: (wl, 0, 0)), pl.BlockSpec((rows, d), rowmap),
                  pl.BlockSpec((1, d), fixed), pl.BlockSpec((1, d), fixed)],
        out_specs=[pl.BlockSpec((rows, d), rowmap), pl.BlockSpec((rows, d), rowmap)],
        out_shape=[jax.ShapeDtypeStruct((t, d), F32), jax.ShapeDtypeStruct((t, d), BF16)],
        input_output_aliases={5: 0, 0: 1},
        compiler_params=_cparams("parallel", "parallel"),
        name="mem_cross_attention",
    )(h, wq, mk, mv, wo, x, g_post.reshape(1, d), g_next.reshape(1, d))


def _rwkv_prep_kernel(bb, lt, u_ref, s0_ref, mu_ref, w0_ref, wup_ref, a0_ref, aup_ref, gup_ref, kk_ref, ka_ref,
                      rk_ref, r_o, w_o, k_o, kk_o, b_o, v_o, g_o, bonus_o, carry_ref):
    l = pl.program_id(1)
    rows = bb * lt
    u = u_ref[:, :RW_COLS]
    if bb == 1:
        first = jnp.where(l == 0, s0_ref[0], carry_ref[...])
        first_rows = jnp.broadcast_to(first, (rows, RW_COLS))
    else:
        first_rows = jnp.broadcast_to(s0_ref[...], (bb, lt, RW_COLS)).reshape(rows, RW_COLS)
    rowid = lax.broadcasted_iota(jnp.int32, (rows, 1), 0) % lt
    prev = jnp.where(rowid == 0, first_rows, pltpu.roll(u, 1, 0))
    if bb == 1:
        carry_ref[...] = u[rows - 1:rows, :]
    z = u + (prev - u) * mu_ref[...]
    r = z[:, 0:MIX_W]
    k = z[:, MIX_W:2 * MIX_W]
    v = z[:, 2 * MIX_W:3 * MIX_W]
    o = 3 * MIX_W
    wd = z[:, o:o + RW_LORA[0]]
    ad = z[:, o + RW_LORA[0]:o + RW_LORA[0] + RW_LORA[1]]
    gd = z[:, o + RW_LORA[0] + RW_LORA[1]:RW_COLS]
    w_log = -_softplus(-(w0_ref[...] + _bdot(jnp.tanh(wd), wup_ref[...]))) - 0.5
    decay = jnp.exp(-jnp.exp(w_log))
    a = _sigmoid(a0_ref[...] + _bdot(ad, aup_ref[...]))
    g = _bdot(_sigmoid(gd), gup_ref[...])
    kk = k * kk_ref[...]
    kk = kk / jnp.maximum(jnp.sqrt(_segsum(kk * kk, RW_HD)), 1e-12)
    k = k * (1.0 + (a - 1.0) * ka_ref[...])
    r_o[...] = r
    w_o[...] = decay
    k_o[...] = k
    kk_o[...] = kk
    b_o[...] = kk * a
    v_o[...] = v
    g_o[...] = g
    bonus_o[...] = _segsum(r * k * rk_ref[...], RW_HD) * v


def _rwkv_prep(z, shift0, p, row0, n_seq, seq_len, bb, lt):
    rows = bb * lt
    nl = seq_len // lt
    assert bb == 1 or nl == 1
    rb0 = row0 // rows
    m = n_seq * seq_len
    fixed = lambda b, l: (0, 0)
    outmap = lambda b, l: (b * nl + l, 0)
    vec = lambda width: pl.BlockSpec((1, width), fixed)
    ospec = pl.BlockSpec((rows, MIX_W), outmap)
    oshape = jax.ShapeDtypeStruct((m, MIX_W), F32)
    return pl.pallas_call(
        functools.partial(_rwkv_prep_kernel, bb, lt),
        grid=(n_seq // bb, nl),
        in_specs=[pl.BlockSpec((rows, RW_COLS), lambda b, l: (rb0 + b * nl + l, 0)),
                  pl.BlockSpec((bb, 1, RW_COLS), lambda b, l: (b, 0, 0)),
                  vec(RW_COLS), vec(MIX_W), pl.BlockSpec((RW_LORA[0], MIX_W), fixed), vec(MIX_W),
                  pl.BlockSpec((RW_LORA[1], MIX_W), fixed), pl.BlockSpec((RW_LORA[2], MIX_W), fixed),
                  vec(MIX_W), vec(MIX_W), vec(MIX_W)],
        out_specs=[ospec] * 8,
        out_shape=[oshape] * 8,
        scratch_shapes=[pltpu.VMEM((1, RW_COLS), F32)],
        compiler_params=_cparams("parallel", "arbitrary"),
        name="rwkv_prep",
    )(z, shift0, p['mu'], p['w0'], p['w_up'], p['a0'], p['a_up'], p['g_up'], p['k_k'], p['k_a'], p['r_k'])


def _rwkv_scan_kernel(tc, r_ref, w_ref, k_ref, kk_ref, b_ref, v_ref, s0_ref, y_ref, s_ref):
    @pl.when(pl.program_id(1) == 0)
    def _():
        s_ref[...] = s0_ref[...]

    y_ref[0, :, RW_VP:, :] = jnp.zeros((tc, RW_VR - RW_VP, LANES), F32)
    keys = lambda ref, t: ref[0, t, :RW_HD, :]

    def step(t, carry):
        for vp in range(RW_VP):
            s = s_ref[0, vp]
            sa = jnp.sum(s * keys(kk_ref, t), axis=0, keepdims=True)
            sn = s * keys(w_ref, t) - sa * keys(b_ref, t) + v_ref[0, t, vp:vp + 1, :] * keys(k_ref, t)
            s_ref[0, vp] = sn
            y_ref[0, t, vp:vp + 1, :] = jnp.sum(sn * keys(r_ref, t), axis=0, keepdims=True)
        return carry

    lax.fori_loop(0, tc, step, 0)


def _rwkv_scan(r, w, k, kk, b, v, s0, lyr_in, tc, depth, lyr, s_prev):
    g, l = r.shape[0], r.shape[1]
    kspec = pl.BlockSpec((1, tc, RW_KR, LANES), lambda i, c: (i, c, 0, 0))
    vspec = pl.BlockSpec((1, tc, RW_VR, LANES), lambda i, c: (i, c, 0, 0))
    sspec = lambda at: pl.BlockSpec((1, RW_VP, RW_HD, LANES), lambda i, c: (at * g + i, 0, 0, 0))
    kern, prev_specs, prev_args, aliases = _carry_through(functools.partial(_rwkv_scan_kernel, tc), 7,
                                                          [None, s_prev])
    return pl.pallas_call(
        kern,
        grid=(g, l // tc),
        in_specs=[kspec] * 5 + [vspec, sspec(lyr_in)] + prev_specs,
        out_specs=[vspec, sspec(lyr)],
        out_shape=[jax.ShapeDtypeStruct((g, l, RW_VR, LANES), F32),
                   jax.ShapeDtypeStruct((depth * g, RW_VP, RW_HD, LANES), F32)],
        input_output_aliases=aliases,
        compiler_params=_cparams("parallel", "arbitrary"),
        name="rwkv_scan",
    )(r, w, k, kk, b, v, s0, *prev_args)


def _rwkv_post_kernel(y_ref, g_ref, bonus_ref, gng_ref, gnb_ref, o_ref):
    y = y_ref[...]
    mean = _segsum(y, RW_HD) * (1.0 / RW_HD)
    yc = y - mean
    var = _segsum(yc * yc, RW_HD) * (1.0 / RW_HD)
    yn = yc * lax.rsqrt(var + RW_GN_EPS) * gng_ref[...] + gnb_ref[...]
    o_ref[...] = ((yn + bonus_ref[...]) * g_ref[...]).astype(o_ref.dtype)


def _rwkv_post(y, g, bonus, gn_g, gn_b, row0, total_rows, prev, tm=256):
    m = y.shape[0]
    rb0 = row0 // tm
    fixed = lambda i: (0, 0)
    spec = pl.BlockSpec((tm, MIX_W), lambda i: (i, 0))
    kern, prev_specs, prev_args, aliases = _carry_through(_rwkv_post_kernel, 5, [prev])
    return pl.pallas_call(
        kern,
        grid=(m // tm,),
        in_specs=[spec, spec, spec, pl.BlockSpec((1, MIX_W), fixed), pl.BlockSpec((1, MIX_W), fixed)] + prev_specs,
        out_specs=pl.BlockSpec((tm, MIX_W), lambda i: (rb0 + i, 0)),
        out_shape=jax.ShapeDtypeStruct((total_rows, MIX_W), BF16),
        input_output_aliases=aliases,
        compiler_params=_cparams("parallel"),
        name="rwkv_post",
    )(y, g, bonus, gn_g, gn_b, *prev_args)


RW_NB = RW_CHAINS // RW_HEADS
RW_TT = LANES


RW_KR = RW_HD + 8
RW_VR = RW_VP + 8


def _natural_to_chain_rows(x_ref, a_ref):
    for bl in range(RW_NB):
        xt = x_ref[bl].T
        for h in range(RW_HEADS):
            c = bl * RW_HEADS + h
            a_ref[c * RW_KR:c * RW_KR + RW_HD, :] = xt[h * RW_HD:(h + 1) * RW_HD]


def _keys_to_scan_kernel(x_ref, o_ref, a_ref):
    _natural_to_chain_rows(x_ref, a_ref)
    o_ref[...] = jnp.zeros_like(o_ref)
    for k in range(RW_HD):
        tile = a_ref[pl.ds(k, RW_CHAINS, stride=RW_KR), :]
        o_ref[0, pl.ds(k, RW_TT, stride=RW_KR), :] = jnp.concatenate([tile, tile], axis=0).T


def _vals_to_scan_kernel(x_ref, o_ref, a_ref):
    _natural_to_chain_rows(x_ref, a_ref)
    o_ref[...] = jnp.zeros_like(o_ref)
    for vp in range(RW_VP):
        halves = [a_ref[pl.ds(vh * RW_VP + vp, RW_CHAINS, stride=RW_KR), :] for vh in range(2)]
        o_ref[0, pl.ds(vp, RW_TT, stride=RW_VR), :] = jnp.concatenate(halves, axis=0).T


def _vals_from_scan_kernel(y_ref, o_ref, a_ref):
    for vp in range(RW_VP):
        tile = y_ref[0, pl.ds(vp, RW_TT, stride=RW_VR), :].T
        for vh in range(2):
            a_ref[pl.ds(vh * RW_VP + vp, RW_CHAINS, stride=RW_KR), :] = tile[vh * RW_CHAINS:(vh + 1) * RW_CHAINS]
    for bl in range(RW_NB):
        rows = [a_ref[c * RW_KR:c * RW_KR + RW_HD, :] for c in range(bl * RW_HEADS, (bl + 1) * RW_HEADS)]
        o_ref[bl] = jnp.concatenate(rows, axis=0).T


def _scan_relayout(kernel_fn, x, n_seq, seq_len, rows_per_t, to_scan, name):
    g, nl = n_seq // RW_NB, seq_len // RW_TT
    nat_spec = pl.BlockSpec((RW_NB, RW_TT, MIX_W), lambda i, l: (i, l, 0))
    scan_spec = pl.BlockSpec((1, RW_TT * rows_per_t, LANES), lambda i, l: (i, l, 0))
    nat_shape = jax.ShapeDtypeStruct((n_seq, seq_len, MIX_W), F32)
    scan_shape = jax.ShapeDtypeStruct((g, seq_len * rows_per_t, LANES), F32)
    x = x.reshape(n_seq, seq_len, MIX_W) if to_scan else x.reshape(g, seq_len * rows_per_t, LANES)
    out = pl.pallas_call(
        kernel_fn,
        grid=(g, nl),
        in_specs=[nat_spec if to_scan else scan_spec],
        out_specs=scan_spec if to_scan else nat_spec,
        out_shape=scan_shape if to_scan else nat_shape,
        scratch_shapes=[pltpu.VMEM((RW_CHAINS * RW_KR, RW_TT), F32)],
        compiler_params=_cparams("parallel", "parallel"),
        name=name,
    )(x)
    return out.reshape(g, seq_len, rows_per_t, LANES) if to_scan else out.reshape(n_seq * seq_len, MIX_W)


def _pad_rows(a, rows):
    return jnp.pad(a, ((0, 0), (0, 0), (0, rows - a.shape[2]), (0, 0)))


def _to_scan_keys(a, n_seq, seq_len):
    if seq_len % RW_TT == 0:
        return _scan_relayout(_keys_to_scan_kernel, a, n_seq, seq_len, RW_KR, True, "rwkv_keys_to_scan")
    g = n_seq // RW_NB
    a = a.reshape(g, RW_NB, seq_len, RW_HEADS, RW_HD).transpose(0, 2, 4, 1, 3).reshape(g, seq_len, RW_HD, RW_CHAINS)
    return _pad_rows(jnp.concatenate([a, a], axis=-1), RW_KR)


def _to_scan_vals(a, n_seq, seq_len):
    if seq_len % RW_TT == 0:
        return _scan_relayout(_vals_to_scan_kernel, a, n_seq, seq_len, RW_VR, True, "rwkv_vals_to_scan")
    g = n_seq // RW_NB
    a = a.reshape(g, RW_NB, seq_len, RW_HEADS, 2, RW_VP).transpose(0, 2, 5, 4, 1, 3)
    return _pad_rows(a.reshape(g, seq_len, RW_VP, LANES), RW_VR)


def _from_scan_vals(y, n_seq, seq_len):
    if seq_len % RW_TT == 0:
        return _scan_relayout(_vals_from_scan_kernel, y, n_seq, seq_len, RW_VR, False, "rwkv_vals_from_scan")
    g = n_seq // RW_NB
    y = y[:, :, :RW_VP].reshape(g, seq_len, RW_VP, 2, RW_NB, RW_HEADS).transpose(0, 4, 1, 5, 3, 2)
    return y.reshape(n_seq * seq_len, MIX_W)


def _state_to_scan(s, n_seq):
    nb = RW_CHAINS // RW_HEADS
    g = n_seq // nb
    s = s.reshape(g, nb, RW_HEADS, 2, RW_VP, RW_HD).transpose(0, 4, 5, 3, 1, 2)
    return s.reshape(g, RW_VP, RW_HD, LANES)


def _state_from_scan(s, n_seq):
    nb = RW_CHAINS // RW_HEADS
    g = n_seq // nb
    s = s.reshape(g, RW_VP, RW_HD, 2, nb, RW_HEADS).transpose(0, 4, 5, 3, 1, 2)
    return s.reshape(n_seq, RW_HEADS, RW_HD, RW_HD)


def _rwkv_branch(z, shift0, s0, lyr_in, p, row0, n_seq, seq_len, bb, lt, tc, depth, lyr, prev):
    r, w, k, kk, b, v, g, bonus = _rwkv_prep(z, shift0, p, row0, n_seq, seq_len, bb, lt)
    keys = [_to_scan_keys(a, n_seq, seq_len) for a in (r, w, k, kk, b)]
    y, s_new = _rwkv_scan(*keys, _to_scan_vals(v, n_seq, seq_len), s0, lyr_in, tc, depth, lyr, prev[1])
    y = _from_scan_vals(y, n_seq, seq_len)
    return _rwkv_post(y, g, bonus, p['gn_g'], p['gn_b'], row0, z.shape[0], prev[0]), s_new


def _seq_masks(rows, lc):
    ri = lax.broadcasted_iota(jnp.int32, (rows, rows), 0)
    ci = lax.broadcasted_iota(jnp.int32, (rows, rows), 1)
    same = (ri // lc) == (ci // lc)
    return ri, ci, same


def _rows_from_seq(x, bb, lc):
    return jnp.broadcast_to(x, (bb, lc, x.shape[-1])).reshape(bb * lc, x.shape[-1])


def _last_of_seq(x, bb, lc):
    x3 = x.reshape(bb, lc, x.shape[-1])
    pick = (lax.broadcasted_iota(jnp.int32, (1, lc, 1), 1) == lc - 1).astype(F32)
    return jnp.sum(x3 * pick, axis=1, keepdims=True)


def _transpose_rows(x):
    rows = x.shape[0]
    if rows < LANES:
        x = jnp.concatenate([x, jnp.zeros((LANES - rows, LANES), x.dtype)], axis=0)
    return x.T[:, :rows]


def _mlstm_kernel(bb, lc, q_ref, k_ref, v_ref, o_ref, gt_ref, gb_ref, ng_ref, c0_ref, n0_ref, m0_ref,
                  y_ref, c_ref, n_ref, m_ref):
    rows = bb * lc

    @pl.when(pl.program_id(1) == 0)
    def _():
        c_ref[...] = c0_ref[...]
        n_ref[...] = n0_ref[...]
        m_ref[...] = m0_ref[...]

    gt = ML_GATE_CAP * jnp.tanh((gt_ref[...] + gb_ref[...]) * (1.0 / ML_GATE_CAP))
    lane = lax.broadcasted_iota(jnp.int32, (rows, LANES), 1)
    ri, ci, same = _seq_masks(rows, lc)
    causal = same & (ci <= ri)
    cum = _dot_exact_lhs(causal.astype(F32), -_softplus(-gt))
    gt_t = _transpose_rows(gt)
    cum_t = _transpose_rows(cum)
    for hd in range(ML_HEADS):
        sl = slice(hd * ML_HD, (hd + 1) * ML_HD)
        c0 = c_ref[0, :, hd]
        n0 = n_ref[0, :, hd:hd + 1, :]
        m0 = m_ref[0, :, :, hd:hd + 1]
        q = q_ref[:, sl]
        k = k_ref[:, sl] * (ML_HD ** -0.5)
        v = v_ref[:, sl]
        bcol = cum[:, ML_HEADS + hd:ML_HEADS + hd + 1]
        igcol = gt[:, hd:hd + 1]
        rowb = gt_t[hd:hd + 1, :] - cum_t[ML_HEADS + hd:ML_HEADS + hd + 1, :]
        dlog = jnp.where(causal, bcol + rowb, -jnp.inf)
        m_inter = bcol + _rows_from_seq(m0, bb, lc)
        m_t = jnp.maximum(m_inter, jnp.max(dlog, axis=-1, keepdims=True))
        qk = lax.dot_general(q.astype(BF16), k.astype(BF16), (((1,), (1,)), ((), ())),
                             preferred_element_type=F32)
        wts = jnp.exp(dlog - m_t) * qk
        s_inter = jnp.exp(m_inter - m_t)
        q3 = q.reshape(bb, lc, ML_HD).astype(BF16)
        qc = jnp.einsum('btd,bde->bte', q3, c0.astype(BF16), preferred_element_type=F32).reshape(rows, ML_HD)
        num = _bdot(wts, v) + s_inter * qc
        qn = jnp.sum(q * _rows_from_seq(n0, bb, lc), axis=-1, keepdims=True)
        den = jnp.sum(wts, axis=-1, keepdims=True) + s_inter * qn
        hh = num / jnp.maximum(jnp.abs(den), jnp.exp(-m_t))
        hh = hh * lax.rsqrt(jnp.mean(hh * hh, axis=-1, keepdims=True) + EPS) * ng_ref[:, sl]
        y_ref[:, sl] = (_sigmoid(o_ref[:, sl]) * hh).astype(y_ref.dtype)

        stat = jnp.where(lane == 0, bcol, jnp.where(lane == 1, m_t, 0.0))
        last = _last_of_seq(stat, bb, lc)
        last_rows = _rows_from_seq(last, bb, lc)
        b_last, m_new = last_rows[:, 0:1], last_rows[:, 1:2]
        w_end = jnp.exp(b_last - bcol + igcol - m_new)
        f_end = jnp.exp(last[:, :, 0:1] + m0 - last[:, :, 1:2])
        kw3 = (k * w_end).reshape(bb, lc, ML_HD)
        upd = jnp.einsum('bsd,bse->bde', kw3.astype(BF16), v.reshape(bb, lc, ML_HD).astype(BF16),
                         preferred_element_type=F32)
        c_ref[0, :, hd] = f_end * c0 + upd
        n_ref[0, :, hd:hd + 1, :] = f_end * n0 + jnp.sum(kw3, axis=1, keepdims=True)
        m_ref[0, :, :, hd:hd + 1] = last[:, :, 1:2]


def _mlstm_seq_kernel(lc, q_ref, k_ref, v_ref, o_ref, gt_ref, gb_ref, ng_ref, c0_ref, n0_ref, m0_ref,
                      y_ref, c_ref, n_ref, m_ref):
    heads = range(ML_HEADS)

    @pl.when(pl.program_id(1) == 0)
    def _():
        c_ref[...] = c0_ref[...]
        n_ref[...] = n0_ref[...]
        m_ref[...] = m0_ref[...]

    gt = ML_GATE_CAP * jnp.tanh((gt_ref[...] + gb_ref[...]) * (1.0 / ML_GATE_CAP))
    ri, ci, _ = _seq_masks(lc, lc)
    causal = ci <= ri
    cum = _dot_exact_lhs(causal.astype(F32), -_softplus(-gt))
    gt_t = _transpose_rows(gt)
    cum_t = _transpose_rows(cum)
    sl = lambda hd: slice(hd * ML_HD, (hd + 1) * ML_HD)
    stack = lambda f: jnp.stack([f(hd) for hd in heads])
    q = stack(lambda hd: q_ref[:, sl(hd)])
    k = stack(lambda hd: k_ref[:, sl(hd)]) * (ML_HD ** -0.5)
    v = stack(lambda hd: v_ref[:, sl(hd)])
    bcol = stack(lambda hd: cum[:, ML_HEADS + hd:ML_HEADS + hd + 1])
    igcol = stack(lambda hd: gt[:, hd:hd + 1])
    rowb = stack(lambda hd: gt_t[hd:hd + 1, :] - cum_t[ML_HEADS + hd:ML_HEADS + hd + 1, :])
    c0 = c_ref[0, 0]
    n0 = stack(lambda hd: n_ref[0, 0, hd:hd + 1, :])
    m0 = stack(lambda hd: m_ref[0, 0, :, hd:hd + 1])
    qb, kb, vb = q.astype(BF16), k.astype(BF16), v.astype(BF16)

    dlog = jnp.where(causal[None], bcol + rowb, -jnp.inf)
    m_inter = bcol + m0
    m_t = jnp.maximum(m_inter, jnp.max(dlog, axis=-1, keepdims=True))
    qk = jnp.einsum('hqd,hkd->hqk', qb, kb, preferred_element_type=F32)
    wts = jnp.exp(dlog - m_t) * qk
    s_inter = jnp.exp(m_inter - m_t)
    qc = jnp.einsum('hqd,hde->hqe', qb, c0.astype(BF16), preferred_element_type=F32)
    num = jnp.einsum('hqk,hke->hqe', wts.astype(BF16), vb, preferred_element_type=F32) + s_inter * qc
    den = jnp.sum(wts, axis=-1, keepdims=True) + s_inter * jnp.sum(q * n0, axis=-1, keepdims=True)
    hh = num / jnp.maximum(jnp.abs(den), jnp.exp(-m_t))
    hh = hh * lax.rsqrt(jnp.mean(hh * hh, axis=-1, keepdims=True) + EPS)
    for hd in heads:
        y_ref[:, sl(hd)] = (_sigmoid(o_ref[:, sl(hd)]) * hh[hd] * ng_ref[:, sl(hd)]).astype(y_ref.dtype)

    b_last, m_new = bcol[:, lc - 1:lc, :], m_t[:, lc - 1:lc, :]
    w_end = jnp.exp(b_last - bcol + igcol - m_new)
    f_end = jnp.exp(b_last + m0 - m_new)
    kw = k * w_end
    upd = jnp.einsum('hsd,hse->hde', kw.astype(BF16), vb, preferred_element_type=F32)
    c_ref[0, 0] = f_end * c0 + upd
    n_new = f_end * n0 + jnp.sum(kw, axis=1, keepdims=True)
    for hd in heads:
        n_ref[0, 0, hd:hd + 1, :] = n_new[hd]
        m_ref[0, 0, :, hd:hd + 1] = m_new[hd]


def _mlstm_branch(z, z_gates, init, lyr_in, gate_bias, norm_g, row0, n_seq, seq_len, bb, lc, depth, lyr, prev):
    rows = bb * lc
    nc = seq_len // lc
    rb0 = row0 // rows
    rowmap = lambda part: (lambda b, c: (rb0 + b * nc + c, part))
    zspec = lambda part: pl.BlockSpec((rows, MIX_W), rowmap(part))
    state_specs = lambda at: [pl.BlockSpec((1, bb, ML_HEADS, ML_HD, ML_HD), lambda b, c: (at, b, 0, 0, 0)),
                              pl.BlockSpec((1, bb, ML_HEADS, ML_HD), lambda b, c: (at, b, 0, 0)),
                              pl.BlockSpec((1, bb, 1, ML_HEADS), lambda b, c: (at, b, 0, 0))]
    body = functools.partial(_mlstm_seq_kernel, lc) if bb == 1 else functools.partial(_mlstm_kernel, bb, lc)
    kern, prev_specs, prev_args, aliases = _carry_through(body, 10, prev)
    return pl.pallas_call(
        kern,
        grid=(n_seq // bb, nc),
        in_specs=[zspec(0), zspec(1), zspec(2), zspec(3),
                  pl.BlockSpec((rows, LANES), lambda b, c: (rb0 + b * nc + c, 0)),
                  pl.BlockSpec((1, LANES), lambda b, c: (0, 0)), pl.BlockSpec((1, MIX_W), lambda b, c: (0, 0))]
                 + state_specs(lyr_in) + prev_specs,
        out_specs=[pl.BlockSpec((rows, MIX_W), lambda b, c: (rb0 + b * nc + c, 0))] + state_specs(lyr),
        out_shape=[jax.ShapeDtypeStruct((z.shape[0], MIX_W), BF16),
                   jax.ShapeDtypeStruct((depth, n_seq, ML_HEADS, ML_HD, ML_HD), F32),
                   jax.ShapeDtypeStruct((depth, n_seq, ML_HEADS, ML_HD), F32),
                   jax.ShapeDtypeStruct((depth, n_seq, 1, ML_HEADS), F32)],
        input_output_aliases=aliases,
        compiler_params=_cparams("parallel", "arbitrary"),
        name="mlstm_chunk",
    )(z, z, z, z, z_gates, gate_bias, norm_g, *init, *prev_args)


def _rope_rows(x, cos, sin):
    half = RT_HD // 2
    x1, x2 = x[:, :half], x[:, half:]
    return jnp.concatenate([x1 * cos - x2 * sin, x1 * sin + x2 * cos], axis=1)


def _ret_kernel(bb, lc, q_ref, k_ref, v_ref, g_ref, cs_ref, dec_ref, cd_ref, s0_ref, y_ref, s_ref):
    rows = bb * lc

    @pl.when(pl.program_id(1) == 0)
    def _():
        s_ref[...] = s0_ref[...]

    cos, sin = cs_ref[:, :RT_HD // 2], cs_ref[:, RT_HD // 2:]
    for hd in range(RT_HEADS):
        sl = slice(hd * RT_HD, (hd + 1) * RT_HD)
        s0 = s_ref[0, :, hd]
        q = _rope_rows(q_ref[:, sl], cos, sin)
        k = _rope_rows(k_ref[:, sl], cos, sin) * (RT_HD ** -0.5)
        v = v_ref[:, sl]
        cd = cd_ref[hd]
        qk = lax.dot_general(q.astype(BF16), k.astype(BF16), (((1,), (1,)), ((), ())),
                             preferred_element_type=F32)
        inner = _bdot(qk * dec_ref[hd], v)
        q3 = q.reshape(bb, lc, RT_HD).astype(BF16)
        cross = jnp.einsum('btd,bde->bte', q3, s0.astype(BF16), preferred_element_type=F32).reshape(rows, RT_HD)
        y = inner + cross * cd[:, 0:1]
        y = y * lax.rsqrt(jnp.mean(y * y, axis=-1, keepdims=True) + EPS)
        gt = g_ref[:, sl]
        y_ref[:, sl] = (gt * _sigmoid(gt) * y).astype(y_ref.dtype)
        kd = (k * cd[:, 1:2]).reshape(bb, lc, RT_HD).astype(BF16)
        upd = jnp.einsum('bsd,bse->bde', kd, v.reshape(bb, lc, RT_HD).astype(BF16), preferred_element_type=F32)
        s_ref[0, :, hd] = cd[0:1, 2:3] * s0 + upd


def _ret_seq_kernel(lc, q_ref, k_ref, v_ref, g_ref, cs_ref, dec_ref, cd_ref, s0_ref, y_ref, s_ref):
    heads = range(RT_HEADS)

    @pl.when(pl.program_id(1) == 0)
    def _():
        s_ref[...] = s0_ref[...]

    cos, sin = cs_ref[:, :RT_HD // 2], cs_ref[:, RT_HD // 2:]
    sl = lambda hd: slice(hd * RT_HD, (hd + 1) * RT_HD)
    stack = lambda f: jnp.stack([f(hd) for hd in heads])
    q = stack(lambda hd: _rope_rows(q_ref[:, sl(hd)], cos, sin))
    k = stack(lambda hd: _rope_rows(k_ref[:, sl(hd)], cos, sin)) * (RT_HD ** -0.5)
    vb = stack(lambda hd: v_ref[:, sl(hd)]).astype(BF16)
    s0 = s_ref[0, 0]
    cd = cd_ref[...]
    qb = q.astype(BF16)
    qk = jnp.einsum('hqd,hkd->hqk', qb, k.astype(BF16), preferred_element_type=F32)
    inner = jnp.einsum('hqk,hke->hqe', (qk * dec_ref[...]).astype(BF16), vb, preferred_element_type=F32)
    cross = jnp.einsum('hqd,hde->hqe', qb, s0.astype(BF16), preferred_element_type=F32)
    y = inner + cross * cd[:, :, 0:1]
    y = y * lax.rsqrt(jnp.mean(y * y, axis=-1, keepdims=True) + EPS)
    for hd in heads:
        gt = g_ref[:, sl(hd)]
        y_ref[:, sl(hd)] = (gt * _sigmoid(gt) * y[hd]).astype(y_ref.dtype)
    kd = (k * cd[:, :, 1:2]).astype(BF16)
    upd = jnp.einsum('hsd,hse->hde', kd, vb, preferred_element_type=F32)
    s_ref[0, 0] = cd[:, 0:1, 2:3] * s0 + upd


def _ret_branch(z, s0, lyr_in, row0, n_seq, seq_len, bb, lc, pos0, depth, lyr, prev):
    rows = bb * lc
    nc = seq_len // lc
    rb0 = row0 // rows
    half = RT_HD // 2
    inv = ROPE_BASE ** (-jnp.arange(half, dtype=F32) / half)
    pos = jnp.broadcast_to((jnp.arange(nc * lc, dtype=F32) + float(pos0)).reshape(nc, 1, lc), (nc, bb, lc))
    ang = pos.reshape(nc * rows)[:, None] * inv[None, :]
    cs = jnp.concatenate([jnp.cos(ang), jnp.sin(ang)], axis=1)
    log_g = jnp.log(1.0 - jnp.exp(jnp.linspace(math.log(1.0 / 32), math.log(1.0 / 512), RT_HEADS)))
    idx = jnp.arange(rows)
    t = (idx % lc).astype(F32)
    diff = t[:, None] - t[None, :]
    same = (idx[:, None] // lc) == (idx[None, :] // lc)
    dec = jnp.where(same & (diff >= 0), jnp.exp(log_g[:, None, None] * jnp.maximum(diff, 0.0)), 0.0)
    cd = jnp.zeros((RT_HEADS, rows, LANES), F32)
    cd = cd.at[:, :, 0].set(jnp.exp(log_g[:, None] * (t[None, :] + 1.0)))
    cd = cd.at[:, :, 1].set(jnp.exp(log_g[:, None] * (lc - 1.0 - t)[None, :]))
    cd = cd.at[:, :, 2].set(jnp.broadcast_to(jnp.exp(log_g * lc)[:, None], (RT_HEADS, rows)))
    rowmap = lambda part: (lambda b, c: (rb0 + b * nc + c, part))
    zspec = lambda part: pl.BlockSpec((rows, MIX_W), rowmap(part))
    sspec = lambda at: pl.BlockSpec((1, bb, RT_HEADS, RT_HD, RT_HD), lambda b, c: (at, b, 0, 0, 0))
    body = functools.partial(_ret_seq_kernel, lc) if bb == 1 else functools.partial(_ret_kernel, bb, lc)
    kern, prev_specs, prev_args, aliases = _carry_through(body, 8, prev)
    return pl.pallas_call(
        kern,
        grid=(n_seq // bb, nc),
        in_specs=[zspec(0), zspec(1), zspec(2), zspec(3),
                  pl.BlockSpec((rows, RT_HD), lambda b, c: (c, 0)),
                  pl.BlockSpec((RT_HEADS, rows, rows), lambda b, c: (0, 0, 0)),
                  pl.BlockSpec((RT_HEADS, rows, LANES), lambda b, c: (0, 0, 0)),
                  sspec(lyr_in)] + prev_specs,
        out_specs=[pl.BlockSpec((rows, MIX_W), lambda b, c: (rb0 + b * nc + c, 0)), sspec(lyr)],
        out_shape=[jax.ShapeDtypeStruct((z.shape[0], MIX_W), BF16),
                   jax.ShapeDtypeStruct((depth, n_seq, RT_HEADS, RT_HD, RT_HD), F32)],
        input_output_aliases=aliases,
        compiler_params=_cparams("parallel", "arbitrary"),
        name="retention_chunk",
    )(z, z, z, z, cs, dec, cd, s0, *prev_args)


def _mm_wt_kernel(x_ref, wt_ref, o_ref, wb_ref):
    @pl.when(pl.program_id(1) == 0)
    def _():
        tn = wb_ref.shape[1]
        for c in range(0, tn, LANES):
            wb_ref[:, c:c + LANES] = wt_ref[0, c:c + LANES, :].T.astype(wb_ref.dtype)

    o_ref[...] = jnp.dot(x_ref[...], wb_ref[...], preferred_element_type=F32)


def _mm_wt(x, w_t, lyr, col0, n, tm, tn):
    m, k = x.shape
    return pl.pallas_call(
        _mm_wt_kernel,
        grid=(n // tn, m // tm),
        in_specs=[pl.BlockSpec((tm, k), lambda j, i: (i, 0)),
                  pl.BlockSpec((pl.Element(1), pl.Element(tn), pl.Element(k)),
                               lambda j, i: (lyr, pl.multiple_of(col0 + j * tn, 8), 0))],
        out_specs=pl.BlockSpec((tm, tn), lambda j, i: (i, j)),
        out_shape=jax.ShapeDtypeStruct((m, n), F32),
        scratch_shapes=[pltpu.VMEM((k, tn), BF16)],
        compiler_params=_cparams("parallel", "arbitrary"),
        name="matmul_wt",
    )(x, w_t)


def _group_cfg(n_seq, seq_len):
    if seq_len % CHUNK == 0:
        return dict(rw_bb=1, rw_lt=256, rw_tc=32, ml_bb=1, rt_bb=1, ch_lc=CHUNK, at_bb=1, at_lq=512)
    return dict(rw_bb=32, rw_lt=seq_len, rw_tc=seq_len, ml_bb=8, rt_bb=8, ch_lc=seq_len, at_bb=8, at_lq=seq_len)


def kernel(x_prompt, x_sample, mem_prompt, state_rwkv_shift, state_rwkv, state_mlstm_c, state_mlstm_n,
           state_mlstm_m, state_ret, cache_mem_k, cache_mem_v, g_pre_mix, g_post_mix, g_pre_x, g_post_x,
           g_pre_ff, g_post_ff, g_mem, w_in, rw_mu, rw_w0, rw_w_up, rw_a0, rw_a_up, rw_g_up, rw_k_k, rw_k_a,
           rw_r_k, rw_gn_g, rw_gn_b, ml_i_b, ml_f_b, ml_norm_g, w_br, w_out, x_wq, x_wkv, x_wo, ff_w1, ff_w2):
    bp, lp, d = x_prompt.shape
    bs, ls, _ = x_sample.shape
    depth = w_in.shape[0]
    n_mem = mem_prompt.shape[1]
    tp, ts = bp * lp, bs * ls
    groups = [(0, bp, lp, 0), (tp, bs, ls, PAST_LEN)]

    x_groups = [x_prompt.reshape(tp, d), x_sample.reshape(ts, d)]
    mem = mem_prompt.reshape(bp * n_mem, d)
    x = h = None
    for (row0, *_), xg in zip(groups, x_groups):
        h = _rmsnorm(xg, g_pre_mix[0], row0=row0, total_rows=tp + ts, prev=h)

    zeros = lambda *s: jnp.zeros(s, F32)
    rw_scan0 = _state_to_scan(state_rwkv.reshape(depth * bs, RW_HEADS, RW_HD, RW_HD), depth * bs)
    inits = [
        lambda l: (zeros(bp, 1, RW_COLS), zeros(bp // RW_NB, RW_VP, RW_HD, LANES),
                   (zeros(1, bp, ML_HEADS, ML_HD, ML_HD), zeros(1, bp, ML_HEADS, ML_HD), zeros(1, bp, 1, ML_HEADS)),
                   zeros(1, bp, RT_HEADS, RT_HD, RT_HD), 0),
        lambda l: (state_rwkv_shift[l].reshape(bs, 1, RW_COLS), rw_scan0,
                   (state_mlstm_c, state_mlstm_n, state_mlstm_m.reshape(depth, bs, 1, ML_HEADS)), state_ret, l),
    ]
    shifts = [[], []]
    rw_states = [None, None]
    ml_states = [[None] * 3, [None] * 3]
    rt_states = [None, None]
    mem_k, mem_v = [], []
    w_in_t = jnp.swapaxes(w_in, 1, 2)
    w_kv, w_brb, w_outb, w_q, w_o, w_f1, w_f2 = (w.astype(BF16) for w in (x_wkv, w_br, w_out, x_wq, x_wo,
                                                                         ff_w1, ff_w2))
    for l in range(depth):
        kv = _mm(_rmsnorm(mem, g_mem[l], tm=256), w_kv, l, tm=256, tn=512)
        mem_k.append(kv[:, :X_W].reshape(bp, n_mem, X_HEADS, X_HD))
        mem_v.append(kv[:, X_W:].reshape(bp, n_mem, X_HEADS, X_HD))

        z_rw = _mm_wt(h, w_in_t, l, 0, RW_COLS, tm=512, tn=RW_COLS // 2)
        z_ml = _mm_wt(h, w_in_t, l, IN_ML, 4 * MIX_W, tm=1024, tn=1024)
        z_g = _mm_wt(h, w_in_t, l, IN_MLG, LANES, tm=1024, tn=LANES)
        z_rg = _mm_wt(h, w_in_t, l, IN_RT, w_in.shape[2] - IN_RT, tm=1024, tn=1280)

        rw_p = dict(mu=rw_mu[l].reshape(1, -1), w0=rw_w0[l].reshape(1, -1), w_up=rw_w_up[l].astype(BF16),
                    a0=rw_a0[l].reshape(1, -1), a_up=rw_a_up[l].astype(BF16), g_up=rw_g_up[l].astype(BF16),
                    k_k=rw_k_k[l].reshape(1, -1), k_a=rw_k_a[l].reshape(1, -1), r_k=rw_r_k[l].reshape(1, -1),
                    gn_g=rw_gn_g[l].reshape(1, -1), gn_b=rw_gn_b[l].reshape(1, -1))
        gate_bias = jnp.concatenate([ml_i_b[l], ml_f_b[l], zeros(LANES - 2 * ML_HEADS)]).reshape(1, LANES)
        norm_g = ml_norm_g[l].reshape(1, MIX_W)

        y_rw = y_ml = y_rt = None
        for gi, (row0, n_seq, seq_len, pos0) in enumerate(groups):
            cfg = _group_cfg(n_seq, seq_len)
            shift0, s_rw0, ml0, rt0, lyr_in = inits[gi](l)
            y_rw, rw_states[gi] = _rwkv_branch(z_rw, shift0, s_rw0, lyr_in, rw_p, row0, n_seq, seq_len,
                                               cfg['rw_bb'], cfg['rw_lt'], cfg['rw_tc'], depth, l,
                                               (y_rw, rw_states[gi]))
            y_ml, *ml_states[gi] = _mlstm_branch(z_ml, z_g, ml0, lyr_in, gate_bias, norm_g, row0, n_seq, seq_len,
                                                 cfg['ml_bb'], cfg['ch_lc'], depth, l, [y_ml] + ml_states[gi])
            y_rt, rt_states[gi] = _ret_branch(z_rg, rt0, lyr_in, row0, n_seq, seq_len, cfg['rt_bb'],
                                              cfg['ch_lc'], pos0, depth, l, [y_rt, rt_states[gi]])
            last = row0 + seq_len - 1
            shifts[gi].append(lax.slice(z_rw, (last, 0), (last + (n_seq - 1) * seq_len + 1, RW_COLS),
                                        (seq_len, 1)))

        merged = _merge(y_rw, y_ml, y_rt, w_brb, l, z_rg, Z_GATE)
        if x is None:
            xh = (None, None)
            for (row0, *_), xg in zip(groups, x_groups):
                xh = _proj_res(merged, w_outb, l, xg, g_post_mix[l], g_pre_x[l], row0=row0, prev=xh)
            x, h = xh
        else:
            x, h = _proj_res(merged, w_outb, l, x, g_post_mix[l], g_pre_x[l])

        for gi, (row0, n_seq, seq_len, pos0) in enumerate(groups):
            cfg = _group_cfg(n_seq, seq_len)
            if gi == 0:
                mk, mv, lyr_in = mem_k[l][None], mem_v[l][None], 0
            else:
                mk, mv, lyr_in = cache_mem_k, cache_mem_v, l
            x, h = _attn(h, x, mk, mv, lyr_in, w_q, w_o, l, g_post_x[l], g_pre_ff[l], row0, n_seq, seq_len,
                         cfg['at_bb'], cfg['at_lq'])

        if l + 1 < depth:
            x, h = _mlp(h, w_f1, w_f2, l, x, g_post_ff[l], g_pre_mix[l + 1])
        else:
            y_out = [_mlp(h, w_f1, w_f2, l, x, g_post_ff[l], g_pre_mix[0], row0=row0, n_rows=n_seq * seq_len)[0]
                     for row0, n_seq, seq_len, _ in groups]

    def states(gi, n_seq):
        c_new, n_new, m_new = ml_states[gi]
        s_rw = _state_from_scan(rw_states[gi], depth * n_seq).reshape(depth, n_seq, RW_HEADS, RW_HD, RW_HD)
        return (jnp.stack(shifts[gi]), s_rw, c_new, n_new, m_new.reshape(depth, n_seq, ML_HEADS), rt_states[gi])

    return (y_out[0].reshape(bp, lp, d), y_out[1].reshape(bs, ls, d), *states(0, bp),
            jnp.stack(mem_k), jnp.stack(mem_v), *states(1, bs))
```

```python
import functools
import math

import jax
import jax.numpy as jnp
from jax import lax
from jax.experimental import pallas as pl
from jax.experimental.pallas import tpu as pltpu

F32 = jnp.float32
BF16 = jnp.bfloat16

D_MODEL = 2048
MIX_W = D_MODEL // 2
RW_HD = 64
RW_HEADS = MIX_W // RW_HD
RW_LORA = (64, 64, 128)
RW_COLS = 3 * MIX_W + sum(RW_LORA)
RW_GN_EPS = 64e-5
ML_HEADS = 8
ML_HD = MIX_W // ML_HEADS
ML_GATE_CAP = 15.0
RT_HEADS = 4
RT_HD = MIX_W // RT_HEADS
ROPE_BASE = 10000.0
X_HEADS = 4
X_HD = 128
X_W = X_HEADS * X_HD
D_FF = 4 * D_MODEL
CHUNK = 64
EPS = 1e-6
PAST_LEN = 16384

IN_ML = RW_COLS
IN_MLG = IN_ML + 4 * MIX_W
IN_RT = IN_MLG + 2 * ML_HEADS
Z_GATE = 4 * MIX_W

LANES = 128
RW_VP = RW_HD // 2
RW_CHAINS = LANES // 2


def _cparams(*sem):
    return pltpu.CompilerParams(dimension_semantics=sem, vmem_limit_bytes=56 * 1024 * 1024)


def _carry_through(kernel_fn, n_in, prev):
    carried = [(i, p) for i, p in enumerate(prev) if p is not None]
    if not carried:
        return kernel_fn, [], [], {}

    def wrapped(*refs):
        return kernel_fn(*refs[:n_in], *refs[n_in + len(carried):])

    return (wrapped, [pl.BlockSpec(memory_space=pl.ANY)] * len(carried), [p for _, p in carried],
            {n_in + j: i for j, (i, _) in enumerate(carried)})


def _sigmoid(x):
    return 1.0 / (1.0 + jnp.exp(-x))


def _softplus(x):
    return jnp.maximum(x, 0.0) + jnp.log(1.0 + jnp.exp(-jnp.abs(x)))


def _bdot(a, b):
    return jnp.dot(a.astype(BF16), b.astype(BF16), preferred_element_type=F32)


def _split3(x):
    hi = x.astype(BF16)
    r1 = x - hi.astype(F32)
    mid = r1.astype(BF16)
    lo = (r1 - mid.astype(F32)).astype(BF16)
    return hi, mid, lo


def _dot_exact_lhs(m, x):
    mb = m.astype(BF16)
    hi, mid, lo = _split3(x)
    return (jnp.dot(mb, hi, preferred_element_type=F32) + jnp.dot(mb, mid, preferred_element_type=F32)
            + jnp.dot(mb, lo, preferred_element_type=F32))


def _dot_exact_rhs(x, m):
    mb = m.astype(BF16)
    hi, mid, lo = _split3(x)
    return (jnp.dot(hi, mb, preferred_element_type=F32) + jnp.dot(mid, mb, preferred_element_type=F32)
            + jnp.dot(lo, mb, preferred_element_type=F32))


def _segsum(x, seg):
    blk = 256
    ri = lax.broadcasted_iota(jnp.int32, (blk, blk), 0) // seg
    ci = lax.broadcasted_iota(jnp.int32, (blk, blk), 1) // seg
    ones = (ri == ci).astype(F32)
    parts = [_dot_exact_rhs(x[:, c:c + blk], ones) for c in range(0, x.shape[1], blk)]
    return parts[0] if len(parts) == 1 else jnp.concatenate(parts, axis=1)


def _rms_rows(x, g):
    return x * lax.rsqrt(jnp.mean(x * x, axis=-1, keepdims=True) + EPS) * g


def _rmsnorm_kernel(x_ref, g_ref, o_ref):
    o_ref[...] = _rms_rows(x_ref[...], g_ref[...]).astype(o_ref.dtype)


def _rmsnorm(x, g, tm=512, row0=0, total_rows=None, prev=None):
    m, d = x.shape
    rb0 = row0 // tm
    kern, prev_specs, prev_args, aliases = _carry_through(_rmsnorm_kernel, 2, [prev])
    return pl.pallas_call(
        kern,
        grid=(m // tm,),
        in_specs=[pl.BlockSpec((tm, d), lambda i: (i, 0)), pl.BlockSpec((1, d), lambda i: (0, 0))] + prev_specs,
        out_specs=pl.BlockSpec((tm, d), lambda i: (rb0 + i, 0)),
        out_shape=jax.ShapeDtypeStruct((total_rows or m, d), BF16),
        input_output_aliases=aliases,
        compiler_params=_cparams("parallel"),
        name="rmsnorm",
    )(x, g.reshape(1, d), *prev_args)


def _mm_kernel(x_ref, w_ref, o_ref):
    o_ref[...] = jnp.dot(x_ref[...], w_ref[0], preferred_element_type=F32)


def _mm(x, w, lyr, tm, tn):
    m, k = x.shape
    n = w.shape[2]
    return pl.pallas_call(
        _mm_kernel,
        grid=(n // tn, m // tm),
        in_specs=[pl.BlockSpec((tm, k), lambda j, i: (i, 0)), pl.BlockSpec((1, k, tn), lambda j, i: (lyr, 0, j))],
        out_specs=pl.BlockSpec((tm, tn), lambda j, i: (i, j)),
        out_shape=jax.ShapeDtypeStruct((m, n), F32),
        compiler_params=_cparams("parallel", "parallel"),
        name="matmul",
    )(x, w)


def _merge_kernel(y0_ref, y1_ref, y2_ref, w_ref, g0_ref, g1_ref, g2_ref, o_ref):
    acc = _sigmoid(g0_ref[...]) * jnp.dot(y0_ref[...], w_ref[0, 0], preferred_element_type=F32)
    acc += _sigmoid(g1_ref[...]) * jnp.dot(y1_ref[...], w_ref[0, 1], preferred_element_type=F32)
    acc += _sigmoid(g2_ref[...]) * jnp.dot(y2_ref[...], w_ref[0, 2], preferred_element_type=F32)
    o_ref[...] = acc.astype(o_ref.dtype)


def _merge(y_rw, y_ml, y_rt, w_br, lyr, z, gate_col0, tm=512, tn=1024):
    m, kw = y_rw.shape
    d = w_br.shape[3]
    gb = gate_col0 // tn
    gspec = lambda c: pl.BlockSpec((tm, tn), lambda j, i: (i, gb + c * (d // tn) + j))
    yspec = pl.BlockSpec((tm, kw), lambda j, i: (i, 0))
    return pl.pallas_call(
        _merge_kernel,
        grid=(d // tn, m // tm),
        in_specs=[yspec, yspec, yspec, pl.BlockSpec((1, 3, kw, tn), lambda j, i: (lyr, 0, 0, j)),
                  gspec(0), gspec(1), gspec(2)],
        out_specs=pl.BlockSpec((tm, tn), lambda j, i: (i, j)),
        out_shape=jax.ShapeDtypeStruct((m, d), BF16),
        compiler_params=_cparams("parallel", "parallel"),
        name="branch_merge",
    )(y_rw, y_ml, y_rt, w_br, z, z, z)


def _residual_epilogue(acc, x_ref, gpost_ref, gnext_ref, xo_ref, ho_ref):
    xn = x_ref[...] + _rms_rows(acc, gpost_ref[...])
    xo_ref[...] = xn
    ho_ref[...] = _rms_rows(xn, gnext_ref[...]).astype(ho_ref.dtype)


def _proj_res_kernel(a_ref, w_ref, x_ref, gpost_ref, gnext_ref, xo_ref, ho_ref):
    acc = jnp.dot(a_ref[...], w_ref[0], preferred_element_type=F32)
    _residual_epilogue(acc, x_ref, gpost_ref, gnext_ref, xo_ref, ho_ref)


def _proj_res(a, w, lyr, x, g_post, g_next, row0=0, prev=(None, None), tm=512):
    total, k = a.shape
    m, d = x.shape
    rb0 = row0 // tm
    at = lambda i: (rb0 + i, 0)
    fixed = lambda i: (0, 0)
    kern, prev_specs, prev_args, aliases = _carry_through(_proj_res_kernel, 5, list(prev))
    return pl.pallas_call(
        kern,
        grid=(m // tm,),
        in_specs=[pl.BlockSpec((tm, k), at), pl.BlockSpec((1, k, d), lambda i: (lyr, 0, 0)),
                  pl.BlockSpec((tm, d), lambda i: (i, 0)),
                  pl.BlockSpec((1, d), fixed), pl.BlockSpec((1, d), fixed)] + prev_specs,
        out_specs=[pl.BlockSpec((tm, d), at), pl.BlockSpec((tm, d), at)],
        out_shape=[jax.ShapeDtypeStruct((total, d), F32), jax.ShapeDtypeStruct((total, d), BF16)],
        input_output_aliases=aliases,
        compiler_params=_cparams("parallel"),
        name="out_proj_residual",
    )(a, w, x, g_post.reshape(1, d), g_next.reshape(1, d), *prev_args)


def _mlp_kernel(h_ref, w1_ref, w2_ref, x_ref, gpost_ref, gnext_ref, xo_ref, ho_ref, acc_ref):
    k = pl.program_id(1)

    @pl.when(k == 0)
    def _():
        acc_ref[...] = jnp.zeros_like(acc_ref)

    u = jnp.dot(h_ref[...], w1_ref[0], preferred_element_type=F32)
    u = jnp.square(jnp.maximum(u, 0.0)).astype(BF16)
    acc_ref[...] += jnp.dot(u, w2_ref[0], preferred_element_type=F32)

    @pl.when(k == pl.num_programs(1) - 1)
    def _():
        _residual_epilogue(acc_ref[...], x_ref, gpost_ref, gnext_ref, xo_ref, ho_ref)


def _mlp(h, w1, w2, lyr, x, g_post, g_next, row0=0, n_rows=None, tm=512, tf=1024):
    d = h.shape[1]
    m = n_rows or h.shape[0]
    f = w1.shape[2]
    rb0 = row0 // tm
    at = lambda i, k: (rb0 + i, 0)
    row = lambda i, k: (i, 0)
    fixed = lambda i, k: (0, 0)
    return pl.pallas_call(
        _mlp_kernel,
        grid=(m // tm, f // tf),
        in_specs=[pl.BlockSpec((tm, d), at), pl.BlockSpec((1, d, tf), lambda i, k: (lyr, 0, k)),
                  pl.BlockSpec((1, tf, d), lambda i, k: (lyr, k, 0)), pl.BlockSpec((tm, d), at),
                  pl.BlockSpec((1, d), fixed), pl.BlockSpec((1, d), fixed)],
        out_specs=[pl.BlockSpec((tm, d), row), pl.BlockSpec((tm, d), row)],
        out_shape=[jax.ShapeDtypeStruct((m, d), F32), jax.ShapeDtypeStruct((m, d), BF16)],
        scratch_shapes=[pltpu.VMEM((tm, d), F32)],
        compiler_params=_cparams("parallel", "arbitrary"),
        name="relu2_mlp",
    )(h, w1, w2, x, g_post.reshape(1, d), g_next.reshape(1, d))


def _head_rows(ref, hd):
    n_mem = ref.shape[2] // X_HEADS
    return ref[0, :, pl.ds(hd, n_mem, stride=X_HEADS), :]


def _attn_kernel(bb, lq, h_ref, wq_ref, mk_ref, mv_ref, wo_ref, x_ref, gpost_ref, gnext_ref, xo_ref, ho_ref):
    q = jnp.dot(h_ref[...], wq_ref[0], preferred_element_type=F32)
    outs = []
    for hd in range(X_HEADS):
        sl = slice(hd * X_HD, (hd + 1) * X_HD)
        q3 = q[:, sl].reshape(bb, lq, X_HD).astype(BF16)
        k3 = _head_rows(mk_ref, hd).astype(BF16)
        v3 = _head_rows(mv_ref, hd).astype(BF16)
        s = jnp.einsum('bqd,bmd->bqm', q3, k3, preferred_element_type=F32) * (X_HD ** -0.5)
        s = s - jnp.max(s, axis=-1, keepdims=True)
        e = jnp.exp(s)
        p = (e / jnp.sum(e, axis=-1, keepdims=True)).astype(BF16)
        o3 = jnp.einsum('bqm,bmd->bqd', p, v3, preferred_element_type=F32)
        outs.append(o3.reshape(bb * lq, X_HD))
    o = jnp.concatenate(outs, axis=1).astype(BF16)
    acc = jnp.dot(o, wo_ref[0], preferred_element_type=F32)
    _residual_epilogue(acc, x_ref, gpost_ref, gnext_ref, xo_ref, ho_ref)


def _attn_seq_kernel(h_ref, wq_ref, mk_ref, mv_ref, wo_ref, x_ref, gpost_ref, gnext_ref, xo_ref, ho_ref):
    heads = range(X_HEADS)
    q = jnp.dot(h_ref[...], wq_ref[0], preferred_element_type=F32)
    stack = lambda f: jnp.stack([f(hd) for hd in heads]).astype(BF16)
    q3 = stack(lambda hd: q[:, hd * X_HD:(hd + 1) * X_HD])
    k3 = stack(lambda hd: _head_rows(mk_ref, hd)[0])
    v3 = stack(lambda hd: _head_rows(mv_ref, hd)[0])
    s = jnp.einsum('hqd,hmd->hqm', q3, k3, preferred_element_type=F32) * (X_HD ** -0.5)
    s = s - jnp.max(s, axis=-1, keepdims=True)
    e = jnp.exp(s)
    p = (e / jnp.sum(e, axis=-1, keepdims=True)).astype(BF16)
    o3 = jnp.einsum('hqm,hmd->hqd', p, v3, preferred_element_type=F32)
    o = jnp.concatenate([o3[hd] for hd in heads], axis=1).astype(BF16)
    acc = jnp.dot(o, wo_ref[0], preferred_element_type=F32)
    _residual_epilogue(acc, x_ref, gpost_ref, gnext_ref, xo_ref, ho_ref)


def _attn(h, x, mk, mv, lyr, wq, wo, wl, g_post, g_next, row0, n_seq, seq_len, bb, lq):
    t, d = h.shape
    rows = bb * lq
    nl = seq_len // lq
    n_mem = mk.shape[2]
    rb0 = row0 // rows
    rowmap = lambda b, l: (rb0 + b * nl + l, 0)
    fixed = lambda b, l: (0, 0)
    body = _attn_seq_kernel if bb == 1 else functools.partial(_attn_kernel, bb, lq)
    kv_specs = [pl.BlockSpec((1, bb, n_mem * X_HEADS, X_HD), lambda b, l: (lyr, b, 0, 0))] * 2
    kv_args = [a.reshape(a.shape[0], a.shape[1], n_mem * X_HEADS, X_HD) for a in (mk, mv)]
    x_at = 3 + len(kv_specs)
    return pl.pallas_call(
        body,
        grid=(n_seq // bb, nl),
        in_specs=[pl.BlockSpec((rows, d), rowmap), pl.BlockSpec((1, d, X_W), lambda b, l: (wl, 0, 0))] + kv_specs
                 + [pl.BlockSpec((1, X_W, d), lambda b, l: (wl, 0, 0)), pl.BlockSpec((rows, d), rowmap),
                    pl.BlockSpec((1, d), fixed), pl.BlockSpec((1, d), fixed)],
        out_specs=[pl.BlockSpec((rows, d), rowmap), pl.BlockSpec((rows, d), rowmap)],
        out_shape=[jax.ShapeDtypeStruct((t, d), F32), jax.ShapeDtypeStruct((t, d), BF16)],
        input_output_aliases={x_at: 0, 0: 1},
        compiler_params=_cparams("parallel", "parallel"),
        name="mem_cross_attention",
    )(h, wq, *kv_args, wo, x, g_post.reshape(1, d), g_next.reshape(1, d))


def _rwkv_prep_kernel(bb, lt, u_ref, s0_ref, mu_ref, w0_ref, wup_ref, a0_ref, aup_ref, gup_ref, kk_ref, ka_ref,
                      rk_ref, r_o, w_o, k_o, kk_o, b_o, v_o, g_o, bonus_o, shift_o, carry_ref):
    l = pl.program_id(1)
    rows = bb * lt
    u = u_ref[...]
    shift_o[...] = u[rows - 1:rows, :].reshape(1, 1, RW_COLS) if bb == 1 else _last_of_seq(u, bb, lt)
    if bb == 1:
        first = jnp.where(l == 0, s0_ref[0], carry_ref[...])
        first_rows = jnp.broadcast_to(first, (rows, RW_COLS))
    else:
        first_rows = jnp.broadcast_to(s0_ref[...], (bb, lt, RW_COLS)).reshape(rows, RW_COLS)
    rowid = lax.broadcasted_iota(jnp.int32, (rows, 1), 0) % lt
    prev = jnp.where(rowid == 0, first_rows, pltpu.roll(u, 1, 0))
    if bb == 1:
        carry_ref[...] = u[rows - 1:rows, :]
    z = u + (prev - u) * mu_ref[...]
    r = z[:, 0:MIX_W]
    k = z[:, MIX_W:2 * MIX_W]
    v = z[:, 2 * MIX_W:3 * MIX_W]
    o = 3 * MIX_W
    wd = z[:, o:o + RW_LORA[0]]
    ad = z[:, o + RW_LORA[0]:o + RW_LORA[0] + RW_LORA[1]]
    gd = z[:, o + RW_LORA[0] + RW_LORA[1]:RW_COLS]
    w_log = -_softplus(-(w0_ref[...] + _bdot(jnp.tanh(wd), wup_ref[...]))) - 0.5
    decay = jnp.exp(-jnp.exp(w_log))
    a = _sigmoid(a0_ref[...] + _bdot(ad, aup_ref[...]))
    g = _bdot(_sigmoid(gd), gup_ref[...])
    kk = k * kk_ref[...]
    kk = kk / jnp.maximum(jnp.sqrt(_segsum(kk * kk, RW_HD)), 1e-12)
    k = k * (1.0 + (a - 1.0) * ka_ref[...])
    r_o[...] = r
    w_o[...] = decay
    k_o[...] = k
    kk_o[...] = kk
    b_o[...] = kk * a
    v_o[...] = v
    g_o[...] = g
    bonus_o[...] = _segsum(r * k * rk_ref[...], RW_HD) * v


def _rwkv_prep(z, shift0, p, row0, n_seq, seq_len, bb, lt):
    rows = bb * lt
    nl = seq_len // lt
    assert bb == 1 or nl == 1
    rb0 = row0 // rows
    m = n_seq * seq_len
    fixed = lambda b, l: (0, 0)
    outmap = lambda b, l: (b * nl + l, 0)
    vec = lambda width: pl.BlockSpec((1, width), fixed)
    ospec = pl.BlockSpec((rows, MIX_W), outmap)
    oshape = jax.ShapeDtypeStruct((m, MIX_W), F32)
    return pl.pallas_call(
        functools.partial(_rwkv_prep_kernel, bb, lt),
        grid=(n_seq // bb, nl),
        in_specs=[pl.BlockSpec((rows, RW_COLS), lambda b, l: (rb0 + b * nl + l, 0)),
                  pl.BlockSpec((bb, 1, RW_COLS), lambda b, l: (b, 0, 0)),
                  vec(RW_COLS), vec(MIX_W), pl.BlockSpec((RW_LORA[0], MIX_W), fixed), vec(MIX_W),
                  pl.BlockSpec((RW_LORA[1], MIX_W), fixed), pl.BlockSpec((RW_LORA[2], MIX_W), fixed),
                  vec(MIX_W), vec(MIX_W), vec(MIX_W)],
        out_specs=[ospec] * 8 + [pl.BlockSpec((bb, 1, RW_COLS), lambda b, l: (b, 0, 0))],
        out_shape=[oshape] * 8 + [jax.ShapeDtypeStruct((n_seq, 1, RW_COLS), F32)],
        scratch_shapes=[pltpu.VMEM((1, RW_COLS), F32)],
        compiler_params=_cparams("parallel", "arbitrary"),
        name="rwkv_prep",
    )(z, shift0, p['mu'], p['w0'], p['w_up'], p['a0'], p['a_up'], p['g_up'], p['k_k'], p['k_a'], p['r_k'])


def _rwkv_scan_kernel(tc, r_ref, w_ref, k_ref, kk_ref, b_ref, v_ref, s0_ref, y_ref, s_ref):
    @pl.when(pl.program_id(1) == 0)
    def _():
        s_ref[...] = s0_ref[...]

    y_ref[0, :, RW_VP:, :] = jnp.zeros((tc, RW_VR - RW_VP, LANES), F32)
    keys = lambda ref, t: ref[0, t, :RW_HD, :]

    def step(t, carry):
        for vp in range(RW_VP):
            s = s_ref[0, vp]
            sa = jnp.sum(s * keys(kk_ref, t), axis=0, keepdims=True)
            sn = s * keys(w_ref, t) - sa * keys(b_ref, t) + v_ref[0, t, vp:vp + 1, :] * keys(k_ref, t)
            s_ref[0, vp] = sn
            y_ref[0, t, vp:vp + 1, :] = jnp.sum(sn * keys(r_ref, t), axis=0, keepdims=True)
        return carry

    lax.fori_loop(0, tc, step, 0)


def _rwkv_scan(r, w, k, kk, b, v, s0, lyr_in, tc, depth, lyr, s_prev):
    g, l = r.shape[0], r.shape[1]
    kspec = pl.BlockSpec((1, tc, RW_KR, LANES), lambda i, c: (i, c, 0, 0))
    vspec = pl.BlockSpec((1, tc, RW_VR, LANES), lambda i, c: (i, c, 0, 0))
    sspec = lambda at: pl.BlockSpec((1, RW_VP, RW_HD, LANES), lambda i, c: (at * g + i, 0, 0, 0))
    kern, prev_specs, prev_args, aliases = _carry_through(functools.partial(_rwkv_scan_kernel, tc), 7,
                                                          [None, s_prev])
    return pl.pallas_call(
        kern,
        grid=(g, l // tc),
        in_specs=[kspec] * 5 + [vspec, sspec(lyr_in)] + prev_specs,
        out_specs=[vspec, sspec(lyr)],
        out_shape=[jax.ShapeDtypeStruct((g, l, RW_VR, LANES), F32),
                   jax.ShapeDtypeStruct((depth * g, RW_VP, RW_HD, LANES), F32)],
        input_output_aliases=aliases,
        compiler_params=_cparams("parallel", "arbitrary"),
        name="rwkv_scan",
    )(r, w, k, kk, b, v, s0, *prev_args)


def _rwkv_post_kernel(y_ref, g_ref, bonus_ref, gng_ref, gnb_ref, o_ref):
    y = y_ref[...]
    mean = _segsum(y, RW_HD) * (1.0 / RW_HD)
    yc = y - mean
    var = _segsum(yc * yc, RW_HD) * (1.0 / RW_HD)
    yn = yc * lax.rsqrt(var + RW_GN_EPS) * gng_ref[...] + gnb_ref[...]
    o_ref[...] = ((yn + bonus_ref[...]) * g_ref[...]).astype(o_ref.dtype)


def _rwkv_post(y, g, bonus, gn_g, gn_b, row0, total_rows, prev, tm=256):
    m = y.shape[0]
    rb0 = row0 // tm
    fixed = lambda i: (0, 0)
    spec = pl.BlockSpec((tm, MIX_W), lambda i: (i, 0))
    kern, prev_specs, prev_args, aliases = _carry_through(_rwkv_post_kernel, 5, [prev])
    return pl.pallas_call(
        kern,
        grid=(m // tm,),
        in_specs=[spec, spec, spec, pl.BlockSpec((1, MIX_W), fixed), pl.BlockSpec((1, MIX_W), fixed)] + prev_specs,
        out_specs=pl.BlockSpec((tm, MIX_W), lambda i: (rb0 + i, 0)),
        out_shape=jax.ShapeDtypeStruct((total_rows, MIX_W), BF16),
        input_output_aliases=aliases,
        compiler_params=_cparams("parallel"),
        name="rwkv_post",
    )(y, g, bonus, gn_g, gn_b, *prev_args)


RW_NB = RW_CHAINS // RW_HEADS
RW_TT = LANES


RW_KR = RW_HD + 8
RW_VR = RW_VP + 8


def _natural_to_chain_rows(x_ref, a_ref):
    for bl in range(RW_NB):
        xt = x_ref[bl].T
        for h in range(RW_HEADS):
            c = bl * RW_HEADS + h
            a_ref[c * RW_KR:c * RW_KR + RW_HD, :] = xt[h * RW_HD:(h + 1) * RW_HD]


def _keys_to_scan_kernel(x_ref, o_ref, a_ref):
    _natural_to_chain_rows(x_ref, a_ref)
    o_ref[...] = jnp.zeros_like(o_ref)
    for k in range(RW_HD):
        tile = a_ref[pl.ds(k, RW_CHAINS, stride=RW_KR), :]
        o_ref[0, pl.ds(k, RW_TT, stride=RW_KR), :] = jnp.concatenate([tile, tile], axis=0).T


def _vals_to_scan_kernel(x_ref, o_ref, a_ref):
    _natural_to_chain_rows(x_ref, a_ref)
    o_ref[...] = jnp.zeros_like(o_ref)
    for vp in range(RW_VP):
        halves = [a_ref[pl.ds(vh * RW_VP + vp, RW_CHAINS, stride=RW_KR), :] for vh in range(2)]
        o_ref[0, pl.ds(vp, RW_TT, stride=RW_VR), :] = jnp.concatenate(halves, axis=0).T


def _vals_from_scan_kernel(y_ref, o_ref, a_ref):
    for vp in range(RW_VP):
        tile = y_ref[0, pl.ds(vp, RW_TT, stride=RW_VR), :].T
        for vh in range(2):
            a_ref[pl.ds(vh * RW_VP + vp, RW_CHAINS, stride=RW_KR), :] = tile[vh * RW_CHAINS:(vh + 1) * RW_CHAINS]
    for bl in range(RW_NB):
        rows = [a_ref[c * RW_KR:c * RW_KR + RW_HD, :] for c in range(bl * RW_HEADS, (bl + 1) * RW_HEADS)]
        o_ref[bl] = jnp.concatenate(rows, axis=0).T


def _scan_relayout(kernel_fn, x, n_seq, seq_len, rows_per_t, to_scan, name):
    g, nl = n_seq // RW_NB, seq_len // RW_TT
    nat_spec = pl.BlockSpec((RW_NB, RW_TT, MIX_W), lambda i, l: (i, l, 0))
    scan_spec = pl.BlockSpec((1, RW_TT * rows_per_t, LANES), lambda i, l: (i, l, 0))
    nat_shape = jax.ShapeDtypeStruct((n_seq, seq_len, MIX_W), F32)
    scan_shape = jax.ShapeDtypeStruct((g, seq_len * rows_per_t, LANES), F32)
    x = x.reshape(n_seq, seq_len, MIX_W) if to_scan else x.reshape(g, seq_len * rows_per_t, LANES)
    out = pl.pallas_call(
        kernel_fn,
        grid=(g, nl),
        in_specs=[nat_spec if to_scan else scan_spec],
        out_specs=scan_spec if to_scan else nat_spec,
        out_shape=scan_shape if to_scan else nat_shape,
        scratch_shapes=[pltpu.VMEM((RW_CHAINS * RW_KR, RW_TT), F32)],
        compiler_params=_cparams("parallel", "parallel"),
        name=name,
    )(x)
    return out.reshape(g, seq_len, rows_per_t, LANES) if to_scan else out.reshape(n_seq * seq_len, MIX_W)


def _pad_rows(a, rows):
    return jnp.pad(a, ((0, 0), (0, 0), (0, rows - a.shape[2]), (0, 0)))


def _to_scan_keys(a, n_seq, seq_len):
    if seq_len % RW_TT == 0:
        return _scan_relayout(_keys_to_scan_kernel, a, n_seq, seq_len, RW_KR, True, "rwkv_keys_to_scan")
    g = n_seq // RW_NB
    a = a.reshape(g, RW_NB, seq_len, RW_HEADS, RW_HD).transpose(0, 2, 4, 1, 3).reshape(g, seq_len, RW_HD, RW_CHAINS)
    return _pad_rows(jnp.concatenate([a, a], axis=-1), RW_KR)


def _to_scan_vals(a, n_seq, seq_len):
    if seq_len % RW_TT == 0:
        return _scan_relayout(_vals_to_scan_kernel, a, n_seq, seq_len, RW_VR, True, "rwkv_vals_to_scan")
    g = n_seq // RW_NB
    a = a.reshape(g, RW_NB, seq_len, RW_HEADS, 2, RW_VP).transpose(0, 2, 5, 4, 1, 3)
    return _pad_rows(a.reshape(g, seq_len, RW_VP, LANES), RW_VR)


def _from_scan_vals(y, n_seq, seq_len):
    if seq_len % RW_TT == 0:
        return _scan_relayout(_vals_from_scan_kernel, y, n_seq, seq_len, RW_VR, False, "rwkv_vals_from_scan")
    g = n_seq // RW_NB
    y = y[:, :, :RW_VP].reshape(g, seq_len, RW_VP, 2, RW_NB, RW_HEADS).transpose(0, 4, 1, 5, 3, 2)
    return y.reshape(n_seq * seq_len, MIX_W)


def _state_to_scan(s, n_seq):
    nb = RW_CHAINS // RW_HEADS
    g = n_seq // nb
    s = s.reshape(g, nb, RW_HEADS, 2, RW_VP, RW_HD).transpose(0, 4, 5, 3, 1, 2)
    return s.reshape(g, RW_VP, RW_HD, LANES)


def _state_from_scan(s, n_seq):
    nb = RW_CHAINS // RW_HEADS
    g = n_seq // nb
    s = s.reshape(g, RW_VP, RW_HD, 2, nb, RW_HEADS).transpose(0, 4, 5, 3, 1, 2)
    return s.reshape(n_seq, RW_HEADS, RW_HD, RW_HD)


def _rwkv_branch(z, shift0, s0, lyr_in, p, row0, n_seq, seq_len, bb, lt, tc, depth, lyr, prev):
    r, w, k, kk, b, v, g, bonus, shift = _rwkv_prep(z, shift0, p, row0, n_seq, seq_len, bb, lt)
    keys = [_to_scan_keys(a, n_seq, seq_len) for a in (r, w, k, kk, b)]
    y, s_new = _rwkv_scan(*keys, _to_scan_vals(v, n_seq, seq_len), s0, lyr_in, tc, depth, lyr, prev[1])
    y = _from_scan_vals(y, n_seq, seq_len)
    y = _rwkv_post(y, g, bonus, p['gn_g'], p['gn_b'], row0, z.shape[0], prev[0])
    return y, s_new, shift.reshape(n_seq, RW_COLS)


def _seq_masks(rows, lc):
    ri = lax.broadcasted_iota(jnp.int32, (rows, rows), 0)
    ci = lax.broadcasted_iota(jnp.int32, (rows, rows), 1)
    same = (ri // lc) == (ci // lc)
    return ri, ci, same


def _rows_from_seq(x, bb, lc):
    return jnp.broadcast_to(x, (bb, lc, x.shape[-1])).reshape(bb * lc, x.shape[-1])


def _last_of_seq(x, bb, lc):
    x3 = x.reshape(bb, lc, x.shape[-1])
    pick = (lax.broadcasted_iota(jnp.int32, (1, lc, 1), 1) == lc - 1).astype(F32)
    return jnp.sum(x3 * pick, axis=1, keepdims=True)


def _transpose_rows(x):
    rows = x.shape[0]
    if rows < LANES:
        x = jnp.concatenate([x, jnp.zeros((LANES - rows, LANES), x.dtype)], axis=0)
    return x.T[:, :rows]


def _mlstm_kernel(bb, lc, q_ref, k_ref, v_ref, o_ref, gt_ref, gb_ref, ng_ref, c0_ref, n0_ref, m0_ref,
                  y_ref, c_ref, n_ref, m_ref):
    rows = bb * lc

    @pl.when(pl.program_id(1) == 0)
    def _():
        c_ref[...] = c0_ref[...]
        n_ref[...] = n0_ref[...]
        m_ref[...] = m0_ref[...]

    gt = ML_GATE_CAP * jnp.tanh((gt_ref[...] + gb_ref[...]) * (1.0 / ML_GATE_CAP))
    lane = lax.broadcasted_iota(jnp.int32, (rows, LANES), 1)
    ri, ci, same = _seq_masks(rows, lc)
    causal = same & (ci <= ri)
    cum = _dot_exact_lhs(causal.astype(F32), -_softplus(-gt))
    gt_t = _transpose_rows(gt)
    cum_t = _transpose_rows(cum)
    for hd in range(ML_HEADS):
        sl = slice(hd * ML_HD, (hd + 1) * ML_HD)
        c0 = c_ref[0, :, hd]
        n0 = n_ref[0, :, hd:hd + 1, :]
        m0 = m_ref[0, :, :, hd:hd + 1]
        q = q_ref[:, sl]
        k = k_ref[:, sl] * (ML_HD ** -0.5)
        v = v_ref[:, sl]
        bcol = cum[:, ML_HEADS + hd:ML_HEADS + hd + 1]
        igcol = gt[:, hd:hd + 1]
        rowb = gt_t[hd:hd + 1, :] - cum_t[ML_HEADS + hd:ML_HEADS + hd + 1, :]
        dlog = jnp.where(causal, bcol + rowb, -jnp.inf)
        m_inter = bcol + _rows_from_seq(m0, bb, lc)
        m_t = jnp.maximum(m_inter, jnp.max(dlog, axis=-1, keepdims=True))
        qk = lax.dot_general(q.astype(BF16), k.astype(BF16), (((1,), (1,)), ((), ())),
                             preferred_element_type=F32)
        wts = jnp.exp(dlog - m_t) * qk
        s_inter = jnp.exp(m_inter - m_t)
        q3 = q.reshape(bb, lc, ML_HD).astype(BF16)
        qc = jnp.einsum('btd,bde->bte', q3, c0.astype(BF16), preferred_element_type=F32).reshape(rows, ML_HD)
        num = _bdot(wts, v) + s_inter * qc
        qn = jnp.sum(q * _rows_from_seq(n0, bb, lc), axis=-1, keepdims=True)
        den = jnp.sum(wts, axis=-1, keepdims=True) + s_inter * qn
        hh = num / jnp.maximum(jnp.abs(den), jnp.exp(-m_t))
        hh = hh * lax.rsqrt(jnp.mean(hh * hh, axis=-1, keepdims=True) + EPS) * ng_ref[:, sl]
        y_ref[:, sl] = (_sigmoid(o_ref[:, sl]) * hh).astype(y_ref.dtype)

        stat = jnp.where(lane == 0, bcol, jnp.where(lane == 1, m_t, 0.0))
        last = _last_of_seq(stat, bb, lc)
        last_rows = _rows_from_seq(last, bb, lc)
        b_last, m_new = last_rows[:, 0:1], last_rows[:, 1:2]
        w_end = jnp.exp(b_last - bcol + igcol - m_new)
        f_end = jnp.exp(last[:, :, 0:1] + m0 - last[:, :, 1:2])
        kw3 = (k * w_end).reshape(bb, lc, ML_HD)
        upd = jnp.einsum('bsd,bse->bde', kw3.astype(BF16), v.reshape(bb, lc, ML_HD).astype(BF16),
                         preferred_element_type=F32)
        c_ref[0, :, hd] = f_end * c0 + upd
        n_ref[0, :, hd:hd + 1, :] = f_end * n0 + jnp.sum(kw3, axis=1, keepdims=True)
        m_ref[0, :, :, hd:hd + 1] = last[:, :, 1:2]


def _mlstm_seq_kernel(lc, q_ref, k_ref, v_ref, o_ref, gt_ref, gb_ref, ng_ref, c0_ref, n0_ref, m0_ref,
                      y_ref, c_ref, n_ref, m_ref):
    heads = range(ML_HEADS)

    @pl.when(pl.program_id(1) == 0)
    def _():
        c_ref[...] = c0_ref[...]
        n_ref[...] = n0_ref[...]
        m_ref[...] = m0_ref[...]

    gt = ML_GATE_CAP * jnp.tanh((gt_ref[...] + gb_ref[...]) * (1.0 / ML_GATE_CAP))
    ri, ci, _ = _seq_masks(lc, lc)
    causal = ci <= ri
    cum = _dot_exact_lhs(causal.astype(F32), -_softplus(-gt))
    gt_t = _transpose_rows(gt)
    cum_t = _transpose_rows(cum)
    sl = lambda hd: slice(hd * ML_HD, (hd + 1) * ML_HD)
    stack = lambda f: jnp.stack([f(hd) for hd in heads])
    q = stack(lambda hd: q_ref[:, sl(hd)])
    k = stack(lambda hd: k_ref[:, sl(hd)]) * (ML_HD ** -0.5)
    v = stack(lambda hd: v_ref[:, sl(hd)])
    bcol = stack(lambda hd: cum[:, ML_HEADS + hd:ML_HEADS + hd + 1])
    igcol = stack(lambda hd: gt[:, hd:hd + 1])
    rowb = stack(lambda hd: gt_t[hd:hd + 1, :] - cum_t[ML_HEADS + hd:ML_HEADS + hd + 1, :])
    c0 = c_ref[0, 0]
    n0 = stack(lambda hd: n_ref[0, 0, hd:hd + 1, :])
    m0 = stack(lambda hd: m_ref[0, 0, :, hd:hd + 1])
    qb, kb, vb = q.astype(BF16), k.astype(BF16), v.astype(BF16)

    dlog = jnp.where(causal[None], bcol + rowb, -jnp.inf)
    m_inter = bcol + m0
    m_t = jnp.maximum(m_inter, jnp.max(dlog, axis=-1, keepdims=True))
    qk = jnp.einsum('hqd,hkd->hqk', qb, kb, preferred_element_type=F32)
    wts = jnp.exp(dlog - m_t) * qk
    s_inter = jnp.exp(m_inter - m_t)
    qc = jnp.einsum('hqd,hde->hqe', qb, c0.astype(BF16), preferred_element_type=F32)
    num = jnp.einsum('hqk,hke->hqe', wts.astype(BF16), vb, preferred_element_type=F32) + s_inter * qc
    den = jnp.sum(wts, axis=-1, keepdims=True) + s_inter * jnp.sum(q * n0, axis=-1, keepdims=True)
    hh = num / jnp.maximum(jnp.abs(den), jnp.exp(-m_t))
    hh = hh * lax.rsqrt(jnp.mean(hh * hh, axis=-1, keepdims=True) + EPS)
    for hd in heads:
        y_ref[:, sl(hd)] = (_sigmoid(o_ref[:, sl(hd)]) * hh[hd] * ng_ref[:, sl(hd)]).astype(y_ref.dtype)

    b_last, m_new = bcol[:, lc - 1:lc, :], m_t[:, lc - 1:lc, :]
    w_end = jnp.exp(b_last - bcol + igcol - m_new)
    f_end = jnp.exp(b_last + m0 - m_new)
    kw = k * w_end
    upd = jnp.einsum('hsd,hse->hde', kw.astype(BF16), vb, preferred_element_type=F32)
    c_ref[0, 0] = f_end * c0 + upd
    n_new = f_end * n0 + jnp.sum(kw, axis=1, keepdims=True)
    for hd in heads:
        n_ref[0, 0, hd:hd + 1, :] = n_new[hd]
        m_ref[0, 0, :, hd:hd + 1] = m_new[hd]


def _mlstm_branch(z, z_gates, init, lyr_in, gate_bias, norm_g, row0, n_seq, seq_len, bb, lc, depth, lyr, prev):
    rows = bb * lc
    nc = seq_len // lc
    rb0 = row0 // rows
    rowmap = lambda part: (lambda b, c: (rb0 + b * nc + c, part))
    zspec = lambda part: pl.BlockSpec((rows, MIX_W), rowmap(part))
    state_specs = lambda at: [pl.BlockSpec((1, bb, ML_HEADS, ML_HD, ML_HD), lambda b, c: (at, b, 0, 0, 0)),
                              pl.BlockSpec((1, bb, ML_HEADS, ML_HD), lambda b, c: (at, b, 0, 0)),
                              pl.BlockSpec((1, bb, 1, ML_HEADS), lambda b, c: (at, b, 0, 0))]
    body = functools.partial(_mlstm_seq_kernel, lc) if bb == 1 else functools.partial(_mlstm_kernel, bb, lc)
    kern, prev_specs, prev_args, aliases = _carry_through(body, 10, prev)
    return pl.pallas_call(
        kern,
        grid=(n_seq // bb, nc),
        in_specs=[zspec(0), zspec(1), zspec(2), zspec(3),
                  pl.BlockSpec((rows, LANES), lambda b, c: (rb0 + b * nc + c, 0)),
                  pl.BlockSpec((1, LANES), lambda b, c: (0, 0)), pl.BlockSpec((1, MIX_W), lambda b, c: (0, 0))]
                 + state_specs(lyr_in) + prev_specs,
        out_specs=[pl.BlockSpec((rows, MIX_W), lambda b, c: (rb0 + b * nc + c, 0))] + state_specs(lyr),
        out_shape=[jax.ShapeDtypeStruct((z.shape[0], MIX_W), BF16),
                   jax.ShapeDtypeStruct((depth, n_seq, ML_HEADS, ML_HD, ML_HD), F32),
                   jax.ShapeDtypeStruct((depth, n_seq, ML_HEADS, ML_HD), F32),
                   jax.ShapeDtypeStruct((depth, n_seq, 1, ML_HEADS), F32)],
        input_output_aliases=aliases,
        compiler_params=_cparams("parallel", "arbitrary"),
        name="mlstm_chunk",
    )(z, z, z, z, z_gates, gate_bias, norm_g, *init, *prev_args)


def _rope_rows(x, cos, sin):
    half = RT_HD // 2
    x1, x2 = x[:, :half], x[:, half:]
    return jnp.concatenate([x1 * cos - x2 * sin, x1 * sin + x2 * cos], axis=1)


def _ret_kernel(bb, lc, q_ref, k_ref, v_ref, g_ref, cs_ref, dec_ref, cd_ref, s0_ref, y_ref, s_ref):
    rows = bb * lc

    @pl.when(pl.program_id(1) == 0)
    def _():
        s_ref[...] = s0_ref[...]

    cos, sin = cs_ref[:, :RT_HD // 2], cs_ref[:, RT_HD // 2:]
    for hd in range(RT_HEADS):
        sl = slice(hd * RT_HD, (hd + 1) * RT_HD)
        s0 = s_ref[0, :, hd]
        q = _rope_rows(q_ref[:, sl], cos, sin)
        k = _rope_rows(k_ref[:, sl], cos, sin) * (RT_HD ** -0.5)
        v = v_ref[:, sl]
        cd = cd_ref[hd]
        qk = lax.dot_general(q.astype(BF16), k.astype(BF16), (((1,), (1,)), ((), ())),
                             preferred_element_type=F32)
        inner = _bdot(qk * dec_ref[hd], v)
        q3 = q.reshape(bb, lc, RT_HD).astype(BF16)
        cross = jnp.einsum('btd,bde->bte', q3, s0.astype(BF16), preferred_element_type=F32).reshape(rows, RT_HD)
        y = inner + cross * cd[:, 0:1]
        y = y * lax.rsqrt(jnp.mean(y * y, axis=-1, keepdims=True) + EPS)
        gt = g_ref[:, sl]
        y_ref[:, sl] = (gt * _sigmoid(gt) * y).astype(y_ref.dtype)
        kd = (k * cd[:, 1:2]).reshape(bb, lc, RT_HD).astype(BF16)
        upd = jnp.einsum('bsd,bse->bde', kd, v.reshape(bb, lc, RT_HD).astype(BF16), preferred_element_type=F32)
        s_ref[0, :, hd] = cd[0:1, 2:3] * s0 + upd


def _ret_seq_kernel(lc, q_ref, k_ref, v_ref, g_ref, cs_ref, dec_ref, cd_ref, s0_ref, y_ref, s_ref):
    heads = range(RT_HEADS)

    @pl.when(pl.program_id(1) == 0)
    def _():
        s_ref[...] = s0_ref[...]

    cos, sin = cs_ref[:, :RT_HD // 2], cs_ref[:, RT_HD // 2:]
    sl = lambda hd: slice(hd * RT_HD, (hd + 1) * RT_HD)
    stack = lambda f: jnp.stack([f(hd) for hd in heads])
    q = stack(lambda hd: _rope_rows(q_ref[:, sl(hd)], cos, sin))
    k = stack(lambda hd: _rope_rows(k_ref[:, sl(hd)], cos, sin)) * (RT_HD ** -0.5)
    vb = stack(lambda hd: v_ref[:, sl(hd)]).astype(BF16)
    s0 = s_ref[0, 0]
    cd = cd_ref[...]
    qb = q.astype(BF16)
    qk = jnp.einsum('hqd,hkd->hqk', qb, k.astype(BF16), preferred_element_type=F32)
    inner = jnp.einsum('hqk,hke->hqe', (qk * dec_ref[...]).astype(BF16), vb, preferred_element_type=F32)
    cross = jnp.einsum('hqd,hde->hqe', qb, s0.astype(BF16), preferred_element_type=F32)
    y = inner + cross * cd[:, :, 0:1]
    y = y * lax.rsqrt(jnp.mean(y * y, axis=-1, keepdims=True) + EPS)
    for hd in heads:
        gt = g_ref[:, sl(hd)]
        y_ref[:, sl(hd)] = (gt * _sigmoid(gt) * y[hd]).astype(y_ref.dtype)
    kd = (k * cd[:, :, 1:2]).astype(BF16)
    upd = jnp.einsum('hsd,hse->hde', kd, vb, preferred_element_type=F32)
    s_ref[0, 0] = cd[:, 0:1, 2:3] * s0 + upd


def _ret_branch(z, s0, lyr_in, row0, n_seq, seq_len, bb, lc, pos0, depth, lyr, prev):
    rows = bb * lc
    nc = seq_len // lc
    rb0 = row0 // rows
    half = RT_HD // 2
    inv = ROPE_BASE ** (-jnp.arange(half, dtype=F32) / half)
    pos = jnp.broadcast_to((jnp.arange(nc * lc, dtype=F32) + float(pos0)).reshape(nc, 1, lc), (nc, bb, lc))
    ang = pos.reshape(nc * rows)[:, None] * inv[None, :]
    cs = jnp.concatenate([jnp.cos(ang), jnp.sin(ang)], axis=1)
    log_g = jnp.log(1.0 - jnp.exp(jnp.linspace(math.log(1.0 / 32), math.log(1.0 / 512), RT_HEADS)))
    idx = jnp.arange(rows)
    t = (idx % lc).astype(F32)
    diff = t[:, None] - t[None, :]
    same = (idx[:, None] // lc) == (idx[None, :] // lc)
    dec = jnp.where(same & (diff >= 0), jnp.exp(log_g[:, None, None] * jnp.maximum(diff, 0.0)), 0.0)
    cd = jnp.zeros((RT_HEADS, rows, LANES), F32)
    cd = cd.at[:, :, 0].set(jnp.exp(log_g[:, None] * (t[None, :] + 1.0)))
    cd = cd.at[:, :, 1].set(jnp.exp(log_g[:, None] * (lc - 1.0 - t)[None, :]))
    cd = cd.at[:, :, 2].set(jnp.broadcast_to(jnp.exp(log_g * lc)[:, None], (RT_HEADS, rows)))
    rowmap = lambda part: (lambda b, c: (rb0 + b * nc + c, part))
    zspec = lambda part: pl.BlockSpec((rows, MIX_W), rowmap(part))
    sspec = lambda at: pl.BlockSpec((1, bb, RT_HEADS, RT_HD, RT_HD), lambda b, c: (at, b, 0, 0, 0))
    body = functools.partial(_ret_seq_kernel, lc) if bb == 1 else functools.partial(_ret_kernel, bb, lc)
    kern, prev_specs, prev_args, aliases = _carry_through(body, 8, prev)
    return pl.pallas_call(
        kern,
        grid=(n_seq // bb, nc),
        in_specs=[zspec(0), zspec(1), zspec(2), zspec(3),
                  pl.BlockSpec((rows, RT_HD), lambda b, c: (c, 0)),
                  pl.BlockSpec((RT_HEADS, rows, rows), lambda b, c: (0, 0, 0)),
                  pl.BlockSpec((RT_HEADS, rows, LANES), lambda b, c: (0, 0, 0)),
                  sspec(lyr_in)] + prev_specs,
        out_specs=[pl.BlockSpec((rows, MIX_W), lambda b, c: (rb0 + b * nc + c, 0)), sspec(lyr)],
        out_shape=[jax.ShapeDtypeStruct((z.shape[0], MIX_W), BF16),
                   jax.ShapeDtypeStruct((depth, n_seq, RT_HEADS, RT_HD, RT_HD), F32)],
        input_output_aliases=aliases,
        compiler_params=_cparams("parallel", "arbitrary"),
        name="retention_chunk",
    )(z, z, z, z, cs, dec, cd, s0, *prev_args)


def _mm_wt_kernel(x_ref, wt_ref, o_ref, wb_ref):
    @pl.when(pl.program_id(1) == 0)
    def _():
        tn = wb_ref.shape[1]
        for c in range(0, tn, LANES):
            wb_ref[:, c:c + LANES] = wt_ref[0, c:c + LANES, :].T.astype(wb_ref.dtype)

    o_ref[...] = jnp.dot(x_ref[...], wb_ref[...], preferred_element_type=F32)


def _mm_wt(x, w_t, lyr, col0, n, tm, tn):
    m, k = x.shape
    return pl.pallas_call(
        _mm_wt_kernel,
        grid=(n // tn, m // tm),
        in_specs=[pl.BlockSpec((tm, k), lambda j, i: (i, 0)),
                  pl.BlockSpec((pl.Element(1), pl.Element(tn), pl.Element(k)),
                               lambda j, i: (lyr, pl.multiple_of(col0 + j * tn, 8), 0))],
        out_specs=pl.BlockSpec((tm, tn), lambda j, i: (i, j)),
        out_shape=jax.ShapeDtypeStruct((m, n), F32),
        scratch_shapes=[pltpu.VMEM((k, tn), BF16)],
        compiler_params=_cparams("parallel", "arbitrary"),
        name="matmul_wt",
    )(x, w_t)


def _group_cfg(n_seq, seq_len):
    if seq_len % CHUNK == 0:
        return dict(rw_bb=1, rw_lt=256, rw_tc=64, ml_bb=1, rt_bb=1, ch_lc=CHUNK, at_bb=1, at_lq=512)
    return dict(rw_bb=32, rw_lt=seq_len, rw_tc=seq_len, ml_bb=8, rt_bb=8, ch_lc=seq_len, at_bb=8, at_lq=seq_len)


def kernel(x_prompt, x_sample, mem_prompt, state_rwkv_shift, state_rwkv, state_mlstm_c, state_mlstm_n,
           state_mlstm_m, state_ret, cache_mem_k, cache_mem_v, g_pre_mix, g_post_mix, g_pre_x, g_post_x,
           g_pre_ff, g_post_ff, g_mem, w_in, rw_mu, rw_w0, rw_w_up, rw_a0, rw_a_up, rw_g_up, rw_k_k, rw_k_a,
           rw_r_k, rw_gn_g, rw_gn_b, ml_i_b, ml_f_b, ml_norm_g, w_br, w_out, x_wq, x_wkv, x_wo, ff_w1, ff_w2):
    bp, lp, d = x_prompt.shape
    bs, ls, _ = x_sample.shape
    depth = w_in.shape[0]
    n_mem = mem_prompt.shape[1]
    tp, ts = bp * lp, bs * ls
    groups = [(0, bp, lp, 0), (tp, bs, ls, PAST_LEN)]

    x_groups = [x_prompt.reshape(tp, d), x_sample.reshape(ts, d)]
    mem = mem_prompt.reshape(bp * n_mem, d)
    x = h = None
    for (row0, *_), xg in zip(groups, x_groups):
        h = _rmsnorm(xg, g_pre_mix[0], row0=row0, total_rows=tp + ts, prev=h)

    zeros = lambda *s: jnp.zeros(s, F32)
    rw_scan0 = _state_to_scan(state_rwkv.reshape(depth * bs, RW_HEADS, RW_HD, RW_HD), depth * bs)
    inits = [
        lambda l: (zeros(bp, 1, RW_COLS), zeros(bp // RW_NB, RW_VP, RW_HD, LANES),
                   (zeros(1, bp, ML_HEADS, ML_HD, ML_HD), zeros(1, bp, ML_HEADS, ML_HD), zeros(1, bp, 1, ML_HEADS)),
                   zeros(1, bp, RT_HEADS, RT_HD, RT_HD), 0),
        lambda l: (state_rwkv_shift[l].reshape(bs, 1, RW_COLS), rw_scan0,
                   (state_mlstm_c, state_mlstm_n, state_mlstm_m.reshape(depth, bs, 1, ML_HEADS)), state_ret, l),
    ]
    shifts = [[], []]
    rw_states = [None, None]
    ml_states = [[None] * 3, [None] * 3]
    rt_states = [None, None]
    mem_k, mem_v = [], []
    w_in_t = jnp.swapaxes(w_in, 1, 2)
    w_kv, w_brb, w_outb, w_q, w_o, w_f1, w_f2 = (w.astype(BF16) for w in (x_wkv, w_br, w_out, x_wq, x_wo,
                                                                         ff_w1, ff_w2))
    for l in range(depth):
        kv = _mm(_rmsnorm(mem, g_mem[l], tm=256), w_kv, l, tm=256, tn=512)
        mem_k.append(kv[:, :X_W].reshape(bp, n_mem, X_HEADS, X_HD))
        mem_v.append(kv[:, X_W:].reshape(bp, n_mem, X_HEADS, X_HD))

        z_rw = _mm_wt(h, w_in_t, l, 0, RW_COLS, tm=512, tn=RW_COLS // 2)
        z_ml = _mm_wt(h, w_in_t, l, IN_ML, 4 * MIX_W, tm=1024, tn=1024)
        z_g = _mm_wt(h, w_in_t, l, IN_MLG, LANES, tm=1024, tn=LANES)
        z_rg = _mm_wt(h, w_in_t, l, IN_RT, w_in.shape[2] - IN_RT, tm=1024, tn=1280)

        rw_p = dict(mu=rw_mu[l].reshape(1, -1), w0=rw_w0[l].reshape(1, -1), w_up=rw_w_up[l].astype(BF16),
                    a0=rw_a0[l].reshape(1, -1), a_up=rw_a_up[l].astype(BF16), g_up=rw_g_up[l].astype(BF16),
                    k_k=rw_k_k[l].reshape(1, -1), k_a=rw_k_a[l].reshape(1, -1), r_k=rw_r_k[l].reshape(1, -1),
                    gn_g=rw_gn_g[l].reshape(1, -1), gn_b=rw_gn_b[l].reshape(1, -1))
        gate_bias = jnp.concatenate([ml_i_b[l], ml_f_b[l], zeros(LANES - 2 * ML_HEADS)]).reshape(1, LANES)
        norm_g = ml_norm_g[l].reshape(1, MIX_W)

        y_rw = y_ml = y_rt = None
        for gi, (row0, n_seq, seq_len, pos0) in enumerate(groups):
            cfg = _group_cfg(n_seq, seq_len)
            shift0, s_rw0, ml0, rt0, lyr_in = inits[gi](l)
            y_rw, rw_states[gi], shift = _rwkv_branch(z_rw, shift0, s_rw0, lyr_in, rw_p, row0, n_seq, seq_len,
                                                      cfg['rw_bb'], cfg['rw_lt'], cfg['rw_tc'], depth, l,
                                                      (y_rw, rw_states[gi]))
            shifts[gi].append(shift)
            y_ml, *ml_states[gi] = _mlstm_branch(z_ml, z_g, ml0, lyr_in, gate_bias, norm_g, row0, n_seq, seq_len,
                                                 cfg['ml_bb'], cfg['ch_lc'], depth, l, [y_ml] + ml_states[gi])
            y_rt, rt_states[gi] = _ret_branch(z_rg, rt0, lyr_in, row0, n_seq, seq_len, cfg['rt_bb'],
                                              cfg['ch_lc'], pos0, depth, l, [y_rt, rt_states[gi]])

        merged = _merge(y_rw, y_ml, y_rt, w_brb, l, z_rg, Z_GATE)
        if x is None:
            xh = (None, None)
            for (row0, *_), xg in zip(groups, x_groups):
                xh = _proj_res(merged, w_outb, l, xg, g_post_mix[l], g_pre_x[l], row0=row0, prev=xh)
            x, h = xh
        else:
            x, h = _proj_res(merged, w_outb, l, x, g_post_mix[l], g_pre_x[l])

        for gi, (row0, n_seq, seq_len, pos0) in enumerate(groups):
            cfg = _group_cfg(n_seq, seq_len)
            if gi == 0:
                mk, mv, lyr_in = mem_k[l][None], mem_v[l][None], 0
            else:
                mk, mv, lyr_in = cache_mem_k, cache_mem_v, l
            x, h = _attn(h, x, mk, mv, lyr_in, w_q, w_o, l, g_post_x[l], g_pre_ff[l], row0, n_seq, seq_len,
                         cfg['at_bb'], cfg['at_lq'])

        if l + 1 < depth:
            x, h = _mlp(h, w_f1, w_f2, l, x, g_post_ff[l], g_pre_mix[l + 1])
        else:
            y_out = [_mlp(h, w_f1, w_f2, l, x, g_post_ff[l], g_pre_mix[0], row0=row0, n_rows=n_seq * seq_len)[0]
                     for row0, n_seq, seq_len, _ in groups]

    def states(gi, n_seq):
        c_new, n_new, m_new = ml_states[gi]
        s_rw = _state_from_scan(rw_states[gi], depth * n_seq).reshape(depth, n_seq, RW_HEADS, RW_HD, RW_HD)
        return (jnp.stack(shifts[gi]), s_rw, c_new, n_new, m_new.reshape(depth, n_seq, ML_HEADS), rt_states[gi])

    return (y_out[0].reshape(bp, lp, d), y_out[1].reshape(bs, ls, d), *states(0, bp),
            jnp.stack(mem_k), jnp.stack(mem_v), *states(1, bs))
```

```python
import functools
import math

import jax
import jax.numpy as jnp
from jax import lax
from jax.experimental import pallas as pl
from jax.experimental.pallas import tpu as pltpu

F32 = jnp.float32
BF16 = jnp.bfloat16

D_MODEL = 2048
MIX_W = D_MODEL // 2
RW_HD = 64
RW_HEADS = MIX_W // RW_HD
RW_LORA = (64, 64, 128)
RW_COLS = 3 * MIX_W + sum(RW_LORA)
RW_GN_EPS = 64e-5
ML_HEADS = 8
ML_HD = MIX_W // ML_HEADS
ML_GATE_CAP = 15.0
RT_HEADS = 4
RT_HD = MIX_W // RT_HEADS
ROPE_BASE = 10000.0
X_HEADS = 4
X_HD = 128
X_W = X_HEADS * X_HD
D_FF = 4 * D_MODEL
CHUNK = 64
EPS = 1e-6
PAST_LEN = 16384

IN_ML = RW_COLS
IN_MLG = IN_ML + 4 * MIX_W
IN_RT = IN_MLG + 2 * ML_HEADS
Z_GATE = 4 * MIX_W

LANES = 128
RW_VP = RW_HD // 2
RW_CHAINS = LANES // 2


def _cparams(*sem):
    return pltpu.CompilerParams(dimension_semantics=sem, vmem_limit_bytes=56 * 1024 * 1024)


def _carry_through(kernel_fn, n_in, prev, out_shapes=(), shared=()):
    prev = [jnp.zeros(out_shapes[i].shape, out_shapes[i].dtype) if p is None and i < len(shared) and shared[i]
            else p for i, p in enumerate(prev)]
    carried = [(i, p) for i, p in enumerate(prev) if p is not None]
    if not carried:
        return kernel_fn, [], [], {}

    def wrapped(*refs):
        return kernel_fn(*refs[:n_in], *refs[n_in + len(carried):])

    return (wrapped, [pl.BlockSpec(memory_space=pl.ANY)] * len(carried), [p for _, p in carried],
            {n_in + j: i for j, (i, _) in enumerate(carried)})


def _sigmoid(x):
    return 1.0 / (1.0 + jnp.exp(-x))


def _softplus(x):
    return jnp.maximum(x, 0.0) + jnp.log(1.0 + jnp.exp(-jnp.abs(x)))


def _bdot(a, b):
    return jnp.dot(a.astype(BF16), b.astype(BF16), preferred_element_type=F32)


def _split3(x):
    hi = x.astype(BF16)
    r1 = x - hi.astype(F32)
    mid = r1.astype(BF16)
    lo = (r1 - mid.astype(F32)).astype(BF16)
    return hi, mid, lo


def _dot_exact_lhs(m, x):
    mb = m.astype(BF16)
    hi, mid, lo = _split3(x)
    return (jnp.dot(mb, hi, preferred_element_type=F32) + jnp.dot(mb, mid, preferred_element_type=F32)
            + jnp.dot(mb, lo, preferred_element_type=F32))


def _dot_exact_rhs(x, m):
    mb = m.astype(BF16)
    hi, mid, lo = _split3(x)
    return (jnp.dot(hi, mb, preferred_element_type=F32) + jnp.dot(mid, mb, preferred_element_type=F32)
            + jnp.dot(lo, mb, preferred_element_type=F32))


def _segsum(x, seg):
    blk = 256
    ri = lax.broadcasted_iota(jnp.int32, (blk, blk), 0) // seg
    ci = lax.broadcasted_iota(jnp.int32, (blk, blk), 1) // seg
    ones = (ri == ci).astype(F32)
    parts = [_dot_exact_rhs(x[:, c:c + blk], ones) for c in range(0, x.shape[1], blk)]
    return parts[0] if len(parts) == 1 else jnp.concatenate(parts, axis=1)


def _rms_rows(x, g):
    return x * lax.rsqrt(jnp.mean(x * x, axis=-1, keepdims=True) + EPS) * g


def _rmsnorm_kernel(x_ref, g_ref, o_ref):
    o_ref[...] = _rms_rows(x_ref[...], g_ref[...]).astype(o_ref.dtype)


def _rmsnorm(x, g, tm=512, row0=0, total_rows=None, prev=None):
    m, d = x.shape
    rb0 = row0 // tm
    out_shape = jax.ShapeDtypeStruct((total_rows or m, d), BF16)
    kern, prev_specs, prev_args, aliases = _carry_through(_rmsnorm_kernel, 2, [prev], [out_shape],
                                                          [out_shape.shape[0] != m])
    return pl.pallas_call(
        kern,
        grid=(m // tm,),
        in_specs=[pl.BlockSpec((tm, d), lambda i: (i, 0)), pl.BlockSpec((1, d), lambda i: (0, 0))] + prev_specs,
        out_specs=pl.BlockSpec((tm, d), lambda i: (rb0 + i, 0)),
        out_shape=out_shape,
        input_output_aliases=aliases,
        compiler_params=_cparams("parallel"),
        name="rmsnorm",
    )(x, g.reshape(1, d), *prev_args)


def _mm_kernel(x_ref, w_ref, o_ref):
    o_ref[...] = jnp.dot(x_ref[...], w_ref[0], preferred_element_type=F32)


def _mm(x, w, lyr, tm, tn):
    m, k = x.shape
    n = w.shape[2]
    return pl.pallas_call(
        _mm_kernel,
        grid=(n // tn, m // tm),
        in_specs=[pl.BlockSpec((tm, k), lambda j, i: (i, 0)), pl.BlockSpec((1, k, tn), lambda j, i: (lyr, 0, j))],
        out_specs=pl.BlockSpec((tm, tn), lambda j, i: (i, j)),
        out_shape=jax.ShapeDtypeStruct((m, n), F32),
        compiler_params=_cparams("parallel", "parallel"),
        name="matmul",
    )(x, w)


def _merge_kernel(y0_ref, y1_ref, y2_ref, w_ref, g0_ref, g1_ref, g2_ref, o_ref):
    acc = _sigmoid(g0_ref[...]) * jnp.dot(y0_ref[...], w_ref[0, 0], preferred_element_type=F32)
    acc += _sigmoid(g1_ref[...]) * jnp.dot(y1_ref[...], w_ref[0, 1], preferred_element_type=F32)
    acc += _sigmoid(g2_ref[...]) * jnp.dot(y2_ref[...], w_ref[0, 2], preferred_element_type=F32)
    o_ref[...] = acc.astype(o_ref.dtype)


def _merge(y_rw, y_ml, y_rt, w_br, lyr, z, gate_col0, tm=512, tn=1024):
    m, kw = y_rw.shape
    d = w_br.shape[3]
    gb = gate_col0 // tn
    gspec = lambda c: pl.BlockSpec((tm, tn), lambda j, i: (i, gb + c * (d // tn) + j))
    yspec = pl.BlockSpec((tm, kw), lambda j, i: (i, 0))
    return pl.pallas_call(
        _merge_kernel,
        grid=(d // tn, m // tm),
        in_specs=[yspec, yspec, yspec, pl.BlockSpec((1, 3, kw, tn), lambda j, i: (lyr, 0, 0, j)),
                  gspec(0), gspec(1), gspec(2)],
        out_specs=pl.BlockSpec((tm, tn), lambda j, i: (i, j)),
        out_shape=jax.ShapeDtypeStruct((m, d), BF16),
        compiler_params=_cparams("parallel", "parallel"),
        name="branch_merge",
    )(y_rw, y_ml, y_rt, w_br, z, z, z)


def _residual_epilogue(acc, x_ref, gpost_ref, gnext_ref, xo_ref, ho_ref):
    xn = x_ref[...] + _rms_rows(acc, gpost_ref[...])
    xo_ref[...] = xn
    ho_ref[...] = _rms_rows(xn, gnext_ref[...]).astype(ho_ref.dtype)


def _proj_res_kernel(a_ref, w_ref, x_ref, gpost_ref, gnext_ref, xo_ref, ho_ref):
    acc = jnp.dot(a_ref[...], w_ref[0], preferred_element_type=F32)
    _residual_epilogue(acc, x_ref, gpost_ref, gnext_ref, xo_ref, ho_ref)


def _proj_res(a, w, lyr, x, g_post, g_next, row0=0, prev=(None, None), tm=512):
    total, k = a.shape
    m, d = x.shape
    rb0 = row0 // tm
    at = lambda i: (rb0 + i, 0)
    fixed = lambda i: (0, 0)
    out_shape = [jax.ShapeDtypeStruct((total, d), F32), jax.ShapeDtypeStruct((total, d), BF16)]
    kern, prev_specs, prev_args, aliases = _carry_through(_proj_res_kernel, 5, list(prev), out_shape,
                                                          [m != total] * 2)
    return pl.pallas_call(
        kern,
        grid=(m // tm,),
        in_specs=[pl.BlockSpec((tm, k), at), pl.BlockSpec((1, k, d), lambda i: (lyr, 0, 0)),
                  pl.BlockSpec((tm, d), lambda i: (i, 0)),
                  pl.BlockSpec((1, d), fixed), pl.BlockSpec((1, d), fixed)] + prev_specs,
        out_specs=[pl.BlockSpec((tm, d), at), pl.BlockSpec((tm, d), at)],
        out_shape=out_shape,
        input_output_aliases=aliases,
        compiler_params=_cparams("parallel"),
        name="out_proj_residual",
    )(a, w, x, g_post.reshape(1, d), g_next.reshape(1, d), *prev_args)


def _mlp_kernel(h_ref, w1_ref, w2_ref, x_ref, gpost_ref, gnext_ref, xo_ref, ho_ref, acc_ref):
    k = pl.program_id(1)

    @pl.when(k == 0)
    def _():
        acc_ref[...] = jnp.zeros_like(acc_ref)

    u = jnp.dot(h_ref[...], w1_ref[0], preferred_element_type=F32)
    u = jnp.square(jnp.maximum(u, 0.0)).astype(BF16)
    acc_ref[...] += jnp.dot(u, w2_ref[0], preferred_element_type=F32)

    @pl.when(k == pl.num_programs(1) - 1)
    def _():
        _residual_epilogue(acc_ref[...], x_ref, gpost_ref, gnext_ref, xo_ref, ho_ref)


def _mlp(h, w1, w2, lyr, x, g_post, g_next, row0=0, n_rows=None, tm=512, tf=1024):
    d = h.shape[1]
    m = n_rows or h.shape[0]
    f = w1.shape[2]
    rb0 = row0 // tm
    at = lambda i, k: (rb0 + i, 0)
    row = lambda i, k: (i, 0)
    fixed = lambda i, k: (0, 0)
    return pl.pallas_call(
        _mlp_kernel,
        grid=(m // tm, f // tf),
        in_specs=[pl.BlockSpec((tm, d), at), pl.BlockSpec((1, d, tf), lambda i, k: (lyr, 0, k)),
                  pl.BlockSpec((1, tf, d), lambda i, k: (lyr, k, 0)), pl.BlockSpec((tm, d), at),
                  pl.BlockSpec((1, d), fixed), pl.BlockSpec((1, d), fixed)],
        out_specs=[pl.BlockSpec((tm, d), row), pl.BlockSpec((tm, d), row)],
        out_shape=[jax.ShapeDtypeStruct((m, d), F32), jax.ShapeDtypeStruct((m, d), BF16)],
        scratch_shapes=[pltpu.VMEM((tm, d), F32)],
        compiler_params=_cparams("parallel", "arbitrary"),
        name="relu2_mlp",
    )(h, w1, w2, x, g_post.reshape(1, d), g_next.reshape(1, d))


def _head_rows(ref, hd):
    n_mem = ref.shape[2] // X_HEADS
    return ref[0, :, pl.ds(hd, n_mem, stride=X_HEADS), :]


def _attn_kernel(bb, lq, h_ref, wq_ref, mk_ref, mv_ref, wo_ref, x_ref, gpost_ref, gnext_ref, xo_ref, ho_ref):
    q = jnp.dot(h_ref[...], wq_ref[0], preferred_element_type=F32)
    outs = []
    for hd in range(X_HEADS):
        sl = slice(hd * X_HD, (hd + 1) * X_HD)
        q3 = q[:, sl].reshape(bb, lq, X_HD).astype(BF16)
        k3 = _head_rows(mk_ref, hd).astype(BF16)
        v3 = _head_rows(mv_ref, hd).astype(BF16)
        s = jnp.einsum('bqd,bmd->bqm', q3, k3, preferred_element_type=F32) * (X_HD ** -0.5)
        s = s - jnp.max(s, axis=-1, keepdims=True)
        e = jnp.exp(s)
        p = (e / jnp.sum(e, axis=-1, keepdims=True)).astype(BF16)
        o3 = jnp.einsum('bqm,bmd->bqd', p, v3, preferred_element_type=F32)
        outs.append(o3.reshape(bb * lq, X_HD))
    o = jnp.concatenate(outs, axis=1).astype(BF16)
    acc = jnp.dot(o, wo_ref[0], preferred_element_type=F32)
    _residual_epilogue(acc, x_ref, gpost_ref, gnext_ref, xo_ref, ho_ref)


def _attn_seq_kernel(h_ref, wq_ref, mk_ref, mv_ref, wo_ref, x_ref, gpost_ref, gnext_ref, xo_ref, ho_ref):
    heads = range(X_HEADS)
    q = jnp.dot(h_ref[...], wq_ref[0], preferred_element_type=F32)
    stack = lambda f: jnp.stack([f(hd) for hd in heads]).astype(BF16)
    q3 = stack(lambda hd: q[:, hd * X_HD:(hd + 1) * X_HD])
    k3 = stack(lambda hd: _head_rows(mk_ref, hd)[0])
    v3 = stack(lambda hd: _head_rows(mv_ref, hd)[0])
    s = jnp.einsum('hqd,hmd->hqm', q3, k3, preferred_element_type=F32) * (X_HD ** -0.5)
    s = s - jnp.max(s, axis=-1, keepdims=True)
    e = jnp.exp(s)
    p = (e / jnp.sum(e, axis=-1, keepdims=True)).astype(BF16)
    o3 = jnp.einsum('hqm,hmd->hqd', p, v3, preferred_element_type=F32)
    o = jnp.concatenate([o3[hd] for hd in heads], axis=1).astype(BF16)
    acc = jnp.dot(o, wo_ref[0], preferred_element_type=F32)
    _residual_epilogue(acc, x_ref, gpost_ref, gnext_ref, xo_ref, ho_ref)


def _attn(h, x, mk, mv, lyr, wq, wo, wl, g_post, g_next, row0, n_seq, seq_len, bb, lq):
    t, d = h.shape
    rows = bb * lq
    nl = seq_len // lq
    n_mem = mk.shape[2]
    rb0 = row0 // rows
    rowmap = lambda b, l: (rb0 + b * nl + l, 0)
    fixed = lambda b, l: (0, 0)
    body = _attn_seq_kernel if bb == 1 else functools.partial(_attn_kernel, bb, lq)
    kv_specs = [pl.BlockSpec((1, bb, n_mem * X_HEADS, X_HD), lambda b, l: (lyr, b, 0, 0))] * 2
    kv_args = [a.reshape(a.shape[0], a.shape[1], n_mem * X_HEADS, X_HD) for a in (mk, mv)]
    x_at = 3 + len(kv_specs)
    return pl.pallas_call(
        body,
        grid=(n_seq // bb, nl),
        in_specs=[pl.BlockSpec((rows, d), rowmap), pl.BlockSpec((1, d, X_W), lambda b, l: (wl, 0, 0))] + kv_specs
                 + [pl.BlockSpec((1, X_W, d), lambda b, l: (wl, 0, 0)), pl.BlockSpec((rows, d), rowmap),
                    pl.BlockSpec((1, d), fixed), pl.BlockSpec((1, d), fixed)],
        out_specs=[pl.BlockSpec((rows, d), rowmap), pl.BlockSpec((rows, d), rowmap)],
        out_shape=[jax.ShapeDtypeStruct((t, d), F32), jax.ShapeDtypeStruct((t, d), BF16)],
        input_output_aliases={x_at: 0, 0: 1},
        compiler_params=_cparams("parallel", "parallel"),
        name="mem_cross_attention",
    )(h, wq, *kv_args, wo, x, g_post.reshape(1, d), g_next.reshape(1, d))


def _rwkv_prep_kernel(bb, lt, u_ref, s0_ref, mu_ref, w0_ref, wup_ref, a0_ref, aup_ref, gup_ref, kk_ref, ka_ref,
                      rk_ref, r_o, w_o, k_o, kk_o, b_o, v_o, g_o, bonus_o, shift_o, carry_ref):
    l = pl.program_id(1)
    rows = bb * lt
    u = u_ref[...]
    shift_o[...] = u[rows - 1:rows, :].reshape(1, 1, RW_COLS) if bb == 1 else _last_of_seq(u, bb, lt)
    if bb == 1:
        first = jnp.where(l == 0, s0_ref[0], carry_ref[...])
        first_rows = jnp.broadcast_to(first, (rows, RW_COLS))
    else:
        first_rows = jnp.broadcast_to(s0_ref[...], (bb, lt, RW_COLS)).reshape(rows, RW_COLS)
    rowid = lax.broadcasted_iota(jnp.int32, (rows, 1), 0) % lt
    prev = jnp.where(rowid == 0, first_rows, pltpu.roll(u, 1, 0))
    if bb == 1:
        carry_ref[...] = u[rows - 1:rows, :]
    z = u + (prev - u) * mu_ref[...]
    r = z[:, 0:MIX_W]
    k = z[:, MIX_W:2 * MIX_W]
    v = z[:, 2 * MIX_W:3 * MIX_W]
    o = 3 * MIX_W
    wd = z[:, o:o + RW_LORA[0]]
    ad = z[:, o + RW_LORA[0]:o + RW_LORA[0] + RW_LORA[1]]
    gd = z[:, o + RW_LORA[0] + RW_LORA[1]:RW_COLS]
    w_log = -_softplus(-(w0_ref[...] + _bdot(jnp.tanh(wd), wup_ref[...]))) - 0.5
    decay = jnp.exp(-jnp.exp(w_log))
    a = _sigmoid(a0_ref[...] + _bdot(ad, aup_ref[...]))
    g = _bdot(_sigmoid(gd), gup_ref[...])
    kk = k * kk_ref[...]
    kk = kk / jnp.maximum(jnp.sqrt(_segsum(kk * kk, RW_HD)), 1e-12)
    k = k * (1.0 + (a - 1.0) * ka_ref[...])
    r_o[...] = r
    w_o[...] = decay
    k_o[...] = k
    kk_o[...] = kk
    b_o[...] = kk * a
    v_o[...] = v
    g_o[...] = g
    bonus_o[...] = _segsum(r * k * rk_ref[...], RW_HD) * v


def _rwkv_prep(z, shift0, p, row0, n_seq, seq_len, bb, lt):
    rows = bb * lt
    nl = seq_len // lt
    assert bb == 1 or nl == 1
    rb0 = row0 // rows
    m = n_seq * seq_len
    fixed = lambda b, l: (0, 0)
    outmap = lambda b, l: (b * nl + l, 0)
    vec = lambda width: pl.BlockSpec((1, width), fixed)
    ospec = pl.BlockSpec((rows, MIX_W), outmap)
    oshape = jax.ShapeDtypeStruct((m, MIX_W), F32)
    return pl.pallas_call(
        functools.partial(_rwkv_prep_kernel, bb, lt),
        grid=(n_seq // bb, nl),
        in_specs=[pl.BlockSpec((rows, RW_COLS), lambda b, l: (rb0 + b * nl + l, 0)),
                  pl.BlockSpec((bb, 1, RW_COLS), lambda b, l: (b, 0, 0)),
                  vec(RW_COLS), vec(MIX_W), pl.BlockSpec((RW_LORA[0], MIX_W), fixed), vec(MIX_W),
                  pl.BlockSpec((RW_LORA[1], MIX_W), fixed), pl.BlockSpec((RW_LORA[2], MIX_W), fixed),
                  vec(MIX_W), vec(MIX_W), vec(MIX_W)],
        out_specs=[ospec] * 8 + [pl.BlockSpec((bb, 1, RW_COLS), lambda b, l: (b, 0, 0))],
        out_shape=[oshape] * 8 + [jax.ShapeDtypeStruct((n_seq, 1, RW_COLS), F32)],
        scratch_shapes=[pltpu.VMEM((1, RW_COLS), F32)],
        compiler_params=_cparams("parallel", "arbitrary"),
        name="rwkv_prep",
    )(z, shift0, p['mu'], p['w0'], p['w_up'], p['a0'], p['a_up'], p['g_up'], p['k_k'], p['k_a'], p['r_k'])


def _rwkv_scan_kernel(tc, r_ref, w_ref, k_ref, kk_ref, b_ref, v_ref, s0_ref, y_ref, s_ref):
    @pl.when(pl.program_id(1) == 0)
    def _():
        s_ref[...] = s0_ref[...]

    y_ref[0, :, RW_VP:, :] = jnp.zeros((tc, RW_VR - RW_VP, LANES), F32)
    keys = lambda ref, t: ref[0, t, :RW_HD, :]

    def step(t, carry):
        for vp in range(RW_VP):
            s = s_ref[0, vp]
            sa = jnp.sum(s * keys(kk_ref, t), axis=0, keepdims=True)
            sn = s * keys(w_ref, t) - sa * keys(b_ref, t) + v_ref[0, t, vp:vp + 1, :] * keys(k_ref, t)
            s_ref[0, vp] = sn
            y_ref[0, t, vp:vp + 1, :] = jnp.sum(sn * keys(r_ref, t), axis=0, keepdims=True)
        return carry

    lax.fori_loop(0, tc, step, 0)


def _rwkv_scan(r, w, k, kk, b, v, s0, lyr_in, tc, depth, lyr, s_prev):
    g, l = r.shape[0], r.shape[1]
    kspec = pl.BlockSpec((1, tc, RW_KR, LANES), lambda i, c: (i, c, 0, 0))
    vspec = pl.BlockSpec((1, tc, RW_VR, LANES), lambda i, c: (i, c, 0, 0))
    sspec = lambda at: pl.BlockSpec((1, RW_VP, RW_HD, LANES), lambda i, c: (at * g + i, 0, 0, 0))
    out_shape = [jax.ShapeDtypeStruct((g, l, RW_VR, LANES), F32),
                 jax.ShapeDtypeStruct((depth * g, RW_VP, RW_HD, LANES), F32)]
    kern, prev_specs, prev_args, aliases = _carry_through(functools.partial(_rwkv_scan_kernel, tc), 7,
                                                          [None, s_prev], out_shape, [False, depth > 1])
    return pl.pallas_call(
        kern,
        grid=(g, l // tc),
        in_specs=[kspec] * 5 + [vspec, sspec(lyr_in)] + prev_specs,
        out_specs=[vspec, sspec(lyr)],
        out_shape=out_shape,
        input_output_aliases=aliases,
        compiler_params=_cparams("parallel", "arbitrary"),
        name="rwkv_scan",
    )(r, w, k, kk, b, v, s0, *prev_args)


def _rwkv_post_kernel(y_ref, g_ref, bonus_ref, gng_ref, gnb_ref, o_ref):
    y = y_ref[...]
    mean = _segsum(y, RW_HD) * (1.0 / RW_HD)
    yc = y - mean
    var = _segsum(yc * yc, RW_HD) * (1.0 / RW_HD)
    yn = yc * lax.rsqrt(var + RW_GN_EPS) * gng_ref[...] + gnb_ref[...]
    o_ref[...] = ((yn + bonus_ref[...]) * g_ref[...]).astype(o_ref.dtype)


def _rwkv_post(y, g, bonus, gn_g, gn_b, row0, total_rows, prev, tm=256):
    m = y.shape[0]
    rb0 = row0 // tm
    fixed = lambda i: (0, 0)
    spec = pl.BlockSpec((tm, MIX_W), lambda i: (i, 0))
    out_shape = jax.ShapeDtypeStruct((total_rows, MIX_W), BF16)
    kern, prev_specs, prev_args, aliases = _carry_through(_rwkv_post_kernel, 5, [prev], [out_shape],
                                                          [m != total_rows])
    return pl.pallas_call(
        kern,
        grid=(m // tm,),
        in_specs=[spec, spec, spec, pl.BlockSpec((1, MIX_W), fixed), pl.BlockSpec((1, MIX_W), fixed)] + prev_specs,
        out_specs=pl.BlockSpec((tm, MIX_W), lambda i: (rb0 + i, 0)),
        out_shape=out_shape,
        input_output_aliases=aliases,
        compiler_params=_cparams("parallel"),
        name="rwkv_post",
    )(y, g, bonus, gn_g, gn_b, *prev_args)


RW_NB = RW_CHAINS // RW_HEADS
RW_TT = LANES


RW_KR = RW_HD + 8
RW_VR = RW_VP + 8


def _natural_to_chain_rows(x_ref, a_ref):
    for bl in range(RW_NB):
        xt = x_ref[bl].T
        for h in range(RW_HEADS):
            c = bl * RW_HEADS + h
            a_ref[c * RW_KR:c * RW_KR + RW_HD, :] = xt[h * RW_HD:(h + 1) * RW_HD]


def _keys_to_scan_kernel(x_ref, o_ref, a_ref):
    _natural_to_chain_rows(x_ref, a_ref)
    o_ref[...] = jnp.zeros_like(o_ref)
    for k in range(RW_HD):
        tile = a_ref[pl.ds(k, RW_CHAINS, stride=RW_KR), :]
        o_ref[0, pl.ds(k, RW_TT, stride=RW_KR), :] = jnp.concatenate([tile, tile], axis=0).T


def _vals_to_scan_kernel(x_ref, o_ref, a_ref):
    _natural_to_chain_rows(x_ref, a_ref)
    o_ref[...] = jnp.zeros_like(o_ref)
    for vp in range(RW_VP):
        halves = [a_ref[pl.ds(vh * RW_VP + vp, RW_CHAINS, stride=RW_KR), :] for vh in range(2)]
        o_ref[0, pl.ds(vp, RW_TT, stride=RW_VR), :] = jnp.concatenate(halves, axis=0).T


def _vals_from_scan_kernel(y_ref, o_ref, a_ref):
    for vp in range(RW_VP):
        tile = y_ref[0, pl.ds(vp, RW_TT, stride=RW_VR), :].T
        for vh in range(2):
            a_ref[pl.ds(vh * RW_VP + vp, RW_CHAINS, stride=RW_KR), :] = tile[vh * RW_CHAINS:(vh + 1) * RW_CHAINS]
    for bl in range(RW_NB):
        rows = [a_ref[c * RW_KR:c * RW_KR + RW_HD, :] for c in range(bl * RW_HEADS, (bl + 1) * RW_HEADS)]
        o_ref[bl] = jnp.concatenate(rows, axis=0).T


def _scan_relayout(kernel_fn, x, n_seq, seq_len, rows_per_t, to_scan, name):
    g, nl = n_seq // RW_NB, seq_len // RW_TT
    nat_spec = pl.BlockSpec((RW_NB, RW_TT, MIX_W), lambda i, l: (i, l, 0))
    scan_spec = pl.BlockSpec((1, RW_TT * rows_per_t, LANES), lambda i, l: (i, l, 0))
    nat_shape = jax.ShapeDtypeStruct((n_seq, seq_len, MIX_W), F32)
    scan_shape = jax.ShapeDtypeStruct((g, seq_len * rows_per_t, LANES), F32)
    x = x.reshape(n_seq, seq_len, MIX_W) if to_scan else x.reshape(g, seq_len * rows_per_t, LANES)
    out = pl.pallas_call(
        kernel_fn,
        grid=(g, nl),
        in_specs=[nat_spec if to_scan else scan_spec],
        out_specs=scan_spec if to_scan else nat_spec,
        out_shape=scan_shape if to_scan else nat_shape,
        scratch_shapes=[pltpu.VMEM((RW_CHAINS * RW_KR, RW_TT), F32)],
        compiler_params=_cparams("parallel", "parallel"),
        name=name,
    )(x)
    return out.reshape(g, seq_len, rows_per_t, LANES) if to_scan else out.reshape(n_seq * seq_len, MIX_W)


def _pad_rows(a, rows):
    return jnp.pad(a, ((0, 0), (0, 0), (0, rows - a.shape[2]), (0, 0)))


def _to_scan_keys(a, n_seq, seq_len):
    if seq_len % RW_TT == 0:
        return _scan_relayout(_keys_to_scan_kernel, a, n_seq, seq_len, RW_KR, True, "rwkv_keys_to_scan")
    g = n_seq // RW_NB
    a = a.reshape(g, RW_NB, seq_len, RW_HEADS, RW_HD).transpose(0, 2, 4, 1, 3).reshape(g, seq_len, RW_HD, RW_CHAINS)
    return _pad_rows(jnp.concatenate([a, a], axis=-1), RW_KR)


def _to_scan_vals(a, n_seq, seq_len):
    if seq_len % RW_TT == 0:
        return _scan_relayout(_vals_to_scan_kernel, a, n_seq, seq_len, RW_VR, True, "rwkv_vals_to_scan")
    g = n_seq // RW_NB
    a = a.reshape(g, RW_NB, seq_len, RW_HEADS, 2, RW_VP).transpose(0, 2, 5, 4, 1, 3)
    return _pad_rows(a.reshape(g, seq_len, RW_VP, LANES), RW_VR)


def _from_scan_vals(y, n_seq, seq_len):
    if seq_len % RW_TT == 0:
        return _scan_relayout(_vals_from_scan_kernel, y, n_seq, seq_len, RW_VR, False, "rwkv_vals_from_scan")
    g = n_seq // RW_NB
    y = y[:, :, :RW_VP].reshape(g, seq_len, RW_VP, 2, RW_NB, RW_HEADS).transpose(0, 4, 1, 5, 3, 2)
    return y.reshape(n_seq * seq_len, MIX_W)


def _state_to_scan(s, n_seq):
    nb = RW_CHAINS // RW_HEADS
    g = n_seq // nb
    s = s.reshape(g, nb, RW_HEADS, 2, RW_VP, RW_HD).transpose(0, 4, 5, 3, 1, 2)
    return s.reshape(g, RW_VP, RW_HD, LANES)


def _state_from_scan(s, n_seq):
    nb = RW_CHAINS // RW_HEADS
    g = n_seq // nb
    s = s.reshape(g, RW_VP, RW_HD, 2, nb, RW_HEADS).transpose(0, 4, 5, 3, 1, 2)
    return s.reshape(n_seq, RW_HEADS, RW_HD, RW_HD)


def _rwkv_branch(z, shift0, s0, lyr_in, p, row0, n_seq, seq_len, bb, lt, tc, depth, lyr, prev):
    r, w, k, kk, b, v, g, bonus, shift = _rwkv_prep(z, shift0, p, row0, n_seq, seq_len, bb, lt)
    keys = [_to_scan_keys(a, n_seq, seq_len) for a in (r, w, k, kk, b)]
    y, s_new = _rwkv_scan(*keys, _to_scan_vals(v, n_seq, seq_len), s0, lyr_in, tc, depth, lyr, prev[1])
    y = _from_scan_vals(y, n_seq, seq_len)
    y = _rwkv_post(y, g, bonus, p['gn_g'], p['gn_b'], row0, z.shape[0], prev[0])
    return y, s_new, shift.reshape(n_seq, RW_COLS)


def _seq_masks(rows, lc):
    ri = lax.broadcasted_iota(jnp.int32, (rows, rows), 0)
    ci = lax.broadcasted_iota(jnp.int32, (rows, rows), 1)
    same = (ri // lc) == (ci // lc)
    return ri, ci, same


def _rows_from_seq(x, bb, lc):
    return jnp.broadcast_to(x, (bb, lc, x.shape[-1])).reshape(bb * lc, x.shape[-1])


def _last_of_seq(x, bb, lc):
    x3 = x.reshape(bb, lc, x.shape[-1])
    pick = (lax.broadcasted_iota(jnp.int32, (1, lc, 1), 1) == lc - 1).astype(F32)
    return jnp.sum(x3 * pick, axis=1, keepdims=True)


def _transpose_rows(x):
    rows = x.shape[0]
    if rows < LANES:
        x = jnp.concatenate([x, jnp.zeros((LANES - rows, LANES), x.dtype)], axis=0)
    return x.T[:, :rows]


def _mlstm_kernel(bb, lc, q_ref, k_ref, v_ref, o_ref, gt_ref, gb_ref, ng_ref, c0_ref, n0_ref, m0_ref,
                  y_ref, c_ref, n_ref, m_ref):
    rows = bb * lc

    @pl.when(pl.program_id(1) == 0)
    def _():
        c_ref[...] = c0_ref[...]
        n_ref[...] = n0_ref[...]
        m_ref[...] = m0_ref[...]

    gt = ML_GATE_CAP * jnp.tanh((gt_ref[...] + gb_ref[...]) * (1.0 / ML_GATE_CAP))
    lane = lax.broadcasted_iota(jnp.int32, (rows, LANES), 1)
    ri, ci, same = _seq_masks(rows, lc)
    causal = same & (ci <= ri)
    cum = _dot_exact_lhs(causal.astype(F32), -_softplus(-gt))
    gt_t = _transpose_rows(gt)
    cum_t = _transpose_rows(cum)
    for hd in range(ML_HEADS):
        sl = slice(hd * ML_HD, (hd + 1) * ML_HD)
        c0 = c_ref[0, :, hd]
        n0 = n_ref[0, :, hd:hd + 1, :]
        m0 = m_ref[0, :, :, hd:hd + 1]
        q = q_ref[:, sl]
        k = k_ref[:, sl] * (ML_HD ** -0.5)
        v = v_ref[:, sl]
        bcol = cum[:, ML_HEADS + hd:ML_HEADS + hd + 1]
        igcol = gt[:, hd:hd + 1]
        rowb = gt_t[hd:hd + 1, :] - cum_t[ML_HEADS + hd:ML_HEADS + hd + 1, :]
        dlog = jnp.where(causal, bcol + rowb, -jnp.inf)
        m_inter = bcol + _rows_from_seq(m0, bb, lc)
        m_t = jnp.maximum(m_inter, jnp.max(dlog, axis=-1, keepdims=True))
        qk = lax.dot_general(q.astype(BF16), k.astype(BF16), (((1,), (1,)), ((), ())),
                             preferred_element_type=F32)
        wts = jnp.exp(dlog - m_t) * qk
        s_inter = jnp.exp(m_inter - m_t)
        q3 = q.reshape(bb, lc, ML_HD).astype(BF16)
        qc = jnp.einsum('btd,bde->bte', q3, c0.astype(BF16), preferred_element_type=F32).reshape(rows, ML_HD)
        num = _bdot(wts, v) + s_inter * qc
        qn = jnp.sum(q * _rows_from_seq(n0, bb, lc), axis=-1, keepdims=True)
        den = jnp.sum(wts, axis=-1, keepdims=True) + s_inter * qn
        hh = num / jnp.maximum(jnp.abs(den), jnp.exp(-m_t))
        hh = hh * lax.rsqrt(jnp.mean(hh * hh, axis=-1, keepdims=True) + EPS) * ng_ref[:, sl]
        y_ref[:, sl] = (_sigmoid(o_ref[:, sl]) * hh).astype(y_ref.dtype)

        stat = jnp.where(lane == 0, bcol, jnp.where(lane == 1, m_t, 0.0))
        last = _last_of_seq(stat, bb, lc)
        last_rows = _rows_from_seq(last, bb, lc)
        b_last, m_new = last_rows[:, 0:1], last_rows[:, 1:2]
        w_end = jnp.exp(b_last - bcol + igcol - m_new)
        f_end = jnp.exp(last[:, :, 0:1] + m0 - last[:, :, 1:2])
        kw3 = (k * w_end).reshape(bb, lc, ML_HD)
        upd = jnp.einsum('bsd,bse->bde', kw3.astype(BF16), v.reshape(bb, lc, ML_HD).astype(BF16),
                         preferred_element_type=F32)
        c_ref[0, :, hd] = f_end * c0 + upd
        n_ref[0, :, hd:hd + 1, :] = f_end * n0 + jnp.sum(kw3, axis=1, keepdims=True)
        m_ref[0, :, :, hd:hd + 1] = last[:, :, 1:2]


def _mlstm_seq_kernel(lc, q_ref, k_ref, v_ref, o_ref, gt_ref, gb_ref, ng_ref, c0_ref, n0_ref, m0_ref,
                      y_ref, c_ref, n_ref, m_ref):
    heads = range(ML_HEADS)

    @pl.when(pl.program_id(1) == 0)
    def _():
        c_ref[...] = c0_ref[...]
        n_ref[...] = n0_ref[...]
        m_ref[...] = m0_ref[...]

    gt = ML_GATE_CAP * jnp.tanh((gt_ref[...] + gb_ref[...]) * (1.0 / ML_GATE_CAP))
    ri, ci, _ = _seq_masks(lc, lc)
    causal = ci <= ri
    cum = _dot_exact_lhs(causal.astype(F32), -_softplus(-gt))
    gt_t = _transpose_rows(gt)
    cum_t = _transpose_rows(cum)
    sl = lambda hd: slice(hd * ML_HD, (hd + 1) * ML_HD)
    stack = lambda f: jnp.stack([f(hd) for hd in heads])
    q = stack(lambda hd: q_ref[:, sl(hd)])
    k = stack(lambda hd: k_ref[:, sl(hd)]) * (ML_HD ** -0.5)
    v = stack(lambda hd: v_ref[:, sl(hd)])
    bcol = stack(lambda hd: cum[:, ML_HEADS + hd:ML_HEADS + hd + 1])
    igcol = stack(lambda hd: gt[:, hd:hd + 1])
    rowb = stack(lambda hd: gt_t[hd:hd + 1, :] - cum_t[ML_HEADS + hd:ML_HEADS + hd + 1, :])
    c0 = c_ref[0, 0]
    n0 = stack(lambda hd: n_ref[0, 0, hd:hd + 1, :])
    m0 = stack(lambda hd: m_ref[0, 0, :, hd:hd + 1])
    qb, kb, vb = q.astype(BF16), k.astype(BF16), v.astype(BF16)

    dlog = jnp.where(causal[None], bcol + rowb, -jnp.inf)
    m_inter = bcol + m0
    m_t = jnp.maximum(m_inter, jnp.max(dlog, axis=-1, keepdims=True))
    qk = jnp.einsum('hqd,hkd->hqk', qb, kb, preferred_element_type=F32)
    wts = jnp.exp(dlog - m_t) * qk
    s_inter = jnp.exp(m_inter - m_t)
    qc = jnp.einsum('hqd,hde->hqe', qb, c0.astype(BF16), preferred_element_type=F32)
    num = jnp.einsum('hqk,hke->hqe', wts.astype(BF16), vb, preferred_element_type=F32) + s_inter * qc
    den = jnp.sum(wts, axis=-1, keepdims=True) + s_inter * jnp.sum(q * n0, axis=-1, keepdims=True)
    hh = num / jnp.maximum(jnp.abs(den), jnp.exp(-m_t))
    hh = hh * lax.rsqrt(jnp.mean(hh * hh, axis=-1, keepdims=True) + EPS)
    for hd in heads:
        y_ref[:, sl(hd)] = (_sigmoid(o_ref[:, sl(hd)]) * hh[hd] * ng_ref[:, sl(hd)]).astype(y_ref.dtype)

    b_last, m_new = bcol[:, lc - 1:lc, :], m_t[:, lc - 1:lc, :]
    w_end = jnp.exp(b_last - bcol + igcol - m_new)
    f_end = jnp.exp(b_last + m0 - m_new)
    kw = k * w_end
    upd = jnp.einsum('hsd,hse->hde', kw.astype(BF16), vb, preferred_element_type=F32)
    c_ref[0, 0] = f_end * c0 + upd
    n_new = f_end * n0 + jnp.sum(kw, axis=1, keepdims=True)
    for hd in heads:
        n_ref[0, 0, hd:hd + 1, :] = n_new[hd]
        m_ref[0, 0, :, hd:hd + 1] = m_new[hd]


def _mlstm_branch(z, z_gates, init, lyr_in, gate_bias, norm_g, row0, n_seq, seq_len, bb, lc, depth, lyr, prev):
    rows = bb * lc
    nc = seq_len // lc
    rb0 = row0 // rows
    rowmap = lambda part: (lambda b, c: (rb0 + b * nc + c, part))
    zspec = lambda part: pl.BlockSpec((rows, MIX_W), rowmap(part))
    state_specs = lambda at: [pl.BlockSpec((1, bb, ML_HEADS, ML_HD, ML_HD), lambda b, c: (at, b, 0, 0, 0)),
                              pl.BlockSpec((1, bb, ML_HEADS, ML_HD), lambda b, c: (at, b, 0, 0)),
                              pl.BlockSpec((1, bb, 1, ML_HEADS), lambda b, c: (at, b, 0, 0))]
    body = functools.partial(_mlstm_seq_kernel, lc) if bb == 1 else functools.partial(_mlstm_kernel, bb, lc)
    out_shape = [jax.ShapeDtypeStruct((z.shape[0], MIX_W), BF16),
                 jax.ShapeDtypeStruct((depth, n_seq, ML_HEADS, ML_HD, ML_HD), F32),
                 jax.ShapeDtypeStruct((depth, n_seq, ML_HEADS, ML_HD), F32),
                 jax.ShapeDtypeStruct((depth, n_seq, 1, ML_HEADS), F32)]
    kern, prev_specs, prev_args, aliases = _carry_through(
        body, 10, prev, out_shape, [n_seq * seq_len != z.shape[0]] + [depth > 1] * 3)
    return pl.pallas_call(
        kern,
        grid=(n_seq // bb, nc),
        in_specs=[zspec(0), zspec(1), zspec(2), zspec(3),
                  pl.BlockSpec((rows, LANES), lambda b, c: (rb0 + b * nc + c, 0)),
                  pl.BlockSpec((1, LANES), lambda b, c: (0, 0)), pl.BlockSpec((1, MIX_W), lambda b, c: (0, 0))]
                 + state_specs(lyr_in) + prev_specs,
        out_specs=[pl.BlockSpec((rows, MIX_W), lambda b, c: (rb0 + b * nc + c, 0))] + state_specs(lyr),
        out_shape=out_shape,
        input_output_aliases=aliases,
        compiler_params=_cparams("parallel", "arbitrary"),
        name="mlstm_chunk",
    )(z, z, z, z, z_gates, gate_bias, norm_g, *init, *prev_args)


def _rope_rows(x, cos, sin):
    half = RT_HD // 2
    x1, x2 = x[:, :half], x[:, half:]
    return jnp.concatenate([x1 * cos - x2 * sin, x1 * sin + x2 * cos], axis=1)


def _ret_kernel(bb, lc, q_ref, k_ref, v_ref, g_ref, cs_ref, dec_ref, cd_ref, s0_ref, y_ref, s_ref):
    rows = bb * lc

    @pl.when(pl.program_id(1) == 0)
    def _():
        s_ref[...] = s0_ref[...]

    cos, sin = cs_ref[:, :RT_HD // 2], cs_ref[:, RT_HD // 2:]
    for hd in range(RT_HEADS):
        sl = slice(hd * RT_HD, (hd + 1) * RT_HD)
        s0 = s_ref[0, :, hd]
        q = _rope_rows(q_ref[:, sl], cos, sin)
        k = _rope_rows(k_ref[:, sl], cos, sin) * (RT_HD ** -0.5)
        v = v_ref[:, sl]
        cd = cd_ref[hd]
        qk = lax.dot_general(q.astype(BF16), k.astype(BF16), (((1,), (1,)), ((), ())),
                             preferred_element_type=F32)
        inner = _bdot(qk * dec_ref[hd], v)
        q3 = q.reshape(bb, lc, RT_HD).astype(BF16)
        cross = jnp.einsum('btd,bde->bte', q3, s0.astype(BF16), preferred_element_type=F32).reshape(rows, RT_HD)
        y = inner + cross * cd[:, 0:1]
        y = y * lax.rsqrt(jnp.mean(y * y, axis=-1, keepdims=True) + EPS)
        gt = g_ref[:, sl]
        y_ref[:, sl] = (gt * _sigmoid(gt) * y).astype(y_ref.dtype)
        kd = (k * cd[:, 1:2]).reshape(bb, lc, RT_HD).astype(BF16)
        upd = jnp.einsum('bsd,bse->bde', kd, v.reshape(bb, lc, RT_HD).astype(BF16), preferred_element_type=F32)
        s_ref[0, :, hd] = cd[0:1, 2:3] * s0 + upd


def _ret_seq_kernel(lc, q_ref, k_ref, v_ref, g_ref, cs_ref, dec_ref, cd_ref, s0_ref, y_ref, s_ref):
    heads = range(RT_HEADS)

    @pl.when(pl.program_id(1) == 0)
    def _():
        s_ref[...] = s0_ref[...]

    cos, sin = cs_ref[:, :RT_HD // 2], cs_ref[:, RT_HD // 2:]
    sl = lambda hd: slice(hd * RT_HD, (hd + 1) * RT_HD)
    stack = lambda f: jnp.stack([f(hd) for hd in heads])
    q = stack(lambda hd: _rope_rows(q_ref[:, sl(hd)], cos, sin))
    k = stack(lambda hd: _rope_rows(k_ref[:, sl(hd)], cos, sin)) * (RT_HD ** -0.5)
    vb = stack(lambda hd: v_ref[:, sl(hd)]).astype(BF16)
    s0 = s_ref[0, 0]
    cd = cd_ref[...]
    qb = q.astype(BF16)
    qk = jnp.einsum('hqd,hkd->hqk', qb, k.astype(BF16), preferred_element_type=F32)
    inner = jnp.einsum('hqk,hke->hqe', (qk * dec_ref[...]).astype(BF16), vb, preferred_element_type=F32)
    cross = jnp.einsum('hqd,hde->hqe', qb, s0.astype(BF16), preferred_element_type=F32)
    y = inner + cross * cd[:, :, 0:1]
    y = y * lax.rsqrt(jnp.mean(y * y, axis=-1, keepdims=True) + EPS)
    for hd in heads:
        gt = g_ref[:, sl(hd)]
        y_ref[:, sl(hd)] = (gt * _sigmoid(gt) * y[hd]).astype(y_ref.dtype)
    kd = (k * cd[:, :, 1:2]).astype(BF16)
    upd = jnp.einsum('hsd,hse->hde', kd, vb, preferred_element_type=F32)
    s_ref[0, 0] = cd[:, 0:1, 2:3] * s0 + upd


def _ret_branch(z, s0, lyr_in, row0, n_seq, seq_len, bb, lc, pos0, depth, lyr, prev):
    rows = bb * lc
    nc = seq_len // lc
    rb0 = row0 // rows
    half = RT_HD // 2
    inv = ROPE_BASE ** (-jnp.arange(half, dtype=F32) / half)
    pos = jnp.broadcast_to((jnp.arange(nc * lc, dtype=F32) + float(pos0)).reshape(nc, 1, lc), (nc, bb, lc))
    ang = pos.reshape(nc * rows)[:, None] * inv[None, :]
    cs = jnp.concatenate([jnp.cos(ang), jnp.sin(ang)], axis=1)
    log_g = jnp.log(1.0 - jnp.exp(jnp.linspace(math.log(1.0 / 32), math.log(1.0 / 512), RT_HEADS)))
    idx = jnp.arange(rows)
    t = (idx % lc).astype(F32)
    diff = t[:, None] - t[None, :]
    same = (idx[:, None] // lc) == (idx[None, :] // lc)
    dec = jnp.where(same & (diff >= 0), jnp.exp(log_g[:, None, None] * jnp.maximum(diff, 0.0)), 0.0)
    cd = jnp.zeros((RT_HEADS, rows, LANES), F32)
    cd = cd.at[:, :, 0].set(jnp.exp(log_g[:, None] * (t[None, :] + 1.0)))
    cd = cd.at[:, :, 1].set(jnp.exp(log_g[:, None] * (lc - 1.0 - t)[None, :]))
    cd = cd.at[:, :, 2].set(jnp.broadcast_to(jnp.exp(log_g * lc)[:, None], (RT_HEADS, rows)))
    rowmap = lambda part: (lambda b, c: (rb0 + b * nc + c, part))
    zspec = lambda part: pl.BlockSpec((rows, MIX_W), rowmap(part))
    sspec = lambda at: pl.BlockSpec((1, bb, RT_HEADS, RT_HD, RT_HD), lambda b, c: (at, b, 0, 0, 0))
    body = functools.partial(_ret_seq_kernel, lc) if bb == 1 else functools.partial(_ret_kernel, bb, lc)
    out_shape = [jax.ShapeDtypeStruct((z.shape[0], MIX_W), BF16),
                 jax.ShapeDtypeStruct((depth, n_seq, RT_HEADS, RT_HD, RT_HD), F32)]
    kern, prev_specs, prev_args, aliases = _carry_through(body, 8, prev, out_shape,
                                                          [n_seq * seq_len != z.shape[0], depth > 1])
    return pl.pallas_call(
        kern,
        grid=(n_seq // bb, nc),
        in_specs=[zspec(0), zspec(1), zspec(2), zspec(3),
                  pl.BlockSpec((rows, RT_HD), lambda b, c: (c, 0)),
                  pl.BlockSpec((RT_HEADS, rows, rows), lambda b, c: (0, 0, 0)),
                  pl.BlockSpec((RT_HEADS, rows, LANES), lambda b, c: (0, 0, 0)),
                  sspec(lyr_in)] + prev_specs,
        out_specs=[pl.BlockSpec((rows, MIX_W), lambda b, c: (rb0 + b * nc + c, 0)), sspec(lyr)],
        out_shape=out_shape,
        input_output_aliases=aliases,
        compiler_params=_cparams("parallel", "arbitrary"),
        name="retention_chunk",
    )(z, z, z, z, cs, dec, cd, s0, *prev_args)


def _mm_wt_kernel(x_ref, wt_ref, o_ref, wb_ref):
    @pl.when(pl.program_id(1) == 0)
    def _():
        tn = wb_ref.shape[1]
        for c in range(0, tn, LANES):
            wb_ref[:, c:c + LANES] = wt_ref[0, c:c + LANES, :].T.astype(wb_ref.dtype)

    o_ref[...] = jnp.dot(x_ref[...], wb_ref[...], preferred_element_type=F32)


def _mm_wt(x, w_t, lyr, col0, n, tm, tn):
    m, k = x.shape
    return pl.pallas_call(
        _mm_wt_kernel,
        grid=(n // tn, m // tm),
        in_specs=[pl.BlockSpec((tm, k), lambda j, i: (i, 0)),
                  pl.BlockSpec((pl.Element(1), pl.Element(tn), pl.Element(k)),
                               lambda j, i: (lyr, pl.multiple_of(col0 + j * tn, 8), 0))],
        out_specs=pl.BlockSpec((tm, tn), lambda j, i: (i, j)),
        out_shape=jax.ShapeDtypeStruct((m, n), F32),
        scratch_shapes=[pltpu.VMEM((k, tn), BF16)],
        compiler_params=_cparams("parallel", "arbitrary"),
        name="matmul_wt",
    )(x, w_t)


def _group_cfg(n_seq, seq_len):
    if seq_len % CHUNK == 0:
        return dict(rw_bb=1, rw_lt=256, rw_tc=64, ml_bb=1, rt_bb=1, ch_lc=CHUNK, at_bb=1, at_lq=512)
    return dict(rw_bb=32, rw_lt=seq_len, rw_tc=seq_len, ml_bb=8, rt_bb=8, ch_lc=seq_len, at_bb=8, at_lq=seq_len)


def kernel(x_prompt, x_sample, mem_prompt, state_rwkv_shift, state_rwkv, state_mlstm_c, state_mlstm_n,
           state_mlstm_m, state_ret, cache_mem_k, cache_mem_v, g_pre_mix, g_post_mix, g_pre_x, g_post_x,
           g_pre_ff, g_post_ff, g_mem, w_in, rw_mu, rw_w0, rw_w_up, rw_a0, rw_a_up, rw_g_up, rw_k_k, rw_k_a,
           rw_r_k, rw_gn_g, rw_gn_b, ml_i_b, ml_f_b, ml_norm_g, w_br, w_out, x_wq, x_wkv, x_wo, ff_w1, ff_w2):
    bp, lp, d = x_prompt.shape
    bs, ls, _ = x_sample.shape
    depth = w_in.shape[0]
    n_mem = mem_prompt.shape[1]
    tp, ts = bp * lp, bs * ls
    groups = [(0, bp, lp, 0), (tp, bs, ls, PAST_LEN)]

    x_groups = [x_prompt.reshape(tp, d), x_sample.reshape(ts, d)]
    mem = mem_prompt.reshape(bp * n_mem, d)
    x = h = None
    for (row0, *_), xg in zip(groups, x_groups):
        h = _rmsnorm(xg, g_pre_mix[0], row0=row0, total_rows=tp + ts, prev=h)

    zeros = lambda *s: jnp.zeros(s, F32)
    rw_scan0 = _state_to_scan(state_rwkv.reshape(depth * bs, RW_HEADS, RW_HD, RW_HD), depth * bs)
    inits = [
        lambda l: (zeros(bp, 1, RW_COLS), zeros(bp // RW_NB, RW_VP, RW_HD, LANES),
                   (zeros(1, bp, ML_HEADS, ML_HD, ML_HD), zeros(1, bp, ML_HEADS, ML_HD), zeros(1, bp, 1, ML_HEADS)),
                   zeros(1, bp, RT_HEADS, RT_HD, RT_HD), 0),
        lambda l: (state_rwkv_shift[l].reshape(bs, 1, RW_COLS), rw_scan0,
                   (state_mlstm_c, state_mlstm_n, state_mlstm_m.reshape(depth, bs, 1, ML_HEADS)), state_ret, l),
    ]
    shifts = [[], []]
    rw_states = [None, None]
    ml_states = [[None] * 3, [None] * 3]
    rt_states = [None, None]
    mem_k, mem_v = [], []
    w_in_t = jnp.swapaxes(w_in, 1, 2)
    w_kv, w_brb, w_outb, w_q, w_o, w_f1, w_f2 = (w.astype(BF16) for w in (x_wkv, w_br, w_out, x_wq, x_wo,
                                                                         ff_w1, ff_w2))
    for l in range(depth):
        kv = _mm(_rmsnorm(mem, g_mem[l], tm=256), w_kv, l, tm=256, tn=512)
        mem_k.append(kv[:, :X_W].reshape(bp, n_mem, X_HEADS, X_HD))
        mem_v.append(kv[:, X_W:].reshape(bp, n_mem, X_HEADS, X_HD))

        z_rw = _mm_wt(h, w_in_t, l, 0, RW_COLS, tm=512, tn=RW_COLS // 2)
        z_ml = _mm_wt(h, w_in_t, l, IN_ML, 4 * MIX_W, tm=1024, tn=1024)
        z_g = _mm_wt(h, w_in_t, l, IN_MLG, LANES, tm=1024, tn=LANES)
        z_rg = _mm_wt(h, w_in_t, l, IN_RT, w_in.shape[2] - IN_RT, tm=1024, tn=1280)

        rw_p = dict(mu=rw_mu[l].reshape(1, -1), w0=rw_w0[l].reshape(1, -1), w_up=rw_w_up[l].astype(BF16),
                    a0=rw_a0[l].reshape(1, -1), a_up=rw_a_up[l].astype(BF16), g_up=rw_g_up[l].astype(BF16),
                    k_k=rw_k_k[l].reshape(1, -1), k_a=rw_k_a[l].reshape(1, -1), r_k=rw_r_k[l].reshape(1, -1),
                    gn_g=rw_gn_g[l].reshape(1, -1), gn_b=rw_gn_b[l].reshape(1, -1))
        gate_bias = jnp.concatenate([ml_i_b[l], ml_f_b[l], zeros(LANES - 2 * ML_HEADS)]).reshape(1, LANES)
        norm_g = ml_norm_g[l].reshape(1, MIX_W)

        y_rw = y_ml = y_rt = None
        for gi, (row0, n_seq, seq_len, pos0) in enumerate(groups):
            cfg = _group_cfg(n_seq, seq_len)
            shift0, s_rw0, ml0, rt0, lyr_in = inits[gi](l)
            y_rw, rw_states[gi], shift = _rwkv_branch(z_rw, shift0, s_rw0, lyr_in, rw_p, row0, n_seq, seq_len,
                                                      cfg['rw_bb'], cfg['rw_lt'], cfg['rw_tc'], depth, l,
                                                      (y_rw, rw_states[gi]))
            shifts[gi].append(shift)
            y_ml, *ml_states[gi] = _mlstm_branch(z_ml, z_g, ml0, lyr_in, gate_bias, norm_g, row0, n_seq, seq_len,
                                                 cfg['ml_bb'], cfg['ch_lc'], depth, l, [y_ml] + ml_states[gi])
            y_rt, rt_states[gi] = _ret_branch(z_rg, rt0, lyr_in, row0, n_seq, seq_len, cfg['rt_bb'],
                                              cfg['ch_lc'], pos0, depth, l, [y_rt, rt_states[gi]])

        merged = _merge(y_rw, y_ml, y_rt, w_brb, l, z_rg, Z_GATE)
        if x is None:
            xh = (None, None)
            for (row0, *_), xg in zip(groups, x_groups):
                xh = _proj_res(merged, w_outb, l, xg, g_post_mix[l], g_pre_x[l], row0=row0, prev=xh)
            x, h = xh
        else:
            x, h = _proj_res(merged, w_outb, l, x, g_post_mix[l], g_pre_x[l])

        for gi, (row0, n_seq, seq_len, pos0) in enumerate(groups):
            cfg = _group_cfg(n_seq, seq_len)
            if gi == 0:
                mk, mv, lyr_in = mem_k[l][None], mem_v[l][None], 0
            else:
                mk, mv, lyr_in = cache_mem_k, cache_mem_v, l
            x, h = _attn(h, x, mk, mv, lyr_in, w_q, w_o, l, g_post_x[l], g_pre_ff[l], row0, n_seq, seq_len,
                         cfg['at_bb'], cfg['at_lq'])

        if l + 1 < depth:
            x, h = _mlp(h, w_f1, w_f2, l, x, g_post_ff[l], g_pre_mix[l + 1])
        else:
            y_out = [_mlp(h, w_f1, w_f2, l, x, g_post_ff[l], g_pre_mix[0], row0=row0, n_rows=n_seq * seq_len)[0]
                     for row0, n_seq, seq_len, _ in groups]

    def states(gi, n_seq):
        c_new, n_new, m_new = ml_states[gi]
        s_rw = _state_from_scan(rw_states[gi], depth * n_seq).reshape(depth, n_seq, RW_HEADS, RW_HD, RW_HD)
        return (jnp.stack(shifts[gi]), s_rw, c_new, n_new, m_new.reshape(depth, n_seq, ML_HEADS), rt_states[gi])

    return (y_out[0].reshape(bp, lp, d), y_out[1].reshape(bs, ls, d), *states(0, bp),
            jnp.stack(mem_k), jnp.stack(mem_v), *states(1, bs))
```

```python
import functools
import math

import jax
import jax.numpy as jnp
from jax import lax
from jax.experimental import pallas as pl
from jax.experimental.pallas import tpu as pltpu

F32 = jnp.float32
BF16 = jnp.bfloat16

D_MODEL = 2048
MIX_W = D_MODEL // 2
RW_HD = 64
RW_HEADS = MIX_W // RW_HD
RW_LORA = (64, 64, 128)
RW_COLS = 3 * MIX_W + sum(RW_LORA)
RW_GN_EPS = 64e-5
ML_HEADS = 8
ML_HD = MIX_W // ML_HEADS
ML_GATE_CAP = 15.0
RT_HEADS = 4
RT_HD = MIX_W // RT_HEADS
ROPE_BASE = 10000.0
X_HEADS = 4
X_HD = 128
X_W = X_HEADS * X_HD
D_FF = 4 * D_MODEL
CHUNK = 64
EPS = 1e-6
PAST_LEN = 16384

Z_MLW = 4 * (D_MODEL // 2) + 128
IN_ML = RW_COLS
IN_MLG = IN_ML + 4 * MIX_W
IN_RT = IN_MLG + 2 * ML_HEADS
Z_GATE = 4 * MIX_W

LANES = 128
RW_VP = RW_HD // 2
RW_CHAINS = LANES // 2


def _cparams(*sem):
    return pltpu.CompilerParams(dimension_semantics=sem, vmem_limit_bytes=56 * 1024 * 1024)


def _carry_through(kernel_fn, n_in, prev, out_shapes=(), shared=()):
    prev = [jnp.zeros(out_shapes[i].shape, out_shapes[i].dtype) if p is None and i < len(shared) and shared[i]
            else p for i, p in enumerate(prev)]
    carried = [(i, p) for i, p in enumerate(prev) if p is not None]
    if not carried:
        return kernel_fn, [], [], {}

    def wrapped(*refs):
        return kernel_fn(*refs[:n_in], *refs[n_in + len(carried):])

    return (wrapped, [pl.BlockSpec(memory_space=pl.ANY)] * len(carried), [p for _, p in carried],
            {n_in + j: i for j, (i, _) in enumerate(carried)})


def _sigmoid(x):
    return 1.0 / (1.0 + jnp.exp(-x))


def _softplus(x):
    return jnp.maximum(x, 0.0) + jnp.log(1.0 + jnp.exp(-jnp.abs(x)))


def _bdot(a, b):
    return jnp.dot(a.astype(BF16), b.astype(BF16), preferred_element_type=F32)


def _split3(x):
    hi = x.astype(BF16)
    r1 = x - hi.astype(F32)
    mid = r1.astype(BF16)
    lo = (r1 - mid.astype(F32)).astype(BF16)
    return hi, mid, lo


def _dot_exact_lhs(m, x):
    mb = m.astype(BF16)
    hi, mid, lo = _split3(x)
    return (jnp.dot(mb, hi, preferred_element_type=F32) + jnp.dot(mb, mid, preferred_element_type=F32)
            + jnp.dot(mb, lo, preferred_element_type=F32))


def _dot_exact_rhs(x, m):
    mb = m.astype(BF16)
    hi, mid, lo = _split3(x)
    return (jnp.dot(hi, mb, preferred_element_type=F32) + jnp.dot(mid, mb, preferred_element_type=F32)
            + jnp.dot(lo, mb, preferred_element_type=F32))


def _segsum(x, seg):
    blk = 256
    ri = lax.broadcasted_iota(jnp.int32, (blk, blk), 0) // seg
    ci = lax.broadcasted_iota(jnp.int32, (blk, blk), 1) // seg
    ones = (ri == ci).astype(F32)
    parts = [_dot_exact_rhs(x[:, c:c + blk], ones) for c in range(0, x.shape[1], blk)]
    return parts[0] if len(parts) == 1 else jnp.concatenate(parts, axis=1)


def _rms_rows(x, g):
    return x * lax.rsqrt(jnp.mean(x * x, axis=-1, keepdims=True) + EPS) * g


def _rmsnorm_kernel(x_ref, g_ref, o_ref):
    o_ref[...] = _rms_rows(x_ref[...], g_ref[...]).astype(o_ref.dtype)


def _rmsnorm(x, g, tm=512, row0=0, total_rows=None, prev=None):
    m, d = x.shape
    rb0 = row0 // tm
    out_shape = jax.ShapeDtypeStruct((total_rows or m, d), BF16)
    kern, prev_specs, prev_args, aliases = _carry_through(_rmsnorm_kernel, 2, [prev], [out_shape],
                                                          [out_shape.shape[0] != m])
    return pl.pallas_call(
        kern,
        grid=(m // tm,),
        in_specs=[pl.BlockSpec((tm, d), lambda i: (i, 0)), pl.BlockSpec((1, d), lambda i: (0, 0))] + prev_specs,
        out_specs=pl.BlockSpec((tm, d), lambda i: (rb0 + i, 0)),
        out_shape=out_shape,
        input_output_aliases=aliases,
        compiler_params=_cparams("parallel"),
        name="rmsnorm",
    )(x, g.reshape(1, d), *prev_args)


def _mm_kernel(x_ref, w_ref, o_ref):
    o_ref[...] = jnp.dot(x_ref[...], w_ref[0], preferred_element_type=F32)


def _mm(x, w, lyr, tm, tn):
    m, k = x.shape
    n = w.shape[2]
    return pl.pallas_call(
        _mm_kernel,
        grid=(n // tn, m // tm),
        in_specs=[pl.BlockSpec((tm, k), lambda j, i: (i, 0)), pl.BlockSpec((1, k, tn), lambda j, i: (lyr, 0, j))],
        out_specs=pl.BlockSpec((tm, tn), lambda j, i: (i, j)),
        out_shape=jax.ShapeDtypeStruct((m, n), F32),
        compiler_params=_cparams("parallel", "parallel"),
        name="matmul",
    )(x, w)


def _merge_kernel(y0_ref, y1_ref, y2_ref, w_ref, g0_ref, g1_ref, g2_ref, o_ref):
    acc = _sigmoid(g0_ref[...]) * jnp.dot(y0_ref[...], w_ref[0, 0], preferred_element_type=F32)
    acc += _sigmoid(g1_ref[...]) * jnp.dot(y1_ref[...], w_ref[0, 1], preferred_element_type=F32)
    acc += _sigmoid(g2_ref[...]) * jnp.dot(y2_ref[...], w_ref[0, 2], preferred_element_type=F32)
    o_ref[...] = acc.astype(o_ref.dtype)


def _merge(y_rw, y_ml, y_rt, w_br, lyr, z, gate_col0, tm=512, tn=1024):
    m, kw = y_rw.shape
    d = w_br.shape[3]
    gb = gate_col0 // tn
    gspec = lambda c: pl.BlockSpec((tm, tn), lambda j, i: (i, gb + c * (d // tn) + j))
    yspec = pl.BlockSpec((tm, kw), lambda j, i: (i, 0))
    return pl.pallas_call(
        _merge_kernel,
        grid=(d // tn, m // tm),
        in_specs=[yspec, yspec, yspec, pl.BlockSpec((1, 3, kw, tn), lambda j, i: (lyr, 0, 0, j)),
                  gspec(0), gspec(1), gspec(2)],
        out_specs=pl.BlockSpec((tm, tn), lambda j, i: (i, j)),
        out_shape=jax.ShapeDtypeStruct((m, d), BF16),
        compiler_params=_cparams("parallel", "parallel"),
        name="branch_merge",
    )(y_rw, y_ml, y_rt, w_br, z, z, z)


def _residual_epilogue(acc, x_ref, gpost_ref, gnext_ref, xo_ref, ho_ref):
    xn = x_ref[...] + _rms_rows(acc, gpost_ref[...])
    xo_ref[...] = xn
    ho_ref[...] = _rms_rows(xn, gnext_ref[...]).astype(ho_ref.dtype)


def _proj_res_kernel(a_ref, w_ref, x_ref, gpost_ref, gnext_ref, xo_ref, ho_ref):
    acc = jnp.dot(a_ref[...], w_ref[0], preferred_element_type=F32)
    _residual_epilogue(acc, x_ref, gpost_ref, gnext_ref, xo_ref, ho_ref)


def _proj_res(a, w, lyr, x, g_post, g_next, row0=0, prev=(None, None), tm=512):
    total, k = a.shape
    m, d = x.shape
    rb0 = row0 // tm
    at = lambda i: (rb0 + i, 0)
    fixed = lambda i: (0, 0)
    out_shape = [jax.ShapeDtypeStruct((total, d), F32), jax.ShapeDtypeStruct((total, d), BF16)]
    kern, prev_specs, prev_args, aliases = _carry_through(_proj_res_kernel, 5, list(prev), out_shape,
                                                          [m != total] * 2)
    return pl.pallas_call(
        kern,
        grid=(m // tm,),
        in_specs=[pl.BlockSpec((tm, k), at), pl.BlockSpec((1, k, d), lambda i: (lyr, 0, 0)),
                  pl.BlockSpec((tm, d), lambda i: (i, 0)),
                  pl.BlockSpec((1, d), fixed), pl.BlockSpec((1, d), fixed)] + prev_specs,
        out_specs=[pl.BlockSpec((tm, d), at), pl.BlockSpec((tm, d), at)],
        out_shape=out_shape,
        input_output_aliases=aliases,
        compiler_params=_cparams("parallel"),
        name="out_proj_residual",
    )(a, w, x, g_post.reshape(1, d), g_next.reshape(1, d), *prev_args)


def _mlp_kernel(h_ref, w1_ref, w2_ref, x_ref, gpost_ref, gnext_ref, xo_ref, ho_ref, acc_ref):
    k = pl.program_id(1)

    @pl.when(k == 0)
    def _():
        acc_ref[...] = jnp.zeros_like(acc_ref)

    u = jnp.dot(h_ref[...], w1_ref[0], preferred_element_type=F32)
    u = jnp.square(jnp.maximum(u, 0.0)).astype(BF16)
    acc_ref[...] += jnp.dot(u, w2_ref[0], preferred_element_type=F32)

    @pl.when(k == pl.num_programs(1) - 1)
    def _():
        _residual_epilogue(acc_ref[...], x_ref, gpost_ref, gnext_ref, xo_ref, ho_ref)


def _mlp(h, w1, w2, lyr, x, g_post, g_next, row0=0, n_rows=None, tm=512, tf=1024):
    d = h.shape[1]
    m = n_rows or h.shape[0]
    f = w1.shape[2]
    rb0 = row0 // tm
    at = lambda i, k: (rb0 + i, 0)
    row = lambda i, k: (i, 0)
    fixed = lambda i, k: (0, 0)
    return pl.pallas_call(
        _mlp_kernel,
        grid=(m // tm, f // tf),
        in_specs=[pl.BlockSpec((tm, d), at), pl.BlockSpec((1, d, tf), lambda i, k: (lyr, 0, k)),
                  pl.BlockSpec((1, tf, d), lambda i, k: (lyr, k, 0)), pl.BlockSpec((tm, d), at),
                  pl.BlockSpec((1, d), fixed), pl.BlockSpec((1, d), fixed)],
        out_specs=[pl.BlockSpec((tm, d), row), pl.BlockSpec((tm, d), row)],
        out_shape=[jax.ShapeDtypeStruct((m, d), F32), jax.ShapeDtypeStruct((m, d), BF16)],
        scratch_shapes=[pltpu.VMEM((tm, d), F32)],
        compiler_params=_cparams("parallel", "arbitrary"),
        name="relu2_mlp",
    )(h, w1, w2, x, g_post.reshape(1, d), g_next.reshape(1, d))


def _head_rows(ref, hd):
    n_mem = ref.shape[2] // X_HEADS
    return ref[0, :, pl.ds(hd, n_mem, stride=X_HEADS), :]


def _attn_kernel(bb, lq, h_ref, wq_ref, mk_ref, mv_ref, wo_ref, x_ref, gpost_ref, gnext_ref, xo_ref, ho_ref):
    q = jnp.dot(h_ref[...], wq_ref[0], preferred_element_type=F32)
    outs = []
    for hd in range(X_HEADS):
        sl = slice(hd * X_HD, (hd + 1) * X_HD)
        q3 = q[:, sl].reshape(bb, lq, X_HD).astype(BF16)
        k3 = _head_rows(mk_ref, hd).astype(BF16)
        v3 = _head_rows(mv_ref, hd).astype(BF16)
        s = jnp.einsum('bqd,bmd->bqm', q3, k3, preferred_element_type=F32) * (X_HD ** -0.5)
        s = s - jnp.max(s, axis=-1, keepdims=True)
        e = jnp.exp(s)
        p = (e / jnp.sum(e, axis=-1, keepdims=True)).astype(BF16)
        o3 = jnp.einsum('bqm,bmd->bqd', p, v3, preferred_element_type=F32)
        outs.append(o3.reshape(bb * lq, X_HD))
    o = jnp.concatenate(outs, axis=1).astype(BF16)
    acc = jnp.dot(o, wo_ref[0], preferred_element_type=F32)
    _residual_epilogue(acc, x_ref, gpost_ref, gnext_ref, xo_ref, ho_ref)


def _attn_seq_kernel(h_ref, wq_ref, mk_ref, mv_ref, wo_ref, x_ref, gpost_ref, gnext_ref, xo_ref, ho_ref):
    heads = range(X_HEADS)
    q = jnp.dot(h_ref[...], wq_ref[0], preferred_element_type=F32)
    stack = lambda f: jnp.stack([f(hd) for hd in heads]).astype(BF16)
    q3 = stack(lambda hd: q[:, hd * X_HD:(hd + 1) * X_HD])
    k3 = stack(lambda hd: _head_rows(mk_ref, hd)[0])
    v3 = stack(lambda hd: _head_rows(mv_ref, hd)[0])
    s = jnp.einsum('hqd,hmd->hqm', q3, k3, preferred_element_type=F32) * (X_HD ** -0.5)
    s = s - jnp.max(s, axis=-1, keepdims=True)
    e = jnp.exp(s)
    p = (e / jnp.sum(e, axis=-1, keepdims=True)).astype(BF16)
    o3 = jnp.einsum('hqm,hmd->hqd', p, v3, preferred_element_type=F32)
    o = jnp.concatenate([o3[hd] for hd in heads], axis=1).astype(BF16)
    acc = jnp.dot(o, wo_ref[0], preferred_element_type=F32)
    _residual_epilogue(acc, x_ref, gpost_ref, gnext_ref, xo_ref, ho_ref)


def _attn(h, x, mk, mv, lyr, wq, wo, wl, g_post, g_next, row0, n_seq, seq_len, bb, lq):
    t, d = h.shape
    rows = bb * lq
    nl = seq_len // lq
    n_mem = mk.shape[2]
    rb0 = row0 // rows
    rowmap = lambda b, l: (rb0 + b * nl + l, 0)
    fixed = lambda b, l: (0, 0)
    body = _attn_seq_kernel if bb == 1 else functools.partial(_attn_kernel, bb, lq)
    kv_specs = [pl.BlockSpec((1, bb, n_mem * X_HEADS, X_HD), lambda b, l: (lyr, b, 0, 0))] * 2
    kv_args = [a.reshape(a.shape[0], a.shape[1], n_mem * X_HEADS, X_HD) for a in (mk, mv)]
    x_at = 3 + len(kv_specs)
    return pl.pallas_call(
        body,
        grid=(n_seq // bb, nl),
        in_specs=[pl.BlockSpec((rows, d), rowmap), pl.BlockSpec((1, d, X_W), lambda b, l: (wl, 0, 0))] + kv_specs
                 + [pl.BlockSpec((1, X_W, d), lambda b, l: (wl, 0, 0)), pl.BlockSpec((rows, d), rowmap),
                    pl.BlockSpec((1, d), fixed), pl.BlockSpec((1, d), fixed)],
        out_specs=[pl.BlockSpec((rows, d), rowmap), pl.BlockSpec((rows, d), rowmap)],
        out_shape=[jax.ShapeDtypeStruct((t, d), F32), jax.ShapeDtypeStruct((t, d), BF16)],
        input_output_aliases={x_at: 0, 0: 1},
        compiler_params=_cparams("parallel", "parallel"),
        name="mem_cross_attention",
    )(h, wq, *kv_args, wo, x, g_post.reshape(1, d), g_next.reshape(1, d))


def _rwkv_prep_kernel(bb, lt, u_ref, s0_ref, mu_ref, w0_ref, wup_ref, a0_ref, aup_ref, gup_ref, kk_ref, ka_ref,
                      rk_ref, r_o, w_o, k_o, kk_o, b_o, v_o, g_o, bonus_o, shift_o, carry_ref):
    l = pl.program_id(1)
    rows = bb * lt
    u = u_ref[...]
    shift_o[...] = u[rows - 1:rows, :].reshape(1, 1, RW_COLS) if bb == 1 else _last_of_seq(u, bb, lt)
    if bb == 1:
        first = jnp.where(l == 0, s0_ref[0], carry_ref[...])
        first_rows = jnp.broadcast_to(first, (rows, RW_COLS))
    else:
        first_rows = jnp.broadcast_to(s0_ref[...], (bb, lt, RW_COLS)).reshape(rows, RW_COLS)
    rowid = lax.broadcasted_iota(jnp.int32, (rows, 1), 0) % lt
    prev = jnp.where(rowid == 0, first_rows, pltpu.roll(u, 1, 0))
    if bb == 1:
        carry_ref[...] = u[rows - 1:rows, :]
    z = u + (prev - u) * mu_ref[...]
    r = z[:, 0:MIX_W]
    k = z[:, MIX_W:2 * MIX_W]
    v = z[:, 2 * MIX_W:3 * MIX_W]
    o = 3 * MIX_W
    wd = z[:, o:o + RW_LORA[0]]
    ad = z[:, o + RW_LORA[0]:o + RW_LORA[0] + RW_LORA[1]]
    gd = z[:, o + RW_LORA[0] + RW_LORA[1]:RW_COLS]
    w_log = -_softplus(-(w0_ref[...] + _bdot(jnp.tanh(wd), wup_ref[...]))) - 0.5
    decay = jnp.exp(-jnp.exp(w_log))
    a = _sigmoid(a0_ref[...] + _bdot(ad, aup_ref[...]))
    g = _bdot(_sigmoid(gd), gup_ref[...])
    kk = k * kk_ref[...]
    kk = kk / jnp.maximum(jnp.sqrt(_segsum(kk * kk, RW_HD)), 1e-12)
    k = k * (1.0 + (a - 1.0) * ka_ref[...])
    r_o[...] = r
    w_o[...] = decay
    k_o[...] = k
    kk_o[...] = kk
    b_o[...] = kk * a
    v_o[...] = v
    g_o[...] = g
    bonus_o[...] = _segsum(r * k * rk_ref[...], RW_HD) * v


def _rwkv_prep(z, shift0, p, row0, n_seq, seq_len, bb, lt):
    rows = bb * lt
    nl = seq_len // lt
    assert bb == 1 or nl == 1
    rb0 = row0 // rows
    m = n_seq * seq_len
    fixed = lambda b, l: (0, 0)
    outmap = lambda b, l: (b * nl + l, 0)
    vec = lambda width: pl.BlockSpec((1, width), fixed)
    ospec = pl.BlockSpec((rows, MIX_W), outmap)
    oshape = jax.ShapeDtypeStruct((m, MIX_W), F32)
    return pl.pallas_call(
        functools.partial(_rwkv_prep_kernel, bb, lt),
        grid=(n_seq // bb, nl),
        in_specs=[pl.BlockSpec((rows, RW_COLS), lambda b, l: (rb0 + b * nl + l, 0)),
                  pl.BlockSpec((bb, 1, RW_COLS), lambda b, l: (b, 0, 0)),
                  vec(RW_COLS), vec(MIX_W), pl.BlockSpec((RW_LORA[0], MIX_W), fixed), vec(MIX_W),
                  pl.BlockSpec((RW_LORA[1], MIX_W), fixed), pl.BlockSpec((RW_LORA[2], MIX_W), fixed),
                  vec(MIX_W), vec(MIX_W), vec(MIX_W)],
        out_specs=[ospec] * 8 + [pl.BlockSpec((bb, 1, RW_COLS), lambda b, l: (b, 0, 0))],
        out_shape=[oshape] * 8 + [jax.ShapeDtypeStruct((n_seq, 1, RW_COLS), F32)],
        scratch_shapes=[pltpu.VMEM((1, RW_COLS), F32)],
        compiler_params=_cparams("parallel", "arbitrary"),
        name="rwkv_prep",
    )(z, shift0, p['mu'], p['w0'], p['w_up'], p['a0'], p['a_up'], p['g_up'], p['k_k'], p['k_a'], p['r_k'])


def _rwkv_scan_kernel(tc, r_ref, w_ref, k_ref, kk_ref, b_ref, v_ref, s0_ref, y_ref, s_ref):
    @pl.when(pl.program_id(1) == 0)
    def _():
        s_ref[...] = s0_ref[...]

    y_ref[0, :, RW_VP:, :] = jnp.zeros((tc, RW_VR - RW_VP, LANES), F32)
    keys = lambda ref, t: ref[0, t, :RW_HD, :]

    def step(t, carry):
        for vp in range(RW_VP):
            s = s_ref[0, vp]
            sa = jnp.sum(s * keys(kk_ref, t), axis=0, keepdims=True)
            sn = s * keys(w_ref, t) - sa * keys(b_ref, t) + v_ref[0, t, vp:vp + 1, :] * keys(k_ref, t)
            s_ref[0, vp] = sn
            y_ref[0, t, vp:vp + 1, :] = jnp.sum(sn * keys(r_ref, t), axis=0, keepdims=True)
        return carry

    lax.fori_loop(0, tc, step, 0)


def _rwkv_scan(r, w, k, kk, b, v, s0, lyr_in, tc, depth, lyr, s_prev):
    g, l = r.shape[0], r.shape[1]
    kspec = pl.BlockSpec((1, tc, RW_KR, LANES), lambda i, c: (i, c, 0, 0))
    vspec = pl.BlockSpec((1, tc, RW_VR, LANES), lambda i, c: (i, c, 0, 0))
    sspec = lambda at: pl.BlockSpec((1, RW_VP, RW_HD, LANES), lambda i, c: (at * g + i, 0, 0, 0))
    out_shape = [jax.ShapeDtypeStruct((g, l, RW_VR, LANES), F32),
                 jax.ShapeDtypeStruct((depth * g, RW_VP, RW_HD, LANES), F32)]
    kern, prev_specs, prev_args, aliases = _carry_through(functools.partial(_rwkv_scan_kernel, tc), 7,
                                                          [None, s_prev], out_shape, [False, depth > 1])
    return pl.pallas_call(
        kern,
        grid=(g, l // tc),
        in_specs=[kspec] * 5 + [vspec, sspec(lyr_in)] + prev_specs,
        out_specs=[vspec, sspec(lyr)],
        out_shape=out_shape,
        input_output_aliases=aliases,
        compiler_params=_cparams("parallel", "arbitrary"),
        name="rwkv_scan",
    )(r, w, k, kk, b, v, s0, *prev_args)


def _rwkv_post_kernel(y_ref, g_ref, bonus_ref, gng_ref, gnb_ref, o_ref):
    y = y_ref[...]
    mean = _segsum(y, RW_HD) * (1.0 / RW_HD)
    yc = y - mean
    var = _segsum(yc * yc, RW_HD) * (1.0 / RW_HD)
    yn = yc * lax.rsqrt(var + RW_GN_EPS) * gng_ref[...] + gnb_ref[...]
    o_ref[...] = ((yn + bonus_ref[...]) * g_ref[...]).astype(o_ref.dtype)


def _rwkv_post(y, g, bonus, gn_g, gn_b, row0, total_rows, prev, tm=256):
    m = y.shape[0]
    rb0 = row0 // tm
    fixed = lambda i: (0, 0)
    spec = pl.BlockSpec((tm, MIX_W), lambda i: (i, 0))
    out_shape = jax.ShapeDtypeStruct((total_rows, MIX_W), BF16)
    kern, prev_specs, prev_args, aliases = _carry_through(_rwkv_post_kernel, 5, [prev], [out_shape],
                                                          [m != total_rows])
    return pl.pallas_call(
        kern,
        grid=(m // tm,),
        in_specs=[spec, spec, spec, pl.BlockSpec((1, MIX_W), fixed), pl.BlockSpec((1, MIX_W), fixed)] + prev_specs,
        out_specs=pl.BlockSpec((tm, MIX_W), lambda i: (rb0 + i, 0)),
        out_shape=out_shape,
        input_output_aliases=aliases,
        compiler_params=_cparams("parallel"),
        name="rwkv_post",
    )(y, g, bonus, gn_g, gn_b, *prev_args)


RW_NB = RW_CHAINS // RW_HEADS
RW_TT = LANES


RW_KR = RW_HD + 8
RW_VR = RW_VP + 8


def _natural_to_chain_rows(x_ref, a_ref):
    for bl in range(RW_NB):
        xt = x_ref[bl].T
        for h in range(RW_HEADS):
            c = bl * RW_HEADS + h
            a_ref[c * RW_KR:c * RW_KR + RW_HD, :] = xt[h * RW_HD:(h + 1) * RW_HD]


def _keys_to_scan_kernel(x_ref, o_ref, a_ref):
    _natural_to_chain_rows(x_ref, a_ref)
    o_ref[...] = jnp.zeros_like(o_ref)
    for k in range(RW_HD):
        tile = a_ref[pl.ds(k, RW_CHAINS, stride=RW_KR), :]
        o_ref[0, pl.ds(k, RW_TT, stride=RW_KR), :] = jnp.concatenate([tile, tile], axis=0).T


def _vals_to_scan_kernel(x_ref, o_ref, a_ref):
    _natural_to_chain_rows(x_ref, a_ref)
    o_ref[...] = jnp.zeros_like(o_ref)
    for vp in range(RW_VP):
        halves = [a_ref[pl.ds(vh * RW_VP + vp, RW_CHAINS, stride=RW_KR), :] for vh in range(2)]
        o_ref[0, pl.ds(vp, RW_TT, stride=RW_VR), :] = jnp.concatenate(halves, axis=0).T


def _vals_from_scan_kernel(y_ref, o_ref, a_ref):
    for vp in range(RW_VP):
        tile = y_ref[0, pl.ds(vp, RW_TT, stride=RW_VR), :].T
        for vh in range(2):
            a_ref[pl.ds(vh * RW_VP + vp, RW_CHAINS, stride=RW_KR), :] = tile[vh * RW_CHAINS:(vh + 1) * RW_CHAINS]
    for bl in range(RW_NB):
        rows = [a_ref[c * RW_KR:c * RW_KR + RW_HD, :] for c in range(bl * RW_HEADS, (bl + 1) * RW_HEADS)]
        o_ref[bl] = jnp.concatenate(rows, axis=0).T


def _scan_relayout(kernel_fn, x, n_seq, seq_len, rows_per_t, to_scan, name):
    g, nl = n_seq // RW_NB, seq_len // RW_TT
    nat_spec = pl.BlockSpec((RW_NB, RW_TT, MIX_W), lambda i, l: (i, l, 0))
    scan_spec = pl.BlockSpec((1, RW_TT * rows_per_t, LANES), lambda i, l: (i, l, 0))
    nat_shape = jax.ShapeDtypeStruct((n_seq, seq_len, MIX_W), F32)
    scan_shape = jax.ShapeDtypeStruct((g, seq_len * rows_per_t, LANES), F32)
    x = x.reshape(n_seq, seq_len, MIX_W) if to_scan else x.reshape(g, seq_len * rows_per_t, LANES)
    out = pl.pallas_call(
        kernel_fn,
        grid=(g, nl),
        in_specs=[nat_spec if to_scan else scan_spec],
        out_specs=scan_spec if to_scan else nat_spec,
        out_shape=scan_shape if to_scan else nat_shape,
        scratch_shapes=[pltpu.VMEM((RW_CHAINS * RW_KR, RW_TT), F32)],
        compiler_params=_cparams("parallel", "parallel"),
        name=name,
    )(x)
    return out.reshape(g, seq_len, rows_per_t, LANES) if to_scan else out.reshape(n_seq * seq_len, MIX_W)


def _pad_rows(a, rows):
    return jnp.pad(a, ((0, 0), (0, 0), (0, rows - a.shape[2]), (0, 0)))


def _to_scan_keys(a, n_seq, seq_len):
    if seq_len % RW_TT == 0:
        return _scan_relayout(_keys_to_scan_kernel, a, n_seq, seq_len, RW_KR, True, "rwkv_keys_to_scan")
    g = n_seq // RW_NB
    a = a.reshape(g, RW_NB, seq_len, RW_HEADS, RW_HD).transpose(0, 2, 4, 1, 3).reshape(g, seq_len, RW_HD, RW_CHAINS)
    return _pad_rows(jnp.concatenate([a, a], axis=-1), RW_KR)


def _to_scan_vals(a, n_seq, seq_len):
    if seq_len % RW_TT == 0:
        return _scan_relayout(_vals_to_scan_kernel, a, n_seq, seq_len, RW_VR, True, "rwkv_vals_to_scan")
    g = n_seq // RW_NB
    a = a.reshape(g, RW_NB, seq_len, RW_HEADS, 2, RW_VP).transpose(0, 2, 5, 4, 1, 3)
    return _pad_rows(a.reshape(g, seq_len, RW_VP, LANES), RW_VR)


def _from_scan_vals(y, n_seq, seq_len):
    if seq_len % RW_TT == 0:
        return _scan_relayout(_vals_from_scan_kernel, y, n_seq, seq_len, RW_VR, False, "rwkv_vals_from_scan")
    g = n_seq // RW_NB
    y = y[:, :, :RW_VP].reshape(g, seq_len, RW_VP, 2, RW_NB, RW_HEADS).transpose(0, 4, 1, 5, 3, 2)
    return y.reshape(n_seq * seq_len, MIX_W)


def _state_to_scan(s, n_seq):
    nb = RW_CHAINS // RW_HEADS
    g = n_seq // nb
    s = s.reshape(g, nb, RW_HEADS, 2, RW_VP, RW_HD).transpose(0, 4, 5, 3, 1, 2)
    return s.reshape(g, RW_VP, RW_HD, LANES)


def _state_from_scan(s, n_seq):
    nb = RW_CHAINS // RW_HEADS
    g = n_seq // nb
    s = s.reshape(g, RW_VP, RW_HD, 2, nb, RW_HEADS).transpose(0, 4, 5, 3, 1, 2)
    return s.reshape(n_seq, RW_HEADS, RW_HD, RW_HD)


def _rwkv_branch(z, shift0, s0, lyr_in, p, row0, n_seq, seq_len, bb, lt, tc, depth, lyr, prev):
    r, w, k, kk, b, v, g, bonus, shift = _rwkv_prep(z, shift0, p, row0, n_seq, seq_len, bb, lt)
    keys = [_to_scan_keys(a, n_seq, seq_len) for a in (r, w, k, kk, b)]
    y, s_new = _rwkv_scan(*keys, _to_scan_vals(v, n_seq, seq_len), s0, lyr_in, tc, depth, lyr, prev[1])
    y = _from_scan_vals(y, n_seq, seq_len)
    y = _rwkv_post(y, g, bonus, p['gn_g'], p['gn_b'], row0, z.shape[0], prev[0])
    return y, s_new, shift.reshape(n_seq, RW_COLS)


def _seq_masks(rows, lc):
    ri = lax.broadcasted_iota(jnp.int32, (rows, rows), 0)
    ci = lax.broadcasted_iota(jnp.int32, (rows, rows), 1)
    same = (ri // lc) == (ci // lc)
    return ri, ci, same


def _rows_from_seq(x, bb, lc):
    return jnp.broadcast_to(x, (bb, lc, x.shape[-1])).reshape(bb * lc, x.shape[-1])


def _last_of_seq(x, bb, lc):
    x3 = x.reshape(bb, lc, x.shape[-1])
    pick = (lax.broadcasted_iota(jnp.int32, (1, lc, 1), 1) == lc - 1).astype(F32)
    return jnp.sum(x3 * pick, axis=1, keepdims=True)


def _transpose_rows(x):
    rows = x.shape[0]
    if rows < LANES:
        x = jnp.concatenate([x, jnp.zeros((LANES - rows, LANES), x.dtype)], axis=0)
    return x.T[:, :rows]


def _mlstm_kernel(bb, lc, q_ref, k_ref, v_ref, o_ref, gt_ref, gb_ref, ng_ref, c0_ref, n0_ref, m0_ref,
                  y_ref, c_ref, n_ref, m_ref):
    rows = bb * lc

    @pl.when(pl.program_id(1) == 0)
    def _():
        c_ref[...] = c0_ref[...]
        n_ref[...] = n0_ref[...]
        m_ref[...] = m0_ref[...]

    gt = ML_GATE_CAP * jnp.tanh((gt_ref[...] + gb_ref[...]) * (1.0 / ML_GATE_CAP))
    lane = lax.broadcasted_iota(jnp.int32, (rows, LANES), 1)
    ri, ci, same = _seq_masks(rows, lc)
    causal = same & (ci <= ri)
    cum = _dot_exact_lhs(causal.astype(F32), -_softplus(-gt))
    gt_t = _transpose_rows(gt)
    cum_t = _transpose_rows(cum)
    for hd in range(ML_HEADS):
        sl = slice(hd * ML_HD, (hd + 1) * ML_HD)
        c0 = c_ref[0, :, hd]
        n0 = n_ref[0, :, hd:hd + 1, :]
        m0 = m_ref[0, :, :, hd:hd + 1]
        q = q_ref[:, sl]
        k = k_ref[:, sl] * (ML_HD ** -0.5)
        v = v_ref[:, sl]
        bcol = cum[:, ML_HEADS + hd:ML_HEADS + hd + 1]
        igcol = gt[:, hd:hd + 1]
        rowb = gt_t[hd:hd + 1, :] - cum_t[ML_HEADS + hd:ML_HEADS + hd + 1, :]
        dlog = jnp.where(causal, bcol + rowb, -jnp.inf)
        m_inter = bcol + _rows_from_seq(m0, bb, lc)
        m_t = jnp.maximum(m_inter, jnp.max(dlog, axis=-1, keepdims=True))
        qk = lax.dot_general(q.astype(BF16), k.astype(BF16), (((1,), (1,)), ((), ())),
                             preferred_element_type=F32)
        wts = jnp.exp(dlog - m_t) * qk
        s_inter = jnp.exp(m_inter - m_t)
        q3 = q.reshape(bb, lc, ML_HD).astype(BF16)
        qc = jnp.einsum('btd,bde->bte', q3, c0.astype(BF16), preferred_element_type=F32).reshape(rows, ML_HD)
        num = _bdot(wts, v) + s_inter * qc
        qn = jnp.sum(q * _rows_from_seq(n0, bb, lc), axis=-1, keepdims=True)
        den = jnp.sum(wts, axis=-1, keepdims=True) + s_inter * qn
        hh = num / jnp.maximum(jnp.abs(den), jnp.exp(-m_t))
        hh = hh * lax.rsqrt(jnp.mean(hh * hh, axis=-1, keepdims=True) + EPS) * ng_ref[:, sl]
        y_ref[:, sl] = (_sigmoid(o_ref[:, sl]) * hh).astype(y_ref.dtype)

        stat = jnp.where(lane == 0, bcol, jnp.where(lane == 1, m_t, 0.0))
        last = _last_of_seq(stat, bb, lc)
        last_rows = _rows_from_seq(last, bb, lc)
        b_last, m_new = last_rows[:, 0:1], last_rows[:, 1:2]
        w_end = jnp.exp(b_last - bcol + igcol - m_new)
        f_end = jnp.exp(last[:, :, 0:1] + m0 - last[:, :, 1:2])
        kw3 = (k * w_end).reshape(bb, lc, ML_HD)
        upd = jnp.einsum('bsd,bse->bde', kw3.astype(BF16), v.reshape(bb, lc, ML_HD).astype(BF16),
                         preferred_element_type=F32)
        c_ref[0, :, hd] = f_end * c0 + upd
        n_ref[0, :, hd:hd + 1, :] = f_end * n0 + jnp.sum(kw3, axis=1, keepdims=True)
        m_ref[0, :, :, hd:hd + 1] = last[:, :, 1:2]


def _mlstm_seq_kernel(lc, q_ref, k_ref, v_ref, o_ref, gt_ref, gb_ref, ng_ref, c0_ref, n0_ref, m0_ref,
                      y_ref, c_ref, n_ref, m_ref):
    heads = range(ML_HEADS)

    @pl.when(pl.program_id(1) == 0)
    def _():
        c_ref[...] = c0_ref[...]
        n_ref[...] = n0_ref[...]
        m_ref[...] = m0_ref[...]

    gt = ML_GATE_CAP * jnp.tanh((gt_ref[...] + gb_ref[...]) * (1.0 / ML_GATE_CAP))
    ri, ci, _ = _seq_masks(lc, lc)
    causal = ci <= ri
    cum = _dot_exact_lhs(causal.astype(F32), -_softplus(-gt))
    gt_t = _transpose_rows(gt)
    cum_t = _transpose_rows(cum)
    sl = lambda hd: slice(hd * ML_HD, (hd + 1) * ML_HD)
    stack = lambda f: jnp.stack([f(hd) for hd in heads])
    q = stack(lambda hd: q_ref[:, sl(hd)])
    k = stack(lambda hd: k_ref[:, sl(hd)]) * (ML_HD ** -0.5)
    v = stack(lambda hd: v_ref[:, sl(hd)])
    bcol = stack(lambda hd: cum[:, ML_HEADS + hd:ML_HEADS + hd + 1])
    igcol = stack(lambda hd: gt[:, hd:hd + 1])
    rowb = stack(lambda hd: gt_t[hd:hd + 1, :] - cum_t[ML_HEADS + hd:ML_HEADS + hd + 1, :])
    c0 = c_ref[0, 0]
    n0 = stack(lambda hd: n_ref[0, 0, hd:hd + 1, :])
    m0 = stack(lambda hd: m_ref[0, 0, :, hd:hd + 1])
    qb, kb, vb = q.astype(BF16), k.astype(BF16), v.astype(BF16)

    dlog = jnp.where(causal[None], bcol + rowb, -jnp.inf)
    m_inter = bcol + m0
    m_t = jnp.maximum(m_inter, jnp.max(dlog, axis=-1, keepdims=True))
    qk = jnp.einsum('hqd,hkd->hqk', qb, kb, preferred_element_type=F32)
    wts = jnp.exp(dlog - m_t) * qk
    s_inter = jnp.exp(m_inter - m_t)
    qc = jnp.einsum('hqd,hde->hqe', qb, c0.astype(BF16), preferred_element_type=F32)
    num = jnp.einsum('hqk,hke->hqe', wts.astype(BF16), vb, preferred_element_type=F32) + s_inter * qc
    den = jnp.sum(wts, axis=-1, keepdims=True) + s_inter * jnp.sum(q * n0, axis=-1, keepdims=True)
    hh = num / jnp.maximum(jnp.abs(den), jnp.exp(-m_t))
    hh = hh * lax.rsqrt(jnp.mean(hh * hh, axis=-1, keepdims=True) + EPS)
    for hd in heads:
        y_ref[:, sl(hd)] = (_sigmoid(o_ref[:, sl(hd)]) * hh[hd] * ng_ref[:, sl(hd)]).astype(y_ref.dtype)

    b_last, m_new = bcol[:, lc - 1:lc, :], m_t[:, lc - 1:lc, :]
    w_end = jnp.exp(b_last - bcol + igcol - m_new)
    f_end = jnp.exp(b_last + m0 - m_new)
    kw = k * w_end
    upd = jnp.einsum('hsd,hse->hde', kw.astype(BF16), vb, preferred_element_type=F32)
    c_ref[0, 0] = f_end * c0 + upd
    n_new = f_end * n0 + jnp.sum(kw, axis=1, keepdims=True)
    for hd in heads:
        n_ref[0, 0, hd:hd + 1, :] = n_new[hd]
        m_ref[0, 0, :, hd:hd + 1] = m_new[hd]


def _mlstm_branch(z, z_gates, init, lyr_in, gate_bias, norm_g, row0, n_seq, seq_len, bb, lc, depth, lyr, prev):
    rows = bb * lc
    nc = seq_len // lc
    rb0 = row0 // rows
    rowmap = lambda part: (lambda b, c: (rb0 + b * nc + c, part))
    zspec = lambda part: pl.BlockSpec((rows, MIX_W), rowmap(part))
    state_specs = lambda at: [pl.BlockSpec((1, bb, ML_HEADS, ML_HD, ML_HD), lambda b, c: (at, b, 0, 0, 0)),
                              pl.BlockSpec((1, bb, ML_HEADS, ML_HD), lambda b, c: (at, b, 0, 0)),
                              pl.BlockSpec((1, bb, 1, ML_HEADS), lambda b, c: (at, b, 0, 0))]
    body = functools.partial(_mlstm_seq_kernel, lc) if bb == 1 else functools.partial(_mlstm_kernel, bb, lc)
    out_shape = [jax.ShapeDtypeStruct((z.shape[0], MIX_W), BF16),
                 jax.ShapeDtypeStruct((depth, n_seq, ML_HEADS, ML_HD, ML_HD), F32),
                 jax.ShapeDtypeStruct((depth, n_seq, ML_HEADS, ML_HD), F32),
                 jax.ShapeDtypeStruct((depth, n_seq, 1, ML_HEADS), F32)]
    kern, prev_specs, prev_args, aliases = _carry_through(
        body, 10, prev, out_shape, [n_seq * seq_len != z.shape[0]] + [depth > 1] * 3)
    return pl.pallas_call(
        kern,
        grid=(n_seq // bb, nc),
        in_specs=[zspec(0), zspec(1), zspec(2), zspec(3),
                  pl.BlockSpec((rows, LANES), lambda b, c: (rb0 + b * nc + c, 4 * MIX_W // LANES)),
                  pl.BlockSpec((1, LANES), lambda b, c: (0, 0)), pl.BlockSpec((1, MIX_W), lambda b, c: (0, 0))]
                 + state_specs(lyr_in) + prev_specs,
        out_specs=[pl.BlockSpec((rows, MIX_W), lambda b, c: (rb0 + b * nc + c, 0))] + state_specs(lyr),
        out_shape=out_shape,
        input_output_aliases=aliases,
        compiler_params=_cparams("parallel", "arbitrary"),
        name="mlstm_chunk",
    )(z, z, z, z, z_gates, gate_bias, norm_g, *init, *prev_args)


def _rope_rows(x, cos, sin):
    half = RT_HD // 2
    x1, x2 = x[:, :half], x[:, half:]
    return jnp.concatenate([x1 * cos - x2 * sin, x1 * sin + x2 * cos], axis=1)


def _ret_kernel(bb, lc, q_ref, k_ref, v_ref, g_ref, cs_ref, dec_ref, cd_ref, s0_ref, y_ref, s_ref):
    rows = bb * lc

    @pl.when(pl.program_id(1) == 0)
    def _():
        s_ref[...] = s0_ref[...]

    cos, sin = cs_ref[:, :RT_HD // 2], cs_ref[:, RT_HD // 2:]
    for hd in range(RT_HEADS):
        sl = slice(hd * RT_HD, (hd + 1) * RT_HD)
        s0 = s_ref[0, :, hd]
        q = _rope_rows(q_ref[:, sl], cos, sin)
        k = _rope_rows(k_ref[:, sl], cos, sin) * (RT_HD ** -0.5)
        v = v_ref[:, sl]
        cd = cd_ref[hd]
        qk = lax.dot_general(q.astype(BF16), k.astype(BF16), (((1,), (1,)), ((), ())),
                             preferred_element_type=F32)
        inner = _bdot(qk * dec_ref[hd], v)
        q3 = q.reshape(bb, lc, RT_HD).astype(BF16)
        cross = jnp.einsum('btd,bde->bte', q3, s0.astype(BF16), preferred_element_type=F32).reshape(rows, RT_HD)
        y = inner + cross * cd[:, 0:1]
        y = y * lax.rsqrt(jnp.mean(y * y, axis=-1, keepdims=True) + EPS)
        gt = g_ref[:, sl]
        y_ref[:, sl] = (gt * _sigmoid(gt) * y).astype(y_ref.dtype)
        kd = (k * cd[:, 1:2]).reshape(bb, lc, RT_HD).astype(BF16)
        upd = jnp.einsum('bsd,bse->bde', kd, v.reshape(bb, lc, RT_HD).astype(BF16), preferred_element_type=F32)
        s_ref[0, :, hd] = cd[0:1, 2:3] * s0 + upd


def _ret_seq_kernel(lc, q_ref, k_ref, v_ref, g_ref, cs_ref, dec_ref, cd_ref, s0_ref, y_ref, s_ref):
    heads = range(RT_HEADS)

    @pl.when(pl.program_id(1) == 0)
    def _():
        s_ref[...] = s0_ref[...]

    cos, sin = cs_ref[:, :RT_HD // 2], cs_ref[:, RT_HD // 2:]
    sl = lambda hd: slice(hd * RT_HD, (hd + 1) * RT_HD)
    stack = lambda f: jnp.stack([f(hd) for hd in heads])
    q = stack(lambda hd: _rope_rows(q_ref[:, sl(hd)], cos, sin))
    k = stack(lambda hd: _rope_rows(k_ref[:, sl(hd)], cos, sin)) * (RT_HD ** -0.5)
    vb = stack(lambda hd: v_ref[:, sl(hd)]).astype(BF16)
    s0 = s_ref[0, 0]
    cd = cd_ref[...]
    qb = q.astype(BF16)
    qk = jnp.einsum('hqd,hkd->hqk', qb, k.astype(BF16), preferred_element_type=F32)
    inner = jnp.einsum('hqk,hke->hqe', (qk * dec_ref[...]).astype(BF16), vb, preferred_element_type=F32)
    cross = jnp.einsum('hqd,hde->hqe', qb, s0.astype(BF16), preferred_element_type=F32)
    y = inner + cross * cd[:, :, 0:1]
    y = y * lax.rsqrt(jnp.mean(y * y, axis=-1, keepdims=True) + EPS)
    for hd in heads:
        gt = g_ref[:, sl(hd)]
        y_ref[:, sl(hd)] = (gt * _sigmoid(gt) * y[hd]).astype(y_ref.dtype)
    kd = (k * cd[:, :, 1:2]).astype(BF16)
    upd = jnp.einsum('hsd,hse->hde', kd, vb, preferred_element_type=F32)
    s_ref[0, 0] = cd[:, 0:1, 2:3] * s0 + upd


def _ret_branch(z, s0, lyr_in, row0, n_seq, seq_len, bb, lc, pos0, depth, lyr, prev):
    rows = bb * lc
    nc = seq_len // lc
    rb0 = row0 // rows
    half = RT_HD // 2
    inv = ROPE_BASE ** (-jnp.arange(half, dtype=F32) / half)
    pos = jnp.broadcast_to((jnp.arange(nc * lc, dtype=F32) + float(pos0)).reshape(nc, 1, lc), (nc, bb, lc))
    ang = pos.reshape(nc * rows)[:, None] * inv[None, :]
    cs = jnp.concatenate([jnp.cos(ang), jnp.sin(ang)], axis=1)
    log_g = jnp.log(1.0 - jnp.exp(jnp.linspace(math.log(1.0 / 32), math.log(1.0 / 512), RT_HEADS)))
    idx = jnp.arange(rows)
    t = (idx % lc).astype(F32)
    diff = t[:, None] - t[None, :]
    same = (idx[:, None] // lc) == (idx[None, :] // lc)
    dec = jnp.where(same & (diff >= 0), jnp.exp(log_g[:, None, None] * jnp.maximum(diff, 0.0)), 0.0)
    cd = jnp.zeros((RT_HEADS, rows, LANES), F32)
    cd = cd.at[:, :, 0].set(jnp.exp(log_g[:, None] * (t[None, :] + 1.0)))
    cd = cd.at[:, :, 1].set(jnp.exp(log_g[:, None] * (lc - 1.0 - t)[None, :]))
    cd = cd.at[:, :, 2].set(jnp.broadcast_to(jnp.exp(log_g * lc)[:, None], (RT_HEADS, rows)))
    rowmap = lambda part: (lambda b, c: (rb0 + b * nc + c, part))
    zspec = lambda part: pl.BlockSpec((rows, MIX_W), rowmap(part))
    sspec = lambda at: pl.BlockSpec((1, bb, RT_HEADS, RT_HD, RT_HD), lambda b, c: (at, b, 0, 0, 0))
    body = functools.partial(_ret_seq_kernel, lc) if bb == 1 else functools.partial(_ret_kernel, bb, lc)
    out_shape = [jax.ShapeDtypeStruct((z.shape[0], MIX_W), BF16),
                 jax.ShapeDtypeStruct((depth, n_seq, RT_HEADS, RT_HD, RT_HD), F32)]
    kern, prev_specs, prev_args, aliases = _carry_through(body, 8, prev, out_shape,
                                                          [n_seq * seq_len != z.shape[0], depth > 1])
    return pl.pallas_call(
        kern,
        grid=(n_seq // bb, nc),
        in_specs=[zspec(0), zspec(1), zspec(2), zspec(3),
                  pl.BlockSpec((rows, RT_HD), lambda b, c: (c, 0)),
                  pl.BlockSpec((RT_HEADS, rows, rows), lambda b, c: (0, 0, 0)),
                  pl.BlockSpec((RT_HEADS, rows, LANES), lambda b, c: (0, 0, 0)),
                  sspec(lyr_in)] + prev_specs,
        out_specs=[pl.BlockSpec((rows, MIX_W), lambda b, c: (rb0 + b * nc + c, 0)), sspec(lyr)],
        out_shape=out_shape,
        input_output_aliases=aliases,
        compiler_params=_cparams("parallel", "arbitrary"),
        name="retention_chunk",
    )(z, z, z, z, cs, dec, cd, s0, *prev_args)


def _mm_wt_kernel(x_ref, wt_ref, o_ref, wb_ref):
    @pl.when(pl.program_id(1) == 0)
    def _():
        tn = wb_ref.shape[1]
        for c in range(0, tn, LANES):
            wb_ref[:, c:c + LANES] = wt_ref[0, c:c + LANES, :].T.astype(wb_ref.dtype)

    o_ref[...] = jnp.dot(x_ref[...], wb_ref[...], preferred_element_type=F32)


def _mm_wt(x, w_t, lyr, col0, n, tm, tn):
    m, k = x.shape
    return pl.pallas_call(
        _mm_wt_kernel,
        grid=(n // tn, m // tm),
        in_specs=[pl.BlockSpec((tm, k), lambda j, i: (i, 0)),
                  pl.BlockSpec((pl.Element(1), pl.Element(tn), pl.Element(k)),
                               lambda j, i: (lyr, pl.multiple_of(col0 + j * tn, 8), 0))],
        out_specs=pl.BlockSpec((tm, tn), lambda j, i: (i, j)),
        out_shape=jax.ShapeDtypeStruct((m, n), F32),
        scratch_shapes=[pltpu.VMEM((k, tn), BF16)],
        compiler_params=_cparams("parallel", "arbitrary"),
        name="matmul_wt",
    )(x, w_t)


def _group_cfg(n_seq, seq_len):
    if seq_len % CHUNK == 0:
        return dict(rw_bb=1, rw_lt=256, rw_tc=64, ml_bb=1, rt_bb=1, ch_lc=CHUNK, at_bb=1, at_lq=512)
    return dict(rw_bb=32, rw_lt=seq_len, rw_tc=seq_len, ml_bb=8, rt_bb=8, ch_lc=seq_len, at_bb=8, at_lq=seq_len)


def kernel(x_prompt, x_sample, mem_prompt, state_rwkv_shift, state_rwkv, state_mlstm_c, state_mlstm_n,
           state_mlstm_m, state_ret, cache_mem_k, cache_mem_v, g_pre_mix, g_post_mix, g_pre_x, g_post_x,
           g_pre_ff, g_post_ff, g_mem, w_in, rw_mu, rw_w0, rw_w_up, rw_a0, rw_a_up, rw_g_up, rw_k_k, rw_k_a,
           rw_r_k, rw_gn_g, rw_gn_b, ml_i_b, ml_f_b, ml_norm_g, w_br, w_out, x_wq, x_wkv, x_wo, ff_w1, ff_w2):
    bp, lp, d = x_prompt.shape
    bs, ls, _ = x_sample.shape
    depth = w_in.shape[0]
    n_mem = mem_prompt.shape[1]
    tp, ts = bp * lp, bs * ls
    groups = [(0, bp, lp, 0), (tp, bs, ls, PAST_LEN)]

    x_groups = [x_prompt.reshape(tp, d), x_sample.reshape(ts, d)]
    mem = mem_prompt.reshape(bp * n_mem, d)
    x = h = None
    for (row0, *_), xg in zip(groups, x_groups):
        h = _rmsnorm(xg, g_pre_mix[0], row0=row0, total_rows=tp + ts, prev=h)

    zeros = lambda *s: jnp.zeros(s, F32)
    rw_scan0 = _state_to_scan(state_rwkv.reshape(depth * bs, RW_HEADS, RW_HD, RW_HD), depth * bs)
    inits = [
        lambda l: (zeros(bp, 1, RW_COLS), zeros(bp // RW_NB, RW_VP, RW_HD, LANES),
                   (zeros(1, bp, ML_HEADS, ML_HD, ML_HD), zeros(1, bp, ML_HEADS, ML_HD), zeros(1, bp, 1, ML_HEADS)),
                   zeros(1, bp, RT_HEADS, RT_HD, RT_HD), 0),
        lambda l: (state_rwkv_shift[l].reshape(bs, 1, RW_COLS), rw_scan0,
                   (state_mlstm_c, state_mlstm_n, state_mlstm_m.reshape(depth, bs, 1, ML_HEADS)), state_ret, l),
    ]
    shifts = [[], []]
    rw_states = [None, None]
    ml_states = [[None] * 3, [None] * 3]
    rt_states = [None, None]
    mem_k, mem_v = [], []
    w_in_t = jnp.swapaxes(w_in, 1, 2)
    w_kv, w_brb, w_outb, w_q, w_o, w_f1, w_f2 = (w.astype(BF16) for w in (x_wkv, w_br, w_out, x_wq, x_wo,
                                                                         ff_w1, ff_w2))
    y_rw = y_ml = y_rt = None
    for l in range(depth):
        kv = _mm(_rmsnorm(mem, g_mem[l], tm=256), w_kv, l, tm=256, tn=512)
        mem_k.append(kv[:, :X_W].reshape(bp, n_mem, X_HEADS, X_HD))
        mem_v.append(kv[:, X_W:].reshape(bp, n_mem, X_HEADS, X_HD))

        z_rw = _mm_wt(h, w_in_t, l, 0, RW_COLS, tm=512, tn=RW_COLS // 2)
        z_ml = _mm_wt(h, w_in_t, l, IN_ML, Z_MLW, tm=1024, tn=Z_MLW // 3)
        z_rg = _mm_wt(h, w_in_t, l, IN_RT, w_in.shape[2] - IN_RT, tm=1024, tn=1280)

        rw_p = dict(mu=rw_mu[l].reshape(1, -1), w0=rw_w0[l].reshape(1, -1), w_up=rw_w_up[l].astype(BF16),
                    a0=rw_a0[l].reshape(1, -1), a_up=rw_a_up[l].astype(BF16), g_up=rw_g_up[l].astype(BF16),
                    k_k=rw_k_k[l].reshape(1, -1), k_a=rw_k_a[l].reshape(1, -1), r_k=rw_r_k[l].reshape(1, -1),
                    gn_g=rw_gn_g[l].reshape(1, -1), gn_b=rw_gn_b[l].reshape(1, -1))
        gate_bias = jnp.concatenate([ml_i_b[l], ml_f_b[l], zeros(LANES - 2 * ML_HEADS)]).reshape(1, LANES)
        norm_g = ml_norm_g[l].reshape(1, MIX_W)

        for gi, (row0, n_seq, seq_len, pos0) in enumerate(groups):
            cfg = _group_cfg(n_seq, seq_len)
            shift0, s_rw0, ml0, rt0, lyr_in = inits[gi](l)
            y_rw, rw_states[gi], shift = _rwkv_branch(z_rw, shift0, s_rw0, lyr_in, rw_p, row0, n_seq, seq_len,
                                                      cfg['rw_bb'], cfg['rw_lt'], cfg['rw_tc'], depth, l,
                                                      (y_rw, rw_states[gi]))
            shifts[gi].append(shift)
            y_ml, *ml_states[gi] = _mlstm_branch(z_ml, z_ml, ml0, lyr_in, gate_bias, norm_g, row0, n_seq, seq_len,
                                                 cfg['ml_bb'], cfg['ch_lc'], depth, l, [y_ml] + ml_states[gi])
            y_rt, rt_states[gi] = _ret_branch(z_rg, rt0, lyr_in, row0, n_seq, seq_len, cfg['rt_bb'],
                                              cfg['ch_lc'], pos0, depth, l, [y_rt, rt_states[gi]])

        merged = _merge(y_rw, y_ml, y_rt, w_brb, l, z_rg, Z_GATE)
        if x is None:
            xh = (None, h)
            for (row0, *_), xg in zip(groups, x_groups):
                xh = _proj_res(merged, w_outb, l, xg, g_post_mix[l], g_pre_x[l], row0=row0, prev=xh)
            x, h = xh
        else:
            x, h = _proj_res(merged, w_outb, l, x, g_post_mix[l], g_pre_x[l])

        for gi, (row0, n_seq, seq_len, pos0) in enumerate(groups):
            cfg = _group_cfg(n_seq, seq_len)
            if gi == 0:
                mk, mv, lyr_in = mem_k[l][None], mem_v[l][None], 0
            else:
                mk, mv, lyr_in = cache_mem_k, cache_mem_v, l
            x, h = _attn(h, x, mk, mv, lyr_in, w_q, w_o, l, g_post_x[l], g_pre_ff[l], row0, n_seq, seq_len,
                         cfg['at_bb'], cfg['at_lq'])

        if l + 1 < depth:
            x, h = _mlp(h, w_f1, w_f2, l, x, g_post_ff[l], g_pre_mix[l + 1])
        else:
            y_out = [_mlp(h, w_f1, w_f2, l, x, g_post_ff[l], g_pre_mix[0], row0=row0, n_rows=n_seq * seq_len)[0]
                     for row0, n_seq, seq_len, _ in groups]

    def states(gi, n_seq):
        c_new, n_new, m_new = ml_states[gi]
        s_rw = _state_from_scan(rw_states[gi], depth * n_seq).reshape(depth, n_seq, RW_HEADS, RW_HD, RW_HD)
        return (jnp.stack(shifts[gi]), s_rw, c_new, n_new, m_new.reshape(depth, n_seq, ML_HEADS), rt_states[gi])

    return (y_out[0].reshape(bp, lp, d), y_out[1].reshape(bs, ls, d), *states(0, bp),
            jnp.stack(mem_k), jnp.stack(mem_v), *states(1, bs))
```

```python
import functools
import math

import jax
import jax.numpy as jnp
from jax import lax
from jax.experimental import pallas as pl
from jax.experimental.pallas import tpu as pltpu

F32 = jnp.float32
BF16 = jnp.bfloat16

D_MODEL = 2048
MIX_W = D_MODEL // 2
RW_HD = 64
RW_HEADS = MIX_W // RW_HD
RW_LORA = (64, 64, 128)
RW_COLS = 3 * MIX_W + sum(RW_LORA)
RW_GN_EPS = 64e-5
ML_HEADS = 8
ML_HD = MIX_W // ML_HEADS
ML_GATE_CAP = 15.0
RT_HEADS = 4
RT_HD = MIX_W // RT_HEADS
ROPE_BASE = 10000.0
X_HEADS = 4
X_HD = 128
X_W = X_HEADS * X_HD
D_FF = 4 * D_MODEL
CHUNK = 64
EPS = 1e-6
PAST_LEN = 16384

Z_MLW = 4 * (D_MODEL // 2) + 128
IN_ML = RW_COLS
IN_MLG = IN_ML + 4 * MIX_W
IN_RT = IN_MLG + 2 * ML_HEADS
Z_GATE = 4 * MIX_W

LANES = 128
RW_VP = RW_HD // 2
RW_CHAINS = LANES // 2


def _cparams(*sem):
    return pltpu.CompilerParams(dimension_semantics=sem, vmem_limit_bytes=56 * 1024 * 1024)


def _carry_through(kernel_fn, n_in, prev, out_shapes=(), shared=()):
    prev = [jnp.zeros(out_shapes[i].shape, out_shapes[i].dtype) if p is None and i < len(shared) and shared[i]
            else p for i, p in enumerate(prev)]
    carried = [(i, p) for i, p in enumerate(prev) if p is not None]
    if not carried:
        return kernel_fn, [], [], {}

    def wrapped(*refs):
        return kernel_fn(*refs[:n_in], *refs[n_in + len(carried):])

    return (wrapped, [pl.BlockSpec(memory_space=pl.ANY)] * len(carried), [p for _, p in carried],
            {n_in + j: i for j, (i, _) in enumerate(carried)})


def _sigmoid(x):
    return 1.0 / (1.0 + jnp.exp(-x))


def _softplus(x):
    return jnp.maximum(x, 0.0) + jnp.log(1.0 + jnp.exp(-jnp.abs(x)))


def _bdot(a, b):
    return jnp.dot(a.astype(BF16), b.astype(BF16), preferred_element_type=F32)


def _split3(x):
    hi = x.astype(BF16)
    r1 = x - hi.astype(F32)
    mid = r1.astype(BF16)
    lo = (r1 - mid.astype(F32)).astype(BF16)
    return hi, mid, lo


def _dot_exact_lhs(m, x):
    mb = m.astype(BF16)
    hi, mid, lo = _split3(x)
    return (jnp.dot(mb, hi, preferred_element_type=F32) + jnp.dot(mb, mid, preferred_element_type=F32)
            + jnp.dot(mb, lo, preferred_element_type=F32))


def _dot_exact_rhs(x, m):
    mb = m.astype(BF16)
    hi, mid, lo = _split3(x)
    return (jnp.dot(hi, mb, preferred_element_type=F32) + jnp.dot(mid, mb, preferred_element_type=F32)
            + jnp.dot(lo, mb, preferred_element_type=F32))


def _segsum(x, seg):
    blk = 256
    ri = lax.broadcasted_iota(jnp.int32, (blk, blk), 0) // seg
    ci = lax.broadcasted_iota(jnp.int32, (blk, blk), 1) // seg
    ones = (ri == ci).astype(F32)
    parts = [_dot_exact_rhs(x[:, c:c + blk], ones) for c in range(0, x.shape[1], blk)]
    return parts[0] if len(parts) == 1 else jnp.concatenate(parts, axis=1)


def _rms_rows(x, g):
    return x * lax.rsqrt(jnp.mean(x * x, axis=-1, keepdims=True) + EPS) * g


def _rmsnorm_kernel(x_ref, g_ref, o_ref):
    o_ref[...] = _rms_rows(x_ref[...], g_ref[...]).astype(o_ref.dtype)


def _rmsnorm(x, g, tm=512, row0=0, total_rows=None, prev=None):
    m, d = x.shape
    rb0 = row0 // tm
    out_shape = jax.ShapeDtypeStruct((total_rows or m, d), BF16)
    kern, prev_specs, prev_args, aliases = _carry_through(_rmsnorm_kernel, 2, [prev], [out_shape],
                                                          [out_shape.shape[0] != m])
    return pl.pallas_call(
        kern,
        grid=(m // tm,),
        in_specs=[pl.BlockSpec((tm, d), lambda i: (i, 0)), pl.BlockSpec((1, d), lambda i: (0, 0))] + prev_specs,
        out_specs=pl.BlockSpec((tm, d), lambda i: (rb0 + i, 0)),
        out_shape=out_shape,
        input_output_aliases=aliases,
        compiler_params=_cparams("parallel"),
        name="rmsnorm",
    )(x, g.reshape(1, d), *prev_args)


def _mm_kernel(x_ref, w_ref, o_ref):
    o_ref[...] = jnp.dot(x_ref[...], w_ref[0], preferred_element_type=F32)


def _mm(x, w, lyr, tm, tn):
    m, k = x.shape
    n = w.shape[2]
    return pl.pallas_call(
        _mm_kernel,
        grid=(n // tn, m // tm),
        in_specs=[pl.BlockSpec((tm, k), lambda j, i: (i, 0)), pl.BlockSpec((1, k, tn), lambda j, i: (lyr, 0, j))],
        out_specs=pl.BlockSpec((tm, tn), lambda j, i: (i, j)),
        out_shape=jax.ShapeDtypeStruct((m, n), F32),
        compiler_params=_cparams("parallel", "parallel"),
        name="matmul",
    )(x, w)


def _merge_kernel(y0_ref, y1_ref, y2_ref, w_ref, g0_ref, g1_ref, g2_ref, o_ref):
    acc = _sigmoid(g0_ref[...]) * jnp.dot(y0_ref[...], w_ref[0, 0], preferred_element_type=F32)
    acc += _sigmoid(g1_ref[...]) * jnp.dot(y1_ref[...], w_ref[0, 1], preferred_element_type=F32)
    acc += _sigmoid(g2_ref[...]) * jnp.dot(y2_ref[...], w_ref[0, 2], preferred_element_type=F32)
    o_ref[...] = acc.astype(o_ref.dtype)


def _merge(y_rw, y_ml, y_rt, w_br, lyr, z, gate_col0, tm=512, tn=1024):
    m, kw = y_rw.shape
    d = w_br.shape[3]
    gb = gate_col0 // tn
    gspec = lambda c: pl.BlockSpec((tm, tn), lambda j, i: (i, gb + c * (d // tn) + j))
    yspec = pl.BlockSpec((tm, kw), lambda j, i: (i, 0))
    return pl.pallas_call(
        _merge_kernel,
        grid=(d // tn, m // tm),
        in_specs=[yspec, yspec, yspec, pl.BlockSpec((1, 3, kw, tn), lambda j, i: (lyr, 0, 0, j)),
                  gspec(0), gspec(1), gspec(2)],
        out_specs=pl.BlockSpec((tm, tn), lambda j, i: (i, j)),
        out_shape=jax.ShapeDtypeStruct((m, d), BF16),
        compiler_params=_cparams("parallel", "parallel"),
        name="branch_merge",
    )(y_rw, y_ml, y_rt, w_br, z, z, z)


def _residual_epilogue(acc, x_ref, gpost_ref, gnext_ref, xo_ref, ho_ref):
    xn = x_ref[...] + _rms_rows(acc, gpost_ref[...])
    xo_ref[...] = xn
    ho_ref[...] = _rms_rows(xn, gnext_ref[...]).astype(ho_ref.dtype)


def _proj_res_kernel(a_ref, w_ref, x_ref, gpost_ref, gnext_ref, xo_ref, ho_ref):
    acc = jnp.dot(a_ref[...], w_ref[0], preferred_element_type=F32)
    _residual_epilogue(acc, x_ref, gpost_ref, gnext_ref, xo_ref, ho_ref)


def _proj_res(a, w, lyr, x, g_post, g_next, row0=0, prev=(None, None), tm=512):
    total, k = a.shape
    m, d = x.shape
    rb0 = row0 // tm
    at = lambda i: (rb0 + i, 0)
    fixed = lambda i: (0, 0)
    out_shape = [jax.ShapeDtypeStruct((total, d), F32), jax.ShapeDtypeStruct((total, d), BF16)]
    kern, prev_specs, prev_args, aliases = _carry_through(_proj_res_kernel, 5, list(prev), out_shape,
                                                          [m != total] * 2)
    return pl.pallas_call(
        kern,
        grid=(m // tm,),
        in_specs=[pl.BlockSpec((tm, k), at), pl.BlockSpec((1, k, d), lambda i: (lyr, 0, 0)),
                  pl.BlockSpec((tm, d), lambda i: (i, 0)),
                  pl.BlockSpec((1, d), fixed), pl.BlockSpec((1, d), fixed)] + prev_specs,
        out_specs=[pl.BlockSpec((tm, d), at), pl.BlockSpec((tm, d), at)],
        out_shape=out_shape,
        input_output_aliases=aliases,
        compiler_params=_cparams("parallel"),
        name="out_proj_residual",
    )(a, w, x, g_post.reshape(1, d), g_next.reshape(1, d), *prev_args)


def _mlp_kernel(h_ref, w1_ref, w2_ref, x_ref, gpost_ref, gnext_ref, xo_ref, ho_ref, acc_ref):
    k = pl.program_id(1)

    @pl.when(k == 0)
    def _():
        acc_ref[...] = jnp.zeros_like(acc_ref)

    u = jnp.dot(h_ref[...], w1_ref[0], preferred_element_type=F32)
    u = jnp.square(jnp.maximum(u, 0.0)).astype(BF16)
    acc_ref[...] += jnp.dot(u, w2_ref[0], preferred_element_type=F32)

    @pl.when(k == pl.num_programs(1) - 1)
    def _():
        _residual_epilogue(acc_ref[...], x_ref, gpost_ref, gnext_ref, xo_ref, ho_ref)


def _mlp(h, w1, w2, lyr, x, g_post, g_next, row0=0, n_rows=None, tm=512, tf=1024):
    d = h.shape[1]
    m = n_rows or h.shape[0]
    f = w1.shape[2]
    rb0 = row0 // tm
    at = lambda i, k: (rb0 + i, 0)
    row = lambda i, k: (i, 0)
    fixed = lambda i, k: (0, 0)
    return pl.pallas_call(
        _mlp_kernel,
        grid=(m // tm, f // tf),
        in_specs=[pl.BlockSpec((tm, d), at), pl.BlockSpec((1, d, tf), lambda i, k: (lyr, 0, k)),
                  pl.BlockSpec((1, tf, d), lambda i, k: (lyr, k, 0)), pl.BlockSpec((tm, d), at),
                  pl.BlockSpec((1, d), fixed), pl.BlockSpec((1, d), fixed)],
        out_specs=[pl.BlockSpec((tm, d), row), pl.BlockSpec((tm, d), row)],
        out_shape=[jax.ShapeDtypeStruct((m, d), F32), jax.ShapeDtypeStruct((m, d), BF16)],
        scratch_shapes=[pltpu.VMEM((tm, d), F32)],
        compiler_params=_cparams("parallel", "arbitrary"),
        name="relu2_mlp",
    )(h, w1, w2, x, g_post.reshape(1, d), g_next.reshape(1, d))


def _head_rows(ref, hd):
    n_mem = ref.shape[2] // X_HEADS
    return ref[0, :, pl.ds(hd, n_mem, stride=X_HEADS), :]


def _attn_kernel(bb, lq, h_ref, wq_ref, mk_ref, mv_ref, wo_ref, x_ref, gpost_ref, gnext_ref, xo_ref, ho_ref):
    q = jnp.dot(h_ref[...], wq_ref[0], preferred_element_type=F32)
    outs = []
    for hd in range(X_HEADS):
        sl = slice(hd * X_HD, (hd + 1) * X_HD)
        q3 = q[:, sl].reshape(bb, lq, X_HD).astype(BF16)
        k3 = _head_rows(mk_ref, hd).astype(BF16)
        v3 = _head_rows(mv_ref, hd).astype(BF16)
        s = jnp.einsum('bqd,bmd->bqm', q3, k3, preferred_element_type=F32) * (X_HD ** -0.5)
        s = s - jnp.max(s, axis=-1, keepdims=True)
        e = jnp.exp(s)
        p = (e / jnp.sum(e, axis=-1, keepdims=True)).astype(BF16)
        o3 = jnp.einsum('bqm,bmd->bqd', p, v3, preferred_element_type=F32)
        outs.append(o3.reshape(bb * lq, X_HD))
    o = jnp.concatenate(outs, axis=1).astype(BF16)
    acc = jnp.dot(o, wo_ref[0], preferred_element_type=F32)
    _residual_epilogue(acc, x_ref, gpost_ref, gnext_ref, xo_ref, ho_ref)


def _attn_seq_kernel(h_ref, wq_ref, mk_ref, mv_ref, wo_ref, x_ref, gpost_ref, gnext_ref, xo_ref, ho_ref):
    heads = range(X_HEADS)
    q = jnp.dot(h_ref[...], wq_ref[0], preferred_element_type=F32)
    stack = lambda f: jnp.stack([f(hd) for hd in heads]).astype(BF16)
    q3 = stack(lambda hd: q[:, hd * X_HD:(hd + 1) * X_HD])
    k3 = stack(lambda hd: _head_rows(mk_ref, hd)[0])
    v3 = stack(lambda hd: _head_rows(mv_ref, hd)[0])
    s = jnp.einsum('hqd,hmd->hqm', q3, k3, preferred_element_type=F32) * (X_HD ** -0.5)
    s = s - jnp.max(s, axis=-1, keepdims=True)
    e = jnp.exp(s)
    p = (e / jnp.sum(e, axis=-1, keepdims=True)).astype(BF16)
    o3 = jnp.einsum('hqm,hmd->hqd', p, v3, preferred_element_type=F32)
    o = jnp.concatenate([o3[hd] for hd in heads], axis=1).astype(BF16)
    acc = jnp.dot(o, wo_ref[0], preferred_element_type=F32)
    _residual_epilogue(acc, x_ref, gpost_ref, gnext_ref, xo_ref, ho_ref)


def _attn(h, x, mk, mv, lyr, wq, wo, wl, g_post, g_next, row0, n_seq, seq_len, bb, lq):
    t, d = h.shape
    rows = bb * lq
    nl = seq_len // lq
    n_mem = mk.shape[2]
    rb0 = row0 // rows
    rowmap = lambda b, l: (rb0 + b * nl + l, 0)
    fixed = lambda b, l: (0, 0)
    body = _attn_seq_kernel if bb == 1 else functools.partial(_attn_kernel, bb, lq)
    kv_specs = [pl.BlockSpec((1, bb, n_mem * X_HEADS, X_HD), lambda b, l: (lyr, b, 0, 0))] * 2
    kv_args = [a.reshape(a.shape[0], a.shape[1], n_mem * X_HEADS, X_HD) for a in (mk, mv)]
    x_at = 3 + len(kv_specs)
    return pl.pallas_call(
        body,
        grid=(n_seq // bb, nl),
        in_specs=[pl.BlockSpec((rows, d), rowmap), pl.BlockSpec((1, d, X_W), lambda b, l: (wl, 0, 0))] + kv_specs
                 + [pl.BlockSpec((1, X_W, d), lambda b, l: (wl, 0, 0)), pl.BlockSpec((rows, d), rowmap),
                    pl.BlockSpec((1, d), fixed), pl.BlockSpec((1, d), fixed)],
        out_specs=[pl.BlockSpec((rows, d), rowmap), pl.BlockSpec((rows, d), rowmap)],
        out_shape=[jax.ShapeDtypeStruct((t, d), F32), jax.ShapeDtypeStruct((t, d), BF16)],
        input_output_aliases={x_at: 0, 0: 1},
        compiler_params=_cparams("parallel", "parallel"),
        name="mem_cross_attention",
    )(h, wq, *kv_args, wo, x, g_post.reshape(1, d), g_next.reshape(1, d))


def _rwkv_prep_kernel(bb, lt, u_ref, s0_ref, mu_ref, w0_ref, wup_ref, a0_ref, aup_ref, gup_ref, kk_ref, ka_ref,
                      rk_ref, r_o, w_o, k_o, kk_o, b_o, v_o, g_o, bonus_o, shift_o, carry_ref):
    l = pl.program_id(1)
    rows = bb * lt
    u = u_ref[...]
    shift_o[...] = u[rows - 1:rows, :].reshape(1, 1, RW_COLS) if bb == 1 else _last_of_seq(u, bb, lt)
    if bb == 1:
        first = jnp.where(l == 0, s0_ref[0], carry_ref[...])
        first_rows = jnp.broadcast_to(first, (rows, RW_COLS))
    else:
        first_rows = jnp.broadcast_to(s0_ref[...], (bb, lt, RW_COLS)).reshape(rows, RW_COLS)
    rowid = lax.broadcasted_iota(jnp.int32, (rows, 1), 0) % lt
    prev = jnp.where(rowid == 0, first_rows, pltpu.roll(u, 1, 0))
    if bb == 1:
        carry_ref[...] = u[rows - 1:rows, :]
    z = u + (prev - u) * mu_ref[...]
    r = z[:, 0:MIX_W]
    k = z[:, MIX_W:2 * MIX_W]
    v = z[:, 2 * MIX_W:3 * MIX_W]
    o = 3 * MIX_W
    wd = z[:, o:o + RW_LORA[0]]
    ad = z[:, o + RW_LORA[0]:o + RW_LORA[0] + RW_LORA[1]]
    gd = z[:, o + RW_LORA[0] + RW_LORA[1]:RW_COLS]
    w_log = -_softplus(-(w0_ref[...] + _bdot(jnp.tanh(wd), wup_ref[...]))) - 0.5
    decay = jnp.exp(-jnp.exp(w_log))
    a = _sigmoid(a0_ref[...] + _bdot(ad, aup_ref[...]))
    g = _bdot(_sigmoid(gd), gup_ref[...])
    kk = k * kk_ref[...]
    kk = kk / jnp.maximum(jnp.sqrt(_segsum(kk * kk, RW_HD)), 1e-12)
    k = k * (1.0 + (a - 1.0) * ka_ref[...])
    r_o[...] = r
    w_o[...] = decay
    k_o[...] = k
    kk_o[...] = kk
    b_o[...] = kk * a
    v_o[...] = v
    g_o[...] = g
    bonus_o[...] = _segsum(r * k * rk_ref[...], RW_HD) * v


def _rwkv_prep(z, shift0, p, row0, n_seq, seq_len, bb, lt):
    rows = bb * lt
    nl = seq_len // lt
    assert bb == 1 or nl == 1
    rb0 = row0 // rows
    m = n_seq * seq_len
    fixed = lambda b, l: (0, 0)
    outmap = lambda b, l: (b * nl + l, 0)
    vec = lambda width: pl.BlockSpec((1, width), fixed)
    ospec = pl.BlockSpec((rows, MIX_W), outmap)
    oshape = jax.ShapeDtypeStruct((m, MIX_W), F32)
    return pl.pallas_call(
        functools.partial(_rwkv_prep_kernel, bb, lt),
        grid=(n_seq // bb, nl),
        in_specs=[pl.BlockSpec((rows, RW_COLS), lambda b, l: (rb0 + b * nl + l, 0)),
                  pl.BlockSpec((bb, 1, RW_COLS), lambda b, l: (b, 0, 0)),
                  vec(RW_COLS), vec(MIX_W), pl.BlockSpec((RW_LORA[0], MIX_W), fixed), vec(MIX_W),
                  pl.BlockSpec((RW_LORA[1], MIX_W), fixed), pl.BlockSpec((RW_LORA[2], MIX_W), fixed),
                  vec(MIX_W), vec(MIX_W), vec(MIX_W)],
        out_specs=[ospec] * 8 + [pl.BlockSpec((bb, 1, RW_COLS), lambda b, l: (b, 0, 0))],
        out_shape=[oshape] * 8 + [jax.ShapeDtypeStruct((n_seq, 1, RW_COLS), F32)],
        scratch_shapes=[pltpu.VMEM((1, RW_COLS), F32)],
        compiler_params=_cparams("parallel", "arbitrary"),
        name="rwkv_prep",
    )(z, shift0, p['mu'], p['w0'], p['w_up'], p['a0'], p['a_up'], p['g_up'], p['k_k'], p['k_a'], p['r_k'])


def _rwkv_scan_kernel(tc, r_ref, w_ref, k_ref, kk_ref, b_ref, v_ref, s0_ref, y_ref, s_ref):
    @pl.when(pl.program_id(1) == 0)
    def _():
        s_ref[...] = s0_ref[...]

    y_ref[0, :, RW_VP:, :] = jnp.zeros((tc, RW_VR - RW_VP, LANES), F32)
    keys = lambda ref, t: ref[0, t, :RW_HD, :]

    def step(t, carry):
        for vp in range(RW_VP):
            s = s_ref[0, vp]
            sa = jnp.sum(s * keys(kk_ref, t), axis=0, keepdims=True)
            sn = s * keys(w_ref, t) - sa * keys(b_ref, t) + v_ref[0, t, vp:vp + 1, :] * keys(k_ref, t)
            s_ref[0, vp] = sn
            y_ref[0, t, vp:vp + 1, :] = jnp.sum(sn * keys(r_ref, t), axis=0, keepdims=True)
        return carry

    lax.fori_loop(0, tc, step, 0)


def _rwkv_scan(r, w, k, kk, b, v, s0, lyr_in, tc, depth, lyr, s_prev):
    g, l = r.shape[0], r.shape[1]
    kspec = pl.BlockSpec((1, tc, RW_KR, LANES), lambda i, c: (i, c, 0, 0))
    vspec = pl.BlockSpec((1, tc, RW_VR, LANES), lambda i, c: (i, c, 0, 0))
    sspec = lambda at: pl.BlockSpec((1, RW_VP, RW_HD, LANES), lambda i, c: (at * g + i, 0, 0, 0))
    out_shape = [jax.ShapeDtypeStruct((g, l, RW_VR, LANES), F32),
                 jax.ShapeDtypeStruct((depth * g, RW_VP, RW_HD, LANES), F32)]
    kern, prev_specs, prev_args, aliases = _carry_through(functools.partial(_rwkv_scan_kernel, tc), 7,
                                                          [None, s_prev], out_shape, [False, depth > 1])
    return pl.pallas_call(
        kern,
        grid=(g, l // tc),
        in_specs=[kspec] * 5 + [vspec, sspec(lyr_in)] + prev_specs,
        out_specs=[vspec, sspec(lyr)],
        out_shape=out_shape,
        input_output_aliases=aliases,
        compiler_params=_cparams("parallel", "arbitrary"),
        name="rwkv_scan",
    )(r, w, k, kk, b, v, s0, *prev_args)


def _rwkv_post_kernel(y_ref, g_ref, bonus_ref, gng_ref, gnb_ref, o_ref):
    y = y_ref[...]
    mean = _segsum(y, RW_HD) * (1.0 / RW_HD)
    yc = y - mean
    var = _segsum(yc * yc, RW_HD) * (1.0 / RW_HD)
    yn = yc * lax.rsqrt(var + RW_GN_EPS) * gng_ref[...] + gnb_ref[...]
    o_ref[...] = ((yn + bonus_ref[...]) * g_ref[...]).astype(o_ref.dtype)


def _rwkv_post(y, g, bonus, gn_g, gn_b, row0, total_rows, prev, tm=256):
    m = y.shape[0]
    rb0 = row0 // tm
    fixed = lambda i: (0, 0)
    spec = pl.BlockSpec((tm, MIX_W), lambda i: (i, 0))
    out_shape = jax.ShapeDtypeStruct((total_rows, MIX_W), BF16)
    kern, prev_specs, prev_args, aliases = _carry_through(_rwkv_post_kernel, 5, [prev], [out_shape],
                                                          [m != total_rows])
    return pl.pallas_call(
        kern,
        grid=(m // tm,),
        in_specs=[spec, spec, spec, pl.BlockSpec((1, MIX_W), fixed), pl.BlockSpec((1, MIX_W), fixed)] + prev_specs,
        out_specs=pl.BlockSpec((tm, MIX_W), lambda i: (rb0 + i, 0)),
        out_shape=out_shape,
        input_output_aliases=aliases,
        compiler_params=_cparams("parallel"),
        name="rwkv_post",
    )(y, g, bonus, gn_g, gn_b, *prev_args)


RW_NB = RW_CHAINS // RW_HEADS
RW_TT = LANES


RW_KR = RW_HD + 8
RW_VR = RW_VP + 8


def _natural_to_chain_rows(x_ref, a_ref):
    for bl in range(RW_NB):
        xt = x_ref[bl].T
        for h in range(RW_HEADS):
            c = bl * RW_HEADS + h
            a_ref[c * RW_KR:c * RW_KR + RW_HD, :] = xt[h * RW_HD:(h + 1) * RW_HD]


def _keys_to_scan_kernel(x_ref, o_ref, a_ref):
    _natural_to_chain_rows(x_ref, a_ref)
    o_ref[...] = jnp.zeros_like(o_ref)
    for k in range(RW_HD):
        tile = a_ref[pl.ds(k, RW_CHAINS, stride=RW_KR), :]
        o_ref[0, pl.ds(k, RW_TT, stride=RW_KR), :] = jnp.concatenate([tile, tile], axis=0).T


def _vals_to_scan_kernel(x_ref, o_ref, a_ref):
    _natural_to_chain_rows(x_ref, a_ref)
    o_ref[...] = jnp.zeros_like(o_ref)
    for vp in range(RW_VP):
        halves = [a_ref[pl.ds(vh * RW_VP + vp, RW_CHAINS, stride=RW_KR), :] for vh in range(2)]
        o_ref[0, pl.ds(vp, RW_TT, stride=RW_VR), :] = jnp.concatenate(halves, axis=0).T


def _vals_from_scan_kernel(y_ref, o_ref, a_ref):
    for vp in range(RW_VP):
        tile = y_ref[0, pl.ds(vp, RW_TT, stride=RW_VR), :].T
        for vh in range(2):
            a_ref[pl.ds(vh * RW_VP + vp, RW_CHAINS, stride=RW_KR), :] = tile[vh * RW_CHAINS:(vh + 1) * RW_CHAINS]
    for bl in range(RW_NB):
        rows = [a_ref[c * RW_KR:c * RW_KR + RW_HD, :] for c in range(bl * RW_HEADS, (bl + 1) * RW_HEADS)]
        o_ref[bl] = jnp.concatenate(rows, axis=0).T


def _scan_relayout(kernel_fn, x, n_seq, seq_len, rows_per_t, to_scan, name):
    g, nl = n_seq // RW_NB, seq_len // RW_TT
    nat_spec = pl.BlockSpec((RW_NB, RW_TT, MIX_W), lambda i, l: (i, l, 0))
    scan_spec = pl.BlockSpec((1, RW_TT * rows_per_t, LANES), lambda i, l: (i, l, 0))
    nat_shape = jax.ShapeDtypeStruct((n_seq, seq_len, MIX_W), F32)
    scan_shape = jax.ShapeDtypeStruct((g, seq_len * rows_per_t, LANES), F32)
    x = x.reshape(n_seq, seq_len, MIX_W) if to_scan else x.reshape(g, seq_len * rows_per_t, LANES)
    out = pl.pallas_call(
        kernel_fn,
        grid=(g, nl),
        in_specs=[nat_spec if to_scan else scan_spec],
        out_specs=scan_spec if to_scan else nat_spec,
        out_shape=scan_shape if to_scan else nat_shape,
        scratch_shapes=[pltpu.VMEM((RW_CHAINS * RW_KR, RW_TT), F32)],
        compiler_params=_cparams("parallel", "parallel"),
        name=name,
    )(x)
    return out.reshape(g, seq_len, rows_per_t, LANES) if to_scan else out.reshape(n_seq * seq_len, MIX_W)


def _pad_rows(a, rows):
    return jnp.pad(a, ((0, 0), (0, 0), (0, rows - a.shape[2]), (0, 0)))


def _to_scan_keys(a, n_seq, seq_len):
    if seq_len % RW_TT == 0:
        return _scan_relayout(_keys_to_scan_kernel, a, n_seq, seq_len, RW_KR, True, "rwkv_keys_to_scan")
    g = n_seq // RW_NB
    a = a.reshape(g, RW_NB, seq_len, RW_HEADS, RW_HD).transpose(0, 2, 4, 1, 3).reshape(g, seq_len, RW_HD, RW_CHAINS)
    return _pad_rows(jnp.concatenate([a, a], axis=-1), RW_KR)


def _to_scan_vals(a, n_seq, seq_len):
    if seq_len % RW_TT == 0:
        return _scan_relayout(_vals_to_scan_kernel, a, n_seq, seq_len, RW_VR, True, "rwkv_vals_to_scan")
    g = n_seq // RW_NB
    a = a.reshape(g, RW_NB, seq_len, RW_HEADS, 2, RW_VP).transpose(0, 2, 5, 4, 1, 3)
    return _pad_rows(a.reshape(g, seq_len, RW_VP, LANES), RW_VR)


def _from_scan_vals(y, n_seq, seq_len):
    if seq_len % RW_TT == 0:
        return _scan_relayout(_vals_from_scan_kernel, y, n_seq, seq_len, RW_VR, False, "rwkv_vals_from_scan")
    g = n_seq // RW_NB
    y = y[:, :, :RW_VP].reshape(g, seq_len, RW_VP, 2, RW_NB, RW_HEADS).transpose(0, 4, 1, 5, 3, 2)
    return y.reshape(n_seq * seq_len, MIX_W)


def _state_to_scan(s, n_seq):
    nb = RW_CHAINS // RW_HEADS
    g = n_seq // nb
    s = s.reshape(g, nb, RW_HEADS, 2, RW_VP, RW_HD).transpose(0, 4, 5, 3, 1, 2)
    return s.reshape(g, RW_VP, RW_HD, LANES)


def _state_from_scan(s, n_seq):
    nb = RW_CHAINS // RW_HEADS
    g = n_seq // nb
    s = s.reshape(g, RW_VP, RW_HD, 2, nb, RW_HEADS).transpose(0, 4, 5, 3, 1, 2)
    return s.reshape(n_seq, RW_HEADS, RW_HD, RW_HD)


def _rwkv_branch(z, shift0, s0, lyr_in, p, row0, n_seq, seq_len, bb, lt, tc, depth, lyr, prev):
    r, w, k, kk, b, v, g, bonus, shift = _rwkv_prep(z, shift0, p, row0, n_seq, seq_len, bb, lt)
    keys = [_to_scan_keys(a, n_seq, seq_len) for a in (r, w, k, kk, b)]
    y, s_new = _rwkv_scan(*keys, _to_scan_vals(v, n_seq, seq_len), s0, lyr_in, tc, depth, lyr, prev[1])
    y = _from_scan_vals(y, n_seq, seq_len)
    y = _rwkv_post(y, g, bonus, p['gn_g'], p['gn_b'], row0, z.shape[0], prev[0])
    return y, s_new, shift.reshape(n_seq, RW_COLS)


def _seq_masks(rows, lc):
    ri = lax.broadcasted_iota(jnp.int32, (rows, rows), 0)
    ci = lax.broadcasted_iota(jnp.int32, (rows, rows), 1)
    same = (ri // lc) == (ci // lc)
    return ri, ci, same


def _rows_from_seq(x, bb, lc):
    return jnp.broadcast_to(x, (bb, lc, x.shape[-1])).reshape(bb * lc, x.shape[-1])


def _last_of_seq(x, bb, lc):
    x3 = x.reshape(bb, lc, x.shape[-1])
    pick = (lax.broadcasted_iota(jnp.int32, (1, lc, 1), 1) == lc - 1).astype(F32)
    return jnp.sum(x3 * pick, axis=1, keepdims=True)


def _transpose_rows(x):
    rows = x.shape[0]
    if rows < LANES:
        x = jnp.concatenate([x, jnp.zeros((LANES - rows, LANES), x.dtype)], axis=0)
    return x.T[:, :rows]


def _mlstm_kernel(bb, lc, q_ref, k_ref, v_ref, o_ref, gt_ref, gb_ref, ng_ref, c0_ref, n0_ref, m0_ref,
                  y_ref, c_ref, n_ref, m_ref):
    rows = bb * lc

    @pl.when(pl.program_id(1) == 0)
    def _():
        c_ref[...] = c0_ref[...]
        n_ref[...] = n0_ref[...]
        m_ref[...] = m0_ref[...]

    gt = ML_GATE_CAP * jnp.tanh((gt_ref[...] + gb_ref[...]) * (1.0 / ML_GATE_CAP))
    lane = lax.broadcasted_iota(jnp.int32, (rows, LANES), 1)
    ri, ci, same = _seq_masks(rows, lc)
    causal = same & (ci <= ri)
    cum = _dot_exact_lhs(causal.astype(F32), -_softplus(-gt))
    gt_t = _transpose_rows(gt)
    cum_t = _transpose_rows(cum)
    for hd in range(ML_HEADS):
        sl = slice(hd * ML_HD, (hd + 1) * ML_HD)
        c0 = c_ref[0, :, hd]
        n0 = n_ref[0, :, hd:hd + 1, :]
        m0 = m_ref[0, :, :, hd:hd + 1]
        q = q_ref[:, sl]
        k = k_ref[:, sl] * (ML_HD ** -0.5)
        v = v_ref[:, sl]
        bcol = cum[:, ML_HEADS + hd:ML_HEADS + hd + 1]
        igcol = gt[:, hd:hd + 1]
        rowb = gt_t[hd:hd + 1, :] - cum_t[ML_HEADS + hd:ML_HEADS + hd + 1, :]
        dlog = jnp.where(causal, bcol + rowb, -jnp.inf)
        m_inter = bcol + _rows_from_seq(m0, bb, lc)
        m_t = jnp.maximum(m_inter, jnp.max(dlog, axis=-1, keepdims=True))
        qk = lax.dot_general(q.astype(BF16), k.astype(BF16), (((1,), (1,)), ((), ())),
                             preferred_element_type=F32)
        wts = jnp.exp(dlog - m_t) * qk
        s_inter = jnp.exp(m_inter - m_t)
        q3 = q.reshape(bb, lc, ML_HD).astype(BF16)
        qc = jnp.einsum('btd,bde->bte', q3, c0.astype(BF16), preferred_element_type=F32).reshape(rows, ML_HD)
        num = _bdot(wts, v) + s_inter * qc
        qn = jnp.sum(q * _rows_from_seq(n0, bb, lc), axis=-1, keepdims=True)
        den = jnp.sum(wts, axis=-1, keepdims=True) + s_inter * qn
        hh = num / jnp.maximum(jnp.abs(den), jnp.exp(-m_t))
        hh = hh * lax.rsqrt(jnp.mean(hh * hh, axis=-1, keepdims=True) + EPS) * ng_ref[:, sl]
        y_ref[:, sl] = (_sigmoid(o_ref[:, sl]) * hh).astype(y_ref.dtype)

        stat = jnp.where(lane == 0, bcol, jnp.where(lane == 1, m_t, 0.0))
        last = _last_of_seq(stat, bb, lc)
        last_rows = _rows_from_seq(last, bb, lc)
        b_last, m_new = last_rows[:, 0:1], last_rows[:, 1:2]
        w_end = jnp.exp(b_last - bcol + igcol - m_new)
        f_end = jnp.exp(last[:, :, 0:1] + m0 - last[:, :, 1:2])
        kw3 = (k * w_end).reshape(bb, lc, ML_HD)
        upd = jnp.einsum('bsd,bse->bde', kw3.astype(BF16), v.reshape(bb, lc, ML_HD).astype(BF16),
                         preferred_element_type=F32)
        c_ref[0, :, hd] = f_end * c0 + upd
        n_ref[0, :, hd:hd + 1, :] = f_end * n0 + jnp.sum(kw3, axis=1, keepdims=True)
        m_ref[0, :, :, hd:hd + 1] = last[:, :, 1:2]


def _mlstm_seq_kernel(lc, q_ref, k_ref, v_ref, o_ref, gt_ref, gb_ref, ng_ref, c0_ref, n0_ref, m0_ref,
                      y_ref, c_ref, n_ref, m_ref):
    heads = range(ML_HEADS)

    @pl.when(pl.program_id(1) == 0)
    def _():
        c_ref[...] = c0_ref[...]
        n_ref[...] = n0_ref[...]
        m_ref[...] = m0_ref[...]

    gt = ML_GATE_CAP * jnp.tanh((gt_ref[...] + gb_ref[...]) * (1.0 / ML_GATE_CAP))
    ri, ci, _ = _seq_masks(lc, lc)
    causal = ci <= ri
    cum = _dot_exact_lhs(causal.astype(F32), -_softplus(-gt))
    gt_t = _transpose_rows(gt)
    cum_t = _transpose_rows(cum)
    sl = lambda hd: slice(hd * ML_HD, (hd + 1) * ML_HD)
    stack = lambda f: jnp.stack([f(hd) for hd in heads])
    q = stack(lambda hd: q_ref[:, sl(hd)])
    k = stack(lambda hd: k_ref[:, sl(hd)]) * (ML_HD ** -0.5)
    v = stack(lambda hd: v_ref[:, sl(hd)])
    bcol = stack(lambda hd: cum[:, ML_HEADS + hd:ML_HEADS + hd + 1])
    igcol = stack(lambda hd: gt[:, hd:hd + 1])
    rowb = stack(lambda hd: gt_t[hd:hd + 1, :] - cum_t[ML_HEADS + hd:ML_HEADS + hd + 1, :])
    c0 = c_ref[0, 0]
    n0 = stack(lambda hd: n_ref[0, 0, hd:hd + 1, :])
    m0 = stack(lambda hd: m_ref[0, 0, :, hd:hd + 1])
    qb, kb, vb = q.astype(BF16), k.astype(BF16), v.astype(BF16)

    dlog = jnp.where(causal[None], bcol + rowb, -jnp.inf)
    m_inter = bcol + m0
    m_t = jnp.maximum(m_inter, jnp.max(dlog, axis=-1, keepdims=True))
    qk = jnp.einsum('hqd,hkd->hqk', qb, kb, preferred_element_type=F32)
    wts = jnp.exp(dlog - m_t) * qk
    s_inter = jnp.exp(m_inter - m_t)
    qc = jnp.einsum('hqd,hde->hqe', qb, c0.astype(BF16), preferred_element_type=F32)
    num = jnp.einsum('hqk,hke->hqe', wts.astype(BF16), vb, preferred_element_type=F32) + s_inter * qc
    den = jnp.sum(wts, axis=-1, keepdims=True) + s_inter * jnp.sum(q * n0, axis=-1, keepdims=True)
    hh = num / jnp.maximum(jnp.abs(den), jnp.exp(-m_t))
    hh = hh * lax.rsqrt(jnp.mean(hh * hh, axis=-1, keepdims=True) + EPS)
    for hd in heads:
        y_ref[:, sl(hd)] = (_sigmoid(o_ref[:, sl(hd)]) * hh[hd] * ng_ref[:, sl(hd)]).astype(y_ref.dtype)

    b_last, m_new = bcol[:, lc - 1:lc, :], m_t[:, lc - 1:lc, :]
    w_end = jnp.exp(b_last - bcol + igcol - m_new)
    f_end = jnp.exp(b_last + m0 - m_new)
    kw = k * w_end
    upd = jnp.einsum('hsd,hse->hde', kw.astype(BF16), vb, preferred_element_type=F32)
    c_ref[0, 0] = f_end * c0 + upd
    n_new = f_end * n0 + jnp.sum(kw, axis=1, keepdims=True)
    for hd in heads:
        n_ref[0, 0, hd:hd + 1, :] = n_new[hd]
        m_ref[0, 0, :, hd:hd + 1] = m_new[hd]


def _mlstm_branch(z, z_gates, init, lyr_in, gate_bias, norm_g, row0, n_seq, seq_len, bb, lc, depth, lyr, prev):
    rows = bb * lc
    nc = seq_len // lc
    rb0 = row0 // rows
    rowmap = lambda part: (lambda b, c: (rb0 + b * nc + c, part))
    zspec = lambda part: pl.BlockSpec((rows, MIX_W), rowmap(part))
    state_specs = lambda at: [pl.BlockSpec((1, bb, ML_HEADS, ML_HD, ML_HD), lambda b, c: (at, b, 0, 0, 0)),
                              pl.BlockSpec((1, bb, ML_HEADS, ML_HD), lambda b, c: (at, b, 0, 0)),
                              pl.BlockSpec((1, bb, 1, ML_HEADS), lambda b, c: (at, b, 0, 0))]
    body = functools.partial(_mlstm_seq_kernel, lc) if bb == 1 else functools.partial(_mlstm_kernel, bb, lc)
    out_shape = [jax.ShapeDtypeStruct((z.shape[0], MIX_W), BF16),
                 jax.ShapeDtypeStruct((depth, n_seq, ML_HEADS, ML_HD, ML_HD), F32),
                 jax.ShapeDtypeStruct((depth, n_seq, ML_HEADS, ML_HD), F32),
                 jax.ShapeDtypeStruct((depth, n_seq, 1, ML_HEADS), F32)]
    kern, prev_specs, prev_args, aliases = _carry_through(
        body, 10, prev, out_shape, [n_seq * seq_len != z.shape[0]] + [depth > 1] * 3)
    return pl.pallas_call(
        kern,
        grid=(n_seq // bb, nc),
        in_specs=[zspec(0), zspec(1), zspec(2), zspec(3),
                  pl.BlockSpec((rows, LANES), lambda b, c: (rb0 + b * nc + c, 4 * MIX_W // LANES)),
                  pl.BlockSpec((1, LANES), lambda b, c: (0, 0)), pl.BlockSpec((1, MIX_W), lambda b, c: (0, 0))]
                 + state_specs(lyr_in) + prev_specs,
        out_specs=[pl.BlockSpec((rows, MIX_W), lambda b, c: (rb0 + b * nc + c, 0))] + state_specs(lyr),
        out_shape=out_shape,
        input_output_aliases=aliases,
        compiler_params=_cparams("parallel", "arbitrary"),
        name="mlstm_chunk",
    )(z, z, z, z, z_gates, gate_bias, norm_g, *init, *prev_args)


def _rope_rows(x, cos, sin):
    half = RT_HD // 2
    x1, x2 = x[:, :half], x[:, half:]
    return jnp.concatenate([x1 * cos - x2 * sin, x1 * sin + x2 * cos], axis=1)


def _ret_kernel(bb, lc, q_ref, k_ref, v_ref, g_ref, cs_ref, dec_ref, cd_ref, s0_ref, y_ref, s_ref):
    rows = bb * lc

    @pl.when(pl.program_id(1) == 0)
    def _():
        s_ref[...] = s0_ref[...]

    cos, sin = cs_ref[:, :RT_HD // 2], cs_ref[:, RT_HD // 2:]
    for hd in range(RT_HEADS):
        sl = slice(hd * RT_HD, (hd + 1) * RT_HD)
        s0 = s_ref[0, :, hd]
        q = _rope_rows(q_ref[:, sl], cos, sin)
        k = _rope_rows(k_ref[:, sl], cos, sin) * (RT_HD ** -0.5)
        v = v_ref[:, sl]
        cd = cd_ref[hd]
        qk = lax.dot_general(q.astype(BF16), k.astype(BF16), (((1,), (1,)), ((), ())),
                             preferred_element_type=F32)
        inner = _bdot(qk * dec_ref[hd], v)
        q3 = q.reshape(bb, lc, RT_HD).astype(BF16)
        cross = jnp.einsum('btd,bde->bte', q3, s0.astype(BF16), preferred_element_type=F32).reshape(rows, RT_HD)
        y = inner + cross * cd[:, 0:1]
        y = y * lax.rsqrt(jnp.mean(y * y, axis=-1, keepdims=True) + EPS)
        gt = g_ref[:, sl]
        y_ref[:, sl] = (gt * _sigmoid(gt) * y).astype(y_ref.dtype)
        kd = (k * cd[:, 1:2]).reshape(bb, lc, RT_HD).astype(BF16)
        upd = jnp.einsum('bsd,bse->bde', kd, v.reshape(bb, lc, RT_HD).astype(BF16), preferred_element_type=F32)
        s_ref[0, :, hd] = cd[0:1, 2:3] * s0 + upd


def _ret_seq_kernel(lc, q_ref, k_ref, v_ref, g_ref, cs_ref, dec_ref, cd_ref, s0_ref, y_ref, s_ref):
    heads = range(RT_HEADS)

    @pl.when(pl.program_id(1) == 0)
    def _():
        s_ref[...] = s0_ref[...]

    cos, sin = cs_ref[:, :RT_HD // 2], cs_ref[:, RT_HD // 2:]
    sl = lambda hd: slice(hd * RT_HD, (hd + 1) * RT_HD)
    stack = lambda f: jnp.stack([f(hd) for hd in heads])
    q = stack(lambda hd: _rope_rows(q_ref[:, sl(hd)], cos, sin))
    k = stack(lambda hd: _rope_rows(k_ref[:, sl(hd)], cos, sin)) * (RT_HD ** -0.5)
    vb = stack(lambda hd: v_ref[:, sl(hd)]).astype(BF16)
    s0 = s_ref[0, 0]
    cd = cd_ref[...]
    qb = q.astype(BF16)
    qk = jnp.einsum('hqd,hkd->hqk', qb, k.astype(BF16), preferred_element_type=F32)
    inner = jnp.einsum('hqk,hke->hqe', (qk * dec_ref[...]).astype(BF16), vb, preferred_element_type=F32)
    cross = jnp.einsum('hqd,hde->hqe', qb, s0.astype(BF16), preferred_element_type=F32)
    y = inner + cross * cd[:, :, 0:1]
    y = y * lax.rsqrt(jnp.mean(y * y, axis=-1, keepdims=True) + EPS)
    for hd in heads:
        gt = g_ref[:, sl(hd)]
        y_ref[:, sl(hd)] = (gt * _sigmoid(gt) * y[hd]).astype(y_ref.dtype)
    kd = (k * cd[:, :, 1:2]).astype(BF16)
    upd = jnp.einsum('hsd,hse->hde', kd, vb, preferred_element_type=F32)
    s_ref[0, 0] = cd[:, 0:1, 2:3] * s0 + upd


def _ret_branch(z, s0, lyr_in, row0, n_seq, seq_len, bb, lc, pos0, depth, lyr, prev):
    rows = bb * lc
    nc = seq_len // lc
    rb0 = row0 // rows
    half = RT_HD // 2
    inv = ROPE_BASE ** (-jnp.arange(half, dtype=F32) / half)
    pos = jnp.broadcast_to((jnp.arange(nc * lc, dtype=F32) + float(pos0)).reshape(nc, 1, lc), (nc, bb, lc))
    ang = pos.reshape(nc * rows)[:, None] * inv[None, :]
    cs = jnp.concatenate([jnp.cos(ang), jnp.sin(ang)], axis=1)
    log_g = jnp.log(1.0 - jnp.exp(jnp.linspace(math.log(1.0 / 32), math.log(1.0 / 512), RT_HEADS)))
    idx = jnp.arange(rows)
    t = (idx % lc).astype(F32)
    diff = t[:, None] - t[None, :]
    same = (idx[:, None] // lc) == (idx[None, :] // lc)
    dec = jnp.where(same & (diff >= 0), jnp.exp(log_g[:, None, None] * jnp.maximum(diff, 0.0)), 0.0)
    cd = jnp.zeros((RT_HEADS, rows, LANES), F32)
    cd = cd.at[:, :, 0].set(jnp.exp(log_g[:, None] * (t[None, :] + 1.0)))
    cd = cd.at[:, :, 1].set(jnp.exp(log_g[:, None] * (lc - 1.0 - t)[None, :]))
    cd = cd.at[:, :, 2].set(jnp.broadcast_to(jnp.exp(log_g * lc)[:, None], (RT_HEADS, rows)))
    rowmap = lambda part: (lambda b, c: (rb0 + b * nc + c, part))
    zspec = lambda part: pl.BlockSpec((rows, MIX_W), rowmap(part))
    sspec = lambda at: pl.BlockSpec((1, bb, RT_HEADS, RT_HD, RT_HD), lambda b, c: (at, b, 0, 0, 0))
    body = functools.partial(_ret_seq_kernel, lc) if bb == 1 else functools.partial(_ret_kernel, bb, lc)
    out_shape = [jax.ShapeDtypeStruct((z.shape[0], MIX_W), BF16),
                 jax.ShapeDtypeStruct((depth, n_seq, RT_HEADS, RT_HD, RT_HD), F32)]
    kern, prev_specs, prev_args, aliases = _carry_through(body, 8, prev, out_shape,
                                                          [n_seq * seq_len != z.shape[0], depth > 1])
    return pl.pallas_call(
        kern,
        grid=(n_seq // bb, nc),
        in_specs=[zspec(0), zspec(1), zspec(2), zspec(3),
                  pl.BlockSpec((rows, RT_HD), lambda b, c: (c, 0)),
                  pl.BlockSpec((RT_HEADS, rows, rows), lambda b, c: (0, 0, 0)),
                  pl.BlockSpec((RT_HEADS, rows, LANES), lambda b, c: (0, 0, 0)),
                  sspec(lyr_in)] + prev_specs,
        out_specs=[pl.BlockSpec((rows, MIX_W), lambda b, c: (rb0 + b * nc + c, 0)), sspec(lyr)],
        out_shape=out_shape,
        input_output_aliases=aliases,
        compiler_params=_cparams("parallel", "arbitrary"),
        name="retention_chunk",
    )(z, z, z, z, cs, dec, cd, s0, *prev_args)


def _mm_wt_kernel(x_ref, wt_ref, o_ref, wb_ref):
    @pl.when(pl.program_id(1) == 0)
    def _():
        tn = wb_ref.shape[1]
        for c in range(0, tn, LANES):
            wb_ref[:, c:c + LANES] = wt_ref[0, c:c + LANES, :].T.astype(wb_ref.dtype)

    o_ref[...] = jnp.dot(x_ref[...], wb_ref[...], preferred_element_type=F32)


def _mm_wt(x, w_t, lyr, col0, n, tm, tn):
    m, k = x.shape
    return pl.pallas_call(
        _mm_wt_kernel,
        grid=(n // tn, m // tm),
        in_specs=[pl.BlockSpec((tm, k), lambda j, i: (i, 0)),
                  pl.BlockSpec((pl.Element(1), pl.Element(tn), pl.Element(k)),
                               lambda j, i: (lyr, pl.multiple_of(col0 + j * tn, 8), 0))],
        out_specs=pl.BlockSpec((tm, tn), lambda j, i: (i, j)),
        out_shape=jax.ShapeDtypeStruct((m, n), F32),
        scratch_shapes=[pltpu.VMEM((k, tn), BF16)],
        compiler_params=_cparams("parallel", "arbitrary"),
        name="matmul_wt",
    )(x, w_t)


def _group_cfg(n_seq, seq_len):
    if seq_len % CHUNK == 0:
        return dict(rw_bb=1, rw_lt=256, rw_tc=64, ml_bb=1, rt_bb=1, ch_lc=CHUNK, at_bb=1, at_lq=512)
    return dict(rw_bb=32, rw_lt=seq_len, rw_tc=seq_len, ml_bb=16, rt_bb=8, ch_lc=seq_len, at_bb=8, at_lq=seq_len)


def kernel(x_prompt, x_sample, mem_prompt, state_rwkv_shift, state_rwkv, state_mlstm_c, state_mlstm_n,
           state_mlstm_m, state_ret, cache_mem_k, cache_mem_v, g_pre_mix, g_post_mix, g_pre_x, g_post_x,
           g_pre_ff, g_post_ff, g_mem, w_in, rw_mu, rw_w0, rw_w_up, rw_a0, rw_a_up, rw_g_up, rw_k_k, rw_k_a,
           rw_r_k, rw_gn_g, rw_gn_b, ml_i_b, ml_f_b, ml_norm_g, w_br, w_out, x_wq, x_wkv, x_wo, ff_w1, ff_w2):
    bp, lp, d = x_prompt.shape
    bs, ls, _ = x_sample.shape
    depth = w_in.shape[0]
    n_mem = mem_prompt.shape[1]
    tp, ts = bp * lp, bs * ls
    groups = [(0, bp, lp, 0), (tp, bs, ls, PAST_LEN)]

    x_groups = [x_prompt.reshape(tp, d), x_sample.reshape(ts, d)]
    mem = mem_prompt.reshape(bp * n_mem, d)
    x = h = None
    for (row0, *_), xg in zip(groups, x_groups):
        h = _rmsnorm(xg, g_pre_mix[0], row0=row0, total_rows=tp + ts, prev=h)

    zeros = lambda *s: jnp.zeros(s, F32)
    rw_scan0 = _state_to_scan(state_rwkv.reshape(depth * bs, RW_HEADS, RW_HD, RW_HD), depth * bs)
    inits = [
        lambda l: (zeros(bp, 1, RW_COLS), zeros(bp // RW_NB, RW_VP, RW_HD, LANES),
                   (zeros(1, bp, ML_HEADS, ML_HD, ML_HD), zeros(1, bp, ML_HEADS, ML_HD), zeros(1, bp, 1, ML_HEADS)),
                   zeros(1, bp, RT_HEADS, RT_HD, RT_HD), 0),
        lambda l: (state_rwkv_shift[l].reshape(bs, 1, RW_COLS), rw_scan0,
                   (state_mlstm_c, state_mlstm_n, state_mlstm_m.reshape(depth, bs, 1, ML_HEADS)), state_ret, l),
    ]
    shifts = [[], []]
    rw_states = [None, None]
    ml_states = [[None] * 3, [None] * 3]
    rt_states = [None, None]
    mem_k, mem_v = [], []
    w_in_t = jnp.swapaxes(w_in, 1, 2)
    w_kv, w_brb, w_outb, w_q, w_o, w_f1, w_f2 = (w.astype(BF16) for w in (x_wkv, w_br, w_out, x_wq, x_wo,
                                                                         ff_w1, ff_w2))
    y_rw = y_ml = y_rt = None
    for l in range(depth):
        kv = _mm(_rmsnorm(mem, g_mem[l], tm=256), w_kv, l, tm=256, tn=512)
        mem_k.append(kv[:, :X_W].reshape(bp, n_mem, X_HEADS, X_HD))
        mem_v.append(kv[:, X_W:].reshape(bp, n_mem, X_HEADS, X_HD))

        z_rw = _mm_wt(h, w_in_t, l, 0, RW_COLS, tm=512, tn=RW_COLS // 2)
        z_ml = _mm_wt(h, w_in_t, l, IN_ML, Z_MLW, tm=1024, tn=Z_MLW // 3)
        z_rg = _mm_wt(h, w_in_t, l, IN_RT, w_in.shape[2] - IN_RT, tm=1024, tn=1280)

        rw_p = dict(mu=rw_mu[l].reshape(1, -1), w0=rw_w0[l].reshape(1, -1), w_up=rw_w_up[l].astype(BF16),
                    a0=rw_a0[l].reshape(1, -1), a_up=rw_a_up[l].astype(BF16), g_up=rw_g_up[l].astype(BF16),
                    k_k=rw_k_k[l].reshape(1, -1), k_a=rw_k_a[l].reshape(1, -1), r_k=rw_r_k[l].reshape(1, -1),
                    gn_g=rw_gn_g[l].reshape(1, -1), gn_b=rw_gn_b[l].reshape(1, -1))
        gate_bias = jnp.concatenate([ml_i_b[l], ml_f_b[l], zeros(LANES - 2 * ML_HEADS)]).reshape(1, LANES)
        norm_g = ml_norm_g[l].reshape(1, MIX_W)

        for gi, (row0, n_seq, seq_len, pos0) in enumerate(groups):
            cfg = _group_cfg(n_seq, seq_len)
            shift0, s_rw0, ml0, rt0, lyr_in = inits[gi](l)
            y_rw, rw_states[gi], shift = _rwkv_branch(z_rw, shift0, s_rw0, lyr_in, rw_p, row0, n_seq, seq_len,
                                                      cfg['rw_bb'], cfg['rw_lt'], cfg['rw_tc'], depth, l,
                                                      (y_rw, rw_states[gi]))
            shifts[gi].append(shift)
            y_ml, *ml_states[gi] = _mlstm_branch(z_ml, z_ml, ml0, lyr_in, gate_bias, norm_g, row0, n_seq, seq_len,
                                                 cfg['ml_bb'], cfg['ch_lc'], depth, l, [y_ml] + ml_states[gi])
            y_rt, rt_states[gi] = _ret_branch(z_rg, rt0, lyr_in, row0, n_seq, seq_len, cfg['rt_bb'],
                                              cfg['ch_lc'], pos0, depth, l, [y_rt, rt_states[gi]])

        merged = _merge(y_rw, y_ml, y_rt, w_brb, l, z_rg, Z_GATE)
        if x is None:
            xh = (None, h)
            for (row0, *_), xg in zip(groups, x_groups):
                xh = _proj_res(merged, w_outb, l, xg, g_post_mix[l], g_pre_x[l], row0=row0, prev=xh)
            x, h = xh
        else:
            x, h = _proj_res(merged, w_outb, l, x, g_post_mix[l], g_pre_x[l])

        for gi, (row0, n_seq, seq_len, pos0) in enumerate(groups):
            cfg = _group_cfg(n_seq, seq_len)
            if gi == 0:
                mk, mv, lyr_in = mem_k[l][None], mem_v[l][None], 0
            else:
                mk, mv, lyr_in = cache_mem_k, cache_mem_v, l
            x, h = _attn(h, x, mk, mv, lyr_in, w_q, w_o, l, g_post_x[l], g_pre_ff[l], row0, n_seq, seq_len,
                         cfg['at_bb'], cfg['at_lq'])

        if l + 1 < depth:
            x, h = _mlp(h, w_f1, w_f2, l, x, g_post_ff[l], g_pre_mix[l + 1])
        else:
            y_out = [_mlp(h, w_f1, w_f2, l, x, g_post_ff[l], g_pre_mix[0], row0=row0, n_rows=n_seq * seq_len)[0]
                     for row0, n_seq, seq_len, _ in groups]

    def states(gi, n_seq):
        c_new, n_new, m_new = ml_states[gi]
        s_rw = _state_from_scan(rw_states[gi], depth * n_seq).reshape(depth, n_seq, RW_HEADS, RW_HD, RW_HD)
        return (jnp.stack(shifts[gi]), s_rw, c_new, n_new, m_new.reshape(depth, n_seq, ML_HEADS), rt_states[gi])

    return (y_out[0].reshape(bp, lp, d), y_out[1].reshape(bs, ls, d), *states(0, bp),
            jnp.stack(mem_k), jnp.stack(mem_v), *states(1, bs))
```

```python
import functools
import math

import jax
import jax.numpy as jnp
from jax import lax
from jax.experimental import pallas as pl
from jax.experimental.pallas import tpu as pltpu

F32 = jnp.float32
BF16 = jnp.bfloat16

D_MODEL = 2048
MIX_W = D_MODEL // 2
RW_HD = 64
RW_HEADS = MIX_W // RW_HD
RW_LORA = (64, 64, 128)
RW_COLS = 3 * MIX_W + sum(RW_LORA)
RW_GN_EPS = 64e-5
ML_HEADS = 8
ML_HD = MIX_W // ML_HEADS
ML_GATE_CAP = 15.0
RT_HEADS = 4
RT_HD = MIX_W // RT_HEADS
ROPE_BASE = 10000.0
X_HEADS = 4
X_HD = 128
X_W = X_HEADS * X_HD
D_FF = 4 * D_MODEL
CHUNK = 64
EPS = 1e-6
PAST_LEN = 16384

Z_MLW = 4 * (D_MODEL // 2) + 128
IN_ML = RW_COLS
IN_MLG = IN_ML + 4 * MIX_W
IN_RT = IN_MLG + 2 * ML_HEADS
Z_GATE = 4 * MIX_W

LANES = 128
RW_VP = RW_HD // 2
RW_CHAINS = LANES // 2


def _cparams(*sem):
    return pltpu.CompilerParams(dimension_semantics=sem, vmem_limit_bytes=56 * 1024 * 1024)


def _carry_through(kernel_fn, n_in, prev, out_shapes=(), shared=()):
    prev = [jnp.zeros(out_shapes[i].shape, out_shapes[i].dtype) if p is None and i < len(shared) and shared[i]
            else p for i, p in enumerate(prev)]
    carried = [(i, p) for i, p in enumerate(prev) if p is not None]
    if not carried:
        return kernel_fn, [], [], {}

    def wrapped(*refs):
        return kernel_fn(*refs[:n_in], *refs[n_in + len(carried):])

    return (wrapped, [pl.BlockSpec(memory_space=pl.ANY)] * len(carried), [p for _, p in carried],
            {n_in + j: i for j, (i, _) in enumerate(carried)})


def _sigmoid(x):
    return 1.0 / (1.0 + jnp.exp(-x))


def _softplus(x):
    return jnp.maximum(x, 0.0) + jnp.log(1.0 + jnp.exp(-jnp.abs(x)))


def _bdot(a, b):
    return jnp.dot(a.astype(BF16), b.astype(BF16), preferred_element_type=F32)


def _split3(x):
    hi = x.astype(BF16)
    r1 = x - hi.astype(F32)
    mid = r1.astype(BF16)
    lo = (r1 - mid.astype(F32)).astype(BF16)
    return hi, mid, lo


def _dot_exact_lhs(m, x):
    mb = m.astype(BF16)
    hi, mid, lo = _split3(x)
    return (jnp.dot(mb, hi, preferred_element_type=F32) + jnp.dot(mb, mid, preferred_element_type=F32)
            + jnp.dot(mb, lo, preferred_element_type=F32))


def _dot_exact_rhs(x, m):
    mb = m.astype(BF16)
    hi, mid, lo = _split3(x)
    return (jnp.dot(hi, mb, preferred_element_type=F32) + jnp.dot(mid, mb, preferred_element_type=F32)
            + jnp.dot(lo, mb, preferred_element_type=F32))


def _segsum(x, seg):
    blk = 256
    ri = lax.broadcasted_iota(jnp.int32, (blk, blk), 0) // seg
    ci = lax.broadcasted_iota(jnp.int32, (blk, blk), 1) // seg
    ones = (ri == ci).astype(F32)
    parts = [_dot_exact_rhs(x[:, c:c + blk], ones) for c in range(0, x.shape[1], blk)]
    return parts[0] if len(parts) == 1 else jnp.concatenate(parts, axis=1)


def _rms_rows(x, g):
    return x * lax.rsqrt(jnp.mean(x * x, axis=-1, keepdims=True) + EPS) * g


def _rmsnorm_kernel(x_ref, g_ref, o_ref):
    o_ref[...] = _rms_rows(x_ref[...], g_ref[...]).astype(o_ref.dtype)


def _rmsnorm(x, g, tm=512, row0=0, total_rows=None, prev=None):
    m, d = x.shape
    rb0 = row0 // tm
    out_shape = jax.ShapeDtypeStruct((total_rows or m, d), BF16)
    kern, prev_specs, prev_args, aliases = _carry_through(_rmsnorm_kernel, 2, [prev], [out_shape],
                                                          [out_shape.shape[0] != m])
    return pl.pallas_call(
        kern,
        grid=(m // tm,),
        in_specs=[pl.BlockSpec((tm, d), lambda i: (i, 0)), pl.BlockSpec((1, d), lambda i: (0, 0))] + prev_specs,
        out_specs=pl.BlockSpec((tm, d), lambda i: (rb0 + i, 0)),
        out_shape=out_shape,
        input_output_aliases=aliases,
        compiler_params=_cparams("parallel"),
        name="rmsnorm",
    )(x, g.reshape(1, d), *prev_args)


def _mm_kernel(x_ref, w_ref, o_ref):
    o_ref[...] = jnp.dot(x_ref[...], w_ref[0], preferred_element_type=F32)


def _mm(x, w, lyr, tm, tn):
    m, k = x.shape
    n = w.shape[2]
    return pl.pallas_call(
        _mm_kernel,
        grid=(n // tn, m // tm),
        in_specs=[pl.BlockSpec((tm, k), lambda j, i: (i, 0)), pl.BlockSpec((1, k, tn), lambda j, i: (lyr, 0, j))],
        out_specs=pl.BlockSpec((tm, tn), lambda j, i: (i, j)),
        out_shape=jax.ShapeDtypeStruct((m, n), F32),
        compiler_params=_cparams("parallel", "parallel"),
        name="matmul",
    )(x, w)


def _merge_kernel(y0_ref, y1_ref, y2_ref, w_ref, g0_ref, g1_ref, g2_ref, o_ref):
    acc = _sigmoid(g0_ref[...]) * jnp.dot(y0_ref[...], w_ref[0, 0], preferred_element_type=F32)
    acc += _sigmoid(g1_ref[...]) * jnp.dot(y1_ref[...], w_ref[0, 1], preferred_element_type=F32)
    acc += _sigmoid(g2_ref[...]) * jnp.dot(y2_ref[...], w_ref[0, 2], preferred_element_type=F32)
    o_ref[...] = acc.astype(o_ref.dtype)


def _merge(y_rw, y_ml, y_rt, w_br, lyr, z, gate_col0, tm=512, tn=1024):
    m, kw = y_rw.shape
    d = w_br.shape[3]
    gb = gate_col0 // tn
    gspec = lambda c: pl.BlockSpec((tm, tn), lambda j, i: (i, gb + c * (d // tn) + j))
    yspec = pl.BlockSpec((tm, kw), lambda j, i: (i, 0))
    return pl.pallas_call(
        _merge_kernel,
        grid=(d // tn, m // tm),
        in_specs=[yspec, yspec, yspec, pl.BlockSpec((1, 3, kw, tn), lambda j, i: (lyr, 0, 0, j)),
                  gspec(0), gspec(1), gspec(2)],
        out_specs=pl.BlockSpec((tm, tn), lambda j, i: (i, j)),
        out_shape=jax.ShapeDtypeStruct((m, d), BF16),
        compiler_params=_cparams("parallel", "parallel"),
        name="branch_merge",
    )(y_rw, y_ml, y_rt, w_br, z, z, z)


def _residual_epilogue(acc, x_ref, gpost_ref, gnext_ref, xo_ref, ho_ref):
    xn = x_ref[...] + _rms_rows(acc, gpost_ref[...])
    xo_ref[...] = xn
    ho_ref[...] = _rms_rows(xn, gnext_ref[...]).astype(ho_ref.dtype)


def _proj_res_kernel(a_ref, w_ref, x_ref, gpost_ref, gnext_ref, xo_ref, ho_ref):
    acc = jnp.dot(a_ref[...], w_ref[0], preferred_element_type=F32)
    _residual_epilogue(acc, x_ref, gpost_ref, gnext_ref, xo_ref, ho_ref)


def _proj_res(a, w, lyr, x, g_post, g_next, row0=0, prev=(None, None), tm=512):
    total, k = a.shape
    m, d = x.shape
    rb0 = row0 // tm
    at = lambda i: (rb0 + i, 0)
    fixed = lambda i: (0, 0)
    out_shape = [jax.ShapeDtypeStruct((total, d), F32), jax.ShapeDtypeStruct((total, d), BF16)]
    kern, prev_specs, prev_args, aliases = _carry_through(_proj_res_kernel, 5, list(prev), out_shape,
                                                          [m != total] * 2)
    return pl.pallas_call(
        kern,
        grid=(m // tm,),
        in_specs=[pl.BlockSpec((tm, k), at), pl.BlockSpec((1, k, d), lambda i: (lyr, 0, 0)),
                  pl.BlockSpec((tm, d), lambda i: (i, 0)),
                  pl.BlockSpec((1, d), fixed), pl.BlockSpec((1, d), fixed)] + prev_specs,
        out_specs=[pl.BlockSpec((tm, d), at), pl.BlockSpec((tm, d), at)],
        out_shape=out_shape,
        input_output_aliases=aliases,
        compiler_params=_cparams("parallel"),
        name="out_proj_residual",
    )(a, w, x, g_post.reshape(1, d), g_next.reshape(1, d), *prev_args)


def _mlp_kernel(h_ref, w1_ref, w2_ref, x_ref, gpost_ref, gnext_ref, xo_ref, ho_ref, acc_ref):
    k = pl.program_id(1)

    @pl.when(k == 0)
    def _():
        acc_ref[...] = jnp.zeros_like(acc_ref)

    u = jnp.dot(h_ref[...], w1_ref[0], preferred_element_type=F32)
    u = jnp.square(jnp.maximum(u, 0.0)).astype(BF16)
    acc_ref[...] += jnp.dot(u, w2_ref[0], preferred_element_type=F32)

    @pl.when(k == pl.num_programs(1) - 1)
    def _():
        _residual_epilogue(acc_ref[...], x_ref, gpost_ref, gnext_ref, xo_ref, ho_ref)


def _mlp(h, w1, w2, lyr, x, g_post, g_next, row0=0, n_rows=None, tm=512, tf=1024):
    d = h.shape[1]
    m = n_rows or h.shape[0]
    f = w1.shape[2]
    rb0 = row0 // tm
    at = lambda i, k: (rb0 + i, 0)
    row = lambda i, k: (i, 0)
    fixed = lambda i, k: (0, 0)
    return pl.pallas_call(
        _mlp_kernel,
        grid=(m // tm, f // tf),
        in_specs=[pl.BlockSpec((tm, d), at), pl.BlockSpec((1, d, tf), lambda i, k: (lyr, 0, k)),
                  pl.BlockSpec((1, tf, d), lambda i, k: (lyr, k, 0)), pl.BlockSpec((tm, d), at),
                  pl.BlockSpec((1, d), fixed), pl.BlockSpec((1, d), fixed)],
        out_specs=[pl.BlockSpec((tm, d), row), pl.BlockSpec((tm, d), row)],
        out_shape=[jax.ShapeDtypeStruct((m, d), F32), jax.ShapeDtypeStruct((m, d), BF16)],
        scratch_shapes=[pltpu.VMEM((tm, d), F32)],
        compiler_params=_cparams("parallel", "arbitrary"),
        name="relu2_mlp",
    )(h, w1, w2, x, g_post.reshape(1, d), g_next.reshape(1, d))


def _head_rows(ref, hd):
    n_mem = ref.shape[2] // X_HEADS
    return ref[0, :, pl.ds(hd, n_mem, stride=X_HEADS), :]


def _attn_kernel(bb, lq, h_ref, wq_ref, mk_ref, mv_ref, wo_ref, x_ref, gpost_ref, gnext_ref, xo_ref, ho_ref):
    q = jnp.dot(h_ref[...], wq_ref[0], preferred_element_type=F32)
    outs = []
    for hd in range(X_HEADS):
        sl = slice(hd * X_HD, (hd + 1) * X_HD)
        q3 = q[:, sl].reshape(bb, lq, X_HD).astype(BF16)
        k3 = _head_rows(mk_ref, hd).astype(BF16)
        v3 = _head_rows(mv_ref, hd).astype(BF16)
        s = jnp.einsum('bqd,bmd->bqm', q3, k3, preferred_element_type=F32) * (X_HD ** -0.5)
        s = s - jnp.max(s, axis=-1, keepdims=True)
        e = jnp.exp(s)
        p = (e / jnp.sum(e, axis=-1, keepdims=True)).astype(BF16)
        o3 = jnp.einsum('bqm,bmd->bqd', p, v3, preferred_element_type=F32)
        outs.append(o3.reshape(bb * lq, X_HD))
    o = jnp.concatenate(outs, axis=1).astype(BF16)
    acc = jnp.dot(o, wo_ref[0], preferred_element_type=F32)
    _residual_epilogue(acc, x_ref, gpost_ref, gnext_ref, xo_ref, ho_ref)


def _attn_seq_kernel(h_ref, wq_ref, mk_ref, mv_ref, wo_ref, x_ref, gpost_ref, gnext_ref, xo_ref, ho_ref):
    heads = range(X_HEADS)
    q = jnp.dot(h_ref[...], wq_ref[0], preferred_element_type=F32)
    stack = lambda f: jnp.stack([f(hd) for hd in heads]).astype(BF16)
    q3 = stack(lambda hd: q[:, hd * X_HD:(hd + 1) * X_HD])
    k3 = stack(lambda hd: _head_rows(mk_ref, hd)[0])
    v3 = stack(lambda hd: _head_rows(mv_ref, hd)[0])
    s = jnp.einsum('hqd,hmd->hqm', q3, k3, preferred_element_type=F32) * (X_HD ** -0.5)
    s = s - jnp.max(s, axis=-1, keepdims=True)
    e = jnp.exp(s)
    p = (e / jnp.sum(e, axis=-1, keepdims=True)).astype(BF16)
    o3 = jnp.einsum('hqm,hmd->hqd', p, v3, preferred_element_type=F32)
    o = jnp.concatenate([o3[hd] for hd in heads], axis=1).astype(BF16)
    acc = jnp.dot(o, wo_ref[0], preferred_element_type=F32)
    _residual_epilogue(acc, x_ref, gpost_ref, gnext_ref, xo_ref, ho_ref)


def _attn(h, x, mk, mv, lyr, wq, wo, wl, g_post, g_next, row0, n_seq, seq_len, bb, lq):
    t, d = h.shape
    rows = bb * lq
    nl = seq_len // lq
    n_mem = mk.shape[2]
    rb0 = row0 // rows
    rowmap = lambda b, l: (rb0 + b * nl + l, 0)
    fixed = lambda b, l: (0, 0)
    body = _attn_seq_kernel if bb == 1 else functools.partial(_attn_kernel, bb, lq)
    kv_specs = [pl.BlockSpec((1, bb, n_mem * X_HEADS, X_HD), lambda b, l: (lyr, b, 0, 0))] * 2
    kv_args = [a.reshape(a.shape[0], a.shape[1], n_mem * X_HEADS, X_HD) for a in (mk, mv)]
    x_at = 3 + len(kv_specs)
    return pl.pallas_call(
        body,
        grid=(n_seq // bb, nl),
        in_specs=[pl.BlockSpec((rows, d), rowmap), pl.BlockSpec((1, d, X_W), lambda b, l: (wl, 0, 0))] + kv_specs
                 + [pl.BlockSpec((1, X_W, d), lambda b, l: (wl, 0, 0)), pl.BlockSpec((rows, d), rowmap),
                    pl.BlockSpec((1, d), fixed), pl.BlockSpec((1, d), fixed)],
        out_specs=[pl.BlockSpec((rows, d), rowmap), pl.BlockSpec((rows, d), rowmap)],
        out_shape=[jax.ShapeDtypeStruct((t, d), F32), jax.ShapeDtypeStruct((t, d), BF16)],
        input_output_aliases={x_at: 0, 0: 1},
        compiler_params=_cparams("parallel", "parallel"),
        name="mem_cross_attention",
    )(h, wq, *kv_args, wo, x, g_post.reshape(1, d), g_next.reshape(1, d))


def _rwkv_prep_kernel(bb, lt, u_ref, s0_ref, mu_ref, w0_ref, wup_ref, a0_ref, aup_ref, gup_ref, kk_ref, ka_ref,
                      rk_ref, r_o, w_o, k_o, kk_o, b_o, v_o, g_o, bonus_o, shift_o, carry_ref):
    l = pl.program_id(1)
    rows = bb * lt
    u = u_ref[...]
    shift_o[...] = u[rows - 1:rows, :].reshape(1, 1, RW_COLS) if bb == 1 else _last_of_seq(u, bb, lt)
    if bb == 1:
        first = jnp.where(l == 0, s0_ref[0], carry_ref[...])
        first_rows = jnp.broadcast_to(first, (rows, RW_COLS))
    else:
        first_rows = jnp.broadcast_to(s0_ref[...], (bb, lt, RW_COLS)).reshape(rows, RW_COLS)
    rowid = lax.broadcasted_iota(jnp.int32, (rows, 1), 0) % lt
    prev = jnp.where(rowid == 0, first_rows, pltpu.roll(u, 1, 0))
    if bb == 1:
        carry_ref[...] = u[rows - 1:rows, :]
    z = u + (prev - u) * mu_ref[...]
    r = z[:, 0:MIX_W]
    k = z[:, MIX_W:2 * MIX_W]
    v = z[:, 2 * MIX_W:3 * MIX_W]
    o = 3 * MIX_W
    wd = z[:, o:o + RW_LORA[0]]
    ad = z[:, o + RW_LORA[0]:o + RW_LORA[0] + RW_LORA[1]]
    gd = z[:, o + RW_LORA[0] + RW_LORA[1]:RW_COLS]
    w_log = -_softplus(-(w0_ref[...] + _bdot(jnp.tanh(wd), wup_ref[...]))) - 0.5
    decay = jnp.exp(-jnp.exp(w_log))
    a = _sigmoid(a0_ref[...] + _bdot(ad, aup_ref[...]))
    g = _bdot(_sigmoid(gd), gup_ref[...])
    kk = k * kk_ref[...]
    kk = kk / jnp.maximum(jnp.sqrt(_segsum(kk * kk, RW_HD)), 1e-12)
    k = k * (1.0 + (a - 1.0) * ka_ref[...])
    r_o[...] = r
    w_o[...] = decay
    k_o[...] = k
    kk_o[...] = kk
    b_o[...] = kk * a
    v_o[...] = v
    g_o[...] = g
    bonus_o[...] = _segsum(r * k * rk_ref[...], RW_HD) * v


def _rwkv_prep(z, shift0, p, row0, n_seq, seq_len, bb, lt):
    rows = bb * lt
    nl = seq_len // lt
    assert bb == 1 or nl == 1
    rb0 = row0 // rows
    m = n_seq * seq_len
    fixed = lambda b, l: (0, 0)
    outmap = lambda b, l: (b * nl + l, 0)
    vec = lambda width: pl.BlockSpec((1, width), fixed)
    ospec = pl.BlockSpec((rows, MIX_W), outmap)
    oshape = jax.ShapeDtypeStruct((m, MIX_W), F32)
    return pl.pallas_call(
        functools.partial(_rwkv_prep_kernel, bb, lt),
        grid=(n_seq // bb, nl),
        in_specs=[pl.BlockSpec((rows, RW_COLS), lambda b, l: (rb0 + b * nl + l, 0)),
                  pl.BlockSpec((bb, 1, RW_COLS), lambda b, l: (b, 0, 0)),
                  vec(RW_COLS), vec(MIX_W), pl.BlockSpec((RW_LORA[0], MIX_W), fixed), vec(MIX_W),
                  pl.BlockSpec((RW_LORA[1], MIX_W), fixed), pl.BlockSpec((RW_LORA[2], MIX_W), fixed),
                  vec(MIX_W), vec(MIX_W), vec(MIX_W)],
        out_specs=[ospec] * 8 + [pl.BlockSpec((bb, 1, RW_COLS), lambda b, l: (b, 0, 0))],
        out_shape=[oshape] * 8 + [jax.ShapeDtypeStruct((n_seq, 1, RW_COLS), F32)],
        scratch_shapes=[pltpu.VMEM((1, RW_COLS), F32)],
        compiler_params=_cparams("parallel", "arbitrary"),
        name="rwkv_prep",
    )(z, shift0, p['mu'], p['w0'], p['w_up'], p['a0'], p['a_up'], p['g_up'], p['k_k'], p['k_a'], p['r_k'])


def _rwkv_scan_kernel(tc, r_ref, w_ref, k_ref, kk_ref, b_ref, v_ref, s0_ref, y_ref, s_ref):
    @pl.when(pl.program_id(1) == 0)
    def _():
        s_ref[...] = s0_ref[...]

    y_ref[0, :, RW_VP:, :] = jnp.zeros((tc, RW_VR - RW_VP, LANES), F32)
    keys = lambda ref, t: ref[0, t, :RW_HD, :]

    def step(t, carry):
        for vp in range(RW_VP):
            s = s_ref[0, vp]
            sa = jnp.sum(s * keys(kk_ref, t), axis=0, keepdims=True)
            sn = s * keys(w_ref, t) - sa * keys(b_ref, t) + v_ref[0, t, vp:vp + 1, :] * keys(k_ref, t)
            s_ref[0, vp] = sn
            y_ref[0, t, vp:vp + 1, :] = jnp.sum(sn * keys(r_ref, t), axis=0, keepdims=True)
        return carry

    lax.fori_loop(0, tc, step, 0)


def _rwkv_scan(r, w, k, kk, b, v, s0, lyr_in, tc, depth, lyr, s_prev):
    g, l = r.shape[0], r.shape[1]
    kspec = pl.BlockSpec((1, tc, RW_KR, LANES), lambda i, c: (i, c, 0, 0))
    vspec = pl.BlockSpec((1, tc, RW_VR, LANES), lambda i, c: (i, c, 0, 0))
    sspec = lambda at: pl.BlockSpec((1, RW_VP, RW_HD, LANES), lambda i, c: (at * g + i, 0, 0, 0))
    out_shape = [jax.ShapeDtypeStruct((g, l, RW_VR, LANES), F32),
                 jax.ShapeDtypeStruct((depth * g, RW_VP, RW_HD, LANES), F32)]
    kern, prev_specs, prev_args, aliases = _carry_through(functools.partial(_rwkv_scan_kernel, tc), 7,
                                                          [None, s_prev], out_shape, [False, depth > 1])
    return pl.pallas_call(
        kern,
        grid=(g, l // tc),
        in_specs=[kspec] * 5 + [vspec, sspec(lyr_in)] + prev_specs,
        out_specs=[vspec, sspec(lyr)],
        out_shape=out_shape,
        input_output_aliases=aliases,
        compiler_params=_cparams("parallel", "arbitrary"),
        name="rwkv_scan",
    )(r, w, k, kk, b, v, s0, *prev_args)


def _rwkv_post_kernel(y_ref, g_ref, bonus_ref, gng_ref, gnb_ref, o_ref):
    y = y_ref[...]
    mean = _segsum(y, RW_HD) * (1.0 / RW_HD)
    yc = y - mean
    var = _segsum(yc * yc, RW_HD) * (1.0 / RW_HD)
    yn = yc * lax.rsqrt(var + RW_GN_EPS) * gng_ref[...] + gnb_ref[...]
    o_ref[...] = ((yn + bonus_ref[...]) * g_ref[...]).astype(o_ref.dtype)


def _rwkv_post(y, g, bonus, gn_g, gn_b, row0, total_rows, prev, tm=256):
    m = y.shape[0]
    rb0 = row0 // tm
    fixed = lambda i: (0, 0)
    spec = pl.BlockSpec((tm, MIX_W), lambda i: (i, 0))
    out_shape = jax.ShapeDtypeStruct((total_rows, MIX_W), BF16)
    kern, prev_specs, prev_args, aliases = _carry_through(_rwkv_post_kernel, 5, [prev], [out_shape],
                                                          [m != total_rows])
    return pl.pallas_call(
        kern,
        grid=(m // tm,),
        in_specs=[spec, spec, spec, pl.BlockSpec((1, MIX_W), fixed), pl.BlockSpec((1, MIX_W), fixed)] + prev_specs,
        out_specs=pl.BlockSpec((tm, MIX_W), lambda i: (rb0 + i, 0)),
        out_shape=out_shape,
        input_output_aliases=aliases,
        compiler_params=_cparams("parallel"),
        name="rwkv_post",
    )(y, g, bonus, gn_g, gn_b, *prev_args)


RW_NB = RW_CHAINS // RW_HEADS
RW_TT = LANES


RW_KR = RW_HD + 8
RW_VR = RW_VP + 8


def _natural_to_chain_rows(x_ref, a_ref):
    for bl in range(RW_NB):
        xt = x_ref[bl].T
        for h in range(RW_HEADS):
            c = bl * RW_HEADS + h
            a_ref[c * RW_KR:c * RW_KR + RW_HD, :] = xt[h * RW_HD:(h + 1) * RW_HD]


def _keys_to_scan_kernel(x_ref, o_ref, a_ref):
    _natural_to_chain_rows(x_ref, a_ref)
    o_ref[...] = jnp.zeros_like(o_ref)
    for k in range(RW_HD):
        tile = a_ref[pl.ds(k, RW_CHAINS, stride=RW_KR), :]
        o_ref[0, pl.ds(k, RW_TT, stride=RW_KR), :] = jnp.concatenate([tile, tile], axis=0).T


def _vals_to_scan_kernel(x_ref, o_ref, a_ref):
    _natural_to_chain_rows(x_ref, a_ref)
    o_ref[...] = jnp.zeros_like(o_ref)
    for vp in range(RW_VP):
        halves = [a_ref[pl.ds(vh * RW_VP + vp, RW_CHAINS, stride=RW_KR), :] for vh in range(2)]
        o_ref[0, pl.ds(vp, RW_TT, stride=RW_VR), :] = jnp.concatenate(halves, axis=0).T


def _vals_from_scan_kernel(y_ref, o_ref, a_ref):
    for vp in range(RW_VP):
        tile = y_ref[0, pl.ds(vp, RW_TT, stride=RW_VR), :].T
        for vh in range(2):
            a_ref[pl.ds(vh * RW_VP + vp, RW_CHAINS, stride=RW_KR), :] = tile[vh * RW_CHAINS:(vh + 1) * RW_CHAINS]
    for bl in range(RW_NB):
        rows = [a_ref[c * RW_KR:c * RW_KR + RW_HD, :] for c in range(bl * RW_HEADS, (bl + 1) * RW_HEADS)]
        o_ref[bl] = jnp.concatenate(rows, axis=0).T


def _scan_relayout(kernel_fn, x, n_seq, seq_len, rows_per_t, to_scan, name):
    g, nl = n_seq // RW_NB, seq_len // RW_TT
    nat_spec = pl.BlockSpec((RW_NB, RW_TT, MIX_W), lambda i, l: (i, l, 0))
    scan_spec = pl.BlockSpec((1, RW_TT * rows_per_t, LANES), lambda i, l: (i, l, 0))
    nat_shape = jax.ShapeDtypeStruct((n_seq, seq_len, MIX_W), F32)
    scan_shape = jax.ShapeDtypeStruct((g, seq_len * rows_per_t, LANES), F32)
    x = x.reshape(n_seq, seq_len, MIX_W) if to_scan else x.reshape(g, seq_len * rows_per_t, LANES)
    out = pl.pallas_call(
        kernel_fn,
        grid=(g, nl),
        in_specs=[nat_spec if to_scan else scan_spec],
        out_specs=scan_spec if to_scan else nat_spec,
        out_shape=scan_shape if to_scan else nat_shape,
        scratch_shapes=[pltpu.VMEM((RW_CHAINS * RW_KR, RW_TT), F32)],
        compiler_params=_cparams("parallel", "parallel"),
        name=name,
    )(x)
    return out.reshape(g, seq_len, rows_per_t, LANES) if to_scan else out.reshape(n_seq * seq_len, MIX_W)


def _pad_rows(a, rows):
    return jnp.pad(a, ((0, 0), (0, 0), (0, rows - a.shape[2]), (0, 0)))


def _to_scan_keys(a, n_seq, seq_len):
    if seq_len % RW_TT == 0:
        return _scan_relayout(_keys_to_scan_kernel, a, n_seq, seq_len, RW_KR, True, "rwkv_keys_to_scan")
    g = n_seq // RW_NB
    a = a.reshape(g, RW_NB, seq_len, RW_HEADS, RW_HD).transpose(0, 2, 4, 1, 3).reshape(g, seq_len, RW_HD, RW_CHAINS)
    return _pad_rows(jnp.concatenate([a, a], axis=-1), RW_KR)


def _to_scan_vals(a, n_seq, seq_len):
    if seq_len % RW_TT == 0:
        return _scan_relayout(_vals_to_scan_kernel, a, n_seq, seq_len, RW_VR, True, "rwkv_vals_to_scan")
    g = n_seq // RW_NB
    a = a.reshape(g, RW_NB, seq_len, RW_HEADS, 2, RW_VP).transpose(0, 2, 5, 4, 1, 3)
    return _pad_rows(a.reshape(g, seq_len, RW_VP, LANES), RW_VR)


def _from_scan_vals(y, n_seq, seq_len):
    if seq_len % RW_TT == 0:
        return _scan_relayout(_vals_from_scan_kernel, y, n_seq, seq_len, RW_VR, False, "rwkv_vals_from_scan")
    g = n_seq // RW_NB
    y = y[:, :, :RW_VP].reshape(g, seq_len, RW_VP, 2, RW_NB, RW_HEADS).transpose(0, 4, 1, 5, 3, 2)
    return y.reshape(n_seq * seq_len, MIX_W)


def _state_to_scan(s, n_seq):
    nb = RW_CHAINS // RW_HEADS
    g = n_seq // nb
    s = s.reshape(g, nb, RW_HEADS, 2, RW_VP, RW_HD).transpose(0, 4, 5, 3, 1, 2)
    return s.reshape(g, RW_VP, RW_HD, LANES)


def _state_from_scan(s, n_seq):
    nb = RW_CHAINS // RW_HEADS
    g = n_seq // nb
    s = s.reshape(g, RW_VP, RW_HD, 2, nb, RW_HEADS).transpose(0, 4, 5, 3, 1, 2)
    return s.reshape(n_seq, RW_HEADS, RW_HD, RW_HD)


def _rwkv_branch(z, shift0, s0, lyr_in, p, row0, n_seq, seq_len, bb, lt, tc, depth, lyr, prev):
    r, w, k, kk, b, v, g, bonus, shift = _rwkv_prep(z, shift0, p, row0, n_seq, seq_len, bb, lt)
    keys = [_to_scan_keys(a, n_seq, seq_len) for a in (r, w, k, kk, b)]
    y, s_new = _rwkv_scan(*keys, _to_scan_vals(v, n_seq, seq_len), s0, lyr_in, tc, depth, lyr, prev[1])
    y = _from_scan_vals(y, n_seq, seq_len)
    y = _rwkv_post(y, g, bonus, p['gn_g'], p['gn_b'], row0, z.shape[0], prev[0])
    return y, s_new, shift.reshape(n_seq, RW_COLS)


def _seq_masks(rows, lc):
    ri = lax.broadcasted_iota(jnp.int32, (rows, rows), 0)
    ci = lax.broadcasted_iota(jnp.int32, (rows, rows), 1)
    same = (ri // lc) == (ci // lc)
    return ri, ci, same


def _rows_from_seq(x, bb, lc):
    return jnp.broadcast_to(x, (bb, lc, x.shape[-1])).reshape(bb * lc, x.shape[-1])


def _last_of_seq(x, bb, lc):
    x3 = x.reshape(bb, lc, x.shape[-1])
    pick = (lax.broadcasted_iota(jnp.int32, (1, lc, 1), 1) == lc - 1).astype(F32)
    return jnp.sum(x3 * pick, axis=1, keepdims=True)


def _transpose_rows(x):
    rows = x.shape[0]
    if rows < LANES:
        x = jnp.concatenate([x, jnp.zeros((LANES - rows, LANES), x.dtype)], axis=0)
    return x.T[:, :rows]


def _mlstm_kernel(bb, lc, q_ref, k_ref, v_ref, o_ref, gt_ref, gb_ref, ng_ref, c0_ref, n0_ref, m0_ref,
                  y_ref, c_ref, n_ref, m_ref):
    rows = bb * lc

    @pl.when(pl.program_id(1) == 0)
    def _():
        c_ref[...] = c0_ref[...]
        n_ref[...] = n0_ref[...]
        m_ref[...] = m0_ref[...]

    gt = ML_GATE_CAP * jnp.tanh((gt_ref[...] + gb_ref[...]) * (1.0 / ML_GATE_CAP))
    lane = lax.broadcasted_iota(jnp.int32, (rows, LANES), 1)
    ri, ci, same = _seq_masks(rows, lc)
    causal = same & (ci <= ri)
    cum = _dot_exact_lhs(causal.astype(F32), -_softplus(-gt))
    gt_t = _transpose_rows(gt)
    cum_t = _transpose_rows(cum)
    for hd in range(ML_HEADS):
        sl = slice(hd * ML_HD, (hd + 1) * ML_HD)
        c0 = c_ref[0, :, hd]
        n0 = n_ref[0, :, hd:hd + 1, :]
        m0 = m_ref[0, :, :, hd:hd + 1]
        q = q_ref[:, sl]
        k = k_ref[:, sl] * (ML_HD ** -0.5)
        v = v_ref[:, sl]
        bcol = cum[:, ML_HEADS + hd:ML_HEADS + hd + 1]
        igcol = gt[:, hd:hd + 1]
        rowb = gt_t[hd:hd + 1, :] - cum_t[ML_HEADS + hd:ML_HEADS + hd + 1, :]
        dlog = jnp.where(causal, bcol + rowb, -jnp.inf)
        m_inter = bcol + _rows_from_seq(m0, bb, lc)
        m_t = jnp.maximum(m_inter, jnp.max(dlog, axis=-1, keepdims=True))
        qk = lax.dot_general(q.astype(BF16), k.astype(BF16), (((1,), (1,)), ((), ())),
                             preferred_element_type=F32)
        wts = jnp.exp(dlog - m_t) * qk
        s_inter = jnp.exp(m_inter - m_t)
        q3 = q.reshape(bb, lc, ML_HD).astype(BF16)
        qc = jnp.einsum('btd,bde->bte', q3, c0.astype(BF16), preferred_element_type=F32).reshape(rows, ML_HD)
        num = _bdot(wts, v) + s_inter * qc
        qn = jnp.sum(q * _rows_from_seq(n0, bb, lc), axis=-1, keepdims=True)
        den = jnp.sum(wts, axis=-1, keepdims=True) + s_inter * qn
        hh = num / jnp.maximum(jnp.abs(den), jnp.exp(-m_t))
        hh = hh * lax.rsqrt(jnp.mean(hh * hh, axis=-1, keepdims=True) + EPS) * ng_ref[:, sl]
        y_ref[:, sl] = (_sigmoid(o_ref[:, sl]) * hh).astype(y_ref.dtype)

        stat = jnp.where(lane == 0, bcol, jnp.where(lane == 1, m_t, 0.0))
        last = _last_of_seq(stat, bb, lc)
        last_rows = _rows_from_seq(last, bb, lc)
        b_last, m_new = last_rows[:, 0:1], last_rows[:, 1:2]
        w_end = jnp.exp(b_last - bcol + igcol - m_new)
        f_end = jnp.exp(last[:, :, 0:1] + m0 - last[:, :, 1:2])
        kw3 = (k * w_end).reshape(bb, lc, ML_HD)
        upd = jnp.einsum('bsd,bse->bde', kw3.astype(BF16), v.reshape(bb, lc, ML_HD).astype(BF16),
                         preferred_element_type=F32)
        c_ref[0, :, hd] = f_end * c0 + upd
        n_ref[0, :, hd:hd + 1, :] = f_end * n0 + jnp.sum(kw3, axis=1, keepdims=True)
        m_ref[0, :, :, hd:hd + 1] = last[:, :, 1:2]


def _mlstm_seq_kernel(lc, q_ref, k_ref, v_ref, o_ref, gt_ref, gb_ref, ng_ref, c0_ref, n0_ref, m0_ref,
                      y_ref, c_ref, n_ref, m_ref):
    heads = range(ML_HEADS)

    @pl.when(pl.program_id(1) == 0)
    def _():
        c_ref[...] = c0_ref[...]
        n_ref[...] = n0_ref[...]
        m_ref[...] = m0_ref[...]

    gt = ML_GATE_CAP * jnp.tanh((gt_ref[...] + gb_ref[...]) * (1.0 / ML_GATE_CAP))
    ri, ci, _ = _seq_masks(lc, lc)
    causal = ci <= ri
    cum = _dot_exact_lhs(causal.astype(F32), -_softplus(-gt))
    gt_t = _transpose_rows(gt)
    cum_t = _transpose_rows(cum)
    sl = lambda hd: slice(hd * ML_HD, (hd + 1) * ML_HD)
    stack = lambda f: jnp.stack([f(hd) for hd in heads])
    q = stack(lambda hd: q_ref[:, sl(hd)])
    k = stack(lambda hd: k_ref[:, sl(hd)]) * (ML_HD ** -0.5)
    v = stack(lambda hd: v_ref[:, sl(hd)])
    bcol = stack(lambda hd: cum[:, ML_HEADS + hd:ML_HEADS + hd + 1])
    igcol = stack(lambda hd: gt[:, hd:hd + 1])
    rowb = stack(lambda hd: gt_t[hd:hd + 1, :] - cum_t[ML_HEADS + hd:ML_HEADS + hd + 1, :])
    c0 = c_ref[0, 0]
    n0 = stack(lambda hd: n_ref[0, 0, hd:hd + 1, :])
    m0 = stack(lambda hd: m_ref[0, 0, :, hd:hd + 1])
    qb, kb, vb = q.astype(BF16), k.astype(BF16), v.astype(BF16)

    dlog = jnp.where(causal[None], bcol + rowb, -jnp.inf)
    m_inter = bcol + m0
    m_t = jnp.maximum(m_inter, jnp.max(dlog, axis=-1, keepdims=True))
    qk = jnp.einsum('hqd,hkd->hqk', qb, kb, preferred_element_type=F32)
    wts = jnp.exp(dlog - m_t) * qk
    s_inter = jnp.exp(m_inter - m_t)
    qc = jnp.einsum('hqd,hde->hqe', qb, c0.astype(BF16), preferred_element_type=F32)
    num = jnp.einsum('hqk,hke->hqe', wts.astype(BF16), vb, preferred_element_type=F32) + s_inter * qc
    den = jnp.sum(wts, axis=-1, keepdims=True) + s_inter * jnp.sum(q * n0, axis=-1, keepdims=True)
    hh = num / jnp.maximum(jnp.abs(den), jnp.exp(-m_t))
    hh = hh * lax.rsqrt(jnp.mean(hh * hh, axis=-1, keepdims=True) + EPS)
    for hd in heads:
        y_ref[:, sl(hd)] = (_sigmoid(o_ref[:, sl(hd)]) * hh[hd] * ng_ref[:, sl(hd)]).astype(y_ref.dtype)

    b_last, m_new = bcol[:, lc - 1:lc, :], m_t[:, lc - 1:lc, :]
    w_end = jnp.exp(b_last - bcol + igcol - m_new)
    f_end = jnp.exp(b_last + m0 - m_new)
    kw = k * w_end
    upd = jnp.einsum('hsd,hse->hde', kw.astype(BF16), vb, preferred_element_type=F32)
    c_ref[0, 0] = f_end * c0 + upd
    n_new = f_end * n0 + jnp.sum(kw, axis=1, keepdims=True)
    for hd in heads:
        n_ref[0, 0, hd:hd + 1, :] = n_new[hd]
        m_ref[0, 0, :, hd:hd + 1] = m_new[hd]


def _mlstm_branch(z, z_gates, init, lyr_in, gate_bias, norm_g, row0, n_seq, seq_len, bb, lc, depth, lyr, prev):
    rows = bb * lc
    nc = seq_len // lc
    rb0 = row0 // rows
    rowmap = lambda part: (lambda b, c: (rb0 + b * nc + c, part))
    zspec = lambda part: pl.BlockSpec((rows, MIX_W), rowmap(part))
    state_specs = lambda at: [pl.BlockSpec((1, bb, ML_HEADS, ML_HD, ML_HD), lambda b, c: (at, b, 0, 0, 0)),
                              pl.BlockSpec((1, bb, ML_HEADS, ML_HD), lambda b, c: (at, b, 0, 0)),
                              pl.BlockSpec((1, bb, 1, ML_HEADS), lambda b, c: (at, b, 0, 0))]
    body = functools.partial(_mlstm_seq_kernel, lc) if bb == 1 else functools.partial(_mlstm_kernel, bb, lc)
    out_shape = [jax.ShapeDtypeStruct((z.shape[0], MIX_W), BF16),
                 jax.ShapeDtypeStruct((depth, n_seq, ML_HEADS, ML_HD, ML_HD), F32),
                 jax.ShapeDtypeStruct((depth, n_seq, ML_HEADS, ML_HD), F32),
                 jax.ShapeDtypeStruct((depth, n_seq, 1, ML_HEADS), F32)]
    kern, prev_specs, prev_args, aliases = _carry_through(
        body, 10, prev, out_shape, [n_seq * seq_len != z.shape[0]] + [depth > 1] * 3)
    return pl.pallas_call(
        kern,
        grid=(n_seq // bb, nc),
        in_specs=[zspec(0), zspec(1), zspec(2), zspec(3),
                  pl.BlockSpec((rows, LANES), lambda b, c: (rb0 + b * nc + c, 4 * MIX_W // LANES)),
                  pl.BlockSpec((1, LANES), lambda b, c: (0, 0)), pl.BlockSpec((1, MIX_W), lambda b, c: (0, 0))]
                 + state_specs(lyr_in) + prev_specs,
        out_specs=[pl.BlockSpec((rows, MIX_W), lambda b, c: (rb0 + b * nc + c, 0))] + state_specs(lyr),
        out_shape=out_shape,
        input_output_aliases=aliases,
        compiler_params=_cparams("parallel", "arbitrary"),
        name="mlstm_chunk",
    )(z, z, z, z, z_gates, gate_bias, norm_g, *init, *prev_args)


def _rope_rows(x, cos, sin):
    half = RT_HD // 2
    x1, x2 = x[:, :half], x[:, half:]
    return jnp.concatenate([x1 * cos - x2 * sin, x1 * sin + x2 * cos], axis=1)


def _ret_kernel(bb, lc, q_ref, k_ref, v_ref, g_ref, cs_ref, dec_ref, cd_ref, s0_ref, y_ref, s_ref):
    rows = bb * lc

    @pl.when(pl.program_id(1) == 0)
    def _():
        s_ref[...] = s0_ref[...]

    cos, sin = cs_ref[:, :RT_HD // 2], cs_ref[:, RT_HD // 2:]
    for hd in range(RT_HEADS):
        sl = slice(hd * RT_HD, (hd + 1) * RT_HD)
        s0 = s_ref[0, :, hd]
        q = _rope_rows(q_ref[:, sl], cos, sin)
        k = _rope_rows(k_ref[:, sl], cos, sin) * (RT_HD ** -0.5)
        v = v_ref[:, sl]
        cd = cd_ref[hd]
        qk = lax.dot_general(q.astype(BF16), k.astype(BF16), (((1,), (1,)), ((), ())),
                             preferred_element_type=F32)
        inner = _bdot(qk * dec_ref[hd], v)
        q3 = q.reshape(bb, lc, RT_HD).astype(BF16)
        cross = jnp.einsum('btd,bde->bte', q3, s0.astype(BF16), preferred_element_type=F32).reshape(rows, RT_HD)
        y = inner + cross * cd[:, 0:1]
        y = y * lax.rsqrt(jnp.mean(y * y, axis=-1, keepdims=True) + EPS)
        gt = g_ref[:, sl]
        y_ref[:, sl] = (gt * _sigmoid(gt) * y).astype(y_ref.dtype)
        kd = (k * cd[:, 1:2]).reshape(bb, lc, RT_HD).astype(BF16)
        upd = jnp.einsum('bsd,bse->bde', kd, v.reshape(bb, lc, RT_HD).astype(BF16), preferred_element_type=F32)
        s_ref[0, :, hd] = cd[0:1, 2:3] * s0 + upd


def _ret_seq_kernel(lc, q_ref, k_ref, v_ref, g_ref, cs_ref, dec_ref, cd_ref, s0_ref, y_ref, s_ref):
    heads = range(RT_HEADS)

    @pl.when(pl.program_id(1) == 0)
    def _():
        s_ref[...] = s0_ref[...]

    cos, sin = cs_ref[:, :RT_HD // 2], cs_ref[:, RT_HD // 2:]
    sl = lambda hd: slice(hd * RT_HD, (hd + 1) * RT_HD)
    stack = lambda f: jnp.stack([f(hd) for hd in heads])
    q = stack(lambda hd: _rope_rows(q_ref[:, sl(hd)], cos, sin))
    k = stack(lambda hd: _rope_rows(k_ref[:, sl(hd)], cos, sin)) * (RT_HD ** -0.5)
    vb = stack(lambda hd: v_ref[:, sl(hd)]).astype(BF16)
    s0 = s_ref[0, 0]
    cd = cd_ref[...]
    qb = q.astype(BF16)
    qk = jnp.einsum('hqd,hkd->hqk', qb, k.astype(BF16), preferred_element_type=F32)
    inner = jnp.einsum('hqk,hke->hqe', (qk * dec_ref[...]).astype(BF16), vb, preferred_element_type=F32)
    cross = jnp.einsum('hqd,hde->hqe', qb, s0.astype(BF16), preferred_element_type=F32)
    y = inner + cross * cd[:, :, 0:1]
    y = y * lax.rsqrt(jnp.mean(y * y, axis=-1, keepdims=True) + EPS)
    for hd in heads:
        gt = g_ref[:, sl(hd)]
        y_ref[:, sl(hd)] = (gt * _sigmoid(gt) * y[hd]).astype(y_ref.dtype)
    kd = (k * cd[:, :, 1:2]).astype(BF16)
    upd = jnp.einsum('hsd,hse->hde', kd, vb, preferred_element_type=F32)
    s_ref[0, 0] = cd[:, 0:1, 2:3] * s0 + upd


def _ret_branch(z, s0, lyr_in, row0, n_seq, seq_len, bb, lc, pos0, depth, lyr, prev):
    rows = bb * lc
    nc = seq_len // lc
    rb0 = row0 // rows
    half = RT_HD // 2
    inv = ROPE_BASE ** (-jnp.arange(half, dtype=F32) / half)
    pos = jnp.broadcast_to((jnp.arange(nc * lc, dtype=F32) + float(pos0)).reshape(nc, 1, lc), (nc, bb, lc))
    ang = pos.reshape(nc * rows)[:, None] * inv[None, :]
    cs = jnp.concatenate([jnp.cos(ang), jnp.sin(ang)], axis=1)
    log_g = jnp.log(1.0 - jnp.exp(jnp.linspace(math.log(1.0 / 32), math.log(1.0 / 512), RT_HEADS)))
    idx = jnp.arange(rows)
    t = (idx % lc).astype(F32)
    diff = t[:, None] - t[None, :]
    same = (idx[:, None] // lc) == (idx[None, :] // lc)
    dec = jnp.where(same & (diff >= 0), jnp.exp(log_g[:, None, None] * jnp.maximum(diff, 0.0)), 0.0)
    cd = jnp.zeros((RT_HEADS, rows, LANES), F32)
    cd = cd.at[:, :, 0].set(jnp.exp(log_g[:, None] * (t[None, :] + 1.0)))
    cd = cd.at[:, :, 1].set(jnp.exp(log_g[:, None] * (lc - 1.0 - t)[None, :]))
    cd = cd.at[:, :, 2].set(jnp.broadcast_to(jnp.exp(log_g * lc)[:, None], (RT_HEADS, rows)))
    rowmap = lambda part: (lambda b, c: (rb0 + b * nc + c, part))
    zspec = lambda part: pl.BlockSpec((rows, MIX_W), rowmap(part))
    sspec = lambda at: pl.BlockSpec((1, bb, RT_HEADS, RT_HD, RT_HD), lambda b, c: (at, b, 0, 0, 0))
    body = functools.partial(_ret_seq_kernel, lc) if bb == 1 else functools.partial(_ret_kernel, bb, lc)
    out_shape = [jax.ShapeDtypeStruct((z.shape[0], MIX_W), BF16),
                 jax.ShapeDtypeStruct((depth, n_seq, RT_HEADS, RT_HD, RT_HD), F32)]
    kern, prev_specs, prev_args, aliases = _carry_through(body, 8, prev, out_shape,
                                                          [n_seq * seq_len != z.shape[0], depth > 1])
    return pl.pallas_call(
        kern,
        grid=(n_seq // bb, nc),
        in_specs=[zspec(0), zspec(1), zspec(2), zspec(3),
                  pl.BlockSpec((rows, RT_HD), lambda b, c: (c, 0)),
                  pl.BlockSpec((RT_HEADS, rows, rows), lambda b, c: (0, 0, 0)),
                  pl.BlockSpec((RT_HEADS, rows, LANES), lambda b, c: (0, 0, 0)),
                  sspec(lyr_in)] + prev_specs,
        out_specs=[pl.BlockSpec((rows, MIX_W), lambda b, c: (rb0 + b * nc + c, 0)), sspec(lyr)],
        out_shape=out_shape,
        input_output_aliases=aliases,
        compiler_params=_cparams("parallel", "arbitrary"),
        name="retention_chunk",
    )(z, z, z, z, cs, dec, cd, s0, *prev_args)


def _mm_wt_kernel(x_ref, wt_ref, o_ref, wb_ref):
    @pl.when(pl.program_id(1) == 0)
    def _():
        tn = wb_ref.shape[1]
        for c in range(0, tn, LANES):
            wb_ref[:, c:c + LANES] = wt_ref[0, c:c + LANES, :].T.astype(wb_ref.dtype)

    o_ref[...] = jnp.dot(x_ref[...], wb_ref[...], preferred_element_type=F32)


def _mm_wt(x, w_t, lyr, col0, n, tm, tn):
    m, k = x.shape
    return pl.pallas_call(
        _mm_wt_kernel,
        grid=(n // tn, m // tm),
        in_specs=[pl.BlockSpec((tm, k), lambda j, i: (i, 0)),
                  pl.BlockSpec((pl.Element(1), pl.Element(tn), pl.Element(k)),
                               lambda j, i: (lyr, pl.multiple_of(col0 + j * tn, 8), 0))],
        out_specs=pl.BlockSpec((tm, tn), lambda j, i: (i, j)),
        out_shape=jax.ShapeDtypeStruct((m, n), F32),
        scratch_shapes=[pltpu.VMEM((k, tn), BF16)],
        compiler_params=_cparams("parallel", "arbitrary"),
        name="matmul_wt",
    )(x, w_t)


def _group_cfg(n_seq, seq_len):
    if seq_len % CHUNK == 0:
        return dict(rw_bb=1, rw_lt=256, rw_tc=64, ml_bb=1, rt_bb=1, ch_lc=CHUNK, at_bb=1, at_lq=512)
    return dict(rw_bb=32, rw_lt=seq_len, rw_tc=seq_len, ml_bb=16, rt_bb=8, ch_lc=seq_len, at_bb=16, at_lq=seq_len)


def kernel(x_prompt, x_sample, mem_prompt, state_rwkv_shift, state_rwkv, state_mlstm_c, state_mlstm_n,
           state_mlstm_m, state_ret, cache_mem_k, cache_mem_v, g_pre_mix, g_post_mix, g_pre_x, g_post_x,
           g_pre_ff, g_post_ff, g_mem, w_in, rw_mu, rw_w0, rw_w_up, rw_a0, rw_a_up, rw_g_up, rw_k_k, rw_k_a,
           rw_r_k, rw_gn_g, rw_gn_b, ml_i_b, ml_f_b, ml_norm_g, w_br, w_out, x_wq, x_wkv, x_wo, ff_w1, ff_w2):
    bp, lp, d = x_prompt.shape
    bs, ls, _ = x_sample.shape
    depth = w_in.shape[0]
    n_mem = mem_prompt.shape[1]
    tp, ts = bp * lp, bs * ls
    groups = [(0, bp, lp, 0), (tp, bs, ls, PAST_LEN)]

    x_groups = [x_prompt.reshape(tp, d), x_sample.reshape(ts, d)]
    mem = mem_prompt.reshape(bp * n_mem, d)
    x = h = None
    for (row0, *_), xg in zip(groups, x_groups):
        h = _rmsnorm(xg, g_pre_mix[0], row0=row0, total_rows=tp + ts, prev=h)

    zeros = lambda *s: jnp.zeros(s, F32)
    rw_scan0 = _state_to_scan(state_rwkv.reshape(depth * bs, RW_HEADS, RW_HD, RW_HD), depth * bs)
    inits = [
        lambda l: (zeros(bp, 1, RW_COLS), zeros(bp // RW_NB, RW_VP, RW_HD, LANES),
                   (zeros(1, bp, ML_HEADS, ML_HD, ML_HD), zeros(1, bp, ML_HEADS, ML_HD), zeros(1, bp, 1, ML_HEADS)),
                   zeros(1, bp, RT_HEADS, RT_HD, RT_HD), 0),
        lambda l: (state_rwkv_shift[l].reshape(bs, 1, RW_COLS), rw_scan0,
                   (state_mlstm_c, state_mlstm_n, state_mlstm_m.reshape(depth, bs, 1, ML_HEADS)), state_ret, l),
    ]
    shifts = [[], []]
    rw_states = [None, None]
    ml_states = [[None] * 3, [None] * 3]
    rt_states = [None, None]
    mem_k, mem_v = [], []
    w_in_t = jnp.swapaxes(w_in, 1, 2)
    w_kv, w_brb, w_outb, w_q, w_o, w_f1, w_f2 = (w.astype(BF16) for w in (x_wkv, w_br, w_out, x_wq, x_wo,
                                                                         ff_w1, ff_w2))
    y_rw = y_ml = y_rt = None
    for l in range(depth):
        kv = _mm(_rmsnorm(mem, g_mem[l], tm=256), w_kv, l, tm=256, tn=512)
        mem_k.append(kv[:, :X_W].reshape(bp, n_mem, X_HEADS, X_HD))
        mem_v.append(kv[:, X_W:].reshape(bp, n_mem, X_HEADS, X_HD))

        z_rw = _mm_wt(h, w_in_t, l, 0, RW_COLS, tm=512, tn=RW_COLS // 2)
        z_ml = _mm_wt(h, w_in_t, l, IN_ML, Z_MLW, tm=1024, tn=Z_MLW // 3)
        z_rg = _mm_wt(h, w_in_t, l, IN_RT, w_in.shape[2] - IN_RT, tm=1024, tn=1280)

        rw_p = dict(mu=rw_mu[l].reshape(1, -1), w0=rw_w0[l].reshape(1, -1), w_up=rw_w_up[l].astype(BF16),
                    a0=rw_a0[l].reshape(1, -1), a_up=rw_a_up[l].astype(BF16), g_up=rw_g_up[l].astype(BF16),
                    k_k=rw_k_k[l].reshape(1, -1), k_a=rw_k_a[l].reshape(1, -1), r_k=rw_r_k[l].reshape(1, -1),
                    gn_g=rw_gn_g[l].reshape(1, -1), gn_b=rw_gn_b[l].reshape(1, -1))
        gate_bias = jnp.concatenate([ml_i_b[l], ml_f_b[l], zeros(LANES - 2 * ML_HEADS)]).reshape(1, LANES)
        norm_g = ml_norm_g[l].reshape(1, MIX_W)

        for gi, (row0, n_seq, seq_len, pos0) in enumerate(groups):
            cfg = _group_cfg(n_seq, seq_len)
            shift0, s_rw0, ml0, rt0, lyr_in = inits[gi](l)
            y_rw, rw_states[gi], shift = _rwkv_branch(z_rw, shift0, s_rw0, lyr_in, rw_p, row0, n_seq, seq_len,
                                                      cfg['rw_bb'], cfg['rw_lt'], cfg['rw_tc'], depth, l,
                                                      (y_rw, rw_states[gi]))
            shifts[gi].append(shift)
            y_ml, *ml_states[gi] = _mlstm_branch(z_ml, z_ml, ml0, lyr_in, gate_bias, norm_g, row0, n_seq, seq_len,
                                                 cfg['ml_bb'], cfg['ch_lc'], depth, l, [y_ml] + ml_states[gi])
            y_rt, rt_states[gi] = _ret_branch(z_rg, rt0, lyr_in, row0, n_seq, seq_len, cfg['rt_bb'],
                                              cfg['ch_lc'], pos0, depth, l, [y_rt, rt_states[gi]])

        merged = _merge(y_rw, y_ml, y_rt, w_brb, l, z_rg, Z_GATE)
        if x is None:
            xh = (None, h)
            for (row0, *_), xg in zip(groups, x_groups):
                xh = _proj_res(merged, w_outb, l, xg, g_post_mix[l], g_pre_x[l], row0=row0, prev=xh)
            x, h = xh
        else:
            x, h = _proj_res(merged, w_outb, l, x, g_post_mix[l], g_pre_x[l])

        for gi, (row0, n_seq, seq_len, pos0) in enumerate(groups):
            cfg = _group_cfg(n_seq, seq_len)
            if gi == 0:
                mk, mv, lyr_in = mem_k[l][None], mem_v[l][None], 0
            else:
                mk, mv, lyr_in = cache_mem_k, cache_mem_v, l
            x, h = _attn(h, x, mk, mv, lyr_in, w_q, w_o, l, g_post_x[l], g_pre_ff[l], row0, n_seq, seq_len,
                         cfg['at_bb'], cfg['at_lq'])

        if l + 1 < depth:
            x, h = _mlp(h, w_f1, w_f2, l, x, g_post_ff[l], g_pre_mix[l + 1])
        else:
            y_out = [_mlp(h, w_f1, w_f2, l, x, g_post_ff[l], g_pre_mix[0], row0=row0, n_rows=n_seq * seq_len)[0]
                     for row0, n_seq, seq_len, _ in groups]

    def states(gi, n_seq):
        c_new, n_new, m_new = ml_states[gi]
        s_rw = _state_from_scan(rw_states[gi], depth * n_seq).reshape(depth, n_seq, RW_HEADS, RW_HD, RW_HD)
        return (jnp.stack(shifts[gi]), s_rw, c_new, n_new, m_new.reshape(depth, n_seq, ML_HEADS), rt_states[gi])

    return (y_out[0].reshape(bp, lp, d), y_out[1].reshape(bs, ls, d), *states(0, bp),
            jnp.stack(mem_k), jnp.stack(mem_v), *states(1, bs))
```

```python
import functools
import math

import jax
import jax.numpy as jnp
from jax import lax
from jax.experimental import pallas as pl
from jax.experimental.pallas import tpu as pltpu

F32 = jnp.float32
BF16 = jnp.bfloat16

D_MODEL = 2048
MIX_W = D_MODEL // 2
RW_HD = 64
RW_HEADS = MIX_W // RW_HD
RW_LORA = (64, 64, 128)
RW_COLS = 3 * MIX_W + sum(RW_LORA)
RW_GN_EPS = 64e-5
ML_HEADS = 8
ML_HD = MIX_W // ML_HEADS
ML_GATE_CAP = 15.0
RT_HEADS = 4
RT_HD = MIX_W // RT_HEADS
ROPE_BASE = 10000.0
X_HEADS = 4
X_HD = 128
X_W = X_HEADS * X_HD
D_FF = 4 * D_MODEL
CHUNK = 64
EPS = 1e-6
PAST_LEN = 16384

Z_MLW = 4 * (D_MODEL // 2) + 128
IN_ML = RW_COLS
IN_MLG = IN_ML + 4 * MIX_W
IN_RT = IN_MLG + 2 * ML_HEADS
Z_GATE = 4 * MIX_W

LANES = 128
RW_VP = RW_HD // 2
RW_CHAINS = LANES // 2


def _cparams(*sem):
    return pltpu.CompilerParams(dimension_semantics=sem, vmem_limit_bytes=56 * 1024 * 1024)


def _carry_through(kernel_fn, n_in, prev, out_shapes=(), shared=()):
    prev = [jnp.zeros(out_shapes[i].shape, out_shapes[i].dtype) if p is None and i < len(shared) and shared[i]
            else p for i, p in enumerate(prev)]
    carried = [(i, p) for i, p in enumerate(prev) if p is not None]
    if not carried:
        return kernel_fn, [], [], {}

    def wrapped(*refs):
        return kernel_fn(*refs[:n_in], *refs[n_in + len(carried):])

    return (wrapped, [pl.BlockSpec(memory_space=pl.ANY)] * len(carried), [p for _, p in carried],
            {n_in + j: i for j, (i, _) in enumerate(carried)})


def _sigmoid(x):
    return 1.0 / (1.0 + jnp.exp(-x))


def _softplus(x):
    return jnp.maximum(x, 0.0) + jnp.log(1.0 + jnp.exp(-jnp.abs(x)))


def _bdot(a, b):
    return jnp.dot(a.astype(BF16), b.astype(BF16), preferred_element_type=F32)


def _split3(x):
    hi = x.astype(BF16)
    r1 = x - hi.astype(F32)
    mid = r1.astype(BF16)
    lo = (r1 - mid.astype(F32)).astype(BF16)
    return hi, mid, lo


def _dot_exact_lhs(m, x):
    mb = m.astype(BF16)
    hi, mid, lo = _split3(x)
    return (jnp.dot(mb, hi, preferred_element_type=F32) + jnp.dot(mb, mid, preferred_element_type=F32)
            + jnp.dot(mb, lo, preferred_element_type=F32))


def _dot_exact_rhs(x, m):
    mb = m.astype(BF16)
    hi, mid, lo = _split3(x)
    return (jnp.dot(hi, mb, preferred_element_type=F32) + jnp.dot(mid, mb, preferred_element_type=F32)
            + jnp.dot(lo, mb, preferred_element_type=F32))


def _segsum(x, seg):
    blk = 256
    ri = lax.broadcasted_iota(jnp.int32, (blk, blk), 0) // seg
    ci = lax.broadcasted_iota(jnp.int32, (blk, blk), 1) // seg
    ones = (ri == ci).astype(F32)
    parts = [_dot_exact_rhs(x[:, c:c + blk], ones) for c in range(0, x.shape[1], blk)]
    return parts[0] if len(parts) == 1 else jnp.concatenate(parts, axis=1)


def _rms_rows(x, g):
    return x * lax.rsqrt(jnp.mean(x * x, axis=-1, keepdims=True) + EPS) * g


def _rmsnorm_kernel(x_ref, g_ref, o_ref):
    o_ref[...] = _rms_rows(x_ref[...], g_ref[...]).astype(o_ref.dtype)


def _rmsnorm(x, g, tm=512, row0=0, total_rows=None, prev=None):
    m, d = x.shape
    rb0 = row0 // tm
    out_shape = jax.ShapeDtypeStruct((total_rows or m, d), BF16)
    kern, prev_specs, prev_args, aliases = _carry_through(_rmsnorm_kernel, 2, [prev], [out_shape],
                                                          [out_shape.shape[0] != m])
    return pl.pallas_call(
        kern,
        grid=(m // tm,),
        in_specs=[pl.BlockSpec((tm, d), lambda i: (i, 0)), pl.BlockSpec((1, d), lambda i: (0, 0))] + prev_specs,
        out_specs=pl.BlockSpec((tm, d), lambda i: (rb0 + i, 0)),
        out_shape=out_shape,
        input_output_aliases=aliases,
        compiler_params=_cparams("parallel"),
        name="rmsnorm",
    )(x, g.reshape(1, d), *prev_args)


def _mm_kernel(x_ref, w_ref, o_ref):
    o_ref[...] = jnp.dot(x_ref[...], w_ref[0], preferred_element_type=F32)


def _mm(x, w, lyr, tm, tn):
    m, k = x.shape
    n = w.shape[2]
    return pl.pallas_call(
        _mm_kernel,
        grid=(n // tn, m // tm),
        in_specs=[pl.BlockSpec((tm, k), lambda j, i: (i, 0)), pl.BlockSpec((1, k, tn), lambda j, i: (lyr, 0, j))],
        out_specs=pl.BlockSpec((tm, tn), lambda j, i: (i, j)),
        out_shape=jax.ShapeDtypeStruct((m, n), F32),
        compiler_params=_cparams("parallel", "parallel"),
        name="matmul",
    )(x, w)


def _merge_kernel(y0_ref, y1_ref, y2_ref, w_ref, g0_ref, g1_ref, g2_ref, o_ref):
    acc = _sigmoid(g0_ref[...]) * jnp.dot(y0_ref[...], w_ref[0, 0], preferred_element_type=F32)
    acc += _sigmoid(g1_ref[...]) * jnp.dot(y1_ref[...], w_ref[0, 1], preferred_element_type=F32)
    acc += _sigmoid(g2_ref[...]) * jnp.dot(y2_ref[...], w_ref[0, 2], preferred_element_type=F32)
    o_ref[...] = acc.astype(o_ref.dtype)


def _merge(y_rw, y_ml, y_rt, w_br, lyr, z, gate_col0, tm=512, tn=1024):
    m, kw = y_rw.shape
    d = w_br.shape[3]
    gb = gate_col0 // tn
    gspec = lambda c: pl.BlockSpec((tm, tn), lambda j, i: (i, gb + c * (d // tn) + j))
    yspec = pl.BlockSpec((tm, kw), lambda j, i: (i, 0))
    return pl.pallas_call(
        _merge_kernel,
        grid=(d // tn, m // tm),
        in_specs=[yspec, yspec, yspec, pl.BlockSpec((1, 3, kw, tn), lambda j, i: (lyr, 0, 0, j)),
                  gspec(0), gspec(1), gspec(2)],
        out_specs=pl.BlockSpec((tm, tn), lambda j, i: (i, j)),
        out_shape=jax.ShapeDtypeStruct((m, d), BF16),
        compiler_params=_cparams("parallel", "parallel"),
        name="branch_merge",
    )(y_rw, y_ml, y_rt, w_br, z, z, z)


def _residual_epilogue(acc, x_ref, gpost_ref, gnext_ref, xo_ref, ho_ref):
    xn = x_ref[...] + _rms_rows(acc, gpost_ref[...])
    xo_ref[...] = xn
    ho_ref[...] = _rms_rows(xn, gnext_ref[...]).astype(ho_ref.dtype)


def _proj_res_kernel(a_ref, w_ref, x_ref, gpost_ref, gnext_ref, xo_ref, ho_ref):
    acc = jnp.dot(a_ref[...], w_ref[0], preferred_element_type=F32)
    _residual_epilogue(acc, x_ref, gpost_ref, gnext_ref, xo_ref, ho_ref)


def _proj_res(a, w, lyr, x, g_post, g_next, row0=0, prev=(None, None), tm=512):
    total, k = a.shape
    m, d = x.shape
    rb0 = row0 // tm
    at = lambda i: (rb0 + i, 0)
    fixed = lambda i: (0, 0)
    out_shape = [jax.ShapeDtypeStruct((total, d), F32), jax.ShapeDtypeStruct((total, d), BF16)]
    kern, prev_specs, prev_args, aliases = _carry_through(_proj_res_kernel, 5, list(prev), out_shape,
                                                          [m != total] * 2)
    return pl.pallas_call(
        kern,
        grid=(m // tm,),
        in_specs=[pl.BlockSpec((tm, k), at), pl.BlockSpec((1, k, d), lambda i: (lyr, 0, 0)),
                  pl.BlockSpec((tm, d), lambda i: (i, 0)),
                  pl.BlockSpec((1, d), fixed), pl.BlockSpec((1, d), fixed)] + prev_specs,
        out_specs=[pl.BlockSpec((tm, d), at), pl.BlockSpec((tm, d), at)],
        out_shape=out_shape,
        input_output_aliases=aliases,
        compiler_params=_cparams("parallel"),
        name="out_proj_residual",
    )(a, w, x, g_post.reshape(1, d), g_next.reshape(1, d), *prev_args)


def _mlp_kernel(h_ref, w1_ref, w2_ref, x_ref, gpost_ref, gnext_ref, xo_ref, ho_ref, acc_ref):
    k = pl.program_id(1)

    @pl.when(k == 0)
    def _():
        acc_ref[...] = jnp.zeros_like(acc_ref)

    u = jnp.dot(h_ref[...], w1_ref[0], preferred_element_type=F32)
    u = jnp.square(jnp.maximum(u, 0.0)).astype(BF16)
    acc_ref[...] += jnp.dot(u, w2_ref[0], preferred_element_type=F32)

    @pl.when(k == pl.num_programs(1) - 1)
    def _():
        _residual_epilogue(acc_ref[...], x_ref, gpost_ref, gnext_ref, xo_ref, ho_ref)


def _mlp(h, w1, w2, lyr, x, g_post, g_next, row0=0, n_rows=None, tm=512, tf=1024):
    d = h.shape[1]
    m = n_rows or h.shape[0]
    f = w1.shape[2]
    rb0 = row0 // tm
    at = lambda i, k: (rb0 + i, 0)
    row = lambda i, k: (i, 0)
    fixed = lambda i, k: (0, 0)
    return pl.pallas_call(
        _mlp_kernel,
        grid=(m // tm, f // tf),
        in_specs=[pl.BlockSpec((tm, d), at), pl.BlockSpec((1, d, tf), lambda i, k: (lyr, 0, k)),
                  pl.BlockSpec((1, tf, d), lambda i, k: (lyr, k, 0)), pl.BlockSpec((tm, d), at),
                  pl.BlockSpec((1, d), fixed), pl.BlockSpec((1, d), fixed)],
        out_specs=[pl.BlockSpec((tm, d), row), pl.BlockSpec((tm, d), row)],
        out_shape=[jax.ShapeDtypeStruct((m, d), F32), jax.ShapeDtypeStruct((m, d), BF16)],
        scratch_shapes=[pltpu.VMEM((tm, d), F32)],
        compiler_params=_cparams("parallel", "arbitrary"),
        name="relu2_mlp",
    )(h, w1, w2, x, g_post.reshape(1, d), g_next.reshape(1, d))


def _head_rows(ref, hd):
    n_mem = ref.shape[2] // X_HEADS
    return ref[0, :, pl.ds(hd, n_mem, stride=X_HEADS), :]


def _attn_kernel(bb, lq, h_ref, wq_ref, mk_ref, mv_ref, wo_ref, x_ref, gpost_ref, gnext_ref, xo_ref, ho_ref):
    q = jnp.dot(h_ref[...], wq_ref[0], preferred_element_type=F32)
    outs = []
    for hd in range(X_HEADS):
        sl = slice(hd * X_HD, (hd + 1) * X_HD)
        q3 = q[:, sl].reshape(bb, lq, X_HD).astype(BF16)
        k3 = _head_rows(mk_ref, hd).astype(BF16)
        v3 = _head_rows(mv_ref, hd).astype(BF16)
        s = jnp.einsum('bqd,bmd->bqm', q3, k3, preferred_element_type=F32) * (X_HD ** -0.5)
        s = s - jnp.max(s, axis=-1, keepdims=True)
        e = jnp.exp(s)
        p = (e / jnp.sum(e, axis=-1, keepdims=True)).astype(BF16)
        o3 = jnp.einsum('bqm,bmd->bqd', p, v3, preferred_element_type=F32)
        outs.append(o3.reshape(bb * lq, X_HD))
    o = jnp.concatenate(outs, axis=1).astype(BF16)
    acc = jnp.dot(o, wo_ref[0], preferred_element_type=F32)
    _residual_epilogue(acc, x_ref, gpost_ref, gnext_ref, xo_ref, ho_ref)


def _attn_seq_kernel(h_ref, wq_ref, mk_ref, mv_ref, wo_ref, x_ref, gpost_ref, gnext_ref, xo_ref, ho_ref):
    heads = range(X_HEADS)
    q = jnp.dot(h_ref[...], wq_ref[0], preferred_element_type=F32)
    stack = lambda f: jnp.stack([f(hd) for hd in heads]).astype(BF16)
    q3 = stack(lambda hd: q[:, hd * X_HD:(hd + 1) * X_HD])
    k3 = stack(lambda hd: _head_rows(mk_ref, hd)[0])
    v3 = stack(lambda hd: _head_rows(mv_ref, hd)[0])
    s = jnp.einsum('hqd,hmd->hqm', q3, k3, preferred_element_type=F32) * (X_HD ** -0.5)
    s = s - jnp.max(s, axis=-1, keepdims=True)
    e = jnp.exp(s)
    p = (e / jnp.sum(e, axis=-1, keepdims=True)).astype(BF16)
    o3 = jnp.einsum('hqm,hmd->hqd', p, v3, preferred_element_type=F32)
    o = jnp.concatenate([o3[hd] for hd in heads], axis=1).astype(BF16)
    acc = jnp.dot(o, wo_ref[0], preferred_element_type=F32)
    _residual_epilogue(acc, x_ref, gpost_ref, gnext_ref, xo_ref, ho_ref)


def _attn(h, x, mk, mv, lyr, wq, wo, wl, g_post, g_next, row0, n_seq, seq_len, bb, lq):
    t, d = h.shape
    rows = bb * lq
    nl = seq_len // lq
    n_mem = mk.shape[2]
    rb0 = row0 // rows
    rowmap = lambda b, l: (rb0 + b * nl + l, 0)
    fixed = lambda b, l: (0, 0)
    body = _attn_seq_kernel if bb == 1 else functools.partial(_attn_kernel, bb, lq)
    kv_specs = [pl.BlockSpec((1, bb, n_mem * X_HEADS, X_HD), lambda b, l: (lyr, b, 0, 0))] * 2
    kv_args = [a.reshape(a.shape[0], a.shape[1], n_mem * X_HEADS, X_HD) for a in (mk, mv)]
    x_at = 3 + len(kv_specs)
    return pl.pallas_call(
        body,
        grid=(n_seq // bb, nl),
        in_specs=[pl.BlockSpec((rows, d), rowmap), pl.BlockSpec((1, d, X_W), lambda b, l: (wl, 0, 0))] + kv_specs
                 + [pl.BlockSpec((1, X_W, d), lambda b, l: (wl, 0, 0)), pl.BlockSpec((rows, d), rowmap),
                    pl.BlockSpec((1, d), fixed), pl.BlockSpec((1, d), fixed)],
        out_specs=[pl.BlockSpec((rows, d), rowmap), pl.BlockSpec((rows, d), rowmap)],
        out_shape=[jax.ShapeDtypeStruct((t, d), F32), jax.ShapeDtypeStruct((t, d), BF16)],
        input_output_aliases={x_at: 0, 0: 1},
        compiler_params=_cparams("parallel", "parallel"),
        name="mem_cross_attention",
    )(h, wq, *kv_args, wo, x, g_post.reshape(1, d), g_next.reshape(1, d))


def _rwkv_prep_kernel(bb, lt, u_ref, s0_ref, mu_ref, w0_ref, wup_ref, a0_ref, aup_ref, gup_ref, kk_ref, ka_ref,
                      rk_ref, r_o, w_o, k_o, kk_o, b_o, v_o, g_o, bonus_o, shift_o, carry_ref):
    l = pl.program_id(1)
    rows = bb * lt
    u = u_ref[...]
    shift_o[...] = u[rows - 1:rows, :].reshape(1, 1, RW_COLS) if bb == 1 else _last_of_seq(u, bb, lt)
    if bb == 1:
        first = jnp.where(l == 0, s0_ref[0], carry_ref[...])
        first_rows = jnp.broadcast_to(first, (rows, RW_COLS))
    else:
        first_rows = jnp.broadcast_to(s0_ref[...], (bb, lt, RW_COLS)).reshape(rows, RW_COLS)
    rowid = lax.broadcasted_iota(jnp.int32, (rows, 1), 0) % lt
    prev = jnp.where(rowid == 0, first_rows, pltpu.roll(u, 1, 0))
    if bb == 1:
        carry_ref[...] = u[rows - 1:rows, :]
    z = u + (prev - u) * mu_ref[...]
    r = z[:, 0:MIX_W]
    k = z[:, MIX_W:2 * MIX_W]
    v = z[:, 2 * MIX_W:3 * MIX_W]
    o = 3 * MIX_W
    wd = z[:, o:o + RW_LORA[0]]
    ad = z[:, o + RW_LORA[0]:o + RW_LORA[0] + RW_LORA[1]]
    gd = z[:, o + RW_LORA[0] + RW_LORA[1]:RW_COLS]
    w_log = -_softplus(-(w0_ref[...] + _bdot(jnp.tanh(wd), wup_ref[...]))) - 0.5
    decay = jnp.exp(-jnp.exp(w_log))
    a = _sigmoid(a0_ref[...] + _bdot(ad, aup_ref[...]))
    g = _bdot(_sigmoid(gd), gup_ref[...])
    kk = k * kk_ref[...]
    kk = kk / jnp.maximum(jnp.sqrt(_segsum(kk * kk, RW_HD)), 1e-12)
    k = k * (1.0 + (a - 1.0) * ka_ref[...])
    r_o[...] = r
    w_o[...] = decay
    k_o[...] = k
    kk_o[...] = kk
    b_o[...] = kk * a
    v_o[...] = v
    g_o[...] = g
    bonus_o[...] = _segsum(r * k * rk_ref[...], RW_HD) * v


def _rwkv_prep(z, shift0, p, row0, n_seq, seq_len, bb, lt):
    rows = bb * lt
    nl = seq_len // lt
    assert bb == 1 or nl == 1
    rb0 = row0 // rows
    m = n_seq * seq_len
    fixed = lambda b, l: (0, 0)
    outmap = lambda b, l: (b * nl + l, 0)
    vec = lambda width: pl.BlockSpec((1, width), fixed)
    ospec = pl.BlockSpec((rows, MIX_W), outmap)
    oshape = jax.ShapeDtypeStruct((m, MIX_W), F32)
    return pl.pallas_call(
        functools.partial(_rwkv_prep_kernel, bb, lt),
        grid=(n_seq // bb, nl),
        in_specs=[pl.BlockSpec((rows, RW_COLS), lambda b, l: (rb0 + b * nl + l, 0)),
                  pl.BlockSpec((bb, 1, RW_COLS), lambda b, l: (b, 0, 0)),
                  vec(RW_COLS), vec(MIX_W), pl.BlockSpec((RW_LORA[0], MIX_W), fixed), vec(MIX_W),
                  pl.BlockSpec((RW_LORA[1], MIX_W), fixed), pl.BlockSpec((RW_LORA[2], MIX_W), fixed),
                  vec(MIX_W), vec(MIX_W), vec(MIX_W)],
        out_specs=[ospec] * 8 + [pl.BlockSpec((bb, 1, RW_COLS), lambda b, l: (b, 0, 0))],
        out_shape=[oshape] * 8 + [jax.ShapeDtypeStruct((n_seq, 1, RW_COLS), F32)],
        scratch_shapes=[pltpu.VMEM((1, RW_COLS), F32)],
        compiler_params=_cparams("parallel", "arbitrary"),
        name="rwkv_prep",
    )(z, shift0, p['mu'], p['w0'], p['w_up'], p['a0'], p['a_up'], p['g_up'], p['k_k'], p['k_a'], p['r_k'])


def _rwkv_scan_kernel(tc, r_ref, w_ref, k_ref, kk_ref, b_ref, v_ref, s0_ref, y_ref, s_ref):
    @pl.when(pl.program_id(1) == 0)
    def _():
        s_ref[...] = s0_ref[...]

    y_ref[0, :, RW_VP:, :] = jnp.zeros((tc, RW_VR - RW_VP, LANES), F32)
    keys = lambda ref, t: ref[0, 0, t, :RW_HD, :]

    def step(t, carry):
        for vp in range(RW_VP):
            s = s_ref[0, vp]
            sa = jnp.sum(s * keys(kk_ref, t), axis=0, keepdims=True)
            sn = s * keys(w_ref, t) - sa * keys(b_ref, t) + v_ref[0, t, vp:vp + 1, :] * keys(k_ref, t)
            s_ref[0, vp] = sn
            y_ref[0, t, vp:vp + 1, :] = jnp.sum(sn * keys(r_ref, t), axis=0, keepdims=True)
        return carry

    lax.fori_loop(0, tc, step, 0)


def _rwkv_scan(r, w, k, kk, b, v, s0, lyr_in, tc, depth, lyr, s_prev):
    ops = (r, w, k, kk, b)
    g, l = r[0].shape[1], r[0].shape[2]
    kspec = lambda at: pl.BlockSpec((1, 1, tc, RW_KR, LANES), lambda i, c: (at, i, c, 0, 0))
    vspec = pl.BlockSpec((1, tc, RW_VR, LANES), lambda i, c: (i, c, 0, 0))
    sspec = lambda at: pl.BlockSpec((1, RW_VP, RW_HD, LANES), lambda i, c: (at * g + i, 0, 0, 0))
    out_shape = [jax.ShapeDtypeStruct((g, l, RW_VR, LANES), F32),
                 jax.ShapeDtypeStruct((depth * g, RW_VP, RW_HD, LANES), F32)]
    kern, prev_specs, prev_args, aliases = _carry_through(functools.partial(_rwkv_scan_kernel, tc), 7,
                                                          [None, s_prev], out_shape, [False, depth > 1])
    return pl.pallas_call(
        kern,
        grid=(g, l // tc),
        in_specs=[kspec(at) for _, at in ops] + [vspec, sspec(lyr_in)] + prev_specs,
        out_specs=[vspec, sspec(lyr)],
        out_shape=out_shape,
        input_output_aliases=aliases,
        compiler_params=_cparams("parallel", "arbitrary"),
        name="rwkv_scan",
    )(*(a for a, _ in ops), v, s0, *prev_args)


def _rwkv_post_kernel(y_ref, g_ref, bonus_ref, gng_ref, gnb_ref, o_ref):
    y = y_ref[...]
    mean = _segsum(y, RW_HD) * (1.0 / RW_HD)
    yc = y - mean
    var = _segsum(yc * yc, RW_HD) * (1.0 / RW_HD)
    yn = yc * lax.rsqrt(var + RW_GN_EPS) * gng_ref[...] + gnb_ref[...]
    o_ref[...] = ((yn + bonus_ref[...]) * g_ref[...]).astype(o_ref.dtype)


def _rwkv_post(y, g, bonus, gn_g, gn_b, row0, total_rows, prev, tm=256):
    m = y.shape[0]
    rb0 = row0 // tm
    fixed = lambda i: (0, 0)
    spec = pl.BlockSpec((tm, MIX_W), lambda i: (i, 0))
    out_shape = jax.ShapeDtypeStruct((total_rows, MIX_W), BF16)
    kern, prev_specs, prev_args, aliases = _carry_through(_rwkv_post_kernel, 5, [prev], [out_shape],
                                                          [m != total_rows])
    return pl.pallas_call(
        kern,
        grid=(m // tm,),
        in_specs=[spec, spec, spec, pl.BlockSpec((1, MIX_W), fixed), pl.BlockSpec((1, MIX_W), fixed)] + prev_specs,
        out_specs=pl.BlockSpec((tm, MIX_W), lambda i: (rb0 + i, 0)),
        out_shape=out_shape,
        input_output_aliases=aliases,
        compiler_params=_cparams("parallel"),
        name="rwkv_post",
    )(y, g, bonus, gn_g, gn_b, *prev_args)


RW_NB = RW_CHAINS // RW_HEADS
RW_TT = LANES


RW_KR = RW_HD + 8
RW_VR = RW_VP + 8


def _natural_to_chain_rows(x_ref, a_ref):
    for bl in range(RW_NB):
        xt = x_ref[bl].T
        for h in range(RW_HEADS):
            c = bl * RW_HEADS + h
            a_ref[c * RW_KR:c * RW_KR + RW_HD, :] = xt[h * RW_HD:(h + 1) * RW_HD]


def _keys_to_scan_kernel(x_ref, o_ref, a_ref):
    _natural_to_chain_rows(x_ref, a_ref)
    o_ref[...] = jnp.zeros_like(o_ref)
    for k in range(RW_HD):
        tile = a_ref[pl.ds(k, RW_CHAINS, stride=RW_KR), :]
        o_ref[0, pl.ds(k, RW_TT, stride=RW_KR), :] = jnp.concatenate([tile, tile], axis=0).T


def _vals_to_scan_kernel(x_ref, o_ref, a_ref):
    _natural_to_chain_rows(x_ref, a_ref)
    o_ref[...] = jnp.zeros_like(o_ref)
    for vp in range(RW_VP):
        halves = [a_ref[pl.ds(vh * RW_VP + vp, RW_CHAINS, stride=RW_KR), :] for vh in range(2)]
        o_ref[0, pl.ds(vp, RW_TT, stride=RW_VR), :] = jnp.concatenate(halves, axis=0).T


def _vals_from_scan_kernel(y_ref, o_ref, a_ref):
    for vp in range(RW_VP):
        tile = y_ref[0, pl.ds(vp, RW_TT, stride=RW_VR), :].T
        for vh in range(2):
            a_ref[pl.ds(vh * RW_VP + vp, RW_CHAINS, stride=RW_KR), :] = tile[vh * RW_CHAINS:(vh + 1) * RW_CHAINS]
    for bl in range(RW_NB):
        rows = [a_ref[c * RW_KR:c * RW_KR + RW_HD, :] for c in range(bl * RW_HEADS, (bl + 1) * RW_HEADS)]
        o_ref[bl] = jnp.concatenate(rows, axis=0).T


def _scan_relayout(kernel_fn, x, n_seq, seq_len, rows_per_t, to_scan, name):
    g, nl = n_seq // RW_NB, seq_len // RW_TT
    nat_spec = pl.BlockSpec((RW_NB, RW_TT, MIX_W), lambda i, l: (i, l, 0))
    scan_spec = pl.BlockSpec((1, RW_TT * rows_per_t, LANES), lambda i, l: (i, l, 0))
    nat_shape = jax.ShapeDtypeStruct((n_seq, seq_len, MIX_W), F32)
    scan_shape = jax.ShapeDtypeStruct((g, seq_len * rows_per_t, LANES), F32)
    x = x.reshape(n_seq, seq_len, MIX_W) if to_scan else x.reshape(g, seq_len * rows_per_t, LANES)
    out = pl.pallas_call(
        kernel_fn,
        grid=(g, nl),
        in_specs=[nat_spec if to_scan else scan_spec],
        out_specs=scan_spec if to_scan else nat_spec,
        out_shape=scan_shape if to_scan else nat_shape,
        scratch_shapes=[pltpu.VMEM((RW_CHAINS * RW_KR, RW_TT), F32)],
        compiler_params=_cparams("parallel", "parallel"),
        name=name,
    )(x)
    return out.reshape(g, seq_len, rows_per_t, LANES) if to_scan else out.reshape(n_seq * seq_len, MIX_W)


def _pad_rows(a, rows):
    return jnp.pad(a, ((0, 0), (0, 0), (0, rows - a.shape[2]), (0, 0)))


def _to_scan_keys(arrs, n_seq, seq_len):
    if seq_len % RW_TT == 0:
        return [(_scan_relayout(_keys_to_scan_kernel, a, n_seq, seq_len, RW_KR, True, "rwkv_keys_to_scan")[None], 0)
                for a in arrs]
    n, g = len(arrs), n_seq // RW_NB
    a = jnp.stack(arrs).reshape(n, g, RW_NB, seq_len, RW_HEADS, RW_HD).transpose(0, 1, 3, 5, 2, 4)
    a = a.reshape(n, g, seq_len, RW_HD, RW_CHAINS)
    a = jnp.concatenate([a, a], axis=-1)
    a = jnp.pad(a, ((0, 0), (0, 0), (0, 0), (0, RW_KR - RW_HD), (0, 0)))
    return [(a, i) for i in range(n)]


def _to_scan_vals(a, n_seq, seq_len):
    if seq_len % RW_TT == 0:
        return _scan_relayout(_vals_to_scan_kernel, a, n_seq, seq_len, RW_VR, True, "rwkv_vals_to_scan")
    g = n_seq // RW_NB
    a = a.reshape(g, RW_NB, seq_len, RW_HEADS, 2, RW_VP).transpose(0, 2, 5, 4, 1, 3)
    return _pad_rows(a.reshape(g, seq_len, RW_VP, LANES), RW_VR)


def _from_scan_vals(y, n_seq, seq_len):
    if seq_len % RW_TT == 0:
        return _scan_relayout(_vals_from_scan_kernel, y, n_seq, seq_len, RW_VR, False, "rwkv_vals_from_scan")
    g = n_seq // RW_NB
    y = y[:, :, :RW_VP].reshape(g, seq_len, RW_VP, 2, RW_NB, RW_HEADS).transpose(0, 4, 1, 5, 3, 2)
    return y.reshape(n_seq * seq_len, MIX_W)


def _state_to_scan(s, n_seq):
    nb = RW_CHAINS // RW_HEADS
    g = n_seq // nb
    s = s.reshape(g, nb, RW_HEADS, 2, RW_VP, RW_HD).transpose(0, 4, 5, 3, 1, 2)
    return s.reshape(g, RW_VP, RW_HD, LANES)


def _state_from_scan(s, n_seq):
    nb = RW_CHAINS // RW_HEADS
    g = n_seq // nb
    s = s.reshape(g, RW_VP, RW_HD, 2, nb, RW_HEADS).transpose(0, 4, 5, 3, 1, 2)
    return s.reshape(n_seq, RW_HEADS, RW_HD, RW_HD)


def _rwkv_branch(z, shift0, s0, lyr_in, p, row0, n_seq, seq_len, bb, lt, tc, depth, lyr, prev):
    r, w, k, kk, b, v, g, bonus, shift = _rwkv_prep(z, shift0, p, row0, n_seq, seq_len, bb, lt)
    keys = _to_scan_keys([r, w, k, kk, b], n_seq, seq_len)
    y, s_new = _rwkv_scan(*keys, _to_scan_vals(v, n_seq, seq_len), s0, lyr_in, tc, depth, lyr, prev[1])
    y = _from_scan_vals(y, n_seq, seq_len)
    y = _rwkv_post(y, g, bonus, p['gn_g'], p['gn_b'], row0, z.shape[0], prev[0])
    return y, s_new, shift.reshape(n_seq, RW_COLS)


def _seq_masks(rows, lc):
    ri = lax.broadcasted_iota(jnp.int32, (rows, rows), 0)
    ci = lax.broadcasted_iota(jnp.int32, (rows, rows), 1)
    same = (ri // lc) == (ci // lc)
    return ri, ci, same


def _rows_from_seq(x, bb, lc):
    return jnp.broadcast_to(x, (bb, lc, x.shape[-1])).reshape(bb * lc, x.shape[-1])


def _last_of_seq(x, bb, lc):
    x3 = x.reshape(bb, lc, x.shape[-1])
    pick = (lax.broadcasted_iota(jnp.int32, (1, lc, 1), 1) == lc - 1).astype(F32)
    return jnp.sum(x3 * pick, axis=1, keepdims=True)


def _transpose_rows(x):
    rows = x.shape[0]
    if rows < LANES:
        x = jnp.concatenate([x, jnp.zeros((LANES - rows, LANES), x.dtype)], axis=0)
    return x.T[:, :rows]


def _mlstm_kernel(bb, lc, q_ref, k_ref, v_ref, o_ref, gt_ref, gb_ref, ng_ref, c0_ref, n0_ref, m0_ref,
                  y_ref, c_ref, n_ref, m_ref):
    rows = bb * lc

    @pl.when(pl.program_id(1) == 0)
    def _():
        c_ref[...] = c0_ref[...]
        n_ref[...] = n0_ref[...]
        m_ref[...] = m0_ref[...]

    gt = ML_GATE_CAP * jnp.tanh((gt_ref[...] + gb_ref[...]) * (1.0 / ML_GATE_CAP))
    lane = lax.broadcasted_iota(jnp.int32, (rows, LANES), 1)
    ri, ci, same = _seq_masks(rows, lc)
    causal = same & (ci <= ri)
    cum = _dot_exact_lhs(causal.astype(F32), -_softplus(-gt))
    gt_t = _transpose_rows(gt)
    cum_t = _transpose_rows(cum)
    for hd in range(ML_HEADS):
        sl = slice(hd * ML_HD, (hd + 1) * ML_HD)
        c0 = c_ref[0, :, hd]
        n0 = n_ref[0, :, hd:hd + 1, :]
        m0 = m_ref[0, :, :, hd:hd + 1]
        q = q_ref[:, sl]
        k = k_ref[:, sl] * (ML_HD ** -0.5)
        v = v_ref[:, sl]
        bcol = cum[:, ML_HEADS + hd:ML_HEADS + hd + 1]
        igcol = gt[:, hd:hd + 1]
        rowb = gt_t[hd:hd + 1, :] - cum_t[ML_HEADS + hd:ML_HEADS + hd + 1, :]
        dlog = jnp.where(causal, bcol + rowb, -jnp.inf)
        m_inter = bcol + _rows_from_seq(m0, bb, lc)
        m_t = jnp.maximum(m_inter, jnp.max(dlog, axis=-1, keepdims=True))
        qk = lax.dot_general(q.astype(BF16), k.astype(BF16), (((1,), (1,)), ((), ())),
                             preferred_element_type=F32)
        wts = jnp.exp(dlog - m_t) * qk
        s_inter = jnp.exp(m_inter - m_t)
        q3 = q.reshape(bb, lc, ML_HD).astype(BF16)
        qc = jnp.einsum('btd,bde->bte', q3, c0.astype(BF16), preferred_element_type=F32).reshape(rows, ML_HD)
        num = _bdot(wts, v) + s_inter * qc
        qn = jnp.sum(q * _rows_from_seq(n0, bb, lc), axis=-1, keepdims=True)
        den = jnp.sum(wts, axis=-1, keepdims=True) + s_inter * qn
        hh = num / jnp.maximum(jnp.abs(den), jnp.exp(-m_t))
        hh = hh * lax.rsqrt(jnp.mean(hh * hh, axis=-1, keepdims=True) + EPS) * ng_ref[:, sl]
        y_ref[:, sl] = (_sigmoid(o_ref[:, sl]) * hh).astype(y_ref.dtype)

        stat = jnp.where(lane == 0, bcol, jnp.where(lane == 1, m_t, 0.0))
        last = _last_of_seq(stat, bb, lc)
        last_rows = _rows_from_seq(last, bb, lc)
        b_last, m_new = last_rows[:, 0:1], last_rows[:, 1:2]
        w_end = jnp.exp(b_last - bcol + igcol - m_new)
        f_end = jnp.exp(last[:, :, 0:1] + m0 - last[:, :, 1:2])
        kw3 = (k * w_end).reshape(bb, lc, ML_HD)
        upd = jnp.einsum('bsd,bse->bde', kw3.astype(BF16), v.reshape(bb, lc, ML_HD).astype(BF16),
                         preferred_element_type=F32)
        c_ref[0, :, hd] = f_end * c0 + upd
        n_ref[0, :, hd:hd + 1, :] = f_end * n0 + jnp.sum(kw3, axis=1, keepdims=True)
        m_ref[0, :, :, hd:hd + 1] = last[:, :, 1:2]


def _mlstm_seq_kernel(lc, q_ref, k_ref, v_ref, o_ref, gt_ref, gb_ref, ng_ref, c0_ref, n0_ref, m0_ref,
                      y_ref, c_ref, n_ref, m_ref):
    heads = range(ML_HEADS)

    @pl.when(pl.program_id(1) == 0)
    def _():
        c_ref[...] = c0_ref[...]
        n_ref[...] = n0_ref[...]
        m_ref[...] = m0_ref[...]

    gt = ML_GATE_CAP * jnp.tanh((gt_ref[...] + gb_ref[...]) * (1.0 / ML_GATE_CAP))
    ri, ci, _ = _seq_masks(lc, lc)
    causal = ci <= ri
    cum = _dot_exact_lhs(causal.astype(F32), -_softplus(-gt))
    gt_t = _transpose_rows(gt)
    cum_t = _transpose_rows(cum)
    sl = lambda hd: slice(hd * ML_HD, (hd + 1) * ML_HD)
    stack = lambda f: jnp.stack([f(hd) for hd in heads])
    q = stack(lambda hd: q_ref[:, sl(hd)])
    k = stack(lambda hd: k_ref[:, sl(hd)]) * (ML_HD ** -0.5)
    v = stack(lambda hd: v_ref[:, sl(hd)])
    bcol = stack(lambda hd: cum[:, ML_HEADS + hd:ML_HEADS + hd + 1])
    igcol = stack(lambda hd: gt[:, hd:hd + 1])
    rowb = stack(lambda hd: gt_t[hd:hd + 1, :] - cum_t[ML_HEADS + hd:ML_HEADS + hd + 1, :])
    c0 = c_ref[0, 0]
    n0 = stack(lambda hd: n_ref[0, 0, hd:hd + 1, :])
    m0 = stack(lambda hd: m_ref[0, 0, :, hd:hd + 1])
    qb, kb, vb = q.astype(BF16), k.astype(BF16), v.astype(BF16)

    dlog = jnp.where(causal[None], bcol + rowb, -jnp.inf)
    m_inter = bcol + m0
    m_t = jnp.maximum(m_inter, jnp.max(dlog, axis=-1, keepdims=True))
    qk = jnp.einsum('hqd,hkd->hqk', qb, kb, preferred_element_type=F32)
    wts = jnp.exp(dlog - m_t) * qk
    s_inter = jnp.exp(m_inter - m_t)
    qc = jnp.einsum('hqd,hde->hqe', qb, c0.astype(BF16), preferred_element_type=F32)
    num = jnp.einsum('hqk,hke->hqe', wts.astype(BF16), vb, preferred_element_type=F32) + s_inter * qc
    den = jnp.sum(wts, axis=-1, keepdims=True) + s_inter * jnp.sum(q * n0, axis=-1, keepdims=True)
    hh = num / jnp.maximum(jnp.abs(den), jnp.exp(-m_t))
    hh = hh * lax.rsqrt(jnp.mean(hh * hh, axis=-1, keepdims=True) + EPS)
    for hd in heads:
        y_ref[:, sl(hd)] = (_sigmoid(o_ref[:, sl(hd)]) * hh[hd] * ng_ref[:, sl(hd)]).astype(y_ref.dtype)

    b_last, m_new = bcol[:, lc - 1:lc, :], m_t[:, lc - 1:lc, :]
    w_end = jnp.exp(b_last - bcol + igcol - m_new)
    f_end = jnp.exp(b_last + m0 - m_new)
    kw = k * w_end
    upd = jnp.einsum('hsd,hse->hde', kw.astype(BF16), vb, preferred_element_type=F32)
    c_ref[0, 0] = f_end * c0 + upd
    n_new = f_end * n0 + jnp.sum(kw, axis=1, keepdims=True)
    for hd in heads:
        n_ref[0, 0, hd:hd + 1, :] = n_new[hd]
        m_ref[0, 0, :, hd:hd + 1] = m_new[hd]


def _mlstm_branch(z, z_gates, init, lyr_in, gate_bias, norm_g, row0, n_seq, seq_len, bb, lc, depth, lyr, prev):
    rows = bb * lc
    nc = seq_len // lc
    rb0 = row0 // rows
    rowmap = lambda part: (lambda b, c: (rb0 + b * nc + c, part))
    zspec = lambda part: pl.BlockSpec((rows, MIX_W), rowmap(part))
    state_specs = lambda at: [pl.BlockSpec((1, bb, ML_HEADS, ML_HD, ML_HD), lambda b, c: (at, b, 0, 0, 0)),
                              pl.BlockSpec((1, bb, ML_HEADS, ML_HD), lambda b, c: (at, b, 0, 0)),
                              pl.BlockSpec((1, bb, 1, ML_HEADS), lambda b, c: (at, b, 0, 0))]
    body = functools.partial(_mlstm_seq_kernel, lc) if bb == 1 else functools.partial(_mlstm_kernel, bb, lc)
    out_shape = [jax.ShapeDtypeStruct((z.shape[0], MIX_W), BF16),
                 jax.ShapeDtypeStruct((depth, n_seq, ML_HEADS, ML_HD, ML_HD), F32),
                 jax.ShapeDtypeStruct((depth, n_seq, ML_HEADS, ML_HD), F32),
                 jax.ShapeDtypeStruct((depth, n_seq, 1, ML_HEADS), F32)]
    kern, prev_specs, prev_args, aliases = _carry_through(
        body, 10, prev, out_shape, [n_seq * seq_len != z.shape[0]] + [depth > 1] * 3)
    return pl.pallas_call(
        kern,
        grid=(n_seq // bb, nc),
        in_specs=[zspec(0), zspec(1), zspec(2), zspec(3),
                  pl.BlockSpec((rows, LANES), lambda b, c: (rb0 + b * nc + c, 4 * MIX_W // LANES)),
                  pl.BlockSpec((1, LANES), lambda b, c: (0, 0)), pl.BlockSpec((1, MIX_W), lambda b, c: (0, 0))]
                 + state_specs(lyr_in) + prev_specs,
        out_specs=[pl.BlockSpec((rows, MIX_W), lambda b, c: (rb0 + b * nc + c, 0))] + state_specs(lyr),
        out_shape=out_shape,
        input_output_aliases=aliases,
        compiler_params=_cparams("parallel", "arbitrary"),
        name="mlstm_chunk",
    )(z, z, z, z, z_gates, gate_bias, norm_g, *init, *prev_args)


def _rope_rows(x, cos, sin):
    half = RT_HD // 2
    x1, x2 = x[:, :half], x[:, half:]
    return jnp.concatenate([x1 * cos - x2 * sin, x1 * sin + x2 * cos], axis=1)


def _ret_kernel(bb, lc, q_ref, k_ref, v_ref, g_ref, cs_ref, dec_ref, cd_ref, s0_ref, y_ref, s_ref):
    rows = bb * lc

    @pl.when(pl.program_id(1) == 0)
    def _():
        s_ref[...] = s0_ref[...]

    cos, sin = cs_ref[:, :RT_HD // 2], cs_ref[:, RT_HD // 2:]
    for hd in range(RT_HEADS):
        sl = slice(hd * RT_HD, (hd + 1) * RT_HD)
        s0 = s_ref[0, :, hd]
        q = _rope_rows(q_ref[:, sl], cos, sin)
        k = _rope_rows(k_ref[:, sl], cos, sin) * (RT_HD ** -0.5)
        v = v_ref[:, sl]
        cd = cd_ref[hd]
        qk = lax.dot_general(q.astype(BF16), k.astype(BF16), (((1,), (1,)), ((), ())),
                             preferred_element_type=F32)
        inner = _bdot(qk * dec_ref[hd], v)
        q3 = q.reshape(bb, lc, RT_HD).astype(BF16)
        cross = jnp.einsum('btd,bde->bte', q3, s0.astype(BF16), preferred_element_type=F32).reshape(rows, RT_HD)
        y = inner + cross * cd[:, 0:1]
        y = y * lax.rsqrt(jnp.mean(y * y, axis=-1, keepdims=True) + EPS)
        gt = g_ref[:, sl]
        y_ref[:, sl] = (gt * _sigmoid(gt) * y).astype(y_ref.dtype)
        kd = (k * cd[:, 1:2]).reshape(bb, lc, RT_HD).astype(BF16)
        upd = jnp.einsum('bsd,bse->bde', kd, v.reshape(bb, lc, RT_HD).astype(BF16), preferred_element_type=F32)
        s_ref[0, :, hd] = cd[0:1, 2:3] * s0 + upd


def _ret_seq_kernel(lc, q_ref, k_ref, v_ref, g_ref, cs_ref, dec_ref, cd_ref, s0_ref, y_ref, s_ref):
    heads = range(RT_HEADS)

    @pl.when(pl.program_id(1) == 0)
    def _():
        s_ref[...] = s0_ref[...]

    cos, sin = cs_ref[:, :RT_HD // 2], cs_ref[:, RT_HD // 2:]
    sl = lambda hd: slice(hd * RT_HD, (hd + 1) * RT_HD)
    stack = lambda f: jnp.stack([f(hd) for hd in heads])
    q = stack(lambda hd: _rope_rows(q_ref[:, sl(hd)], cos, sin))
    k = stack(lambda hd: _rope_rows(k_ref[:, sl(hd)], cos, sin)) * (RT_HD ** -0.5)
    vb = stack(lambda hd: v_ref[:, sl(hd)]).astype(BF16)
    s0 = s_ref[0, 0]
    cd = cd_ref[...]
    qb = q.astype(BF16)
    qk = jnp.einsum('hqd,hkd->hqk', qb, k.astype(BF16), preferred_element_type=F32)
    inner = jnp.einsum('hqk,hke->hqe', (qk * dec_ref[...]).astype(BF16), vb, preferred_element_type=F32)
    cross = jnp.einsum('hqd,hde->hqe', qb, s0.astype(BF16), preferred_element_type=F32)
    y = inner + cross * cd[:, :, 0:1]
    y = y * lax.rsqrt(jnp.mean(y * y, axis=-1, keepdims=True) + EPS)
    for hd in heads:
        gt = g_ref[:, sl(hd)]
        y_ref[:, sl(hd)] = (gt * _sigmoid(gt) * y[hd]).astype(y_ref.dtype)
    kd = (k * cd[:, :, 1:2]).astype(BF16)
    upd = jnp.einsum('hsd,hse->hde', kd, vb, preferred_element_type=F32)
    s_ref[0, 0] = cd[:, 0:1, 2:3] * s0 + upd


def _ret_branch(z, s0, lyr_in, row0, n_seq, seq_len, bb, lc, pos0, depth, lyr, prev):
    rows = bb * lc
    nc = seq_len // lc
    rb0 = row0 // rows
    half = RT_HD // 2
    inv = ROPE_BASE ** (-jnp.arange(half, dtype=F32) / half)
    pos = jnp.broadcast_to((jnp.arange(nc * lc, dtype=F32) + float(pos0)).reshape(nc, 1, lc), (nc, bb, lc))
    ang = pos.reshape(nc * rows)[:, None] * inv[None, :]
    cs = jnp.concatenate([jnp.cos(ang), jnp.sin(ang)], axis=1)
    log_g = jnp.log(1.0 - jnp.exp(jnp.linspace(math.log(1.0 / 32), math.log(1.0 / 512), RT_HEADS)))
    idx = jnp.arange(rows)
    t = (idx % lc).astype(F32)
    diff = t[:, None] - t[None, :]
    same = (idx[:, None] // lc) == (idx[None, :] // lc)
    dec = jnp.where(same & (diff >= 0), jnp.exp(log_g[:, None, None] * jnp.maximum(diff, 0.0)), 0.0)
    cd = jnp.zeros((RT_HEADS, rows, LANES), F32)
    cd = cd.at[:, :, 0].set(jnp.exp(log_g[:, None] * (t[None, :] + 1.0)))
    cd = cd.at[:, :, 1].set(jnp.exp(log_g[:, None] * (lc - 1.0 - t)[None, :]))
    cd = cd.at[:, :, 2].set(jnp.broadcast_to(jnp.exp(log_g * lc)[:, None], (RT_HEADS, rows)))
    rowmap = lambda part: (lambda b, c: (rb0 + b * nc + c, part))
    zspec = lambda part: pl.BlockSpec((rows, MIX_W), rowmap(part))
    sspec = lambda at: pl.BlockSpec((1, bb, RT_HEADS, RT_HD, RT_HD), lambda b, c: (at, b, 0, 0, 0))
    body = functools.partial(_ret_seq_kernel, lc) if bb == 1 else functools.partial(_ret_kernel, bb, lc)
    out_shape = [jax.ShapeDtypeStruct((z.shape[0], MIX_W), BF16),
                 jax.ShapeDtypeStruct((depth, n_seq, RT_HEADS, RT_HD, RT_HD), F32)]
    kern, prev_specs, prev_args, aliases = _carry_through(body, 8, prev, out_shape,
                                                          [n_seq * seq_len != z.shape[0], depth > 1])
    return pl.pallas_call(
        kern,
        grid=(n_seq // bb, nc),
        in_specs=[zspec(0), zspec(1), zspec(2), zspec(3),
                  pl.BlockSpec((rows, RT_HD), lambda b, c: (c, 0)),
                  pl.BlockSpec((RT_HEADS, rows, rows), lambda b, c: (0, 0, 0)),
                  pl.BlockSpec((RT_HEADS, rows, LANES), lambda b, c: (0, 0, 0)),
                  sspec(lyr_in)] + prev_specs,
        out_specs=[pl.BlockSpec((rows, MIX_W), lambda b, c: (rb0 + b * nc + c, 0)), sspec(lyr)],
        out_shape=out_shape,
        input_output_aliases=aliases,
        compiler_params=_cparams("parallel", "arbitrary"),
        name="retention_chunk",
    )(z, z, z, z, cs, dec, cd, s0, *prev_args)


def _mm_wt_kernel(x_ref, wt_ref, o_ref, wb_ref):
    @pl.when(pl.program_id(1) == 0)
    def _():
        tn = wb_ref.shape[1]
        for c in range(0, tn, LANES):
            wb_ref[:, c:c + LANES] = wt_ref[0, c:c + LANES, :].T.astype(wb_ref.dtype)

    o_ref[...] = jnp.dot(x_ref[...], wb_ref[...], preferred_element_type=F32)


def _mm_wt(x, w_t, lyr, col0, n, tm, tn):
    m, k = x.shape
    return pl.pallas_call(
        _mm_wt_kernel,
        grid=(n // tn, m // tm),
        in_specs=[pl.BlockSpec((tm, k), lambda j, i: (i, 0)),
                  pl.BlockSpec((pl.Element(1), pl.Element(tn), pl.Element(k)),
                               lambda j, i: (lyr, pl.multiple_of(col0 + j * tn, 8), 0))],
        out_specs=pl.BlockSpec((tm, tn), lambda j, i: (i, j)),
        out_shape=jax.ShapeDtypeStruct((m, n), F32),
        scratch_shapes=[pltpu.VMEM((k, tn), BF16)],
        compiler_params=_cparams("parallel", "arbitrary"),
        name="matmul_wt",
    )(x, w_t)


def _group_cfg(n_seq, seq_len):
    if seq_len % CHUNK == 0:
        return dict(rw_bb=1, rw_lt=256, rw_tc=64, ml_bb=1, rt_bb=1, ch_lc=CHUNK, at_bb=1, at_lq=512)
    return dict(rw_bb=32, rw_lt=seq_len, rw_tc=seq_len, ml_bb=16, rt_bb=8, ch_lc=seq_len, at_bb=16, at_lq=seq_len)


def kernel(x_prompt, x_sample, mem_prompt, state_rwkv_shift, state_rwkv, state_mlstm_c, state_mlstm_n,
           state_mlstm_m, state_ret, cache_mem_k, cache_mem_v, g_pre_mix, g_post_mix, g_pre_x, g_post_x,
           g_pre_ff, g_post_ff, g_mem, w_in, rw_mu, rw_w0, rw_w_up, rw_a0, rw_a_up, rw_g_up, rw_k_k, rw_k_a,
           rw_r_k, rw_gn_g, rw_gn_b, ml_i_b, ml_f_b, ml_norm_g, w_br, w_out, x_wq, x_wkv, x_wo, ff_w1, ff_w2):
    bp, lp, d = x_prompt.shape
    bs, ls, _ = x_sample.shape
    depth = w_in.shape[0]
    n_mem = mem_prompt.shape[1]
    tp, ts = bp * lp, bs * ls
    groups = [(0, bp, lp, 0), (tp, bs, ls, PAST_LEN)]

    x_groups = [x_prompt.reshape(tp, d), x_sample.reshape(ts, d)]
    mem = mem_prompt.reshape(bp * n_mem, d)
    x = h = None
    for (row0, *_), xg in zip(groups, x_groups):
        h = _rmsnorm(xg, g_pre_mix[0], row0=row0, total_rows=tp + ts, prev=h)

    zeros = lambda *s: jnp.zeros(s, F32)
    rw_scan0 = _state_to_scan(state_rwkv.reshape(depth * bs, RW_HEADS, RW_HD, RW_HD), depth * bs)
    inits = [
        lambda l: (zeros(bp, 1, RW_COLS), zeros(bp // RW_NB, RW_VP, RW_HD, LANES),
                   (zeros(1, bp, ML_HEADS, ML_HD, ML_HD), zeros(1, bp, ML_HEADS, ML_HD), zeros(1, bp, 1, ML_HEADS)),
                   zeros(1, bp, RT_HEADS, RT_HD, RT_HD), 0),
        lambda l: (state_rwkv_shift[l].reshape(bs, 1, RW_COLS), rw_scan0,
                   (state_mlstm_c, state_mlstm_n, state_mlstm_m.reshape(depth, bs, 1, ML_HEADS)), state_ret, l),
    ]
    shifts = [[], []]
    rw_states = [None, None]
    ml_states = [[None] * 3, [None] * 3]
    rt_states = [None, None]
    mem_k, mem_v = [], []
    w_in_t = jnp.swapaxes(w_in, 1, 2)
    w_kv, w_brb, w_outb, w_q, w_o, w_f1, w_f2 = (w.astype(BF16) for w in (x_wkv, w_br, w_out, x_wq, x_wo,
                                                                         ff_w1, ff_w2))
    y_rw = y_ml = y_rt = None
    for l in range(depth):
        kv = _mm(_rmsnorm(mem, g_mem[l], tm=256), w_kv, l, tm=256, tn=512)
        mem_k.append(kv[:, :X_W].reshape(bp, n_mem, X_HEADS, X_HD))
        mem_v.append(kv[:, X_W:].reshape(bp, n_mem, X_HEADS, X_HD))

        z_rw = _mm_wt(h, w_in_t, l, 0, RW_COLS, tm=512, tn=RW_COLS // 2)
        z_ml = _mm_wt(h, w_in_t, l, IN_ML, Z_MLW, tm=1024, tn=Z_MLW // 3)
        z_rg = _mm_wt(h, w_in_t, l, IN_RT, w_in.shape[2] - IN_RT, tm=1024, tn=1280)

        rw_p = dict(mu=rw_mu[l].reshape(1, -1), w0=rw_w0[l].reshape(1, -1), w_up=rw_w_up[l].astype(BF16),
                    a0=rw_a0[l].reshape(1, -1), a_up=rw_a_up[l].astype(BF16), g_up=rw_g_up[l].astype(BF16),
                    k_k=rw_k_k[l].reshape(1, -1), k_a=rw_k_a[l].reshape(1, -1), r_k=rw_r_k[l].reshape(1, -1),
                    gn_g=rw_gn_g[l].reshape(1, -1), gn_b=rw_gn_b[l].reshape(1, -1))
        gate_bias = jnp.concatenate([ml_i_b[l], ml_f_b[l], zeros(LANES - 2 * ML_HEADS)]).reshape(1, LANES)
        norm_g = ml_norm_g[l].reshape(1, MIX_W)

        for gi, (row0, n_seq, seq_len, pos0) in enumerate(groups):
            cfg = _group_cfg(n_seq, seq_len)
            shift0, s_rw0, ml0, rt0, lyr_in = inits[gi](l)
            y_rw, rw_states[gi], shift = _rwkv_branch(z_rw, shift0, s_rw0, lyr_in, rw_p, row0, n_seq, seq_len,
                                                      cfg['rw_bb'], cfg['rw_lt'], cfg['rw_tc'], depth, l,
                                                      (y_rw, rw_states[gi]))
            shifts[gi].append(shift)
            y_ml, *ml_states[gi] = _mlstm_branch(z_ml, z_ml, ml0, lyr_in, gate_bias, norm_g, row0, n_seq, seq_len,
                                                 cfg['ml_bb'], cfg['ch_lc'], depth, l, [y_ml] + ml_states[gi])
            y_rt, rt_states[gi] = _ret_branch(z_rg, rt0, lyr_in, row0, n_seq, seq_len, cfg['rt_bb'],
                                              cfg['ch_lc'], pos0, depth, l, [y_rt, rt_states[gi]])

        merged = _merge(y_rw, y_ml, y_rt, w_brb, l, z_rg, Z_GATE)
        if x is None:
            xh = (None, h)
            for (row0, *_), xg in zip(groups, x_groups):
                xh = _proj_res(merged, w_outb, l, xg, g_post_mix[l], g_pre_x[l], row0=row0, prev=xh)
            x, h = xh
        else:
            x, h = _proj_res(merged, w_outb, l, x, g_post_mix[l], g_pre_x[l])

        for gi, (row0, n_seq, seq_len, pos0) in enumerate(groups):
            cfg = _group_cfg(n_seq, seq_len)
            if gi == 0:
                mk, mv, lyr_in = mem_k[l][None], mem_v[l][None], 0
            else:
                mk, mv, lyr_in = cache_mem_k, cache_mem_v, l
            x, h = _attn(h, x, mk, mv, lyr_in, w_q, w_o, l, g_post_x[l], g_pre_ff[l], row0, n_seq, seq_len,
                         cfg['at_bb'], cfg['at_lq'])

        if l + 1 < depth:
            x, h = _mlp(h, w_f1, w_f2, l, x, g_post_ff[l], g_pre_mix[l + 1])
        else:
            y_out = [_mlp(h, w_f1, w_f2, l, x, g_post_ff[l], g_pre_mix[0], row0=row0, n_rows=n_seq * seq_len)[0]
                     for row0, n_seq, seq_len, _ in groups]

    def states(gi, n_seq):
        c_new, n_new, m_new = ml_states[gi]
        s_rw = _state_from_scan(rw_states[gi], depth * n_seq).reshape(depth, n_seq, RW_HEADS, RW_HD, RW_HD)
        return (jnp.stack(shifts[gi]), s_rw, c_new, n_new, m_new.reshape(depth, n_seq, ML_HEADS), rt_states[gi])

    return (y_out[0].reshape(bp, lp, d), y_out[1].reshape(bs, ls, d), *states(0, bp),
            jnp.stack(mem_k), jnp.stack(mem_v), *states(1, bs))
```
